```python
import math
import jax
import jax.numpy as jnp
from jax import lax
import numpy as np

D_MODEL = 1024
BATCH = 8
SEQ = 4096
DEPTH = 1

N_HEADS = 8
N_KV_HEADS = 2
HEAD_DIM = 64
WINDOW = 128
BLOCK = 128
ATTN_Q = N_HEADS * HEAD_DIM
ATTN_KV = N_KV_HEADS * HEAD_DIM
N_BUCKETS = 32
MAX_EXACT = N_BUCKETS // 2
MAX_DISTANCE = 128
GM_WIDTH = 512
GM_GROUPS = 4
GM_GROUP_DIM = GM_WIDTH // GM_GROUPS
GM_CHUNK = 128
IN_WIDTH = ATTN_Q + 2 * ATTN_KV + 2 * GM_WIDTH + 2 * D_MODEL
N_GROUPS = 4
EXPERTS_PER_GROUP = 8
N_EXPERTS = N_GROUPS * EXPERTS_PER_GROUP
TOP_K = 2
D_EXPERT = 512
MOE_BLOCK = 128

EPS = 1e-6
NEG = -1e30

kernel_name = 'hybrid_swa_gmlp_hmoe_adaln'


def rms_norm(x, g):
    xf = x.astype(jnp.float32)
    y = xf * lax.rsqrt(jnp.mean(xf * xf, axis=-1, keepdims=True) + EPS)
    return (y * g.astype(jnp.float32)).astype(x.dtype)


def layer_norm(x, g, b):
    xf = x.astype(jnp.float32)
    mu = jnp.mean(xf, axis=-1, keepdims=True)
    var = jnp.mean(jnp.square(xf - mu), axis=-1, keepdims=True)
    y = (xf - mu) * lax.rsqrt(var + EPS)
    return (y * g.astype(jnp.float32) + b.astype(jnp.float32)).astype(x.dtype)


def t5_bucket(dist):
    n = jnp.maximum(dist, 0)
    nf = jnp.maximum(n, 1).astype(jnp.float32)
    large = MAX_EXACT + (jnp.log(nf / MAX_EXACT) / math.log(MAX_DISTANCE / MAX_EXACT)
                         * (N_BUCKETS - MAX_EXACT)).astype(jnp.int32)
    large = jnp.minimum(large, N_BUCKETS - 1)
    return jnp.where(n < MAX_EXACT, n, large)


def band(t):
    prev = jnp.concatenate([jnp.zeros_like(t[:, :1]), t[:, :-1]], axis=1)
    return jnp.concatenate([prev, t], axis=2)


def sliding_window_attention(q, k, v, positions, sinks, rel_bias):
    B, S, _ = q.shape
    nb = S // BLOCK
    grp = N_HEADS // N_KV_HEADS
    q = q.reshape(B, nb, BLOCK, N_KV_HEADS, grp, HEAD_DIM)
    kb = band(k.reshape(B, nb, BLOCK, N_KV_HEADS, HEAD_DIM))
    vb = band(v.reshape(B, nb, BLOCK, N_KV_HEADS, HEAD_DIM))
    pq = positions.reshape(B, nb, BLOCK)
    pk = band(pq)
    bucket = t5_bucket(pq[..., :, None] - pk[..., None, :])
    bias = rel_bias[bucket].reshape(B, nb, BLOCK, 2 * BLOCK, N_KV_HEADS, grp)
    bias = bias.transpose(0, 1, 4, 5, 2, 3).astype(jnp.float32)
    qi = jnp.arange(BLOCK)[:, None] + BLOCK
    kj = jnp.arange(2 * BLOCK)[None, :]
    local = (kj <= qi) & (qi - kj < WINDOW)
    first = (jnp.arange(nb)[:, None, None] > 0) | (kj[None] >= BLOCK)
    mask = (local[None] & first)[None, :, None, None]
    scale = HEAD_DIM ** -0.5
    s = jnp.einsum('bnqhgd,bnshd->bnhgqs', q, kb).astype(jnp.float32)
    s = jnp.where(mask, s * scale + bias, NEG)
    sink = sinks.astype(jnp.float32).reshape(N_KV_HEADS, grp)[None, None, :, :, None, None]
    m = jnp.maximum(jnp.max(s, axis=-1, keepdims=True), sink)
    p = jnp.exp(s - m)
    p = p / (jnp.sum(p, axis=-1, keepdims=True) + jnp.exp(sink - m))
    o = jnp.einsum('bnhgqs,bnshd->bnqhgd', p.astype(v.dtype), vb)
    return o.reshape(B, S, ATTN_Q)


def spatial_gating(u, v, ln_g, ln_b, w_s, b_s):
    B, S, _ = v.shape
    nc = S // GM_CHUNK
    vn = layer_norm(v, ln_g, ln_b).reshape(B, nc, GM_CHUNK, GM_GROUPS, GM_GROUP_DIM)
    w = w_s * jnp.tril(jnp.ones((GM_CHUNK, GM_CHUNK), w_s.dtype))
    sv = jnp.einsum('gts,bnsgc->bntgc', w, vn) + b_s.T[:, :, None]
    return u * sv.reshape(B, S, GM_WIDTH)


def hierarchical_moe(h, w_rg, b_rg, w_re, b_re, w_gate, w_up, w_down):
    B, S, D = h.shape
    T = B * S
    hf = h.reshape(T, D)
    lg = (hf @ w_rg).astype(jnp.float32) + b_rg.astype(jnp.float32)
    g_idx = jnp.argmax(lg, axis=-1)
    p_g = jnp.take_along_axis(jax.nn.softmax(lg, axis=-1), g_idx[:, None], axis=-1)[:, 0]
    le = ((hf @ w_re).astype(jnp.float32) + b_re.astype(jnp.float32)).reshape(T, N_GROUPS, EXPERTS_PER_GROUP)
    le = jnp.take_along_axis(le, g_idx[:, None, None], axis=1)[:, 0]
    top_p, top_i = lax.top_k(jax.nn.softmax(le, axis=-1), TOP_K)
    top_p = top_p / jnp.sum(top_p, axis=-1, keepdims=True)
    weights = p_g[:, None] * top_p
    expert_id = g_idx[:, None] * EXPERTS_PER_GROUP + top_i
    A = T * TOP_K
    e_flat = expert_id.reshape(A)
    tok_flat = jnp.repeat(jnp.arange(T, dtype=jnp.int32), TOP_K)
    w_flat = weights.reshape(A)
    order = jnp.argsort(e_flat)
    e_sorted, tok_sorted, w_sorted = e_flat[order], tok_flat[order], w_flat[order]
    counts = jnp.bincount(e_flat, length=N_EXPERTS)
    starts = jnp.cumsum(counts) - counts
    padded = (counts + MOE_BLOCK - 1) // MOE_BLOCK * MOE_BLOCK
    pends = jnp.cumsum(padded)
    pstarts = pends - padded
    dest = pstarts[e_sorted] + (jnp.arange(A) - starts[e_sorted])
    R = (A + MOE_BLOCK - 1) // MOE_BLOCK * MOE_BLOCK + N_EXPERTS * MOE_BLOCK
    NB = R // MOE_BLOCK
    row_tok = jnp.zeros((R,), jnp.int32).at[dest].set(tok_sorted)
    row_w = jnp.zeros((R,), jnp.float32).at[dest].set(w_sorted)
    block_e = jnp.minimum(jnp.searchsorted(pends, jnp.arange(NB) * MOE_BLOCK, side='right'), N_EXPERTS - 1)
    xr = hf[row_tok].reshape(NB, MOE_BLOCK, D)

    def expert_block(args):
        xb, e = args
        return (jax.nn.silu(xb @ w_gate[e]) * (xb @ w_up[e])) @ w_down[e]

    yr = lax.map(expert_block, (xr, block_e)).reshape(R, D)
    out = jax.ops.segment_sum(yr.astype(jnp.float32) * row_w[:, None], row_tok, num_segments=T)
    return out.astype(h.dtype).reshape(B, S, D)


def setup_inputs(seed: int = 0) -> dict:
    key = jax.random.key(seed)
    ks = jax.random.split(key, 26)

    def nrm(k, shape, scale):
        return jax.random.normal(k, shape, jnp.float32) * scale

    L = DEPTH
    return {
        'x': nrm(ks[0], (BATCH, SEQ, D_MODEL), 1.0),
        'c': nrm(ks[1], (BATCH, D_MODEL), 1.0),
        'positions': jnp.broadcast_to(jnp.arange(SEQ, dtype=jnp.int32), (BATCH, SEQ)),
        'rel_bias': nrm(ks[2], (N_BUCKETS, N_HEADS), 0.5),
        'w_ada': nrm(ks[3], (L, D_MODEL, 6 * D_MODEL), D_MODEL ** -0.5),
        'b_ada': nrm(ks[4], (L, 6 * D_MODEL), 0.02),
        'norm1_g': 1.0 + nrm(ks[5], (L, D_MODEL), 0.02),
        'w_in': nrm(ks[6], (L, D_MODEL, IN_WIDTH), D_MODEL ** -0.5),
        'sinks': nrm(ks[7], (L, N_HEADS), 0.5),
        'gm_ln_g': 1.0 + nrm(ks[8], (L, GM_WIDTH), 0.02),
        'gm_ln_b': nrm(ks[9], (L, GM_WIDTH), 0.02),
        'gm_w_s': nrm(ks[10], (L, GM_GROUPS, GM_CHUNK, GM_CHUNK), GM_CHUNK ** -0.5),
        'gm_b_s': 1.0 + nrm(ks[11], (L, GM_GROUPS, GM_CHUNK), 0.02),
        'p_a': nrm(ks[12], (L, ATTN_Q, D_MODEL), ATTN_Q ** -0.5),
        'p_b': nrm(ks[13], (L, GM_WIDTH, D_MODEL), GM_WIDTH ** -0.5),
        'w_o': nrm(ks[14], (L, D_MODEL, D_MODEL), D_MODEL ** -0.5),
        'norm2_g': 1.0 + nrm(ks[15], (L, D_MODEL), 0.02),
        'w_router_g': nrm(ks[16], (L, D_MODEL, N_GROUPS), D_MODEL ** -0.5),
        'b_router_g': nrm(ks[17], (L, N_GROUPS), 0.01),
        'w_router_e': nrm(ks[18], (L, D_MODEL, N_EXPERTS), D_MODEL ** -0.5),
        'b_router_e': nrm(ks[19], (L, N_EXPERTS), 0.01),
        'w_gate': nrm(ks[20], (L, N_EXPERTS, D_MODEL, D_EXPERT), D_MODEL ** -0.5),
        'w_up': nrm(ks[21], (L, N_EXPERTS, D_MODEL, D_EXPERT), D_MODEL ** -0.5),
        'w_down': nrm(ks[22], (L, N_EXPERTS, D_EXPERT, D_MODEL), D_EXPERT ** -0.5),
        'final_g': 1.0 + nrm(ks[23], (D_MODEL,), 0.02),
    }


def reference(x, c, positions, rel_bias, w_ada, b_ada, norm1_g, w_in, sinks, gm_ln_g, gm_ln_b,
              gm_w_s, gm_b_s, p_a, p_b, w_o, norm2_g, w_router_g, b_router_g, w_router_e,
              b_router_e, w_gate, w_up, w_down, final_g):
    splits = np.cumsum([ATTN_Q, ATTN_KV, ATTN_KV, GM_WIDTH, GM_WIDTH, D_MODEL]).tolist()
    cs = jax.nn.silu(c)
    for l in range(DEPTH):
        mod = cs @ w_ada[l] + b_ada[l]
        sh1, sc1, g1, sh2, sc2, g2 = [m[:, None, :] for m in jnp.split(mod, 6, axis=-1)]
        h = rms_norm(x, norm1_g[l]) * (1.0 + sc1) + sh1
        q, k, v, gu, gv, ga, gb = jnp.split(h @ w_in[l], splits, axis=-1)
        y_a = sliding_window_attention(q, k, v, positions, sinks[l], rel_bias)
        y_b = spatial_gating(jax.nn.gelu(gu), jax.nn.gelu(gv), gm_ln_g[l], gm_ln_b[l], gm_w_s[l], gm_b_s[l])
        merged = jax.nn.sigmoid(ga) * (y_a @ p_a[l]) + jax.nn.sigmoid(gb) * (y_b @ p_b[l])
        x = x + g1 * (merged @ w_o[l])
        h = rms_norm(x, norm2_g[l]) * (1.0 + sc2) + sh2
        x = x + g2 * hierarchical_moe(h, w_router_g[l], b_router_g[l], w_router_e[l], b_router_e[l],
                                      w_gate[l], w_up[l], w_down[l])
    return rms_norm(x, final_g)
```

```python
import functools
import math

import jax
import jax.numpy as jnp
from jax import lax
from jax.experimental import pallas as pl
from jax.experimental.pallas import tpu as pltpu

D_MODEL = 1024
N_HEADS = 8
N_KV_HEADS = 2
HEAD_DIM = 64
BLOCK = 128
ATTN_Q = N_HEADS * HEAD_DIM
ATTN_KV = N_KV_HEADS * HEAD_DIM
N_BUCKETS = 32
MAX_EXACT = N_BUCKETS // 2
MAX_DISTANCE = 128
GM_WIDTH = 512
GM_GROUPS = 4
GM_CHUNK = 128
N_GROUPS = 4
EXPERTS_PER_GROUP = 8
N_EXPERTS = N_GROUPS * EXPERTS_PER_GROUP
TOP_K = 2
D_EXPERT = 512
EPS = 1e-6
NEG = -1e30

LANES = 128
ROUTER_LANES = LANES
EXPERT_LANE0 = N_GROUPS
TM = 512
BM = 256
TN_DISPATCH = 512
TN_COMBINE = 256
VMEM_LIMIT = 56 * 1024 * 1024

Q0, K0, V0, GU0, GV0, GA0, GB0, IN_END = (0, 512, 640, 768, 1280, 1792, 2816, 3840)


def _dot(a, b):
    return jnp.dot(a, b, preferred_element_type=jnp.float32)


def _dot_nt(a, b):
    return lax.dot_general(a, b, (((1,), (1,)), ((), ())), preferred_element_type=jnp.float32)


def _gelu_tanh(x):
    return 0.5 * x * (1.0 + jnp.tanh(math.sqrt(2.0 / math.pi) * (x + 0.044715 * (x * x * x))))


def _sigmoid(x):
    return 1.0 / (1.0 + jnp.exp(-x))


def _adaln_kernel(c_ref, w_ref, b_ref, o_ref):
    c = c_ref[...]
    cs = c * _sigmoid(c)
    o_ref[...] = _dot(cs, w_ref[...]) + b_ref[...]


def _adaln_mod(c, w, b):
    bsz, d = c.shape
    n = w.shape[1]
    tn = 1024
    return pl.pallas_call(
        _adaln_kernel,
        grid=(n // tn,),
        in_specs=[pl.BlockSpec((bsz, d), lambda i: (0, 0)),
                  pl.BlockSpec((d, tn), lambda i: (0, i)),
                  pl.BlockSpec((1, tn), lambda i: (0, i))],
        out_specs=pl.BlockSpec((bsz, tn), lambda i: (0, i)),
        out_shape=jax.ShapeDtypeStruct((bsz, n), jnp.float32),
        compiler_params=pltpu.CompilerParams(dimension_semantics=("arbitrary",)),
        name="adaln_mod",
    )(c, w, b.reshape(1, n))


def _mixer_kernel(relb_ref, sinks_ref,
                  x_ref, mod_ref, pos_ref, n1g_ref, win_ref, lng_ref, lnb_ref, ws_ref, bsb_ref,
                  pa_ref, pb_ref, wo_ref, n2g_ref, wr_ref, br_ref,
                  x1_ref, h2_ref, ri_ref, rw_ref, cnt_ref,
                  kbuf, vbuf, pbuf, biasm, key, carry, ya, yb, flag):
    b = pl.program_id(0)
    j = pl.program_id(1)
    nblk = TM // BLOCK
    bf16 = jnp.bfloat16

    @pl.when((b == 0) & (j == 0))
    def _():
        carry[...] = jnp.zeros_like(carry)
        key[...] = jnp.zeros_like(key)
        flag[0] = 1

    @pl.when(j == 0)
    def _():
        kbuf[0:BLOCK, :] = jnp.zeros((BLOCK, LANES), jnp.float32)
        vbuf[0:BLOCK, :] = jnp.zeros((BLOCK, LANES), jnp.float32)
        pbuf[:, 0:BLOCK] = jnp.zeros((1, BLOCK), jnp.int32)

    x = x_ref[...]
    sh1, sc1, g1 = mod_ref[0:1, :], mod_ref[1:2, :], mod_ref[2:3, :]
    sh2, sc2, g2 = mod_ref[3:4, :], mod_ref[4:5, :], mod_ref[5:6, :]

    xn = x * lax.rsqrt(jnp.mean(x * x, axis=-1, keepdims=True) + EPS)
    h = (xn * (n1g_ref[...] * (1.0 + sc1)) + sh1).astype(bf16)

    lane = lax.broadcasted_iota(jnp.int32, (1, LANES), 1)
    lo = lane < HEAD_DIM
    q = _dot(h, win_ref[:, Q0:K0]) * (HEAD_DIM ** -0.5)
    lo4 = jnp.concatenate([lo] * (ATTN_Q // LANES), axis=1)
    q_lo = jnp.where(lo4, q, 0.0).astype(bf16)
    q_hi = jnp.where(lo4, 0.0, q).astype(bf16)
    kv = _dot(h, win_ref[:, K0:GU0])
    kbuf[BLOCK:, :] = kv[:, 0:LANES]
    vbuf[BLOCK:, :] = kv[:, LANES:]
    pbuf[:, BLOCK:] = pos_ref[...]
    kf = kbuf[...]
    vf = vbuf[...]
    kr = pltpu.roll(kf, HEAD_DIM, 1)
    vr = pltpu.roll(vf, HEAD_DIM, 1)
    kd = (jnp.where(lo, kf, kr).astype(bf16), jnp.where(lo, kr, kf).astype(bf16))
    v_lo = (jnp.where(lo, vf, 0.0).astype(bf16), jnp.where(lo, vr, 0.0).astype(bf16))
    v_hi = (jnp.where(lo, 0.0, vr).astype(bf16), jnp.where(lo, 0.0, vf).astype(bf16))

    qi = lax.broadcasted_iota(jnp.int32, (BLOCK, 2 * BLOCK), 0)
    kj = lax.broadcasted_iota(jnp.int32, (BLOCK, 2 * BLOCK), 1)
    local = (kj > qi) & (kj <= qi + BLOCK)

    for i in range(nblk):
        pk = pbuf[:, i * BLOCK:(i + 2) * BLOCK]
        pq = pbuf[:, (i + 1) * BLOCK:(i + 2) * BLOCK]
        rel = pk - pq[:, 0:1]
        is_first = (j == 0) if i == 0 else None
        mismatch = jnp.max((rel != key[...]).astype(jnp.int32))
        if i == 0:
            stale = (mismatch != 0) | is_first | (flag[0] != 0)
        else:
            stale = (mismatch != 0) | (flag[0] != 0)

        @pl.when(stale)
        def _():
            pq_col = jnp.transpose(jnp.broadcast_to(pq, (BLOCK, BLOCK)))
            dist = jnp.concatenate([pq_col, pq_col], axis=1) - pk
            n = jnp.maximum(dist, 0)
            nf = jnp.maximum(n, 1).astype(jnp.float32)
            large = MAX_EXACT + (jnp.log(nf / MAX_EXACT) / math.log(MAX_DISTANCE / MAX_EXACT)
                                 * (N_BUCKETS - MAX_EXACT)).astype(jnp.int32)
            large = jnp.minimum(large, N_BUCKETS - 1)
            bucket = jnp.where(n < MAX_EXACT, n, large)
            if i == 0:
                allowed = local & (kj >= jnp.where(is_first, BLOCK, 0))
            else:
                allowed = local
            acc = [jnp.zeros((BLOCK, 2 * BLOCK), jnp.float32) for _ in range(N_HEADS)]
            for bk in range(N_BUCKETS):
                eq = bucket == bk
                for hd in range(N_HEADS):
                    acc[hd] = jnp.where(eq, relb_ref[bk * N_HEADS + hd], acc[hd])
            for hd in range(N_HEADS):
                biasm[hd] = jnp.where(allowed, acc[hd], NEG)
            key[...] = rel
            if i == 0:
                flag[0] = is_first.astype(jnp.int32)
            else:
                flag[0] = 0

        rows = slice(i * BLOCK, (i + 1) * BLOCK)
        band = slice(i * BLOCK, (i + 2) * BLOCK)
        for pr in range(N_HEADS // 2):
            kvh = (2 * pr) // (N_HEADS // N_KV_HEADS)
            cols = slice(pr * LANES, (pr + 1) * LANES)
            kb = kd[kvh][band]
            o = None
            for half, (qq, vv) in enumerate(((q_lo, v_lo), (q_hi, v_hi))):
                hd = 2 * pr + half
                s = _dot_nt(qq[rows, cols], kb) + biasm[hd]
                sink = sinks_ref[hd]
                m = jnp.maximum(jnp.max(s, axis=-1, keepdims=True), sink)
                p = jnp.exp(s - m)
                den = jnp.sum(p, axis=-1, keepdims=True) + jnp.exp(sink - m)
                oh = _dot(p.astype(bf16), vv[kvh][band]) * (1.0 / den)
                o = oh if o is None else o + oh
            ya[rows, cols] = o.astype(bf16)

    kbuf[0:BLOCK, :] = kv[TM - BLOCK:, 0:LANES]
    vbuf[0:BLOCK, :] = kv[TM - BLOCK:, LANES:]
    pbuf[:, 0:BLOCK] = pos_ref[:, TM - BLOCK:]

    u = _gelu_tanh(_dot(h, win_ref[:, GU0:GV0]))
    vg = _gelu_tanh(_dot(h, win_ref[:, GV0:GA0]))
    mu = jnp.mean(vg, axis=-1, keepdims=True)
    vc = vg - mu
    var = jnp.mean(vc * vc, axis=-1, keepdims=True)
    vn = (vc * lax.rsqrt(var + EPS) * lng_ref[...] + lnb_ref[...]).astype(bf16)
    ti = lax.broadcasted_iota(jnp.int32, (GM_CHUNK, GM_CHUNK), 0)
    si = lax.broadcasted_iota(jnp.int32, (GM_CHUNK, GM_CHUNK), 1)
    tril = si <= ti
    for g in range(GM_GROUPS):
        wg = jnp.where(tril, ws_ref[g], 0.0).astype(bf16)
        cols = slice(g * LANES, (g + 1) * LANES)
        for cidx in range(TM // GM_CHUNK):
            rows = slice(cidx * GM_CHUNK, (cidx + 1) * GM_CHUNK)
            sv = _dot(wg, vn[rows, cols]) + bsb_ref[g]
            yb[rows, cols] = (u[rows, cols] * sv).astype(bf16)

    ga = _dot(h, win_ref[:, GA0:GB0])
    gb = _dot(h, win_ref[:, GB0:IN_END])
    merged = _sigmoid(ga) * _dot(ya[...], pa_ref[...]) + _sigmoid(gb) * _dot(yb[...], pb_ref[...])
    x1 = x + g1 * _dot(merged.astype(bf16), wo_ref[...])
    x1_ref[...] = x1

    xn2 = x1 * lax.rsqrt(jnp.mean(x1 * x1, axis=-1, keepdims=True) + EPS)
    h2 = xn2 * (n2g_ref[...] * (1.0 + sc2)) + sh2
    h2_ref[...] = h2
    logits = _dot(h2.astype(bf16), wr_ref[...]) + br_ref[...]
    lane_f = lane.astype(jnp.float32)
    big = float(2 * LANES)
    is_grp = lane < N_GROUPS
    lg = jnp.where(is_grp, logits, NEG)
    lg_max = jnp.max(lg, axis=-1, keepdims=True)
    g_idx = jnp.min(jnp.where(lg == lg_max, lane_f, big), axis=-1, keepdims=True)
    p_g = 1.0 / jnp.sum(jnp.where(is_grp, jnp.exp(lg - lg_max), 0.0), axis=-1, keepdims=True)
    lane_grp = jnp.floor((lane_f - EXPERT_LANE0) * (1.0 / EXPERTS_PER_GROUP))
    in_grp = (lane >= EXPERT_LANE0) & (lane < EXPERT_LANE0 + N_EXPERTS) & (lane_grp == g_idx)
    le = jnp.where(in_grp, logits, NEG)
    m1 = jnp.max(le, axis=-1, keepdims=True)
    i1 = jnp.min(jnp.where(le == m1, lane_f, big), axis=-1, keepdims=True)
    oh1 = lane_f == i1
    le2 = jnp.where(oh1, NEG, le)
    m2 = jnp.max(le2, axis=-1, keepdims=True)
    i2 = jnp.min(jnp.where(le2 == m2, lane_f, big), axis=-1, keepdims=True)
    oh2 = lane_f == i2
    e2 = jnp.exp(m2 - m1)
    w1 = p_g / (1.0 + e2)
    w2 = p_g * e2 / (1.0 + e2)

    oh = jnp.where(oh1 | oh2, 1.0, 0.0)
    tr = lax.broadcasted_iota(jnp.int32, (TM, TM), 0)
    tc = lax.broadcasted_iota(jnp.int32, (TM, TM), 1)
    strict = jnp.where(tc < tr, 1.0, 0.0).astype(bf16)
    before = _dot(strict, oh.astype(bf16)) + carry[...]
    r1 = jnp.sum(jnp.where(oh1, before, 0.0), axis=-1, keepdims=True)
    r2 = jnp.sum(jnp.where(oh2, before, 0.0), axis=-1, keepdims=True)
    carry[...] = carry[...] + jnp.sum(oh, axis=0, keepdims=True)
    cnt_ref[...] = carry[...]

    ri = jnp.where(lane == 0, i1 - EXPERT_LANE0,
                   jnp.where(lane == 1, i2 - EXPERT_LANE0,
                             jnp.where(lane == 2, r1, jnp.where(lane == 3, r2, 0.0))))
    ri_ref[...] = ri.astype(jnp.int32)
    rw_ref[...] = jnp.where(lane == 0, w1, jnp.where(lane == 1, w2, 0.0))


def _mixer(x, mod, positions, rel_bias, n1g, w_in, sinks, lng, lnb, w_s, bsb, p_a, p_b, w_o, n2g, w_r, b_r):
    bsz, seq, d = x.shape
    nj = seq // TM
    const = lambda *shape: pl.BlockSpec(shape, lambda b, j: (0,) * len(shape), pipeline_mode=pl.Buffered(1))
    smem = pl.BlockSpec(memory_space=pltpu.SMEM)
    tile = lambda w: pl.BlockSpec((None, TM, w), lambda b, j: (b, j, 0))
    return pl.pallas_call(
        _mixer_kernel,
        grid=(bsz, nj),
        in_specs=[smem, smem,
                  tile(d),
                  pl.BlockSpec((None, 6, d), lambda b, j: (b, 0, 0)),
                  pl.BlockSpec((None, None, 1, TM), lambda b, j: (b, j, 0, 0)),
                  const(1, d), const(d, IN_END), const(1, GM_WIDTH), const(1, GM_WIDTH),
                  const(GM_GROUPS, GM_CHUNK, GM_CHUNK), const(GM_GROUPS, GM_CHUNK, LANES),
                  const(ATTN_Q, d), const(GM_WIDTH, d), const(d, d), const(1, d),
                  const(d, ROUTER_LANES), const(1, ROUTER_LANES)],
        out_specs=[tile(d), tile(d), tile(LANES), tile(LANES),
                   pl.BlockSpec((1, LANES), lambda b, j: (0, 0))],
        out_shape=[jax.ShapeDtypeStruct((bsz, seq, d), jnp.float32),
                   jax.ShapeDtypeStruct((bsz, seq, d), jnp.float32),
                   jax.ShapeDtypeStruct((bsz, seq, LANES), jnp.int32),
                   jax.ShapeDtypeStruct((bsz, seq, LANES), jnp.float32),
                   jax.ShapeDtypeStruct((1, LANES), jnp.float32)],
        scratch_shapes=[pltpu.VMEM((TM + BLOCK, LANES), jnp.float32),
                        pltpu.VMEM((TM + BLOCK, LANES), jnp.float32),
                        pltpu.VMEM((1, TM + BLOCK), jnp.int32),
                        pltpu.VMEM((N_HEADS, BLOCK, 2 * BLOCK), jnp.float32),
                        pltpu.VMEM((1, 2 * BLOCK), jnp.int32),
                        pltpu.VMEM((1, LANES), jnp.float32),
                        pltpu.VMEM((TM, ATTN_Q), jnp.bfloat16),
                        pltpu.VMEM((TM, GM_WIDTH), jnp.bfloat16),
                        pltpu.SMEM((1,), jnp.int32)],
        compiler_params=pltpu.CompilerParams(dimension_semantics=("arbitrary", "arbitrary"),
                                             vmem_limit_bytes=VMEM_LIMIT),
        name="mixer",
    )(rel_bias.reshape(-1), sinks, x, mod, positions.reshape(bsz, nj, 1, TM), n1g, w_in, lng, lnb,
      w_s, bsb, p_a, p_b, w_o, n2g, w_r, b_r)


def _row_copy(src_hbm, dst_hbm, sem, src_row, dst_row):
    return pltpu.make_async_copy(src_hbm.at[pl.ds(src_row, 1)], dst_hbm.at[pl.ds(dst_row, 1)], sem)


def _dispatch_kernel(dest_ref, h_hbm, xr_hbm, sem):
    base = pl.program_id(0) * TN_DISPATCH

    def issue(t, carry):
        tok = base + t
        for k in range(TOP_K):
            _row_copy(h_hbm, xr_hbm, sem, tok, dest_ref[TOP_K * tok + k]).start()
        return carry

    lax.fori_loop(0, TN_DISPATCH, issue, 0)

    def wait(t, carry):
        for k in range(TOP_K):
            _row_copy(h_hbm, xr_hbm, sem, 0, 0).wait()
        return carry

    lax.fori_loop(0, TN_DISPATCH, wait, 0)


def _dispatch(dest, h2, n_rows):
    t, d = h2.shape
    return pl.pallas_call(
        _dispatch_kernel,
        grid_spec=pltpu.PrefetchScalarGridSpec(
            num_scalar_prefetch=1,
            grid=(t // TN_DISPATCH,),
            in_specs=[pl.BlockSpec(memory_space=pl.ANY)],
            out_specs=pl.BlockSpec(memory_space=pl.ANY),
            scratch_shapes=[pltpu.SemaphoreType.DMA(())]),
        out_shape=jax.ShapeDtypeStruct((n_rows, d), h2.dtype),
        compiler_params=pltpu.CompilerParams(dimension_semantics=("arbitrary",)),
        name="dispatch",
    )(dest, h2)


def _expert_kernel(be_ref, valid_ref, xr_ref, wg_ref, wu_ref, wd_ref, yr_ref):
    i = pl.program_id(0)
    valid = valid_ref[i]

    @pl.when(valid > 0)
    def _():
        row = lax.broadcasted_iota(jnp.int32, (BM, 1), 0)
        xb = jnp.where(row < valid, xr_ref[...], 0.0).astype(jnp.bfloat16)
        hg = _dot(xb, wg_ref[...])
        hu = _dot(xb, wu_ref[...])
        hid = (hg * _sigmoid(hg) * hu).astype(jnp.bfloat16)
        yr_ref[...] = _dot(hid, wd_ref[...])


def _experts(block_e, block_valid, last_blk, xr, w_gate, w_up, w_down):
    n_rows, d = xr.shape
    nb = n_rows // BM
    def row_map(i, be, bv, lb):
        return (jnp.minimum(i, lb[0]), 0)

    def w_map(i, be, bv, lb):
        return (be[jnp.minimum(i, lb[0])], 0, 0)

    return pl.pallas_call(
        _expert_kernel_lb,
        grid_spec=pltpu.PrefetchScalarGridSpec(
            num_scalar_prefetch=3,
            grid=(nb,),
            in_specs=[pl.BlockSpec((BM, d), row_map),
                      pl.BlockSpec((None, d, D_EXPERT), w_map),
                      pl.BlockSpec((None, d, D_EXPERT), w_map),
                      pl.BlockSpec((None, D_EXPERT, d), w_map)],
            out_specs=pl.BlockSpec((BM, d), row_map)),
        out_shape=jax.ShapeDtypeStruct((n_rows, d), jnp.float32),
        compiler_params=pltpu.CompilerParams(dimension_semantics=("arbitrary",),
                                             vmem_limit_bytes=VMEM_LIMIT),
        name="experts",
    )(block_e, block_valid, last_blk, xr, w_gate, w_up, w_down)


def _expert_kernel_lb(be_ref, valid_ref, lb_ref, xr_ref, wg_ref, wu_ref, wd_ref, yr_ref):
    del lb_ref
    _expert_kernel(be_ref, valid_ref, xr_ref, wg_ref, wu_ref, wd_ref, yr_ref)


def _combine_kernel(dest_ref, x1_ref, rw_ref, mod_ref, fg_ref, yr_hbm, out_ref, buf, sem, *, final_norm):
    base = pl.program_id(0) * TN_COMBINE

    def row_gather(t, k, src_row):
        return pltpu.make_async_copy(yr_hbm.at[pl.ds(src_row, 1)], buf.at[k, pl.ds(t, 1)], sem)

    def issue(t, carry):
        for k in range(TOP_K):
            row_gather(t, k, dest_ref[TOP_K * (base + t) + k]).start()
        return carry

    lax.fori_loop(0, TN_COMBINE, issue, 0)

    def wait(t, carry):
        for k in range(TOP_K):
            row_gather(t, k, 0).wait()
        return carry

    lax.fori_loop(0, TN_COMBINE, wait, 0)

    rw = rw_ref[...]
    moe = rw[:, 0:1] * buf[0] + rw[:, 1:2] * buf[1]
    x2 = x1_ref[...] + mod_ref[5:6, :] * moe
    if final_norm:
        x2 = x2 * lax.rsqrt(jnp.mean(x2 * x2, axis=-1, keepdims=True) + EPS) * fg_ref[...]
    out_ref[...] = x2


def _combine(dest, x1, rw, mod, final_g, yr, seq, final_norm):
    t, d = x1.shape
    per_seq = seq // TN_COMBINE
    return pl.pallas_call(
        functools.partial(_combine_kernel, final_norm=final_norm),
        grid_spec=pltpu.PrefetchScalarGridSpec(
            num_scalar_prefetch=1,
            grid=(t // TN_COMBINE,),
            in_specs=[pl.BlockSpec((TN_COMBINE, d), lambda i, dst: (i, 0)),
                      pl.BlockSpec((TN_COMBINE, LANES), lambda i, dst: (i, 0)),
                      pl.BlockSpec((None, 6, d), lambda i, dst: (i // per_seq, 0, 0)),
                      pl.BlockSpec((1, d), lambda i, dst: (0, 0)),
                      pl.BlockSpec(memory_space=pl.ANY)],
            out_specs=pl.BlockSpec((TN_COMBINE, d), lambda i, dst: (i, 0)),
            scratch_shapes=[pltpu.VMEM((TOP_K, TN_COMBINE, d), jnp.float32),
                            pltpu.SemaphoreType.DMA(())]),
        out_shape=jax.ShapeDtypeStruct((t, d), jnp.float32),
        compiler_params=pltpu.CompilerParams(dimension_semantics=("arbitrary",),
                                             vmem_limit_bytes=VMEM_LIMIT),
        name="combine",
    )(dest, x1, rw, mod, final_g.reshape(1, d), yr)


def kernel(x, c, positions, rel_bias, w_ada, b_ada, norm1_g, w_in, sinks, gm_ln_g, gm_ln_b, gm_w_s, gm_b_s,
           p_a, p_b, w_o, norm2_g, w_router_g, b_router_g, w_router_e, b_router_e, w_gate, w_up, w_down,
           final_g):
    bsz, seq, d = x.shape
    t = bsz * seq
    bf16 = jnp.bfloat16
    depth = w_ada.shape[0]
    n_rows = t * TOP_K + N_EXPERTS * BM
    nb = n_rows // BM
    for l in range(depth):
        mod = _adaln_mod(c, w_ada[l], b_ada[l]).reshape(bsz, 6, d)
        pad = ROUTER_LANES - N_GROUPS - N_EXPERTS
        w_r = jnp.concatenate([w_router_g[l], w_router_e[l], jnp.zeros((d, pad), jnp.float32)], axis=1)
        b_r = jnp.concatenate([b_router_g[l], b_router_e[l], jnp.zeros((pad,), jnp.float32)]).reshape(1, -1)
        bsb = jnp.broadcast_to(gm_b_s[l][:, :, None], (GM_GROUPS, GM_CHUNK, LANES))
        x1, h2, ri, rw, cnt = _mixer(
            x, mod, positions, rel_bias, norm1_g[l].reshape(1, d), w_in[l].astype(bf16), sinks[l],
            gm_ln_g[l].reshape(1, -1), gm_ln_b[l].reshape(1, -1), gm_w_s[l], bsb,
            p_a[l].astype(bf16), p_b[l].astype(bf16), w_o[l].astype(bf16), norm2_g[l].reshape(1, d),
            w_r.astype(bf16), b_r)

        counts = cnt[0, EXPERT_LANE0:EXPERT_LANE0 + N_EXPERTS].astype(jnp.int32)
        padded = (counts + BM - 1) // BM * BM
        pends = jnp.cumsum(padded)
        pstarts = pends - padded
        ri = ri.reshape(t, LANES)
        dest = (pstarts[ri[:, 0:TOP_K]] + ri[:, TOP_K:2 * TOP_K]).reshape(-1)
        blk_row0 = jnp.arange(nb, dtype=jnp.int32) * BM
        block_e = jnp.minimum(jnp.searchsorted(pends, blk_row0, side='right'), N_EXPERTS - 1).astype(jnp.int32)
        block_valid = jnp.clip(pstarts[block_e] + counts[block_e] - blk_row0, 0, BM).astype(jnp.int32)
        last_blk = (jnp.maximum(pends[-1] // BM, 1) - 1).astype(jnp.int32).reshape(1)

        xr = _dispatch(dest, h2.reshape(t, d), n_rows)
        yr = _experts(block_e, block_valid, last_blk, xr, w_gate[l].astype(bf16), w_up[l].astype(bf16),
                      w_down[l].astype(bf16))
        x = _combine(dest, x1.reshape(t, d), rw.reshape(t, LANES), mod, final_g, yr, seq,
                     final_norm=(l == depth - 1)).reshape(bsz, seq, d)
    return x
```

```python
import functools
import math

import jax
import jax.numpy as jnp
from jax import lax
from jax.experimental import pallas as pl
from jax.experimental.pallas import tpu as pltpu

D_MODEL = 1024
N_HEADS = 8
N_KV_HEADS = 2
HEAD_DIM = 64
BLOCK = 128
ATTN_Q = N_HEADS * HEAD_DIM
ATTN_KV = N_KV_HEADS * HEAD_DIM
N_BUCKETS = 32
MAX_EXACT = N_BUCKETS // 2
MAX_DISTANCE = 128
GM_WIDTH = 512
GM_GROUPS = 4
GM_CHUNK = 128
N_GROUPS = 4
EXPERTS_PER_GROUP = 8
N_EXPERTS = N_GROUPS * EXPERTS_PER_GROUP
TOP_K = 2
D_EXPERT = 512
EPS = 1e-6
NEG = -1e30

LANES = 128
ROUTER_LANES = LANES
EXPERT_LANE0 = N_GROUPS
TM = 512
BM = 256
TN_COMBINE = 256
ROUTE_ROWS = 8
GATHER_UNROLL = 8
VMEM_LIMIT = 56 * 1024 * 1024

Q0, K0, V0, GU0, GV0, GA0, GB0, IN_END = (0, 512, 640, 768, 1280, 1792, 2816, 3840)


def _dot(a, b):
    return jnp.dot(a, b, preferred_element_type=jnp.float32)


def _dot_nt(a, b):
    return lax.dot_general(a, b, (((1,), (1,)), ((), ())), preferred_element_type=jnp.float32)


def _gelu_tanh(x):
    return 0.5 * x * (1.0 + jnp.tanh(math.sqrt(2.0 / math.pi) * (x + 0.044715 * (x * x * x))))


def _sigmoid(x):
    return 1.0 / (1.0 + jnp.exp(-x))


def _adaln_kernel(c_ref, w_ref, b_ref, o_ref):
    c = c_ref[...]
    cs = c * _sigmoid(c)
    o_ref[...] = _dot(cs, w_ref[...]) + b_ref[...]


def _adaln_mod(c, w, b):
    bsz, d = c.shape
    n = w.shape[1]
    tn = 1024
    return pl.pallas_call(
        _adaln_kernel,
        grid=(n // tn,),
        in_specs=[pl.BlockSpec((bsz, d), lambda i: (0, 0)),
                  pl.BlockSpec((d, tn), lambda i: (0, i)),
                  pl.BlockSpec((1, tn), lambda i: (0, i))],
        out_specs=pl.BlockSpec((bsz, tn), lambda i: (0, i)),
        out_shape=jax.ShapeDtypeStruct((bsz, n), jnp.float32),
        compiler_params=pltpu.CompilerParams(dimension_semantics=("arbitrary",)),
        name="adaln_mod",
    )(c, w, b.reshape(1, n))


def _mixer_kernel(relb_ref, sinks_ref,
                  x_ref, mod_ref, pos_ref, n1g_ref, win_ref, lng_ref, lnb_ref, ws_ref, bsb_ref,
                  pa_ref, pb_ref, wo_ref, n2g_ref, wr_ref, br_ref,
                  x1_ref, h2_ref, ri_ref, rw_ref, cnt_ref,
                  kbuf, vbuf, pbuf, biasm, key, carry, ya, yb, flag):
    b = pl.program_id(0)
    j = pl.program_id(1)
    nblk = TM // BLOCK
    bf16 = jnp.bfloat16

    @pl.when((b == 0) & (j == 0))
    def _():
        carry[...] = jnp.zeros_like(carry)
        key[...] = jnp.zeros_like(key)
        flag[0] = 1

    @pl.when(j == 0)
    def _():
        kbuf[0:BLOCK, :] = jnp.zeros((BLOCK, LANES), jnp.float32)
        vbuf[0:BLOCK, :] = jnp.zeros((BLOCK, LANES), jnp.float32)
        pbuf[:, 0:BLOCK] = jnp.zeros((1, BLOCK), jnp.int32)

    x = x_ref[...]
    sh1, sc1, g1 = mod_ref[0:1, :], mod_ref[1:2, :], mod_ref[2:3, :]
    sh2, sc2, g2 = mod_ref[3:4, :], mod_ref[4:5, :], mod_ref[5:6, :]

    xn = x * lax.rsqrt(jnp.mean(x * x, axis=-1, keepdims=True) + EPS)
    h = (xn * (n1g_ref[...] * (1.0 + sc1)) + sh1).astype(bf16)

    lane = lax.broadcasted_iota(jnp.int32, (1, LANES), 1)
    lo = lane < HEAD_DIM
    q = _dot(h, win_ref[:, Q0:K0]) * (HEAD_DIM ** -0.5)
    lo4 = jnp.concatenate([lo] * (ATTN_Q // LANES), axis=1)
    q_lo = jnp.where(lo4, q, 0.0).astype(bf16)
    q_hi = jnp.where(lo4, 0.0, q).astype(bf16)
    kv = _dot(h, win_ref[:, K0:GU0])
    kbuf[BLOCK:, :] = kv[:, 0:LANES]
    vbuf[BLOCK:, :] = kv[:, LANES:]
    pbuf[:, BLOCK:] = pos_ref[...]
    kf = kbuf[...]
    vf = vbuf[...]
    kr = pltpu.roll(kf, HEAD_DIM, 1)
    vr = pltpu.roll(vf, HEAD_DIM, 1)
    kd = (jnp.where(lo, kf, kr).astype(bf16), jnp.where(lo, kr, kf).astype(bf16))
    v_lo = (jnp.where(lo, vf, 0.0).astype(bf16), jnp.where(lo, vr, 0.0).astype(bf16))
    v_hi = (jnp.where(lo, 0.0, vr).astype(bf16), jnp.where(lo, 0.0, vf).astype(bf16))

    qi = lax.broadcasted_iota(jnp.int32, (BLOCK, 2 * BLOCK), 0)
    kj = lax.broadcasted_iota(jnp.int32, (BLOCK, 2 * BLOCK), 1)
    local = (kj > qi) & (kj <= qi + BLOCK)

    for i in range(nblk):
        pk = pbuf[:, i * BLOCK:(i + 2) * BLOCK]
        pq = pbuf[:, (i + 1) * BLOCK:(i + 2) * BLOCK]
        rel = pk - pq[:, 0:1]
        is_first = (j == 0) if i == 0 else None
        mismatch = jnp.max((rel != key[...]).astype(jnp.int32))
        if i == 0:
            stale = (mismatch != 0) | is_first | (flag[0] != 0)
        else:
            stale = (mismatch != 0) | (flag[0] != 0)

        @pl.when(stale)
        def _():
            pq_col = jnp.transpose(jnp.broadcast_to(pq, (BLOCK, BLOCK)))
            dist = jnp.concatenate([pq_col, pq_col], axis=1) - pk
            n = jnp.maximum(dist, 0)
            nf = jnp.maximum(n, 1).astype(jnp.float32)
            large = MAX_EXACT + (jnp.log(nf / MAX_EXACT) / math.log(MAX_DISTANCE / MAX_EXACT)
                                 * (N_BUCKETS - MAX_EXACT)).astype(jnp.int32)
            large = jnp.minimum(large, N_BUCKETS - 1)
            bucket = jnp.where(n < MAX_EXACT, n, large)
            if i == 0:
                allowed = local & (kj >= jnp.where(is_first, BLOCK, 0))
            else:
                allowed = local
            acc = [jnp.zeros((BLOCK, 2 * BLOCK), jnp.float32) for _ in range(N_HEADS)]
            for bk in range(N_BUCKETS):
                eq = bucket == bk
                for hd in range(N_HEADS):
                    acc[hd] = jnp.where(eq, relb_ref[bk * N_HEADS + hd], acc[hd])
            for hd in range(N_HEADS):
                biasm[hd] = jnp.where(allowed, acc[hd], NEG)
            key[...] = rel
            if i == 0:
                flag[0] = is_first.astype(jnp.int32)
            else:
                flag[0] = 0

        rows = slice(i * BLOCK, (i + 1) * BLOCK)
        band = slice(i * BLOCK, (i + 2) * BLOCK)
        for pr in range(N_HEADS // 2):
            kvh = (2 * pr) // (N_HEADS // N_KV_HEADS)
            cols = slice(pr * LANES, (pr + 1) * LANES)
            kb = kd[kvh][band]
            o = None
            for half, (qq, vv) in enumerate(((q_lo, v_lo), (q_hi, v_hi))):
                hd = 2 * pr + half
                s = _dot_nt(qq[rows, cols], kb) + biasm[hd]
                sink = sinks_ref[hd]
                m = jnp.maximum(jnp.max(s, axis=-1, keepdims=True), sink)
                p = jnp.exp(s - m)
                den = jnp.sum(p, axis=-1, keepdims=True) + jnp.exp(sink - m)
                oh = _dot(p.astype(bf16), vv[kvh][band]) * (1.0 / den)
                o = oh if o is None else o + oh
            ya[rows, cols] = o.astype(bf16)

    kbuf[0:BLOCK, :] = kv[TM - BLOCK:, 0:LANES]
    vbuf[0:BLOCK, :] = kv[TM - BLOCK:, LANES:]
    pbuf[:, 0:BLOCK] = pos_ref[:, TM - BLOCK:]

    u = _gelu_tanh(_dot(h, win_ref[:, GU0:GV0]))
    vg = _gelu_tanh(_dot(h, win_ref[:, GV0:GA0]))
    mu = jnp.mean(vg, axis=-1, keepdims=True)
    vc = vg - mu
    var = jnp.mean(vc * vc, axis=-1, keepdims=True)
    vn = (vc * lax.rsqrt(var + EPS) * lng_ref[...] + lnb_ref[...]).astype(bf16)
    ti = lax.broadcasted_iota(jnp.int32, (GM_CHUNK, GM_CHUNK), 0)
    si = lax.broadcasted_iota(jnp.int32, (GM_CHUNK, GM_CHUNK), 1)
    tril = si <= ti
    for g in range(GM_GROUPS):
        wg = jnp.where(tril, ws_ref[g], 0.0).astype(bf16)
        cols = slice(g * LANES, (g + 1) * LANES)
        for cidx in range(TM // GM_CHUNK):
            rows = slice(cidx * GM_CHUNK, (cidx + 1) * GM_CHUNK)
            sv = _dot(wg, vn[rows, cols]) + bsb_ref[g]
            yb[rows, cols] = (u[rows, cols] * sv).astype(bf16)

    ga = _dot(h, win_ref[:, GA0:GB0])
    gb = _dot(h, win_ref[:, GB0:IN_END])
    merged = _sigmoid(ga) * _dot(ya[...], pa_ref[...]) + _sigmoid(gb) * _dot(yb[...], pb_ref[...])
    x1 = x + g1 * _dot(merged.astype(bf16), wo_ref[...])
    x1_ref[...] = x1

    xn2 = x1 * lax.rsqrt(jnp.mean(x1 * x1, axis=-1, keepdims=True) + EPS)
    h2 = xn2 * (n2g_ref[...] * (1.0 + sc2)) + sh2
    h2_ref[...] = h2
    logits = _dot(h2.astype(bf16), wr_ref[...]) + br_ref[...]
    lane_f = lane.astype(jnp.float32)
    big = float(2 * LANES)
    is_grp = lane < N_GROUPS
    lg = jnp.where(is_grp, logits, NEG)
    lg_max = jnp.max(lg, axis=-1, keepdims=True)
    g_idx = jnp.min(jnp.where(lg == lg_max, lane_f, big), axis=-1, keepdims=True)
    p_g = 1.0 / jnp.sum(jnp.where(is_grp, jnp.exp(lg - lg_max), 0.0), axis=-1, keepdims=True)
    lane_grp = jnp.floor((lane_f - EXPERT_LANE0) * (1.0 / EXPERTS_PER_GROUP))
    in_grp = (lane >= EXPERT_LANE0) & (lane < EXPERT_LANE0 + N_EXPERTS) & (lane_grp == g_idx)
    le = jnp.where(in_grp, logits, NEG)
    m1 = jnp.max(le, axis=-1, keepdims=True)
    i1 = jnp.min(jnp.where(le == m1, lane_f, big), axis=-1, keepdims=True)
    oh1 = lane_f == i1
    le2 = jnp.where(oh1, NEG, le)
    m2 = jnp.max(le2, axis=-1, keepdims=True)
    i2 = jnp.min(jnp.where(le2 == m2, lane_f, big), axis=-1, keepdims=True)
    oh2 = lane_f == i2
    e2 = jnp.exp(m2 - m1)
    w1 = p_g / (1.0 + e2)
    w2 = p_g * e2 / (1.0 + e2)

    oh = jnp.where(oh1 | oh2, 1.0, 0.0)
    tr = lax.broadcasted_iota(jnp.int32, (TM, TM), 0)
    tc = lax.broadcasted_iota(jnp.int32, (TM, TM), 1)
    strict = jnp.where(tc < tr, 1.0, 0.0).astype(bf16)
    before = _dot(strict, oh.astype(bf16)) + carry[...]
    r1 = jnp.sum(jnp.where(oh1, before, 0.0), axis=-1, keepdims=True)
    r2 = jnp.sum(jnp.where(oh2, before, 0.0), axis=-1, keepdims=True)
    carry[...] = carry[...] + jnp.sum(oh, axis=0, keepdims=True)
    cnt_ref[...] = carry[...]

    ri = jnp.where(lane == 0, i1 - EXPERT_LANE0,
                   jnp.where(lane == 1, i2 - EXPERT_LANE0,
                             jnp.where(lane == 2, r1, jnp.where(lane == 3, r2, 0.0))))
    ri_ref[...] = jnp.transpose(ri)[0:ROUTE_ROWS, :]
    rw_ref[...] = jnp.where(lane == 0, w1, jnp.where(lane == 1, w2, 0.0))


def _mixer(x, mod, positions, rel_bias, n1g, w_in, sinks, lng, lnb, w_s, bsb, p_a, p_b, w_o, n2g, w_r, b_r):
    bsz, seq, d = x.shape
    nj = seq // TM
    const = lambda *shape: pl.BlockSpec(shape, lambda b, j: (0,) * len(shape), pipeline_mode=pl.Buffered(1))
    smem = pl.BlockSpec(memory_space=pltpu.SMEM)
    tile = lambda w: pl.BlockSpec((None, TM, w), lambda b, j: (b, j, 0))
    return pl.pallas_call(
        _mixer_kernel,
        grid=(bsz, nj),
        in_specs=[smem, smem,
                  tile(d),
                  pl.BlockSpec((None, 6, d), lambda b, j: (b, 0, 0)),
                  pl.BlockSpec((None, None, 1, TM), lambda b, j: (b, j, 0, 0)),
                  const(1, d), const(d, IN_END), const(1, GM_WIDTH), const(1, GM_WIDTH),
                  const(GM_GROUPS, GM_CHUNK, GM_CHUNK), const(GM_GROUPS, GM_CHUNK, LANES),
                  const(ATTN_Q, d), const(GM_WIDTH, d), const(d, d), const(1, d),
                  const(d, ROUTER_LANES), const(1, ROUTER_LANES)],
        out_specs=[tile(d), tile(d),
                   pl.BlockSpec((ROUTE_ROWS, TM), lambda b, j: (0, b * nj + j)),
                   tile(LANES),
                   pl.BlockSpec((1, LANES), lambda b, j: (0, 0))],
        out_shape=[jax.ShapeDtypeStruct((bsz, seq, d), jnp.float32),
                   jax.ShapeDtypeStruct((bsz, seq, d), jnp.float32),
                   jax.ShapeDtypeStruct((ROUTE_ROWS, bsz * seq), jnp.float32),
                   jax.ShapeDtypeStruct((bsz, seq, LANES), jnp.float32),
                   jax.ShapeDtypeStruct((1, LANES), jnp.float32)],
        scratch_shapes=[pltpu.VMEM((TM + BLOCK, LANES), jnp.float32),
                        pltpu.VMEM((TM + BLOCK, LANES), jnp.float32),
                        pltpu.VMEM((1, TM + BLOCK), jnp.int32),
                        pltpu.VMEM((N_HEADS, BLOCK, 2 * BLOCK), jnp.float32),
                        pltpu.VMEM((1, 2 * BLOCK), jnp.int32),
                        pltpu.VMEM((1, LANES), jnp.float32),
                        pltpu.VMEM((TM, ATTN_Q), jnp.bfloat16),
                        pltpu.VMEM((TM, GM_WIDTH), jnp.bfloat16),
                        pltpu.SMEM((1,), jnp.int32)],
        compiler_params=pltpu.CompilerParams(dimension_semantics=("arbitrary", "arbitrary"),
                                             vmem_limit_bytes=VMEM_LIMIT),
        name="mixer",
    )(rel_bias.reshape(-1), sinks, x, mod, positions.reshape(bsz, nj, 1, TM), n1g, w_in, lng, lnb,
      w_s, bsb, p_a, p_b, w_o, n2g, w_r, b_r)


def _gather_rows(idx_ref, idx0, n, src_hbm, dst, sem):
    def issue(r, carry):
        pltpu.make_async_copy(src_hbm.at[pl.ds(idx_ref[idx0 + r], 1)], dst.at[pl.ds(r, 1)], sem).start()
        return carry

    lax.fori_loop(0, n, issue, 0, unroll=GATHER_UNROLL)


def _wait_rows(n, src_hbm, dst, sem):
    pltpu.make_async_copy(src_hbm.at[pl.ds(0, n)], dst, sem).wait()


def _expert_kernel(rt_ref, be_ref, lb_ref, h_hbm, wg_ref, wu_ref, wd_ref, yr_ref, xbuf, wgb, wub, wdb, sem):
    i = pl.program_id(0)
    last = lb_ref[0]
    slot = i % 2

    @pl.when(i == 0)
    def _():
        _gather_rows(rt_ref, 0, BM, h_hbm, xbuf.at[0], sem.at[0])

    @pl.when(i < last)
    def _():
        _gather_rows(rt_ref, (i + 1) * BM, BM, h_hbm, xbuf.at[1 - slot], sem.at[1 - slot])

    @pl.when(i <= last)
    def _():
        changed = (i == 0) | (be_ref[i] != be_ref[jnp.maximum(i - 1, 0)])

        @pl.when(changed)
        def _():
            wgb[...] = wg_ref[...].astype(jnp.bfloat16)
            wub[...] = wu_ref[...].astype(jnp.bfloat16)
            wdb[...] = wd_ref[...].astype(jnp.bfloat16)

        _wait_rows(BM, h_hbm, xbuf.at[slot], sem.at[slot])
        xb = xbuf[slot].astype(jnp.bfloat16)
        hg = _dot(xb, wgb[...])
        hu = _dot(xb, wub[...])
        hid = (hg * _sigmoid(hg) * hu).astype(jnp.bfloat16)
        yr_ref[...] = _dot(hid, wdb[...])

    @pl.when(i > last)
    def _():
        yr_ref[...] = jnp.zeros_like(yr_ref)


def _experts(row_tok, block_e, last_blk, h2, w_gate, w_up, w_down):
    d = h2.shape[1]
    n_rows = row_tok.shape[0]

    def row_map(i, rt, be, lb):
        return (i, 0)

    def w_map(i, rt, be, lb):
        return (be[jnp.minimum(i, lb[0])], 0, 0)

    return pl.pallas_call(
        _expert_kernel,
        grid_spec=pltpu.PrefetchScalarGridSpec(
            num_scalar_prefetch=3,
            grid=(n_rows // BM,),
            in_specs=[pl.BlockSpec(memory_space=pl.ANY),
                      pl.BlockSpec((None, d, D_EXPERT), w_map),
                      pl.BlockSpec((None, d, D_EXPERT), w_map),
                      pl.BlockSpec((None, D_EXPERT, d), w_map)],
            out_specs=pl.BlockSpec((BM, d), row_map),
            scratch_shapes=[pltpu.VMEM((2, BM, d), jnp.float32),
                            pltpu.VMEM((d, D_EXPERT), jnp.bfloat16),
                            pltpu.VMEM((d, D_EXPERT), jnp.bfloat16),
                            pltpu.VMEM((D_EXPERT, d), jnp.bfloat16),
                            pltpu.SemaphoreType.DMA((2,))]),
        out_shape=jax.ShapeDtypeStruct((n_rows, d), jnp.float32),
        compiler_params=pltpu.CompilerParams(dimension_semantics=("arbitrary",),
                                             vmem_limit_bytes=VMEM_LIMIT),
        name="experts",
    )(row_tok, block_e, last_blk, h2, w_gate, w_up, w_down)


def _combine_kernel(dest_ref, x1_ref, rw_ref, mod_ref, fg_ref, yr_hbm, out_ref, buf, sem, *, final_norm):
    n_tok = pl.num_programs(0) * TN_COMBINE
    base = pl.program_id(0) * TN_COMBINE
    for k in range(TOP_K):
        _gather_rows(dest_ref, k * n_tok + base, TN_COMBINE, yr_hbm, buf.at[k], sem)
    for k in range(TOP_K):
        _wait_rows(TN_COMBINE, yr_hbm, buf.at[k], sem)

    rw = rw_ref[...]
    moe = rw[:, 0:1] * buf[0] + rw[:, 1:2] * buf[1]
    x2 = x1_ref[...] + mod_ref[5:6, :] * moe
    if final_norm:
        x2 = x2 * lax.rsqrt(jnp.mean(x2 * x2, axis=-1, keepdims=True) + EPS) * fg_ref[...]
    out_ref[...] = x2


def _combine(dest, x1, rw, mod, final_g, yr, seq, final_norm):
    t, d = x1.shape
    per_seq = seq // TN_COMBINE
    return pl.pallas_call(
        functools.partial(_combine_kernel, final_norm=final_norm),
        grid_spec=pltpu.PrefetchScalarGridSpec(
            num_scalar_prefetch=1,
            grid=(t // TN_COMBINE,),
            in_specs=[pl.BlockSpec((TN_COMBINE, d), lambda i, dst: (i, 0)),
                      pl.BlockSpec((TN_COMBINE, LANES), lambda i, dst: (i, 0)),
                      pl.BlockSpec((None, 6, d), lambda i, dst: (i // per_seq, 0, 0)),
                      pl.BlockSpec((1, d), lambda i, dst: (0, 0)),
                      pl.BlockSpec(memory_space=pl.ANY)],
            out_specs=pl.BlockSpec((TN_COMBINE, d), lambda i, dst: (i, 0)),
            scratch_shapes=[pltpu.VMEM((TOP_K, TN_COMBINE, d), jnp.float32),
                            pltpu.SemaphoreType.DMA(())]),
        out_shape=jax.ShapeDtypeStruct((t, d), jnp.float32),
        compiler_params=pltpu.CompilerParams(dimension_semantics=("arbitrary",),
                                             vmem_limit_bytes=VMEM_LIMIT),
        name="combine",
    )(dest, x1, rw, mod, final_g.reshape(1, d), yr)


def kernel(x, c, positions, rel_bias, w_ada, b_ada, norm1_g, w_in, sinks, gm_ln_g, gm_ln_b, gm_w_s, gm_b_s,
           p_a, p_b, w_o, norm2_g, w_router_g, b_router_g, w_router_e, b_router_e, w_gate, w_up, w_down,
           final_g):
    bsz, seq, d = x.shape
    t = bsz * seq
    bf16 = jnp.bfloat16
    depth = w_ada.shape[0]
    n_rows = t * TOP_K + N_EXPERTS * BM
    nb = n_rows // BM
    for l in range(depth):
        mod = _adaln_mod(c, w_ada[l], b_ada[l]).reshape(bsz, 6, d)
        pad = ROUTER_LANES - N_GROUPS - N_EXPERTS
        w_r = jnp.concatenate([w_router_g[l], w_router_e[l], jnp.zeros((d, pad), jnp.float32)], axis=1)
        b_r = jnp.concatenate([b_router_g[l], b_router_e[l], jnp.zeros((pad,), jnp.float32)]).reshape(1, -1)
        bsb = jnp.broadcast_to(gm_b_s[l][:, :, None], (GM_GROUPS, GM_CHUNK, LANES))
        x1, h2, ri, rw, cnt = _mixer(
            x, mod, positions, rel_bias, norm1_g[l].reshape(1, d), w_in[l].astype(bf16), sinks[l],
            gm_ln_g[l].reshape(1, -1), gm_ln_b[l].reshape(1, -1), gm_w_s[l], bsb,
            p_a[l].astype(bf16), p_b[l].astype(bf16), w_o[l].astype(bf16), norm2_g[l].reshape(1, d),
            w_r.astype(bf16), b_r)

        counts = cnt[0, EXPERT_LANE0:EXPERT_LANE0 + N_EXPERTS].astype(jnp.int32)
        padded = (counts + BM - 1) // BM * BM
        pends = jnp.cumsum(padded)
        pstarts = pends - padded
        eid = ri[0:TOP_K].astype(jnp.int32)
        rank = ri[TOP_K:2 * TOP_K].astype(jnp.int32)
        onehot = eid[:, :, None] == jnp.arange(N_EXPERTS, dtype=jnp.int32)
        dest = (jnp.sum(jnp.where(onehot, pstarts, 0), axis=-1) + rank).reshape(-1)
        tok = jnp.tile(jnp.arange(t, dtype=jnp.int32), TOP_K)
        row_tok = jnp.zeros((n_rows,), jnp.int32).at[dest].set(tok, unique_indices=True)
        blk_row0 = jnp.arange(nb, dtype=jnp.int32) * BM
        block_e = jnp.minimum(jnp.sum(blk_row0[:, None] >= pends[None, :], axis=1), N_EXPERTS - 1).astype(jnp.int32)
        last_blk = (jnp.maximum(pends[-1] // BM, 1) - 1).astype(jnp.int32).reshape(1)

        yr = _experts(row_tok, block_e, last_blk, h2.reshape(t, d), w_gate[l], w_up[l], w_down[l])
        x = _combine(dest, x1.reshape(t, d), rw.reshape(t, LANES), mod, final_g, yr, seq,
                     final_norm=(l == depth - 1)).reshape(bsz, seq, d)
    return x
```

```python
import functools
import math

import jax
import jax.numpy as jnp
from jax import lax
from jax.experimental import pallas as pl
from jax.experimental.pallas import tpu as pltpu

D_MODEL = 1024
N_HEADS = 8
N_KV_HEADS = 2
HEAD_DIM = 64
BLOCK = 128
ATTN_Q = N_HEADS * HEAD_DIM
ATTN_KV = N_KV_HEADS * HEAD_DIM
N_BUCKETS = 32
MAX_EXACT = N_BUCKETS // 2
MAX_DISTANCE = 128
GM_WIDTH = 512
GM_GROUPS = 4
GM_CHUNK = 128
N_GROUPS = 4
EXPERTS_PER_GROUP = 8
N_EXPERTS = N_GROUPS * EXPERTS_PER_GROUP
TOP_K = 2
D_EXPERT = 512
EPS = 1e-6
NEG = -1e30

LANES = 128
ROUTER_LANES = LANES
EXPERT_LANE0 = N_GROUPS
TM = 512
BM = 256
CHUNK = 8
SORT_ROWS = TM * TOP_K + N_EXPERTS * CHUNK
SORT_CHUNKS = SORT_ROWS // CHUNK
BLOCK_CHUNKS = BM // CHUNK
GATHER_UNROLL = 8
VMEM_LIMIT = 56 * 1024 * 1024

Q0, K0, V0, GU0, GV0, GA0, GB0, IN_END = (0, 512, 640, 768, 1280, 1792, 2816, 3840)


def _dot(a, b):
    return jnp.dot(a, b, preferred_element_type=jnp.float32)


def _dot_nt(a, b):
    return lax.dot_general(a, b, (((1,), (1,)), ((), ())), preferred_element_type=jnp.float32)


def _gelu_tanh(x):
    return 0.5 * x * (1.0 + jnp.tanh(math.sqrt(2.0 / math.pi) * (x + 0.044715 * (x * x * x))))


def _sigmoid(x):
    return 1.0 / (1.0 + jnp.exp(-x))


def _adaln_kernel(c_ref, w_ref, b_ref, o_ref):
    c = c_ref[...]
    cs = c * _sigmoid(c)
    o_ref[...] = _dot(cs, w_ref[...]) + b_ref[...]


def _adaln_mod(c, w, b):
    bsz, d = c.shape
    n = w.shape[1]
    tn = 1024
    return pl.pallas_call(
        _adaln_kernel,
        grid=(n // tn,),
        in_specs=[pl.BlockSpec((bsz, d), lambda i: (0, 0)),
                  pl.BlockSpec((d, tn), lambda i: (0, i)),
                  pl.BlockSpec((1, tn), lambda i: (0, i))],
        out_specs=pl.BlockSpec((bsz, tn), lambda i: (0, i)),
        out_shape=jax.ShapeDtypeStruct((bsz, n), jnp.float32),
        compiler_params=pltpu.CompilerParams(dimension_semantics=("arbitrary",)),
        name="adaln_mod",
    )(c, w, b.reshape(1, n))


def _mixer_kernel(relb_ref, sinks_ref,
                  x_ref, mod_ref, pos_ref, n1g_ref, win_ref, lng_ref, lnb_ref, ws_ref, bsb_ref,
                  pa_ref, pb_ref, wo_ref, n2g_ref, wr_ref, br_ref,
                  x1_ref, xs_ref, rw_ref, cnt_ref,
                  kbuf, vbuf, pbuf, biasm, key, ya, yb, flag):
    b = pl.program_id(0)
    j = pl.program_id(1)
    nblk = TM // BLOCK
    bf16 = jnp.bfloat16

    @pl.when((b == 0) & (j == 0))
    def _():
        key[...] = jnp.zeros_like(key)
        flag[0] = 1

    @pl.when(j == 0)
    def _():
        kbuf[0:BLOCK, :] = jnp.zeros((BLOCK, LANES), jnp.float32)
        vbuf[0:BLOCK, :] = jnp.zeros((BLOCK, LANES), jnp.float32)
        pbuf[:, 0:BLOCK] = jnp.zeros((1, BLOCK), jnp.int32)

    x = x_ref[...]
    sh1, sc1, g1 = mod_ref[0:1, :], mod_ref[1:2, :], mod_ref[2:3, :]
    sh2, sc2, g2 = mod_ref[3:4, :], mod_ref[4:5, :], mod_ref[5:6, :]

    xn = x * lax.rsqrt(jnp.mean(x * x, axis=-1, keepdims=True) + EPS)
    h = (xn * (n1g_ref[...] * (1.0 + sc1)) + sh1).astype(bf16)

    lane = lax.broadcasted_iota(jnp.int32, (1, LANES), 1)
    lo = lane < HEAD_DIM
    q = _dot(h, win_ref[:, Q0:K0]) * (HEAD_DIM ** -0.5)
    lo4 = jnp.concatenate([lo] * (ATTN_Q // LANES), axis=1)
    q_lo = jnp.where(lo4, q, 0.0).astype(bf16)
    q_hi = jnp.where(lo4, 0.0, q).astype(bf16)
    kv = _dot(h, win_ref[:, K0:GU0])
    kbuf[BLOCK:, :] = kv[:, 0:LANES]
    vbuf[BLOCK:, :] = kv[:, LANES:]
    pbuf[:, BLOCK:] = pos_ref[...]
    kf = kbuf[...]
    vf = vbuf[...]
    kr = pltpu.roll(kf, HEAD_DIM, 1)
    vr = pltpu.roll(vf, HEAD_DIM, 1)
    kd = (jnp.where(lo, kf, kr).astype(bf16), jnp.where(lo, kr, kf).astype(bf16))
    v_lo = (jnp.where(lo, vf, 0.0).astype(bf16), jnp.where(lo, vr, 0.0).astype(bf16))
    v_hi = (jnp.where(lo, 0.0, vr).astype(bf16), jnp.where(lo, 0.0, vf).astype(bf16))

    qi = lax.broadcasted_iota(jnp.int32, (BLOCK, 2 * BLOCK), 0)
    kj = lax.broadcasted_iota(jnp.int32, (BLOCK, 2 * BLOCK), 1)
    local = (kj > qi) & (kj <= qi + BLOCK)

    for i in range(nblk):
        pk = pbuf[:, i * BLOCK:(i + 2) * BLOCK]
        pq = pbuf[:, (i + 1) * BLOCK:(i + 2) * BLOCK]
        rel = pk - pq[:, 0:1]
        is_first = (j == 0) if i == 0 else None
        mismatch = jnp.max((rel != key[...]).astype(jnp.int32))
        if i == 0:
            stale = (mismatch != 0) | is_first | (flag[0] != 0)
        else:
            stale = (mismatch != 0) | (flag[0] != 0)

        @pl.when(stale)
        def _():
            pq_col = jnp.transpose(jnp.broadcast_to(pq, (BLOCK, BLOCK)))
            dist = jnp.concatenate([pq_col, pq_col], axis=1) - pk
            n = jnp.maximum(dist, 0)
            nf = jnp.maximum(n, 1).astype(jnp.float32)
            large = MAX_EXACT + (jnp.log(nf / MAX_EXACT) / math.log(MAX_DISTANCE / MAX_EXACT)
                                 * (N_BUCKETS - MAX_EXACT)).astype(jnp.int32)
            large = jnp.minimum(large, N_BUCKETS - 1)
            bucket = jnp.where(n < MAX_EXACT, n, large)
            if i == 0:
                allowed = local & (kj >= jnp.where(is_first, BLOCK, 0))
            else:
                allowed = local
            acc = [jnp.zeros((BLOCK, 2 * BLOCK), jnp.float32) for _ in range(N_HEADS)]
            for bk in range(N_BUCKETS):
                eq = bucket == bk
                for hd in range(N_HEADS):
                    acc[hd] = jnp.where(eq, relb_ref[bk * N_HEADS + hd], acc[hd])
            for hd in range(N_HEADS):
                biasm[hd] = jnp.where(allowed, acc[hd], NEG)
            key[...] = rel
            if i == 0:
                flag[0] = is_first.astype(jnp.int32)
            else:
                flag[0] = 0

        rows = slice(i * BLOCK, (i + 1) * BLOCK)
        band = slice(i * BLOCK, (i + 2) * BLOCK)
        for pr in range(N_HEADS // 2):
            kvh = (2 * pr) // (N_HEADS // N_KV_HEADS)
            cols = slice(pr * LANES, (pr + 1) * LANES)
            kb = kd[kvh][band]
            o = None
            for half, (qq, vv) in enumerate(((q_lo, v_lo), (q_hi, v_hi))):
                hd = 2 * pr + half
                s = _dot_nt(qq[rows, cols], kb) + biasm[hd]
                sink = sinks_ref[hd]
                m = jnp.maximum(jnp.max(s, axis=-1, keepdims=True), sink)
                p = jnp.exp(s - m)
                den = jnp.sum(p, axis=-1, keepdims=True) + jnp.exp(sink - m)
                oh = _dot(p.astype(bf16), vv[kvh][band]) * (1.0 / den)
                o = oh if o is None else o + oh
            ya[rows, cols] = o.astype(bf16)

    kbuf[0:BLOCK, :] = kv[TM - BLOCK:, 0:LANES]
    vbuf[0:BLOCK, :] = kv[TM - BLOCK:, LANES:]
    pbuf[:, 0:BLOCK] = pos_ref[:, TM - BLOCK:]

    u = _gelu_tanh(_dot(h, win_ref[:, GU0:GV0]))
    vg = _gelu_tanh(_dot(h, win_ref[:, GV0:GA0]))
    mu = jnp.mean(vg, axis=-1, keepdims=True)
    vc = vg - mu
    var = jnp.mean(vc * vc, axis=-1, keepdims=True)
    vn = (vc * lax.rsqrt(var + EPS) * lng_ref[...] + lnb_ref[...]).astype(bf16)
    ti = lax.broadcasted_iota(jnp.int32, (GM_CHUNK, GM_CHUNK), 0)
    si = lax.broadcasted_iota(jnp.int32, (GM_CHUNK, GM_CHUNK), 1)
    tril = si <= ti
    for g in range(GM_GROUPS):
        wg = jnp.where(tril, ws_ref[g], 0.0).astype(bf16)
        cols = slice(g * LANES, (g + 1) * LANES)
        for cidx in range(TM // GM_CHUNK):
            rows = slice(cidx * GM_CHUNK, (cidx + 1) * GM_CHUNK)
            sv = _dot(wg, vn[rows, cols]) + bsb_ref[g]
            yb[rows, cols] = (u[rows, cols] * sv).astype(bf16)

    ga = _dot(h, win_ref[:, GA0:GB0])
    gb = _dot(h, win_ref[:, GB0:IN_END])
    merged = _sigmoid(ga) * _dot(ya[...], pa_ref[...]) + _sigmoid(gb) * _dot(yb[...], pb_ref[...])
    x1 = x + g1 * _dot(merged.astype(bf16), wo_ref[...])
    x1_ref[...] = x1

    xn2 = x1 * lax.rsqrt(jnp.mean(x1 * x1, axis=-1, keepdims=True) + EPS)
    h2 = (xn2 * (n2g_ref[...] * (1.0 + sc2)) + sh2).astype(bf16)
    logits = _dot(h2, wr_ref[...]) + br_ref[...]
    lane_f = lane.astype(jnp.float32)
    big = float(2 * LANES)
    is_grp = lane < N_GROUPS
    lg = jnp.where(is_grp, logits, NEG)
    lg_max = jnp.max(lg, axis=-1, keepdims=True)
    g_idx = jnp.min(jnp.where(lg == lg_max, lane_f, big), axis=-1, keepdims=True)
    p_g = 1.0 / jnp.sum(jnp.where(is_grp, jnp.exp(lg - lg_max), 0.0), axis=-1, keepdims=True)
    lane_grp = jnp.floor((lane_f - EXPERT_LANE0) * (1.0 / EXPERTS_PER_GROUP))
    in_grp = (lane >= EXPERT_LANE0) & (lane < EXPERT_LANE0 + N_EXPERTS) & (lane_grp == g_idx)
    le = jnp.where(in_grp, logits, NEG)
    m1 = jnp.max(le, axis=-1, keepdims=True)
    i1 = jnp.min(jnp.where(le == m1, lane_f, big), axis=-1, keepdims=True)
    oh1 = lane_f == i1
    le2 = jnp.where(oh1, NEG, le)
    m2 = jnp.max(le2, axis=-1, keepdims=True)
    i2 = jnp.min(jnp.where(le2 == m2, lane_f, big), axis=-1, keepdims=True)
    oh2 = lane_f == i2
    e2 = jnp.exp(m2 - m1)
    w1 = p_g / (1.0 + e2)
    w2 = p_g * e2 / (1.0 + e2)

    oh = jnp.where(oh1, 1.0, jnp.where(oh2, 1.0, 0.0))
    n_e = jnp.sum(oh, axis=0, keepdims=True)
    cnt_ref[...] = n_e
    padded = jnp.floor((n_e + (CHUNK - 1)) * (1.0 / CHUNK)) * CHUNK
    src_lane = lax.broadcasted_iota(jnp.int32, (LANES, LANES), 0)
    dst_lane = lax.broadcasted_iota(jnp.int32, (LANES, LANES), 1)
    upper = jnp.where(src_lane < dst_lane, 1.0, 0.0).astype(bf16)
    run0 = _dot(jnp.broadcast_to(padded, (8, LANES)).astype(bf16), upper)[0:1, :]
    tr = lax.broadcasted_iota(jnp.int32, (TM, TM), 0)
    tc = lax.broadcasted_iota(jnp.int32, (TM, TM), 1)
    strict = jnp.where(tc < tr, 1.0, 0.0).astype(bf16)
    slot = _dot(strict, oh.astype(bf16)) + run0
    pos1 = jnp.sum(jnp.where(oh1, slot, 0.0), axis=-1, keepdims=True)
    pos2 = jnp.sum(jnp.where(oh2, slot, 0.0), axis=-1, keepdims=True)
    rw_ref[...] = jnp.where(lane == 0, w1, jnp.where(lane == 1, w2,
                            jnp.where(lane == 2, pos1, jnp.where(lane == 3, pos2, 0.0))))

    pos_rows = jnp.transpose(jnp.where(lane == 0, pos1, jnp.where(lane == 1, pos2, 0.0)))
    p1 = pos_rows[0:1, :].astype(jnp.int32)
    p2 = pos_rows[1:2, :].astype(jnp.int32)
    sp = lax.broadcasted_iota(jnp.int32, (SORT_ROWS, TM), 0)
    perm = jnp.where(sp == p1, 1.0, jnp.where(sp == p2, 1.0, 0.0)).astype(bf16)
    xs = lax.bitcast_convert_type(_dot(perm, h2), jnp.uint32)
    half = D_MODEL // 2
    xs_ref[...] = (xs[:, 0:half] >> 16) | (xs[:, half:] & jnp.uint32(0xFFFF0000))


def _mixer(x, mod, positions, rel_bias, n1g, w_in, sinks, lng, lnb, w_s, bsb, p_a, p_b, w_o, n2g, w_r, b_r):
    bsz, seq, d = x.shape
    nj = seq // TM
    const = lambda *shape: pl.BlockSpec(shape, lambda b, j: (0,) * len(shape), pipeline_mode=pl.Buffered(1))
    smem = pl.BlockSpec(memory_space=pltpu.SMEM)
    tile = lambda w: pl.BlockSpec((None, TM, w), lambda b, j: (b, j, 0))
    return pl.pallas_call(
        _mixer_kernel,
        grid=(bsz, nj),
        in_specs=[smem, smem,
                  tile(d),
                  pl.BlockSpec((None, 6, d), lambda b, j: (b, 0, 0)),
                  pl.BlockSpec((None, None, 1, TM), lambda b, j: (b, j, 0, 0)),
                  const(1, d), const(d, IN_END), const(1, GM_WIDTH), const(1, GM_WIDTH),
                  const(GM_GROUPS, GM_CHUNK, GM_CHUNK), const(GM_GROUPS, GM_CHUNK, LANES),
                  const(ATTN_Q, d), const(GM_WIDTH, d), const(d, d), const(1, d),
                  const(d, ROUTER_LANES), const(1, ROUTER_LANES)],
        out_specs=[tile(d),
                   pl.BlockSpec((SORT_ROWS, d // 2), lambda b, j: (b * nj + j, 0)),
                   tile(LANES),
                   pl.BlockSpec((None, 1, LANES), lambda b, j: (b * nj + j, 0, 0))],
        out_shape=[jax.ShapeDtypeStruct((bsz, seq, d), jnp.float32),
                   jax.ShapeDtypeStruct((bsz * nj * SORT_ROWS, d // 2), jnp.uint32),
                   jax.ShapeDtypeStruct((bsz, seq, LANES), jnp.float32),
                   jax.ShapeDtypeStruct((bsz * nj, 1, LANES), jnp.float32)],
        scratch_shapes=[pltpu.VMEM((TM + BLOCK, LANES), jnp.float32),
                        pltpu.VMEM((TM + BLOCK, LANES), jnp.float32),
                        pltpu.VMEM((1, TM + BLOCK), jnp.int32),
                        pltpu.VMEM((N_HEADS, BLOCK, 2 * BLOCK), jnp.float32),
                        pltpu.VMEM((1, 2 * BLOCK), jnp.int32),
                        pltpu.VMEM((TM, ATTN_Q), jnp.bfloat16),
                        pltpu.VMEM((TM, GM_WIDTH), jnp.bfloat16),
                        pltpu.SMEM((1,), jnp.int32)],
        compiler_params=pltpu.CompilerParams(dimension_semantics=("arbitrary", "arbitrary"),
                                             vmem_limit_bytes=VMEM_LIMIT),
        name="mixer",
    )(rel_bias.reshape(-1), sinks, x, mod, positions.reshape(bsz, nj, 1, TM), n1g, w_in, lng, lnb,
      w_s, bsb, p_a, p_b, w_o, n2g, w_r, b_r)


def _gather_chunks(idx_ref, idx0, n, src_hbm, dst, sem):
    def issue(c, carry):
        src = pl.multiple_of(idx_ref[idx0 + c] * CHUNK, CHUNK)
        pltpu.make_async_copy(src_hbm.at[pl.ds(src, CHUNK)],
                              dst.at[pl.ds(pl.multiple_of(c * CHUNK, CHUNK), CHUNK)], sem).start()
        return carry

    lax.fori_loop(0, n, issue, 0, unroll=GATHER_UNROLL)


def _wait_chunks(n, src_hbm, dst, sem):
    pltpu.make_async_copy(src_hbm.at[pl.ds(0, n * CHUNK)], dst, sem).wait()


def _expert_kernel(src_ref, be_ref, lb_ref, xs_hbm, wg_ref, wu_ref, wd_ref, yr_ref, xbuf, wgb, wub, wdb, sem):
    i = pl.program_id(0)
    last = lb_ref[0]
    slot = i % 2

    @pl.when(i == 0)
    def _():
        _gather_chunks(src_ref, 0, BLOCK_CHUNKS, xs_hbm, xbuf.at[0], sem.at[0])

    @pl.when(i < last)
    def _():
        _gather_chunks(src_ref, (i + 1) * BLOCK_CHUNKS, BLOCK_CHUNKS, xs_hbm, xbuf.at[1 - slot], sem.at[1 - slot])

    @pl.when(i <= last)
    def _():
        changed = (i == 0) | (be_ref[i] != be_ref[jnp.maximum(i - 1, 0)])

        @pl.when(changed)
        def _():
            wgb[...] = wg_ref[...].astype(jnp.bfloat16)
            wub[...] = wu_ref[...].astype(jnp.bfloat16)
            wdb[...] = wd_ref[...].astype(jnp.bfloat16)

        _wait_chunks(BLOCK_CHUNKS, xs_hbm, xbuf.at[slot], sem.at[slot])
        packed = xbuf[slot]
        x_lo = lax.bitcast_convert_type(packed << 16, jnp.float32)
        x_hi = lax.bitcast_convert_type(packed & jnp.uint32(0xFFFF0000), jnp.float32)
        xb = jnp.concatenate([x_lo, x_hi], axis=1).astype(jnp.bfloat16)
        hg = _dot(xb, wgb[...])
        hu = _dot(xb, wub[...])
        hid = (hg * _sigmoid(hg) * hu).astype(jnp.bfloat16)
        yr_ref[...] = _dot(hid, wdb[...])

    @pl.when(i > last)
    def _():
        yr_ref[...] = jnp.zeros_like(yr_ref)


def _experts(src_chunk, block_e, last_blk, xs, w_gate, w_up, w_down):
    d = w_gate.shape[1]
    n_rows = src_chunk.shape[0] * CHUNK

    def row_map(i, rt, be, lb):
        return (i, 0)

    def w_map(i, rt, be, lb):
        return (be[jnp.minimum(i, lb[0])], 0, 0)

    return pl.pallas_call(
        _expert_kernel,
        grid_spec=pltpu.PrefetchScalarGridSpec(
            num_scalar_prefetch=3,
            grid=(n_rows // BM,),
            in_specs=[pl.BlockSpec(memory_space=pl.ANY),
                      pl.BlockSpec((None, d, D_EXPERT), w_map),
                      pl.BlockSpec((None, d, D_EXPERT), w_map),
                      pl.BlockSpec((None, D_EXPERT, d), w_map)],
            out_specs=pl.BlockSpec((BM, d), row_map),
            scratch_shapes=[pltpu.VMEM((2, BM, d // 2), jnp.uint32),
                            pltpu.VMEM((d, D_EXPERT), jnp.bfloat16),
                            pltpu.VMEM((d, D_EXPERT), jnp.bfloat16),
                            pltpu.VMEM((D_EXPERT, d), jnp.bfloat16),
                            pltpu.SemaphoreType.DMA((2,))]),
        out_shape=jax.ShapeDtypeStruct((n_rows, d), jnp.float32),
        compiler_params=pltpu.CompilerParams(dimension_semantics=("arbitrary",),
                                             vmem_limit_bytes=VMEM_LIMIT),
        name="experts",
    )(src_chunk, block_e, last_blk, xs, w_gate, w_up, w_down)


def _combine_kernel(src_ref, x1_ref, rw_ref, mod_ref, fg_ref, yr_hbm, out_ref, ybuf, sem, *, final_norm):
    i = pl.program_id(0)
    slot = i % 2

    @pl.when(i == 0)
    def _():
        _gather_chunks(src_ref, 0, SORT_CHUNKS, yr_hbm, ybuf.at[0], sem.at[0])

    @pl.when(i + 1 < pl.num_programs(0))
    def _():
        _gather_chunks(src_ref, (i + 1) * SORT_CHUNKS, SORT_CHUNKS, yr_hbm, ybuf.at[1 - slot], sem.at[1 - slot])

    rw = rw_ref[...]
    sp = lax.broadcasted_iota(jnp.int32, (1, SORT_ROWS), 1)
    p1 = rw[:, 2:3].astype(jnp.int32)
    p2 = rw[:, 3:4].astype(jnp.int32)
    wmat = jnp.where(sp == p1, rw[:, 0:1], jnp.where(sp == p2, rw[:, 1:2], 0.0)).astype(jnp.bfloat16)
    _wait_chunks(SORT_CHUNKS, yr_hbm, ybuf.at[slot], sem.at[slot])
    moe = _dot(wmat, ybuf[slot].astype(jnp.bfloat16))
    x2 = x1_ref[...] + mod_ref[5:6, :] * moe
    if final_norm:
        x2 = x2 * lax.rsqrt(jnp.mean(x2 * x2, axis=-1, keepdims=True) + EPS) * fg_ref[...]
    out_ref[...] = x2


def _combine(src_chunk, x1, rw, mod, final_g, yr, seq, final_norm):
    t, d = x1.shape
    per_seq = seq // TM
    return pl.pallas_call(
        functools.partial(_combine_kernel, final_norm=final_norm),
        grid_spec=pltpu.PrefetchScalarGridSpec(
            num_scalar_prefetch=1,
            grid=(t // TM,),
            in_specs=[pl.BlockSpec((TM, d), lambda i, src: (i, 0)),
                      pl.BlockSpec((TM, LANES), lambda i, src: (i, 0)),
                      pl.BlockSpec((None, 6, d), lambda i, src: (i // per_seq, 0, 0)),
                      pl.BlockSpec((1, d), lambda i, src: (0, 0)),
                      pl.BlockSpec(memory_space=pl.ANY)],
            out_specs=pl.BlockSpec((TM, d), lambda i, src: (i, 0)),
            scratch_shapes=[pltpu.VMEM((2, SORT_ROWS, d), jnp.float32),
                            pltpu.SemaphoreType.DMA((2,))]),
        out_shape=jax.ShapeDtypeStruct((t, d), jnp.float32),
        compiler_params=pltpu.CompilerParams(dimension_semantics=("arbitrary",),
                                             vmem_limit_bytes=VMEM_LIMIT),
        name="combine",
    )(src_chunk, x1, rw, mod, final_g.reshape(1, d), yr)


def kernel(x, c, positions, rel_bias, w_ada, b_ada, norm1_g, w_in, sinks, gm_ln_g, gm_ln_b, gm_w_s, gm_b_s,
           p_a, p_b, w_o, norm2_g, w_router_g, b_router_g, w_router_e, b_router_e, w_gate, w_up, w_down,
           final_g):
    bsz, seq, d = x.shape
    t = bsz * seq
    bf16 = jnp.bfloat16
    depth = w_ada.shape[0]
    n_tiles = t // TM
    n_chunks = t * TOP_K // CHUNK + n_tiles * N_EXPERTS + N_EXPERTS * BLOCK_CHUNKS
    i32 = jnp.int32
    for l in range(depth):
        mod = _adaln_mod(c, w_ada[l], b_ada[l]).reshape(bsz, 6, d)
        pad = ROUTER_LANES - N_GROUPS - N_EXPERTS
        w_r = jnp.concatenate([w_router_g[l], w_router_e[l], jnp.zeros((d, pad), jnp.float32)], axis=1)
        b_r = jnp.concatenate([b_router_g[l], b_router_e[l], jnp.zeros((pad,), jnp.float32)]).reshape(1, -1)
        bsb = jnp.broadcast_to(gm_b_s[l][:, :, None], (GM_GROUPS, GM_CHUNK, LANES))
        x1, xs, rw, cnt = _mixer(
            x, mod, positions, rel_bias, norm1_g[l].reshape(1, d), w_in[l].astype(bf16), sinks[l],
            gm_ln_g[l].reshape(1, -1), gm_ln_b[l].reshape(1, -1), gm_w_s[l], bsb,
            p_a[l].astype(bf16), p_b[l].astype(bf16), w_o[l].astype(bf16), norm2_g[l].reshape(1, d),
            w_r.astype(bf16), b_r)

        n = cnt[:, 0, EXPERT_LANE0:EXPERT_LANE0 + N_EXPERTS].astype(i32)
        c = (n + CHUNK - 1) // CHUNK
        run0_tile = jnp.cumsum(c, axis=1) - c
        run0_exp = jnp.cumsum(c, axis=0) - c
        tot = jnp.sum(c, axis=0)
        seg = (tot + BLOCK_CHUNKS - 1) // BLOCK_CHUNKS * BLOCK_CHUNKS
        seg_end = jnp.cumsum(seg)
        seg0 = seg_end - seg

        j = jnp.arange(n_chunks, dtype=i32)
        e_j = jnp.minimum(jnp.sum(j[:, None] >= seg_end[None, :], axis=1), N_EXPERTS - 1).astype(i32)
        cj = j - seg0[e_j]
        ends_j = (run0_exp + c).T[e_j]
        tile_j = jnp.minimum(jnp.sum(ends_j <= cj[:, None], axis=1), n_tiles - 1).astype(i32)
        src_j = tile_j * SORT_CHUNKS + run0_tile[tile_j, e_j] + cj - run0_exp[tile_j, e_j]
        src_j = jnp.where(cj < tot[e_j], src_j, 0).astype(i32)
        block_e = e_j[::BLOCK_CHUNKS]
        last_blk = (jnp.maximum(seg_end[-1] // BLOCK_CHUNKS, 1) - 1).astype(i32).reshape(1)

        q = jnp.arange(SORT_CHUNKS, dtype=i32)
        e_q = jnp.minimum(jnp.sum(q[None, :, None] >= (run0_tile + c)[:, None, :], axis=2), N_EXPERTS - 1).astype(i32)
        take = lambda tbl: jnp.take_along_axis(tbl, e_q, axis=1)
        dst_q = seg0[e_q] + take(run0_exp) + q[None, :] - take(run0_tile)
        dst_q = jnp.where(q[None, :] < jnp.sum(c, axis=1, keepdims=True), dst_q, 0).astype(i32).reshape(-1)

        yr = _experts(src_j, block_e, last_blk, xs, w_gate[l], w_up[l], w_down[l])
        x = _combine(dst_q, x1.reshape(t, d), rw.reshape(t, LANES), mod, final_g, yr, seq,
                     final_norm=(l == depth - 1)).reshape(bsz, seq, d)
    return x
```

```python
import functools
import math

import jax
import jax.numpy as jnp
from jax import lax
from jax.experimental import pallas as pl
from jax.experimental.pallas import tpu as pltpu

D_MODEL = 1024
N_HEADS = 8
N_KV_HEADS = 2
HEAD_DIM = 64
BLOCK = 128
ATTN_Q = N_HEADS * HEAD_DIM
ATTN_KV = N_KV_HEADS * HEAD_DIM
N_BUCKETS = 32
MAX_EXACT = N_BUCKETS // 2
MAX_DISTANCE = 128
GM_WIDTH = 512
GM_GROUPS = 4
GM_CHUNK = 128
N_GROUPS = 4
EXPERTS_PER_GROUP = 8
N_EXPERTS = N_GROUPS * EXPERTS_PER_GROUP
TOP_K = 2
D_EXPERT = 512
EPS = 1e-6
NEG = -1e30

LANES = 128
ROUTER_LANES = LANES
EXPERT_LANE0 = N_GROUPS
TM = 512
BM = 256
CHUNK = 8
SORT_ROWS = TM * TOP_K + N_EXPERTS * CHUNK
SORT_CHUNKS = SORT_ROWS // CHUNK
BLOCK_CHUNKS = BM // CHUNK
GATHER_UNROLL = 8
VMEM_LIMIT = 56 * 1024 * 1024

Q0, K0, V0, GU0, GV0, GA0, GB0, IN_END = (0, 512, 640, 768, 1280, 1792, 2816, 3840)


def _dot(a, b):
    return jnp.dot(a, b, preferred_element_type=jnp.float32)


def _dot_nt(a, b):
    return lax.dot_general(a, b, (((1,), (1,)), ((), ())), preferred_element_type=jnp.float32)


def _gelu_tanh(x):
    return 0.5 * x * (1.0 + jnp.tanh(math.sqrt(2.0 / math.pi) * (x + 0.044715 * (x * x * x))))


def _sigmoid(x):
    return 1.0 / (1.0 + jnp.exp(-x))


def _adaln_kernel(c_ref, w_ref, b_ref, o_ref):
    c = c_ref[...]
    cs = c * _sigmoid(c)
    o_ref[...] = _dot(cs, w_ref[...]) + b_ref[...]


def _adaln_mod(c, w, b):
    bsz, d = c.shape
    n = w.shape[1]
    tn = 1024
    return pl.pallas_call(
        _adaln_kernel,
        grid=(n // tn,),
        in_specs=[pl.BlockSpec((bsz, d), lambda i: (0, 0)),
                  pl.BlockSpec((d, tn), lambda i: (0, i)),
                  pl.BlockSpec((1, tn), lambda i: (0, i))],
        out_specs=pl.BlockSpec((bsz, tn), lambda i: (0, i)),
        out_shape=jax.ShapeDtypeStruct((bsz, n), jnp.float32),
        compiler_params=pltpu.CompilerParams(dimension_semantics=("arbitrary",)),
        name="adaln_mod",
    )(c, w, b.reshape(1, n))


def _mixer_kernel(relb_ref, sinks_ref,
                  x_ref, mod_ref, pos_ref, n1g_ref, win_ref, lng_ref, lnb_ref, ws_ref, bsb_ref,
                  pa_ref, pb_ref, wo_ref, n2g_ref, wr_ref, br_ref,
                  x1_ref, xs_ref, rw_ref, cnt_ref,
                  kbuf, vbuf, pbuf, biasm, key, ya, yb, strict, flag):
    b = pl.program_id(0)
    j = pl.program_id(1)
    nblk = TM // BLOCK
    bf16 = jnp.bfloat16

    @pl.when((b == 0) & (j == 0))
    def _():
        key[...] = jnp.zeros_like(key)
        for i in range(nblk):
            flag[i] = 1
        tr = lax.broadcasted_iota(jnp.int32, (TM, TM), 0)
        tc = lax.broadcasted_iota(jnp.int32, (TM, TM), 1)
        strict[...] = jnp.where(tc < tr, 1.0, 0.0).astype(bf16)

    @pl.when(j == 0)
    def _():
        kbuf[0:BLOCK, :] = jnp.zeros((BLOCK, LANES), jnp.float32)
        vbuf[0:BLOCK, :] = jnp.zeros((BLOCK, LANES), jnp.float32)
        pbuf[:, 0:BLOCK] = jnp.zeros((1, BLOCK), jnp.int32)

    pbuf[:, BLOCK:] = pos_ref[...]
    qi = lax.broadcasted_iota(jnp.int32, (BLOCK, BLOCK), 0)
    kc = lax.broadcasted_iota(jnp.int32, (BLOCK, BLOCK), 1)
    from_prev = kc > qi
    rels, stales = [], []
    for i in range(nblk):
        pk = pbuf[:, i * BLOCK:(i + 2) * BLOCK]
        rel = pk - pk[:, BLOCK:BLOCK + 1]
        mismatch = jnp.max((rel != key[i]).astype(jnp.int32))
        stale = (mismatch != 0) | (flag[i] != 0)
        rels.append(rel)
        stales.append((stale | (j == 0)) if i == 0 else stale)
    for i in range(nblk):
        @pl.when(stales[i])
        def _():
            pk = pbuf[:, i * BLOCK:(i + 2) * BLOCK]
            pq = pk[:, BLOCK:]
            pq_col = jnp.transpose(jnp.broadcast_to(pq, (BLOCK, BLOCK)))
            no_prev = (j == 0) if i == 0 else None
            for hd in range(N_HEADS):
                biasm[i * N_HEADS + hd] = jnp.zeros((BLOCK, BLOCK), jnp.float32)
            for side in range(2):
                dist = pq_col - pk[:, side * BLOCK:(side + 1) * BLOCK]
                n = jnp.maximum(dist, 0)
                nf = jnp.maximum(n, 1).astype(jnp.float32)
                large = MAX_EXACT + (jnp.log(nf / MAX_EXACT) / math.log(MAX_DISTANCE / MAX_EXACT)
                                     * (N_BUCKETS - MAX_EXACT)).astype(jnp.int32)
                large = jnp.minimum(large, N_BUCKETS - 1)
                bucket = jnp.where(n < MAX_EXACT, n, large)
                use = from_prev if side == 0 else jnp.logical_not(from_prev)
                for hd in range(N_HEADS):
                    acc = jnp.zeros((BLOCK, BLOCK), jnp.float32)
                    for bk in range(N_BUCKETS):
                        acc = jnp.where(bucket == bk, relb_ref[bk * N_HEADS + hd], acc)
                    if side == 0 and i == 0:
                        acc = jnp.where(no_prev, NEG, acc)
                    slot = i * N_HEADS + hd
                    biasm[slot] = jnp.where(use, acc, biasm[slot])
            key[i] = rels[i]
            flag[i] = no_prev.astype(jnp.int32) if i == 0 else 0

    x = x_ref[...]
    sh1, sc1, g1 = mod_ref[0:1, :], mod_ref[1:2, :], mod_ref[2:3, :]
    sh2, sc2, g2 = mod_ref[3:4, :], mod_ref[4:5, :], mod_ref[5:6, :]

    xn = x * lax.rsqrt(jnp.mean(x * x, axis=-1, keepdims=True) + EPS)
    h = (xn * (n1g_ref[...] * (1.0 + sc1)) + sh1).astype(bf16)

    lane = lax.broadcasted_iota(jnp.int32, (1, LANES), 1)
    lo = lane < HEAD_DIM
    q = _dot(h, win_ref[:, Q0:K0]) * (HEAD_DIM ** -0.5)
    lo4 = jnp.concatenate([lo] * (ATTN_Q // LANES), axis=1)
    q_lo = jnp.where(lo4, q, 0.0).astype(bf16)
    q_hi = jnp.where(lo4, 0.0, q).astype(bf16)
    kv = _dot(h, win_ref[:, K0:GU0])
    kbuf[BLOCK:, :] = kv[:, 0:LANES]
    vbuf[BLOCK:, :] = kv[:, LANES:]
    kf = kbuf[...]
    vf = vbuf[...]
    kr = pltpu.roll(kf, HEAD_DIM, 1)
    vr = pltpu.roll(vf, HEAD_DIM, 1)
    kd = (jnp.where(lo, kf, kr).astype(bf16), jnp.where(lo, kr, kf).astype(bf16))
    v_lo = (jnp.where(lo, vf, 0.0).astype(bf16), jnp.where(lo, vr, 0.0).astype(bf16))
    v_hi = (jnp.where(lo, 0.0, vr).astype(bf16), jnp.where(lo, 0.0, vf).astype(bf16))

    for i in range(nblk):
        rows = slice(i * BLOCK, (i + 1) * BLOCK)
        band = slice(i * BLOCK, (i + 2) * BLOCK)
        for pr in range(N_HEADS // 2):
            kvh = (2 * pr) // (N_HEADS // N_KV_HEADS)
            cols = slice(pr * LANES, (pr + 1) * LANES)
            kb = kd[kvh][band]
            o = None
            for half, (qq, vv) in enumerate(((q_lo, v_lo), (q_hi, v_hi))):
                hd = 2 * pr + half
                s2 = _dot_nt(qq[rows, cols], kb)
                s = jnp.where(from_prev, s2[:, 0:BLOCK], s2[:, BLOCK:]) + biasm[i * N_HEADS + hd]
                sink = sinks_ref[hd]
                m = jnp.maximum(jnp.max(s, axis=-1, keepdims=True), sink)
                p = jnp.exp(s - m)
                den = jnp.sum(p, axis=-1, keepdims=True) + jnp.exp(sink - m)
                p2 = jnp.concatenate([jnp.where(from_prev, p, 0.0), jnp.where(from_prev, 0.0, p)], axis=1)
                oh = _dot(p2.astype(bf16), vv[kvh][band]) * (1.0 / den)
                o = oh if o is None else o + oh
            ya[rows, cols] = o.astype(bf16)

    kbuf[0:BLOCK, :] = kv[TM - BLOCK:, 0:LANES]
    vbuf[0:BLOCK, :] = kv[TM - BLOCK:, LANES:]
    pbuf[:, 0:BLOCK] = pos_ref[:, TM - BLOCK:]

    u = _gelu_tanh(_dot(h, win_ref[:, GU0:GV0]))
    vg = _gelu_tanh(_dot(h, win_ref[:, GV0:GA0]))
    mu = jnp.mean(vg, axis=-1, keepdims=True)
    vc = vg - mu
    var = jnp.mean(vc * vc, axis=-1, keepdims=True)
    vn = (vc * lax.rsqrt(var + EPS) * lng_ref[...] + lnb_ref[...]).astype(bf16)
    ti = lax.broadcasted_iota(jnp.int32, (GM_CHUNK, GM_CHUNK), 0)
    si = lax.broadcasted_iota(jnp.int32, (GM_CHUNK, GM_CHUNK), 1)
    tril = si <= ti
    for g in range(GM_GROUPS):
        wg = jnp.where(tril, ws_ref[g], 0.0).astype(bf16)
        cols = slice(g * LANES, (g + 1) * LANES)
        for cidx in range(TM // GM_CHUNK):
            rows = slice(cidx * GM_CHUNK, (cidx + 1) * GM_CHUNK)
            sv = _dot(wg, vn[rows, cols]) + bsb_ref[g]
            yb[rows, cols] = (u[rows, cols] * sv).astype(bf16)

    ga = _dot(h, win_ref[:, GA0:GB0])
    gb = _dot(h, win_ref[:, GB0:IN_END])
    merged = _sigmoid(ga) * _dot(ya[...], pa_ref[...]) + _sigmoid(gb) * _dot(yb[...], pb_ref[...])
    x1 = x + g1 * _dot(merged.astype(bf16), wo_ref[...])
    x1_ref[...] = x1

    xn2 = x1 * lax.rsqrt(jnp.mean(x1 * x1, axis=-1, keepdims=True) + EPS)
    h2 = (xn2 * (n2g_ref[...] * (1.0 + sc2)) + sh2).astype(bf16)
    logits = _dot(h2, wr_ref[...]) + br_ref[...]
    lane_f = lane.astype(jnp.float32)
    big = float(2 * LANES)
    is_grp = lane < N_GROUPS
    lg = jnp.where(is_grp, logits, NEG)
    lg_max = jnp.max(lg, axis=-1, keepdims=True)
    g_idx = jnp.min(jnp.where(lg == lg_max, lane_f, big), axis=-1, keepdims=True)
    p_g = 1.0 / jnp.sum(jnp.where(is_grp, jnp.exp(lg - lg_max), 0.0), axis=-1, keepdims=True)
    lane_grp = jnp.floor((lane_f - EXPERT_LANE0) * (1.0 / EXPERTS_PER_GROUP))
    in_grp = (lane >= EXPERT_LANE0) & (lane < EXPERT_LANE0 + N_EXPERTS) & (lane_grp == g_idx)
    le = jnp.where(in_grp, logits, NEG)
    m1 = jnp.max(le, axis=-1, keepdims=True)
    i1 = jnp.min(jnp.where(le == m1, lane_f, big), axis=-1, keepdims=True)
    oh1 = lane_f == i1
    le2 = jnp.where(oh1, NEG, le)
    m2 = jnp.max(le2, axis=-1, keepdims=True)
    i2 = jnp.min(jnp.where(le2 == m2, lane_f, big), axis=-1, keepdims=True)
    oh2 = lane_f == i2
    e2 = jnp.exp(m2 - m1)
    w1 = p_g / (1.0 + e2)
    w2 = p_g * e2 / (1.0 + e2)

    oh = jnp.where(oh1, 1.0, jnp.where(oh2, 1.0, 0.0))
    n_e = jnp.sum(oh, axis=0, keepdims=True)
    cnt_ref[...] = n_e
    padded = jnp.floor((n_e + (CHUNK - 1)) * (1.0 / CHUNK)) * CHUNK
    src_lane = lax.broadcasted_iota(jnp.int32, (LANES, LANES), 0)
    dst_lane = lax.broadcasted_iota(jnp.int32, (LANES, LANES), 1)
    upper = jnp.where(src_lane < dst_lane, 1.0, 0.0).astype(bf16)
    run0 = _dot(jnp.broadcast_to(padded, (8, LANES)).astype(bf16), upper)[0:1, :]
    slot = _dot(strict[...], oh.astype(bf16)) + run0
    pos1 = jnp.sum(jnp.where(oh1, slot, 0.0), axis=-1, keepdims=True)
    pos2 = jnp.sum(jnp.where(oh2, slot, 0.0), axis=-1, keepdims=True)
    rw_ref[...] = jnp.where(lane == 0, w1, jnp.where(lane == 1, w2,
                            jnp.where(lane == 2, pos1, jnp.where(lane == 3, pos2, 0.0))))

    pos_rows = jnp.transpose(jnp.where(lane == 0, pos1, jnp.where(lane == 1, pos2, 0.0)))
    p1 = pos_rows[0:1, :].astype(jnp.int32)
    p2 = pos_rows[1:2, :].astype(jnp.int32)
    sp = lax.broadcasted_iota(jnp.int32, (SORT_ROWS, TM), 0)
    perm = jnp.where(sp == p1, 1.0, jnp.where(sp == p2, 1.0, 0.0)).astype(bf16)
    xs = lax.bitcast_convert_type(_dot(perm, h2), jnp.uint32)
    half = D_MODEL // 2
    xs_ref[...] = (xs[:, 0:half] >> 16) | (xs[:, half:] & jnp.uint32(0xFFFF0000))


def _mixer(x, mod, positions, rel_bias, n1g, w_in, sinks, lng, lnb, w_s, bsb, p_a, p_b, w_o, n2g, w_r, b_r):
    bsz, seq, d = x.shape
    nj = seq // TM
    const = lambda *shape: pl.BlockSpec(shape, lambda b, j: (0,) * len(shape), pipeline_mode=pl.Buffered(1))
    smem = pl.BlockSpec(memory_space=pltpu.SMEM)
    tile = lambda w: pl.BlockSpec((None, TM, w), lambda b, j: (b, j, 0))
    return pl.pallas_call(
        _mixer_kernel,
        grid=(bsz, nj),
        in_specs=[smem, smem,
                  tile(d),
                  pl.BlockSpec((None, 6, d), lambda b, j: (b, 0, 0)),
                  pl.BlockSpec((None, None, 1, TM), lambda b, j: (b, j, 0, 0)),
                  const(1, d), const(d, IN_END), const(1, GM_WIDTH), const(1, GM_WIDTH),
                  const(GM_GROUPS, GM_CHUNK, GM_CHUNK), const(GM_GROUPS, GM_CHUNK, LANES),
                  const(ATTN_Q, d), const(GM_WIDTH, d), const(d, d), const(1, d),
                  const(d, ROUTER_LANES), const(1, ROUTER_LANES)],
        out_specs=[tile(d),
                   pl.BlockSpec((SORT_ROWS, d // 2), lambda b, j: (b * nj + j, 0)),
                   tile(LANES),
                   pl.BlockSpec((None, 1, LANES), lambda b, j: (b * nj + j, 0, 0))],
        out_shape=[jax.ShapeDtypeStruct((bsz, seq, d), jnp.float32),
                   jax.ShapeDtypeStruct((bsz * nj * SORT_ROWS, d // 2), jnp.uint32),
                   jax.ShapeDtypeStruct((bsz, seq, LANES), jnp.float32),
                   jax.ShapeDtypeStruct((bsz * nj, 1, LANES), jnp.float32)],
        scratch_shapes=[pltpu.VMEM((TM + BLOCK, LANES), jnp.float32),
                        pltpu.VMEM((TM + BLOCK, LANES), jnp.float32),
                        pltpu.VMEM((1, TM + BLOCK), jnp.int32),
                        pltpu.VMEM((TM // BLOCK * N_HEADS, BLOCK, BLOCK), jnp.float32),
                        pltpu.VMEM((TM // BLOCK, 1, 2 * BLOCK), jnp.int32),
                        pltpu.VMEM((TM, ATTN_Q), jnp.bfloat16),
                        pltpu.VMEM((TM, GM_WIDTH), jnp.bfloat16),
                        pltpu.VMEM((TM, TM), jnp.bfloat16),
                        pltpu.SMEM((TM // BLOCK,), jnp.int32)],
        compiler_params=pltpu.CompilerParams(dimension_semantics=("arbitrary", "arbitrary"),
                                             vmem_limit_bytes=VMEM_LIMIT),
        name="mixer",
    )(rel_bias.reshape(-1), sinks, x, mod, positions.reshape(bsz, nj, 1, TM), n1g, w_in, lng, lnb,
      w_s, bsb, p_a, p_b, w_o, n2g, w_r, b_r)


def _gather_chunks(idx_ref, idx0, n, src_hbm, dst, sem):
    def issue(c, carry):
        src = pl.multiple_of(idx_ref[idx0 + c] * CHUNK, CHUNK)
        pltpu.make_async_copy(src_hbm.at[pl.ds(src, CHUNK)],
                              dst.at[pl.ds(pl.multiple_of(c * CHUNK, CHUNK), CHUNK)], sem).start()
        return carry

    lax.fori_loop(0, n, issue, 0, unroll=GATHER_UNROLL)


def _wait_chunks(n, src_hbm, dst, sem):
    pltpu.make_async_copy(src_hbm.at[pl.ds(0, n * CHUNK)], dst, sem).wait()


def _expert_kernel(src_ref, be_ref, lb_ref, xs_hbm, wg_ref, wu_ref, wd_ref, yr_ref, xbuf, wgb, wub, wdb, sem):
    i = pl.program_id(0)
    last = lb_ref[0]
    slot = i % 2

    @pl.when(i == 0)
    def _():
        _gather_chunks(src_ref, 0, BLOCK_CHUNKS, xs_hbm, xbuf.at[0], sem.at[0])

    @pl.when(i < last)
    def _():
        _gather_chunks(src_ref, (i + 1) * BLOCK_CHUNKS, BLOCK_CHUNKS, xs_hbm, xbuf.at[1 - slot], sem.at[1 - slot])

    @pl.when(i <= last)
    def _():
        changed = (i == 0) | (be_ref[i] != be_ref[jnp.maximum(i - 1, 0)])

        @pl.when(changed)
        def _():
            wgb[...] = wg_ref[...].astype(jnp.bfloat16)
            wub[...] = wu_ref[...].astype(jnp.bfloat16)
            wdb[...] = wd_ref[...].astype(jnp.bfloat16)

        _wait_chunks(BLOCK_CHUNKS, xs_hbm, xbuf.at[slot], sem.at[slot])
        packed = xbuf[slot]
        x_lo = lax.bitcast_convert_type(packed << 16, jnp.float32)
        x_hi = lax.bitcast_convert_type(packed & jnp.uint32(0xFFFF0000), jnp.float32)
        xb = jnp.concatenate([x_lo, x_hi], axis=1).astype(jnp.bfloat16)
        hg = _dot(xb, wgb[...])
        hu = _dot(xb, wub[...])
        hid = (hg * _sigmoid(hg) * hu).astype(jnp.bfloat16)
        yr_ref[...] = _dot(hid, wdb[...])

    @pl.when(i > last)
    def _():
        yr_ref[...] = jnp.zeros_like(yr_ref)


def _experts(src_chunk, block_e, last_blk, xs, w_gate, w_up, w_down):
    d = w_gate.shape[1]
    n_rows = src_chunk.shape[0] * CHUNK

    def row_map(i, rt, be, lb):
        return (i, 0)

    def w_map(i, rt, be, lb):
        return (be[jnp.minimum(i, lb[0])], 0, 0)

    return pl.pallas_call(
        _expert_kernel,
        grid_spec=pltpu.PrefetchScalarGridSpec(
            num_scalar_prefetch=3,
            grid=(n_rows // BM,),
            in_specs=[pl.BlockSpec(memory_space=pl.ANY),
                      pl.BlockSpec((None, d, D_EXPERT), w_map),
                      pl.BlockSpec((None, d, D_EXPERT), w_map),
                      pl.BlockSpec((None, D_EXPERT, d), w_map)],
            out_specs=pl.BlockSpec((BM, d), row_map),
            scratch_shapes=[pltpu.VMEM((2, BM, d // 2), jnp.uint32),
                            pltpu.VMEM((d, D_EXPERT), jnp.bfloat16),
                            pltpu.VMEM((d, D_EXPERT), jnp.bfloat16),
                            pltpu.VMEM((D_EXPERT, d), jnp.bfloat16),
                            pltpu.SemaphoreType.DMA((2,))]),
        out_shape=jax.ShapeDtypeStruct((n_rows, d), jnp.float32),
        compiler_params=pltpu.CompilerParams(dimension_semantics=("arbitrary",),
                                             vmem_limit_bytes=VMEM_LIMIT),
        name="experts",
    )(src_chunk, block_e, last_blk, xs, w_gate, w_up, w_down)


def _combine_kernel(src_ref, x1_ref, rw_ref, mod_ref, fg_ref, yr_hbm, out_ref, ybuf, sem, *, final_norm):
    i = pl.program_id(0)
    slot = i % 2

    @pl.when(i == 0)
    def _():
        _gather_chunks(src_ref, 0, SORT_CHUNKS, yr_hbm, ybuf.at[0], sem.at[0])

    @pl.when(i + 1 < pl.num_programs(0))
    def _():
        _gather_chunks(src_ref, (i + 1) * SORT_CHUNKS, SORT_CHUNKS, yr_hbm, ybuf.at[1 - slot], sem.at[1 - slot])

    rw = rw_ref[...]
    sp = lax.broadcasted_iota(jnp.int32, (1, SORT_ROWS), 1)
    p1 = rw[:, 2:3].astype(jnp.int32)
    p2 = rw[:, 3:4].astype(jnp.int32)
    wmat = jnp.where(sp == p1, rw[:, 0:1], jnp.where(sp == p2, rw[:, 1:2], 0.0)).astype(jnp.bfloat16)
    _wait_chunks(SORT_CHUNKS, yr_hbm, ybuf.at[slot], sem.at[slot])
    moe = _dot(wmat, ybuf[slot].astype(jnp.bfloat16))
    x2 = x1_ref[...] + mod_ref[5:6, :] * moe
    if final_norm:
        x2 = x2 * lax.rsqrt(jnp.mean(x2 * x2, axis=-1, keepdims=True) + EPS) * fg_ref[...]
    out_ref[...] = x2


def _combine(src_chunk, x1, rw, mod, final_g, yr, seq, final_norm):
    t, d = x1.shape
    per_seq = seq // TM
    return pl.pallas_call(
        functools.partial(_combine_kernel, final_norm=final_norm),
        grid_spec=pltpu.PrefetchScalarGridSpec(
            num_scalar_prefetch=1,
            grid=(t // TM,),
            in_specs=[pl.BlockSpec((TM, d), lambda i, src: (i, 0)),
                      pl.BlockSpec((TM, LANES), lambda i, src: (i, 0)),
                      pl.BlockSpec((None, 6, d), lambda i, src: (i // per_seq, 0, 0)),
                      pl.BlockSpec((1, d), lambda i, src: (0, 0)),
                      pl.BlockSpec(memory_space=pl.ANY)],
            out_specs=pl.BlockSpec((TM, d), lambda i, src: (i, 0)),
            scratch_shapes=[pltpu.VMEM((2, SORT_ROWS, d), jnp.float32),
                            pltpu.SemaphoreType.DMA((2,))]),
        out_shape=jax.ShapeDtypeStruct((t, d), jnp.float32),
        compiler_params=pltpu.CompilerParams(dimension_semantics=("arbitrary",),
                                             vmem_limit_bytes=VMEM_LIMIT),
        name="combine",
    )(src_chunk, x1, rw, mod, final_g.reshape(1, d), yr)


def kernel(x, c, positions, rel_bias, w_ada, b_ada, norm1_g, w_in, sinks, gm_ln_g, gm_ln_b, gm_w_s, gm_b_s,
           p_a, p_b, w_o, norm2_g, w_router_g, b_router_g, w_router_e, b_router_e, w_gate, w_up, w_down,
           final_g):
    bsz, seq, d = x.shape
    t = bsz * seq
    bf16 = jnp.bfloat16
    depth = w_ada.shape[0]
    n_tiles = t // TM
    n_chunks = t * TOP_K // CHUNK + n_tiles * N_EXPERTS + N_EXPERTS * BLOCK_CHUNKS
    i32 = jnp.int32
    for l in range(depth):
        mod = _adaln_mod(c, w_ada[l], b_ada[l]).reshape(bsz, 6, d)
        pad = ROUTER_LANES - N_GROUPS - N_EXPERTS
        w_r = jnp.concatenate([w_router_g[l], w_router_e[l], jnp.zeros((d, pad), jnp.float32)], axis=1)
        b_r = jnp.concatenate([b_router_g[l], b_router_e[l], jnp.zeros((pad,), jnp.float32)]).reshape(1, -1)
        bsb = jnp.broadcast_to(gm_b_s[l][:, :, None], (GM_GROUPS, GM_CHUNK, LANES))
        x1, xs, rw, cnt = _mixer(
            x, mod, positions, rel_bias, norm1_g[l].reshape(1, d), w_in[l].astype(bf16), sinks[l],
            gm_ln_g[l].reshape(1, -1), gm_ln_b[l].reshape(1, -1), gm_w_s[l], bsb,
            p_a[l].astype(bf16), p_b[l].astype(bf16), w_o[l].astype(bf16), norm2_g[l].reshape(1, d),
            w_r.astype(bf16), b_r)

        n = cnt[:, 0, EXPERT_LANE0:EXPERT_LANE0 + N_EXPERTS].astype(i32)
        c = (n + CHUNK - 1) // CHUNK
        run0_tile = jnp.cumsum(c, axis=1) - c
        run0_exp = jnp.cumsum(c, axis=0) - c
        tot = jnp.sum(c, axis=0)
        seg = (tot + BLOCK_CHUNKS - 1) // BLOCK_CHUNKS * BLOCK_CHUNKS
        seg_end = jnp.cumsum(seg)
        seg0 = seg_end - seg

        pos_tile = jnp.arange(n_tiles, dtype=i32)[:, None] * SORT_CHUNKS + run0_tile
        pos_exp = seg0[None, :] + run0_exp
        shift = (pos_tile - pos_exp).reshape(-1)
        lo_exp, hi_exp = pos_exp.reshape(-1), (pos_exp + c).reshape(-1)

        j = jnp.arange(n_chunks, dtype=i32)[:, None]
        inside = (j >= lo_exp[None, :]) & (j < hi_exp[None, :])
        src_j = jnp.sum(jnp.where(inside, j + shift[None, :], 0), axis=1).astype(i32)
        blk0 = jnp.arange(n_chunks // BLOCK_CHUNKS, dtype=i32) * BLOCK_CHUNKS
        block_e = jnp.minimum(jnp.sum(blk0[:, None] >= seg_end[None, :], axis=1), N_EXPERTS - 1).astype(i32)
        last_blk = (jnp.maximum(seg_end[-1] // BLOCK_CHUNKS, 1) - 1).astype(i32).reshape(1)

        q = jnp.arange(SORT_CHUNKS, dtype=i32)[None, :, None]
        in_run = (q >= run0_tile[:, None, :]) & (q < (run0_tile + c)[:, None, :])
        dst_q = jnp.sum(jnp.where(in_run, q + (pos_exp - run0_tile)[:, None, :], 0), axis=2).astype(i32).reshape(-1)

        yr = _experts(src_j, block_e, last_blk, xs, w_gate[l], w_up[l], w_down[l])
        x = _combine(dst_q, x1.reshape(t, d), rw.reshape(t, LANES), mod, final_g, yr, seq,
                     final_norm=(l == depth - 1)).reshape(bsz, seq, d)
    return x
```

```python
import functools
import math

import jax
import jax.numpy as jnp
from jax import lax
from jax.experimental import pallas as pl
from jax.experimental.pallas import tpu as pltpu

D_MODEL = 1024
N_HEADS = 8
N_KV_HEADS = 2
HEAD_DIM = 64
BLOCK = 128
ATTN_Q = N_HEADS * HEAD_DIM
ATTN_KV = N_KV_HEADS * HEAD_DIM
N_BUCKETS = 32
MAX_EXACT = N_BUCKETS // 2
MAX_DISTANCE = 128
GM_WIDTH = 512
GM_GROUPS = 4
GM_CHUNK = 128
N_GROUPS = 4
EXPERTS_PER_GROUP = 8
N_EXPERTS = N_GROUPS * EXPERTS_PER_GROUP
TOP_K = 2
D_EXPERT = 512
EPS = 1e-6
NEG = -1e30

LANES = 128
ROUTER_LANES = LANES
EXPERT_LANE0 = N_GROUPS
TM = 512
BM = 512
CHUNK = 8
SORT_ROWS = TM * TOP_K + N_EXPERTS * CHUNK
SORT_CHUNKS = SORT_ROWS // CHUNK
BLOCK_CHUNKS = BM // CHUNK
GATHER_UNROLL = 8
VMEM_LIMIT = 56 * 1024 * 1024

Q0, K0, V0, GU0, GV0, GA0, GB0, IN_END = (0, 512, 640, 768, 1280, 1792, 2816, 3840)


def _dot(a, b):
    return jnp.dot(a, b, preferred_element_type=jnp.float32)


def _dot_nt(a, b):
    return lax.dot_general(a, b, (((1,), (1,)), ((), ())), preferred_element_type=jnp.float32)


def _gelu_tanh(x):
    return 0.5 * x * (1.0 + jnp.tanh(math.sqrt(2.0 / math.pi) * (x + 0.044715 * (x * x * x))))


def _sigmoid(x):
    return 1.0 / (1.0 + jnp.exp(-x))


def _pack_bf16_pairs(x):
    bits = lax.bitcast_convert_type(x, jnp.uint32)
    half = x.shape[1] // 2
    return (bits[:, 0:half] >> 16) | (bits[:, half:] & jnp.uint32(0xFFFF0000))


def _unpack_bf16_pairs(w):
    lo = lax.bitcast_convert_type(w << 16, jnp.float32)
    hi = lax.bitcast_convert_type(w & jnp.uint32(0xFFFF0000), jnp.float32)
    return jnp.concatenate([lo, hi], axis=1).astype(jnp.bfloat16)


def _adaln_kernel(c_ref, w_ref, b_ref, o_ref):
    c = c_ref[...]
    cs = c * _sigmoid(c)
    o_ref[...] = _dot(cs, w_ref[...]) + b_ref[...]


def _adaln_mod(c, w, b):
    bsz, d = c.shape
    n = w.shape[1]
    tn = 1024
    return pl.pallas_call(
        _adaln_kernel,
        grid=(n // tn,),
        in_specs=[pl.BlockSpec((bsz, d), lambda i: (0, 0)),
                  pl.BlockSpec((d, tn), lambda i: (0, i)),
                  pl.BlockSpec((1, tn), lambda i: (0, i))],
        out_specs=pl.BlockSpec((bsz, tn), lambda i: (0, i)),
        out_shape=jax.ShapeDtypeStruct((bsz, n), jnp.float32),
        compiler_params=pltpu.CompilerParams(dimension_semantics=("arbitrary",)),
        name="adaln_mod",
    )(c, w, b.reshape(1, n))


def _mixer_kernel(relb_ref, sinks_ref,
                  x_ref, mod_ref, pos_ref, n1g_ref, win_ref, lng_ref, lnb_ref, ws_ref, bsb_ref,
                  pa_ref, pb_ref, wo_ref, n2g_ref, wr_ref, br_ref,
                  x1_ref, xs_ref, rw_ref, cnt_ref,
                  kbuf, vbuf, pbuf, biasm, key, ya, yb, strict, flag):
    b = pl.program_id(0)
    j = pl.program_id(1)
    nblk = TM // BLOCK
    bf16 = jnp.bfloat16

    @pl.when((b == 0) & (j == 0))
    def _():
        key[...] = jnp.zeros_like(key)
        for i in range(nblk):
            flag[i] = 1
        tr = lax.broadcasted_iota(jnp.int32, (TM, TM), 0)
        tc = lax.broadcasted_iota(jnp.int32, (TM, TM), 1)
        strict[...] = jnp.where(tc < tr, 1.0, 0.0).astype(bf16)

    @pl.when(j == 0)
    def _():
        kbuf[0:BLOCK, :] = jnp.zeros((BLOCK, LANES), jnp.float32)
        vbuf[0:BLOCK, :] = jnp.zeros((BLOCK, LANES), jnp.float32)
        pbuf[:, 0:BLOCK] = jnp.zeros((1, BLOCK), jnp.int32)

    pbuf[:, BLOCK:] = pos_ref[...]
    qi = lax.broadcasted_iota(jnp.int32, (BLOCK, BLOCK), 0)
    kc = lax.broadcasted_iota(jnp.int32, (BLOCK, BLOCK), 1)
    from_prev = kc > qi
    rels, stales = [], []
    for i in range(nblk):
        pk = pbuf[:, i * BLOCK:(i + 2) * BLOCK]
        rel = pk - pk[:, BLOCK:BLOCK + 1]
        mismatch = jnp.max((rel != key[i]).astype(jnp.int32))
        stale = (mismatch != 0) | (flag[i] != 0)
        rels.append(rel)
        stales.append((stale | (j == 0)) if i == 0 else stale)
    for i in range(nblk):
        @pl.when(stales[i])
        def _():
            pk = pbuf[:, i * BLOCK:(i + 2) * BLOCK]
            pq = pk[:, BLOCK:]
            pq_col = jnp.transpose(jnp.broadcast_to(pq, (BLOCK, BLOCK)))
            no_prev = (j == 0) if i == 0 else None
            for hd in range(N_HEADS):
                biasm[i * N_HEADS + hd] = jnp.zeros((BLOCK, BLOCK), jnp.float32)
            for side in range(2):
                dist = pq_col - pk[:, side * BLOCK:(side + 1) * BLOCK]
                n = jnp.maximum(dist, 0)
                nf = jnp.maximum(n, 1).astype(jnp.float32)
                large = MAX_EXACT + (jnp.log(nf / MAX_EXACT) / math.log(MAX_DISTANCE / MAX_EXACT)
                                     * (N_BUCKETS - MAX_EXACT)).astype(jnp.int32)
                large = jnp.minimum(large, N_BUCKETS - 1)
                bucket = jnp.where(n < MAX_EXACT, n, large)
                use = from_prev if side == 0 else jnp.logical_not(from_prev)
                for hd in range(N_HEADS):
                    acc = jnp.zeros((BLOCK, BLOCK), jnp.float32)
                    for bk in range(N_BUCKETS):
                        acc = jnp.where(bucket == bk, relb_ref[bk * N_HEADS + hd], acc)
                    if side == 0 and i == 0:
                        acc = jnp.where(no_prev, NEG, acc)
                    slot = i * N_HEADS + hd
                    biasm[slot] = jnp.where(use, acc, biasm[slot])
            key[i] = rels[i]
            flag[i] = no_prev.astype(jnp.int32) if i == 0 else 0

    x = x_ref[...]
    sh1, sc1, g1 = mod_ref[0:1, :], mod_ref[1:2, :], mod_ref[2:3, :]
    sh2, sc2, g2 = mod_ref[3:4, :], mod_ref[4:5, :], mod_ref[5:6, :]

    xn = x * lax.rsqrt(jnp.mean(x * x, axis=-1, keepdims=True) + EPS)
    h = (xn * (n1g_ref[...] * (1.0 + sc1)) + sh1).astype(bf16)

    lane = lax.broadcasted_iota(jnp.int32, (1, LANES), 1)
    lo = lane < HEAD_DIM
    q = _dot(h, win_ref[:, Q0:K0]) * (HEAD_DIM ** -0.5)
    lo4 = jnp.concatenate([lo] * (ATTN_Q // LANES), axis=1)
    q_lo = jnp.where(lo4, q, 0.0).astype(bf16)
    q_hi = jnp.where(lo4, 0.0, q).astype(bf16)
    kv = _dot(h, win_ref[:, K0:GU0])
    kbuf[BLOCK:, :] = kv[:, 0:LANES]
    vbuf[BLOCK:, :] = kv[:, LANES:]
    kf = kbuf[...]
    vf = vbuf[...]
    kr = pltpu.roll(kf, HEAD_DIM, 1)
    vr = pltpu.roll(vf, HEAD_DIM, 1)
    kd = (jnp.where(lo, kf, kr).astype(bf16), jnp.where(lo, kr, kf).astype(bf16))
    v_lo = (jnp.where(lo, vf, 0.0).astype(bf16), jnp.where(lo, vr, 0.0).astype(bf16))
    v_hi = (jnp.where(lo, 0.0, vr).astype(bf16), jnp.where(lo, 0.0, vf).astype(bf16))

    for i in range(nblk):
        rows = slice(i * BLOCK, (i + 1) * BLOCK)
        band = slice(i * BLOCK, (i + 2) * BLOCK)
        for pr in range(N_HEADS // 2):
            kvh = (2 * pr) // (N_HEADS // N_KV_HEADS)
            cols = slice(pr * LANES, (pr + 1) * LANES)
            kb = kd[kvh][band]
            o = None
            for half, (qq, vv) in enumerate(((q_lo, v_lo), (q_hi, v_hi))):
                hd = 2 * pr + half
                s2 = _dot_nt(qq[rows, cols], kb)
                s = jnp.where(from_prev, s2[:, 0:BLOCK], s2[:, BLOCK:]) + biasm[i * N_HEADS + hd]
                sink = sinks_ref[hd]
                m = jnp.maximum(jnp.max(s, axis=-1, keepdims=True), sink)
                p = jnp.exp(s - m)
                den = jnp.sum(p, axis=-1, keepdims=True) + jnp.exp(sink - m)
                p2 = jnp.concatenate([jnp.where(from_prev, p, 0.0), jnp.where(from_prev, 0.0, p)], axis=1)
                oh = _dot(p2.astype(bf16), vv[kvh][band]) * (1.0 / den)
                o = oh if o is None else o + oh
            ya[rows, cols] = o.astype(bf16)

    kbuf[0:BLOCK, :] = kv[TM - BLOCK:, 0:LANES]
    vbuf[0:BLOCK, :] = kv[TM - BLOCK:, LANES:]
    pbuf[:, 0:BLOCK] = pos_ref[:, TM - BLOCK:]

    u = _gelu_tanh(_dot(h, win_ref[:, GU0:GV0]))
    vg = _gelu_tanh(_dot(h, win_ref[:, GV0:GA0]))
    mu = jnp.mean(vg, axis=-1, keepdims=True)
    vc = vg - mu
    var = jnp.mean(vc * vc, axis=-1, keepdims=True)
    vn = (vc * lax.rsqrt(var + EPS) * lng_ref[...] + lnb_ref[...]).astype(bf16)
    ti = lax.broadcasted_iota(jnp.int32, (GM_CHUNK, GM_CHUNK), 0)
    si = lax.broadcasted_iota(jnp.int32, (GM_CHUNK, GM_CHUNK), 1)
    tril = si <= ti
    for g in range(GM_GROUPS):
        wg = jnp.where(tril, ws_ref[g], 0.0).astype(bf16)
        cols = slice(g * LANES, (g + 1) * LANES)
        for cidx in range(TM // GM_CHUNK):
            rows = slice(cidx * GM_CHUNK, (cidx + 1) * GM_CHUNK)
            sv = _dot(wg, vn[rows, cols]) + bsb_ref[g]
            yb[rows, cols] = (u[rows, cols] * sv).astype(bf16)

    ga = _dot(h, win_ref[:, GA0:GB0])
    gb = _dot(h, win_ref[:, GB0:IN_END])
    merged = _sigmoid(ga) * _dot(ya[...], pa_ref[...]) + _sigmoid(gb) * _dot(yb[...], pb_ref[...])
    x1 = x + g1 * _dot(merged.astype(bf16), wo_ref[...])
    x1_ref[...] = x1

    xn2 = x1 * lax.rsqrt(jnp.mean(x1 * x1, axis=-1, keepdims=True) + EPS)
    h2 = (xn2 * (n2g_ref[...] * (1.0 + sc2)) + sh2).astype(bf16)
    logits = _dot(h2, wr_ref[...]) + br_ref[...]
    lane_f = lane.astype(jnp.float32)
    big = float(2 * LANES)
    is_grp = lane < N_GROUPS
    lg = jnp.where(is_grp, logits, NEG)
    lg_max = jnp.max(lg, axis=-1, keepdims=True)
    g_idx = jnp.min(jnp.where(lg == lg_max, lane_f, big), axis=-1, keepdims=True)
    p_g = 1.0 / jnp.sum(jnp.where(is_grp, jnp.exp(lg - lg_max), 0.0), axis=-1, keepdims=True)
    lane_grp = jnp.floor((lane_f - EXPERT_LANE0) * (1.0 / EXPERTS_PER_GROUP))
    in_grp = (lane >= EXPERT_LANE0) & (lane < EXPERT_LANE0 + N_EXPERTS) & (lane_grp == g_idx)
    le = jnp.where(in_grp, logits, NEG)
    m1 = jnp.max(le, axis=-1, keepdims=True)
    i1 = jnp.min(jnp.where(le == m1, lane_f, big), axis=-1, keepdims=True)
    oh1 = lane_f == i1
    le2 = jnp.where(oh1, NEG, le)
    m2 = jnp.max(le2, axis=-1, keepdims=True)
    i2 = jnp.min(jnp.where(le2 == m2, lane_f, big), axis=-1, keepdims=True)
    oh2 = lane_f == i2
    e2 = jnp.exp(m2 - m1)
    w1 = p_g / (1.0 + e2)
    w2 = p_g * e2 / (1.0 + e2)

    oh = jnp.where(oh1, 1.0, jnp.where(oh2, 1.0, 0.0))
    n_e = jnp.sum(oh, axis=0, keepdims=True)
    cnt_ref[...] = n_e
    padded = jnp.floor((n_e + (CHUNK - 1)) * (1.0 / CHUNK)) * CHUNK
    src_lane = lax.broadcasted_iota(jnp.int32, (LANES, LANES), 0)
    dst_lane = lax.broadcasted_iota(jnp.int32, (LANES, LANES), 1)
    upper = jnp.where(src_lane < dst_lane, 1.0, 0.0).astype(bf16)
    run0 = _dot(jnp.broadcast_to(padded, (8, LANES)).astype(bf16), upper)[0:1, :]
    slot = _dot(strict[...], oh.astype(bf16)) + run0
    pos1 = jnp.sum(jnp.where(oh1, slot, 0.0), axis=-1, keepdims=True)
    pos2 = jnp.sum(jnp.where(oh2, slot, 0.0), axis=-1, keepdims=True)
    rw_ref[...] = jnp.where(lane == 0, w1, jnp.where(lane == 1, w2,
                            jnp.where(lane == 2, pos1, jnp.where(lane == 3, pos2, 0.0))))

    pos_rows = jnp.transpose(jnp.where(lane == 0, pos1, jnp.where(lane == 1, pos2, 0.0)))
    p1 = pos_rows[0:1, :].astype(jnp.int32)
    p2 = pos_rows[1:2, :].astype(jnp.int32)
    sp = lax.broadcasted_iota(jnp.int32, (SORT_ROWS, TM), 0)
    perm = jnp.where(sp == p1, 1.0, jnp.where(sp == p2, 1.0, 0.0)).astype(bf16)
    xs_ref[...] = _pack_bf16_pairs(_dot(perm, h2))


def _mixer(x, mod, positions, rel_bias, n1g, w_in, sinks, lng, lnb, w_s, bsb, p_a, p_b, w_o, n2g, w_r, b_r):
    bsz, seq, d = x.shape
    nj = seq // TM
    const = lambda *shape: pl.BlockSpec(shape, lambda b, j: (0,) * len(shape), pipeline_mode=pl.Buffered(1))
    smem = pl.BlockSpec(memory_space=pltpu.SMEM)
    tile = lambda w: pl.BlockSpec((None, TM, w), lambda b, j: (b, j, 0))
    return pl.pallas_call(
        _mixer_kernel,
        grid=(bsz, nj),
        in_specs=[smem, smem,
                  tile(d),
                  pl.BlockSpec((None, 6, d), lambda b, j: (b, 0, 0)),
                  pl.BlockSpec((None, None, 1, TM), lambda b, j: (b, j, 0, 0)),
                  const(1, d), const(d, IN_END), const(1, GM_WIDTH), const(1, GM_WIDTH),
                  const(GM_GROUPS, GM_CHUNK, GM_CHUNK), const(GM_GROUPS, GM_CHUNK, LANES),
                  const(ATTN_Q, d), const(GM_WIDTH, d), const(d, d), const(1, d),
                  const(d, ROUTER_LANES), const(1, ROUTER_LANES)],
        out_specs=[tile(d),
                   pl.BlockSpec((SORT_ROWS, d // 2), lambda b, j: (b * nj + j, 0)),
                   tile(LANES),
                   pl.BlockSpec((None, 1, LANES), lambda b, j: (b * nj + j, 0, 0))],
        out_shape=[jax.ShapeDtypeStruct((bsz, seq, d), jnp.float32),
                   jax.ShapeDtypeStruct((bsz * nj * SORT_ROWS, d // 2), jnp.uint32),
                   jax.ShapeDtypeStruct((bsz, seq, LANES), jnp.float32),
                   jax.ShapeDtypeStruct((bsz * nj, 1, LANES), jnp.float32)],
        scratch_shapes=[pltpu.VMEM((TM + BLOCK, LANES), jnp.float32),
                        pltpu.VMEM((TM + BLOCK, LANES), jnp.float32),
                        pltpu.VMEM((1, TM + BLOCK), jnp.int32),
                        pltpu.VMEM((TM // BLOCK * N_HEADS, BLOCK, BLOCK), jnp.float32),
                        pltpu.VMEM((TM // BLOCK, 1, 2 * BLOCK), jnp.int32),
                        pltpu.VMEM((TM, ATTN_Q), jnp.bfloat16),
                        pltpu.VMEM((TM, GM_WIDTH), jnp.bfloat16),
                        pltpu.VMEM((TM, TM), jnp.bfloat16),
                        pltpu.SMEM((TM // BLOCK,), jnp.int32)],
        compiler_params=pltpu.CompilerParams(dimension_semantics=("arbitrary", "arbitrary"),
                                             vmem_limit_bytes=VMEM_LIMIT),
        name="mixer",
    )(rel_bias.reshape(-1), sinks, x, mod, positions.reshape(bsz, nj, 1, TM), n1g, w_in, lng, lnb,
      w_s, bsb, p_a, p_b, w_o, n2g, w_r, b_r)


def _gather_chunks(idx_ref, idx0, n, src_hbm, dst, sem):
    def issue(c, carry):
        src = pl.multiple_of(idx_ref[idx0 + c] * CHUNK, CHUNK)
        pltpu.make_async_copy(src_hbm.at[pl.ds(src, CHUNK)],
                              dst.at[pl.ds(pl.multiple_of(c * CHUNK, CHUNK), CHUNK)], sem).start()
        return carry

    lax.fori_loop(0, n, issue, 0, unroll=GATHER_UNROLL)


def _wait_chunks(n, src_hbm, dst, sem):
    pltpu.make_async_copy(src_hbm.at[pl.ds(0, n * CHUNK)], dst, sem).wait()


def _expert_kernel(src_ref, be_ref, lb_ref, xs_hbm, wg_ref, wu_ref, wd_ref, yr_ref, xbuf, wgb, wub, wdb, sem):
    i = pl.program_id(0)
    last = lb_ref[0]
    slot = i % 2

    @pl.when(i == 0)
    def _():
        _gather_chunks(src_ref, 0, BLOCK_CHUNKS, xs_hbm, xbuf.at[0], sem.at[0])

    @pl.when(i < last)
    def _():
        _gather_chunks(src_ref, (i + 1) * BLOCK_CHUNKS, BLOCK_CHUNKS, xs_hbm, xbuf.at[1 - slot], sem.at[1 - slot])

    @pl.when(i <= last)
    def _():
        changed = (i == 0) | (be_ref[i] != be_ref[jnp.maximum(i - 1, 0)])

        @pl.when(changed)
        def _():
            wgb[...] = wg_ref[...].astype(jnp.bfloat16)
            wub[...] = wu_ref[...].astype(jnp.bfloat16)
            wdb[...] = wd_ref[...].astype(jnp.bfloat16)

        _wait_chunks(BLOCK_CHUNKS, xs_hbm, xbuf.at[slot], sem.at[slot])
        xb = _unpack_bf16_pairs(xbuf[slot])
        hg = _dot(xb, wgb[...])
        hu = _dot(xb, wub[...])
        hid = (hg * _sigmoid(hg) * hu).astype(jnp.bfloat16)
        yr_ref[...] = _pack_bf16_pairs(_dot(hid, wdb[...]).astype(jnp.bfloat16).astype(jnp.float32))

    @pl.when(i > last)
    def _():
        yr_ref[...] = jnp.zeros_like(yr_ref)


def _experts(src_chunk, block_e, last_blk, xs, w_gate, w_up, w_down):
    d = w_gate.shape[1]
    n_rows = src_chunk.shape[0] * CHUNK

    def row_map(i, rt, be, lb):
        return (i, 0)

    def w_map(i, rt, be, lb):
        return (be[jnp.minimum(i, lb[0])], 0, 0)

    return pl.pallas_call(
        _expert_kernel,
        grid_spec=pltpu.PrefetchScalarGridSpec(
            num_scalar_prefetch=3,
            grid=(n_rows // BM,),
            in_specs=[pl.BlockSpec(memory_space=pl.ANY),
                      pl.BlockSpec((None, d, D_EXPERT), w_map),
                      pl.BlockSpec((None, d, D_EXPERT), w_map),
                      pl.BlockSpec((None, D_EXPERT, d), w_map)],
            out_specs=pl.BlockSpec((BM, d // 2), row_map),
            scratch_shapes=[pltpu.VMEM((2, BM, d // 2), jnp.uint32),
                            pltpu.VMEM((d, D_EXPERT), jnp.bfloat16),
                            pltpu.VMEM((d, D_EXPERT), jnp.bfloat16),
                            pltpu.VMEM((D_EXPERT, d), jnp.bfloat16),
                            pltpu.SemaphoreType.DMA((2,))]),
        out_shape=jax.ShapeDtypeStruct((n_rows, d // 2), jnp.uint32),
        compiler_params=pltpu.CompilerParams(dimension_semantics=("arbitrary",),
                                             vmem_limit_bytes=VMEM_LIMIT),
        name="experts",
    )(src_chunk, block_e, last_blk, xs, w_gate, w_up, w_down)


def _combine_kernel(src_ref, x1_ref, rw_ref, mod_ref, fg_ref, yr_hbm, out_ref, ybuf, sem, *, final_norm):
    i = pl.program_id(0)
    slot = i % 2

    @pl.when(i == 0)
    def _():
        _gather_chunks(src_ref, 0, SORT_CHUNKS, yr_hbm, ybuf.at[0], sem.at[0])

    @pl.when(i + 1 < pl.num_programs(0))
    def _():
        _gather_chunks(src_ref, (i + 1) * SORT_CHUNKS, SORT_CHUNKS, yr_hbm, ybuf.at[1 - slot], sem.at[1 - slot])

    rw = rw_ref[...]
    sp = lax.broadcasted_iota(jnp.int32, (1, SORT_ROWS), 1)
    p1 = rw[:, 2:3].astype(jnp.int32)
    p2 = rw[:, 3:4].astype(jnp.int32)
    wmat = jnp.where(sp == p1, rw[:, 0:1], jnp.where(sp == p2, rw[:, 1:2], 0.0)).astype(jnp.bfloat16)
    _wait_chunks(SORT_CHUNKS, yr_hbm, ybuf.at[slot], sem.at[slot])
    moe = _dot(wmat, _unpack_bf16_pairs(ybuf[slot]))
    x2 = x1_ref[...] + mod_ref[5:6, :] * moe
    if final_norm:
        x2 = x2 * lax.rsqrt(jnp.mean(x2 * x2, axis=-1, keepdims=True) + EPS) * fg_ref[...]
    out_ref[...] = x2


def _combine(src_chunk, x1, rw, mod, final_g, yr, seq, final_norm):
    t, d = x1.shape
    per_seq = seq // TM
    return pl.pallas_call(
        functools.partial(_combine_kernel, final_norm=final_norm),
        grid_spec=pltpu.PrefetchScalarGridSpec(
            num_scalar_prefetch=1,
            grid=(t // TM,),
            in_specs=[pl.BlockSpec((TM, d), lambda i, src: (i, 0)),
                      pl.BlockSpec((TM, LANES), lambda i, src: (i, 0)),
                      pl.BlockSpec((None, 6, d), lambda i, src: (i // per_seq, 0, 0)),
                      pl.BlockSpec((1, d), lambda i, src: (0, 0)),
                      pl.BlockSpec(memory_space=pl.ANY)],
            out_specs=pl.BlockSpec((TM, d), lambda i, src: (i, 0)),
            scratch_shapes=[pltpu.VMEM((2, SORT_ROWS, d // 2), jnp.uint32),
                            pltpu.SemaphoreType.DMA((2,))]),
        out_shape=jax.ShapeDtypeStruct((t, d), jnp.float32),
        compiler_params=pltpu.CompilerParams(dimension_semantics=("arbitrary",),
                                             vmem_limit_bytes=VMEM_LIMIT),
        name="combine",
    )(src_chunk, x1, rw, mod, final_g.reshape(1, d), yr)


def kernel(x, c, positions, rel_bias, w_ada, b_ada, norm1_g, w_in, sinks, gm_ln_g, gm_ln_b, gm_w_s, gm_b_s,
           p_a, p_b, w_o, norm2_g, w_router_g, b_router_g, w_router_e, b_router_e, w_gate, w_up, w_down,
           final_g):
    bsz, seq, d = x.shape
    t = bsz * seq
    bf16 = jnp.bfloat16
    depth = w_ada.shape[0]
    n_tiles = t // TM
    n_chunks = t * TOP_K // CHUNK + n_tiles * N_EXPERTS + N_EXPERTS * BLOCK_CHUNKS
    i32 = jnp.int32
    for l in range(depth):
        mod = _adaln_mod(c, w_ada[l], b_ada[l]).reshape(bsz, 6, d)
        pad = ROUTER_LANES - N_GROUPS - N_EXPERTS
        w_r = jnp.concatenate([w_router_g[l], w_router_e[l], jnp.zeros((d, pad), jnp.float32)], axis=1)
        b_r = jnp.concatenate([b_router_g[l], b_router_e[l], jnp.zeros((pad,), jnp.float32)]).reshape(1, -1)
        bsb = jnp.broadcast_to(gm_b_s[l][:, :, None], (GM_GROUPS, GM_CHUNK, LANES))
        x1, xs, rw, cnt = _mixer(
            x, mod, positions, rel_bias, norm1_g[l].reshape(1, d), w_in[l].astype(bf16), sinks[l],
            gm_ln_g[l].reshape(1, -1), gm_ln_b[l].reshape(1, -1), gm_w_s[l], bsb,
            p_a[l].astype(bf16), p_b[l].astype(bf16), w_o[l].astype(bf16), norm2_g[l].reshape(1, d),
            w_r.astype(bf16), b_r)

        n = cnt[:, 0, EXPERT_LANE0:EXPERT_LANE0 + N_EXPERTS].astype(i32)
        c = (n + CHUNK - 1) // CHUNK
        run0_tile = jnp.cumsum(c, axis=1) - c
        run0_exp = jnp.cumsum(c, axis=0) - c
        tot = jnp.sum(c, axis=0)
        seg = (tot + BLOCK_CHUNKS - 1) // BLOCK_CHUNKS * BLOCK_CHUNKS
        seg_end = jnp.cumsum(seg)
        seg0 = seg_end - seg

        pos_tile = jnp.arange(n_tiles, dtype=i32)[:, None] * SORT_CHUNKS + run0_tile
        pos_exp = seg0[None, :] + run0_exp
        shift = (pos_tile - pos_exp).reshape(-1)
        lo_exp, hi_exp = pos_exp.reshape(-1), (pos_exp + c).reshape(-1)

        j = jnp.arange(n_chunks, dtype=i32)[:, None]
        inside = (j >= lo_exp[None, :]) & (j < hi_exp[None, :])
        src_j = jnp.sum(jnp.where(inside, j + shift[None, :], 0), axis=1).astype(i32)
        blk0 = jnp.arange(n_chunks // BLOCK_CHUNKS, dtype=i32) * BLOCK_CHUNKS
        block_e = jnp.minimum(jnp.sum(blk0[:, None] >= seg_end[None, :], axis=1), N_EXPERTS - 1).astype(i32)
        last_blk = (jnp.maximum(seg_end[-1] // BLOCK_CHUNKS, 1) - 1).astype(i32).reshape(1)

        q = jnp.arange(SORT_CHUNKS, dtype=i32)[None, :, None]
        in_run = (q >= run0_tile[:, None, :]) & (q < (run0_tile + c)[:, None, :])
        dst_q = jnp.sum(jnp.where(in_run, q + (pos_exp - run0_tile)[:, None, :], 0), axis=2).astype(i32).reshape(-1)

        yr = _experts(src_j, block_e, last_blk, xs, w_gate[l], w_up[l], w_down[l])
        x = _combine(dst_q, x1.reshape(t, d), rw.reshape(t, LANES), mod, final_g, yr, seq,
                     final_norm=(l == depth - 1)).reshape(bsz, seq, d)
    return x
```

```python
import functools
import math

import jax
import jax.numpy as jnp
from jax import lax
from jax.experimental import pallas as pl
from jax.experimental.pallas import tpu as pltpu

D_MODEL = 1024
N_HEADS = 8
N_KV_HEADS = 2
HEAD_DIM = 64
BLOCK = 128
ATTN_Q = N_HEADS * HEAD_DIM
ATTN_KV = N_KV_HEADS * HEAD_DIM
N_BUCKETS = 32
MAX_EXACT = N_BUCKETS // 2
MAX_DISTANCE = 128
GM_WIDTH = 512
GM_GROUPS = 4
GM_CHUNK = 128
N_GROUPS = 4
EXPERTS_PER_GROUP = 8
N_EXPERTS = N_GROUPS * EXPERTS_PER_GROUP
TOP_K = 2
D_EXPERT = 512
EPS = 1e-6
NEG = -1e30

LANES = 128
ROUTER_LANES = LANES
EXPERT_LANE0 = N_GROUPS
TM = 512
BM = 512
CHUNK = 8
SORT_ROWS = TM * TOP_K + N_EXPERTS * CHUNK
SORT_CHUNKS = SORT_ROWS // CHUNK
BLOCK_CHUNKS = BM // CHUNK
GATHER_UNROLL = 8
VMEM_LIMIT = 56 * 1024 * 1024

Q0, K0, V0, GU0, GV0, GA0, GB0, IN_END = (0, 512, 640, 768, 1280, 1792, 2816, 3840)


def _dot(a, b):
    return jnp.dot(a, b, preferred_element_type=jnp.float32)


def _dot_nt(a, b):
    return lax.dot_general(a, b, (((1,), (1,)), ((), ())), preferred_element_type=jnp.float32)


def _gelu_tanh(x):
    return 0.5 * x * (1.0 + jnp.tanh(math.sqrt(2.0 / math.pi) * (x + 0.044715 * (x * x * x))))


def _sigmoid(x):
    return 1.0 / (1.0 + jnp.exp(-x))


def _pack_bf16_pairs(x):
    bits = lax.bitcast_convert_type(x, jnp.uint32)
    half = x.shape[1] // 2
    return (bits[:, 0:half] >> 16) | (bits[:, half:] & jnp.uint32(0xFFFF0000))


def _unpack_bf16_pairs(w):
    lo = lax.bitcast_convert_type(w << 16, jnp.float32)
    hi = lax.bitcast_convert_type(w & jnp.uint32(0xFFFF0000), jnp.float32)
    return jnp.concatenate([lo, hi], axis=1).astype(jnp.bfloat16)


def _adaln_kernel(c_ref, w_ref, b_ref, o_ref):
    c = c_ref[...]
    cs = c * _sigmoid(c)
    o_ref[...] = _dot(cs, w_ref[...]) + b_ref[...]


def _adaln_mod(c, w, b):
    bsz, d = c.shape
    n = w.shape[1]
    tn = 1024
    return pl.pallas_call(
        _adaln_kernel,
        grid=(n // tn,),
        in_specs=[pl.BlockSpec((bsz, d), lambda i: (0, 0)),
                  pl.BlockSpec((d, tn), lambda i: (0, i)),
                  pl.BlockSpec((1, tn), lambda i: (0, i))],
        out_specs=pl.BlockSpec((bsz, tn), lambda i: (0, i)),
        out_shape=jax.ShapeDtypeStruct((bsz, n), jnp.float32),
        compiler_params=pltpu.CompilerParams(dimension_semantics=("arbitrary",)),
        name="adaln_mod",
    )(c, w, b.reshape(1, n))


def _mixer_kernel(relb_ref, sinks_ref,
                  x_ref, mod_ref, pos_ref, n1g_ref, win_ref, lng_ref, lnb_ref, ws_ref, bsb_ref,
                  pa_ref, pb_ref, wo_ref, n2g_ref, wr_ref, br_ref,
                  x1_ref, xs_ref, rw_ref, cnt_ref,
                  kbuf, vbuf, pbuf, biasm, key, ya, yb, strict, sbuf, mbuf, flag):
    b = pl.program_id(0)
    j = pl.program_id(1)
    nblk = TM // BLOCK
    bf16 = jnp.bfloat16

    @pl.when((b == 0) & (j == 0))
    def _():
        key[...] = jnp.zeros_like(key)
        for i in range(nblk):
            flag[i] = 1
        tr = lax.broadcasted_iota(jnp.int32, (TM, TM), 0)
        tc = lax.broadcasted_iota(jnp.int32, (TM, TM), 1)
        strict[...] = jnp.where(tc < tr, 1.0, 0.0).astype(bf16)

    @pl.when(j == 0)
    def _():
        kbuf[0:BLOCK, :] = jnp.zeros((BLOCK, LANES), jnp.float32)
        vbuf[0:BLOCK, :] = jnp.zeros((BLOCK, LANES), jnp.float32)
        pbuf[:, 0:BLOCK] = jnp.zeros((1, BLOCK), jnp.int32)

    pbuf[:, BLOCK:] = pos_ref[...]
    qi = lax.broadcasted_iota(jnp.int32, (BLOCK, BLOCK), 0)
    kc = lax.broadcasted_iota(jnp.int32, (BLOCK, BLOCK), 1)
    from_prev = kc > qi
    rels, stales = [], []
    for i in range(nblk):
        pk = pbuf[:, i * BLOCK:(i + 2) * BLOCK]
        rel = pk - pk[:, BLOCK:BLOCK + 1]
        mismatch = jnp.max((rel != key[i]).astype(jnp.int32))
        stale = (mismatch != 0) | (flag[i] != 0)
        rels.append(rel)
        stales.append((stale | (j == 0)) if i == 0 else stale)
    for i in range(nblk):
        @pl.when(stales[i])
        def _():
            pk = pbuf[:, i * BLOCK:(i + 2) * BLOCK]
            pq = pk[:, BLOCK:]
            pq_col = jnp.transpose(jnp.broadcast_to(pq, (BLOCK, BLOCK)))
            no_prev = (j == 0) if i == 0 else None
            for hd in range(N_HEADS):
                biasm[i * N_HEADS + hd] = jnp.zeros((BLOCK, BLOCK), jnp.float32)
            for side in range(2):
                dist = pq_col - pk[:, side * BLOCK:(side + 1) * BLOCK]
                n = jnp.maximum(dist, 0)
                nf = jnp.maximum(n, 1).astype(jnp.float32)
                large = MAX_EXACT + (jnp.log(nf / MAX_EXACT) / math.log(MAX_DISTANCE / MAX_EXACT)
                                     * (N_BUCKETS - MAX_EXACT)).astype(jnp.int32)
                large = jnp.minimum(large, N_BUCKETS - 1)
                bucket = jnp.where(n < MAX_EXACT, n, large)
                use = from_prev if side == 0 else jnp.logical_not(from_prev)
                for hd in range(N_HEADS):
                    acc = jnp.zeros((BLOCK, BLOCK), jnp.float32)
                    for bk in range(N_BUCKETS):
                        acc = jnp.where(bucket == bk, relb_ref[bk * N_HEADS + hd], acc)
                    if side == 0 and i == 0:
                        acc = jnp.where(no_prev, NEG, acc)
                    slot = i * N_HEADS + hd
                    biasm[slot] = jnp.where(use, acc, biasm[slot])
            key[i] = rels[i]
            flag[i] = no_prev.astype(jnp.int32) if i == 0 else 0

    x = x_ref[...]
    sh1, sc1, g1 = mod_ref[0:1, :], mod_ref[1:2, :], mod_ref[2:3, :]
    sh2, sc2, g2 = mod_ref[3:4, :], mod_ref[4:5, :], mod_ref[5:6, :]

    xn = x * lax.rsqrt(jnp.mean(x * x, axis=-1, keepdims=True) + EPS)
    h = (xn * (n1g_ref[...] * (1.0 + sc1)) + sh1).astype(bf16)

    lane = lax.broadcasted_iota(jnp.int32, (1, LANES), 1)
    lo = lane < HEAD_DIM
    q = _dot(h, win_ref[:, Q0:K0]) * (HEAD_DIM ** -0.5)
    lo4 = jnp.concatenate([lo] * (ATTN_Q // LANES), axis=1)
    q_lo = jnp.where(lo4, q, 0.0).astype(bf16)
    q_hi = jnp.where(lo4, 0.0, q).astype(bf16)
    kv = _dot(h, win_ref[:, K0:GU0])
    kbuf[BLOCK:, :] = kv[:, 0:LANES]
    vbuf[BLOCK:, :] = kv[:, LANES:]
    kf = kbuf[...]
    vf = vbuf[...]
    kr = pltpu.roll(kf, HEAD_DIM, 1)
    vr = pltpu.roll(vf, HEAD_DIM, 1)
    kd = (jnp.where(lo, kf, kr).astype(bf16), jnp.where(lo, kr, kf).astype(bf16))
    v_lo = (jnp.where(lo, vf, 0.0).astype(bf16), jnp.where(lo, vr, 0.0).astype(bf16))
    v_hi = (jnp.where(lo, 0.0, vr).astype(bf16), jnp.where(lo, 0.0, vf).astype(bf16))

    ones_blk = jnp.ones((2 * BLOCK, LANES), bf16)

    def attend(i):
        rows = slice(i * BLOCK, (i + 1) * BLOCK)
        band = slice(i * BLOCK, (i + 2) * BLOCK)
        for hd in range(N_HEADS):
            kvh = hd // (N_HEADS // N_KV_HEADS)
            cols = slice(hd // 2 * LANES, (hd // 2 + 1) * LANES)
            qq = q_lo if hd % 2 == 0 else q_hi
            s2 = _dot_nt(qq[rows, cols], kd[kvh][band])
            s = jnp.where(from_prev, s2[:, 0:BLOCK], s2[:, BLOCK:]) + biasm[i * N_HEADS + hd]
            sbuf[hd] = s
            m = jnp.maximum(jnp.max(s, axis=-1, keepdims=True), sinks_ref[hd])
            mbuf[hd] = jnp.broadcast_to(m, (BLOCK, BLOCK))
        for pr in range(N_HEADS // 2):
            kvh = (2 * pr) // (N_HEADS // N_KV_HEADS)
            o = None
            for half, vv in enumerate((v_lo, v_hi)):
                hd = 2 * pr + half
                m = mbuf[hd]
                p = jnp.exp(sbuf[hd] - m)
                p2 = jnp.concatenate([jnp.where(from_prev, p, 0.0), jnp.where(from_prev, 0.0, p)], axis=1)
                r = _dot(p2.astype(bf16), jnp.concatenate([vv[kvh][band], ones_blk], axis=1))
                den = r[:, LANES:] + jnp.exp(sinks_ref[hd] - m)
                oh = r[:, 0:LANES] * (1.0 / den)
                o = oh if o is None else o + oh
            ya[rows, slice(pr * LANES, (pr + 1) * LANES)] = o.astype(bf16)

    u = _gelu_tanh(_dot(h, win_ref[:, GU0:GV0]))
    attend(0)
    vg = _gelu_tanh(_dot(h, win_ref[:, GV0:GA0]))
    mu = jnp.mean(vg, axis=-1, keepdims=True)
    vc = vg - mu
    var = jnp.mean(vc * vc, axis=-1, keepdims=True)
    vn = (vc * lax.rsqrt(var + EPS) * lng_ref[...] + lnb_ref[...]).astype(bf16)
    attend(1)
    ti = lax.broadcasted_iota(jnp.int32, (GM_CHUNK, GM_CHUNK), 0)
    si = lax.broadcasted_iota(jnp.int32, (GM_CHUNK, GM_CHUNK), 1)
    tril = si <= ti
    for g in range(GM_GROUPS):
        wg = jnp.where(tril, ws_ref[g], 0.0).astype(bf16)
        cols = slice(g * LANES, (g + 1) * LANES)
        for cidx in range(TM // GM_CHUNK):
            rows = slice(cidx * GM_CHUNK, (cidx + 1) * GM_CHUNK)
            sv = _dot(wg, vn[rows, cols]) + bsb_ref[g]
            yb[rows, cols] = (u[rows, cols] * sv).astype(bf16)
    gate_a = _sigmoid(_dot(h, win_ref[:, GA0:GB0]))
    attend(2)
    gate_b = _sigmoid(_dot(h, win_ref[:, GB0:IN_END]))
    attend(3)

    kbuf[0:BLOCK, :] = kv[TM - BLOCK:, 0:LANES]
    vbuf[0:BLOCK, :] = kv[TM - BLOCK:, LANES:]
    pbuf[:, 0:BLOCK] = pos_ref[:, TM - BLOCK:]

    merged = gate_a * _dot(ya[...], pa_ref[...]) + gate_b * _dot(yb[...], pb_ref[...])
    x1 = x + g1 * _dot(merged.astype(bf16), wo_ref[...])
    x1_ref[...] = x1

    xn2 = x1 * lax.rsqrt(jnp.mean(x1 * x1, axis=-1, keepdims=True) + EPS)
    h2 = (xn2 * (n2g_ref[...] * (1.0 + sc2)) + sh2).astype(bf16)
    logits = _dot(h2, wr_ref[...]) + br_ref[...]
    lane_f = lane.astype(jnp.float32)
    big = float(2 * LANES)
    is_grp = lane < N_GROUPS
    lg = jnp.where(is_grp, logits, NEG)
    lg_max = jnp.max(lg, axis=-1, keepdims=True)
    g_idx = jnp.min(jnp.where(lg == lg_max, lane_f, big), axis=-1, keepdims=True)
    p_g = 1.0 / jnp.sum(jnp.where(is_grp, jnp.exp(lg - lg_max), 0.0), axis=-1, keepdims=True)
    lane_grp = jnp.floor((lane_f - EXPERT_LANE0) * (1.0 / EXPERTS_PER_GROUP))
    in_grp = (lane >= EXPERT_LANE0) & (lane < EXPERT_LANE0 + N_EXPERTS) & (lane_grp == g_idx)
    le = jnp.where(in_grp, logits, NEG)
    m1 = jnp.max(le, axis=-1, keepdims=True)
    i1 = jnp.min(jnp.where(le == m1, lane_f, big), axis=-1, keepdims=True)
    oh1 = lane_f == i1
    le2 = jnp.where(oh1, NEG, le)
    m2 = jnp.max(le2, axis=-1, keepdims=True)
    i2 = jnp.min(jnp.where(le2 == m2, lane_f, big), axis=-1, keepdims=True)
    oh2 = lane_f == i2
    e2 = jnp.exp(m2 - m1)
    w1 = p_g / (1.0 + e2)
    w2 = p_g * e2 / (1.0 + e2)

    oh = jnp.where(oh1, 1.0, jnp.where(oh2, 1.0, 0.0))
    n_e = jnp.sum(oh, axis=0, keepdims=True)
    cnt_ref[...] = n_e
    padded = jnp.floor((n_e + (CHUNK - 1)) * (1.0 / CHUNK)) * CHUNK
    src_lane = lax.broadcasted_iota(jnp.int32, (LANES, LANES), 0)
    dst_lane = lax.broadcasted_iota(jnp.int32, (LANES, LANES), 1)
    upper = jnp.where(src_lane < dst_lane, 1.0, 0.0).astype(bf16)
    run0 = _dot(jnp.broadcast_to(padded, (8, LANES)).astype(bf16), upper)[0:1, :]
    slot = _dot(strict[...], oh.astype(bf16)) + run0
    pos1 = jnp.sum(jnp.where(oh1, slot, 0.0), axis=-1, keepdims=True)
    pos2 = jnp.sum(jnp.where(oh2, slot, 0.0), axis=-1, keepdims=True)
    rw_ref[...] = jnp.where(lane == 0, w1, jnp.where(lane == 1, w2,
                            jnp.where(lane == 2, pos1, jnp.where(lane == 3, pos2, 0.0))))

    pos_rows = jnp.transpose(jnp.where(lane == 0, pos1, jnp.where(lane == 1, pos2, 0.0)))
    p1 = pos_rows[0:1, :].astype(jnp.int32)
    p2 = pos_rows[1:2, :].astype(jnp.int32)
    sp = lax.broadcasted_iota(jnp.int32, (SORT_ROWS, TM), 0)
    perm = jnp.where(sp == p1, 1.0, jnp.where(sp == p2, 1.0, 0.0)).astype(bf16)
    xs_ref[...] = _pack_bf16_pairs(_dot(perm, h2))


def _mixer(x, mod, positions, rel_bias, n1g, w_in, sinks, lng, lnb, w_s, bsb, p_a, p_b, w_o, n2g, w_r, b_r):
    bsz, seq, d = x.shape
    nj = seq // TM
    const = lambda *shape: pl.BlockSpec(shape, lambda b, j: (0,) * len(shape), pipeline_mode=pl.Buffered(1))
    smem = pl.BlockSpec(memory_space=pltpu.SMEM)
    tile = lambda w: pl.BlockSpec((None, TM, w), lambda b, j: (b, j, 0))
    return pl.pallas_call(
        _mixer_kernel,
        grid=(bsz, nj),
        in_specs=[smem, smem,
                  tile(d),
                  pl.BlockSpec((None, 6, d), lambda b, j: (b, 0, 0)),
                  pl.BlockSpec((None, None, 1, TM), lambda b, j: (b, j, 0, 0)),
                  const(1, d), const(d, IN_END), const(1, GM_WIDTH), const(1, GM_WIDTH),
                  const(GM_GROUPS, GM_CHUNK, GM_CHUNK), const(GM_GROUPS, GM_CHUNK, LANES),
                  const(ATTN_Q, d), const(GM_WIDTH, d), const(d, d), const(1, d),
                  const(d, ROUTER_LANES), const(1, ROUTER_LANES)],
        out_specs=[tile(d),
                   pl.BlockSpec((SORT_ROWS, d // 2), lambda b, j: (b * nj + j, 0)),
                   tile(LANES),
                   pl.BlockSpec((None, 1, LANES), lambda b, j: (b * nj + j, 0, 0))],
        out_shape=[jax.ShapeDtypeStruct((bsz, seq, d), jnp.float32),
                   jax.ShapeDtypeStruct((bsz * nj * SORT_ROWS, d // 2), jnp.uint32),
                   jax.ShapeDtypeStruct((bsz, seq, LANES), jnp.float32),
                   jax.ShapeDtypeStruct((bsz * nj, 1, LANES), jnp.float32)],
        scratch_shapes=[pltpu.VMEM((TM + BLOCK, LANES), jnp.float32),
                        pltpu.VMEM((TM + BLOCK, LANES), jnp.float32),
                        pltpu.VMEM((1, TM + BLOCK), jnp.int32),
                        pltpu.VMEM((TM // BLOCK * N_HEADS, BLOCK, BLOCK), jnp.float32),
                        pltpu.VMEM((TM // BLOCK, 1, 2 * BLOCK), jnp.int32),
                        pltpu.VMEM((TM, ATTN_Q), jnp.bfloat16),
                        pltpu.VMEM((TM, GM_WIDTH), jnp.bfloat16),
                        pltpu.VMEM((TM, TM), jnp.bfloat16),
                        pltpu.VMEM((N_HEADS, BLOCK, BLOCK), jnp.float32),
                        pltpu.VMEM((N_HEADS, BLOCK, BLOCK), jnp.float32),
                        pltpu.SMEM((TM // BLOCK,), jnp.int32)],
        compiler_params=pltpu.CompilerParams(dimension_semantics=("arbitrary", "arbitrary"),
                                             vmem_limit_bytes=VMEM_LIMIT),
        name="mixer",
    )(rel_bias.reshape(-1), sinks, x, mod, positions.reshape(bsz, nj, 1, TM), n1g, w_in, lng, lnb,
      w_s, bsb, p_a, p_b, w_o, n2g, w_r, b_r)


def _gather_chunks(idx_ref, idx0, n, src_hbm, dst, sem):
    def issue(c, carry):
        src = pl.multiple_of(idx_ref[idx0 + c] * CHUNK, CHUNK)
        pltpu.make_async_copy(src_hbm.at[pl.ds(src, CHUNK)],
                              dst.at[pl.ds(pl.multiple_of(c * CHUNK, CHUNK), CHUNK)], sem).start()
        return carry

    lax.fori_loop(0, n, issue, 0, unroll=GATHER_UNROLL)


def _wait_chunks(n, src_hbm, dst, sem):
    pltpu.make_async_copy(src_hbm.at[pl.ds(0, n * CHUNK)], dst, sem).wait()


def _expert_kernel(src_ref, be_ref, lb_ref, xs_hbm, wg_ref, wu_ref, wd_ref, yr_ref, xbuf, wgb, wub, wdb, sem):
    i = pl.program_id(0)
    last = lb_ref[0]
    slot = i % 2

    @pl.when(i == 0)
    def _():
        _gather_chunks(src_ref, 0, BLOCK_CHUNKS, xs_hbm, xbuf.at[0], sem.at[0])

    @pl.when(i < last)
    def _():
        _gather_chunks(src_ref, (i + 1) * BLOCK_CHUNKS, BLOCK_CHUNKS, xs_hbm, xbuf.at[1 - slot], sem.at[1 - slot])

    @pl.when(i <= last)
    def _():
        changed = (i == 0) | (be_ref[i] != be_ref[jnp.maximum(i - 1, 0)])

        @pl.when(changed)
        def _():
            wgb[...] = wg_ref[...].astype(jnp.bfloat16)
            wub[...] = wu_ref[...].astype(jnp.bfloat16)
            wdb[...] = wd_ref[...].astype(jnp.bfloat16)

        _wait_chunks(BLOCK_CHUNKS, xs_hbm, xbuf.at[slot], sem.at[slot])
        xb = _unpack_bf16_pairs(xbuf[slot])
        hg = _dot(xb, wgb[...])
        hu = _dot(xb, wub[...])
        hid = (hg * _sigmoid(hg) * hu).astype(jnp.bfloat16)
        yr_ref[...] = _pack_bf16_pairs(_dot(hid, wdb[...]).astype(jnp.bfloat16).astype(jnp.float32))

    @pl.when(i > last)
    def _():
        yr_ref[...] = jnp.zeros_like(yr_ref)


def _experts(src_chunk, block_e, last_blk, xs, w_gate, w_up, w_down):
    d = w_gate.shape[1]
    n_rows = src_chunk.shape[0] * CHUNK

    def row_map(i, rt, be, lb):
        return (i, 0)

    def w_map(i, rt, be, lb):
        return (be[jnp.minimum(i, lb[0])], 0, 0)

    return pl.pallas_call(
        _expert_kernel,
        grid_spec=pltpu.PrefetchScalarGridSpec(
            num_scalar_prefetch=3,
            grid=(n_rows // BM,),
            in_specs=[pl.BlockSpec(memory_space=pl.ANY),
                      pl.BlockSpec((None, d, D_EXPERT), w_map),
                      pl.BlockSpec((None, d, D_EXPERT), w_map),
                      pl.BlockSpec((None, D_EXPERT, d), w_map)],
            out_specs=pl.BlockSpec((BM, d // 2), row_map),
            scratch_shapes=[pltpu.VMEM((2, BM, d // 2), jnp.uint32),
                            pltpu.VMEM((d, D_EXPERT), jnp.bfloat16),
                            pltpu.VMEM((d, D_EXPERT), jnp.bfloat16),
                            pltpu.VMEM((D_EXPERT, d), jnp.bfloat16),
                            pltpu.SemaphoreType.DMA((2,))]),
        out_shape=jax.ShapeDtypeStruct((n_rows, d // 2), jnp.uint32),
        compiler_params=pltpu.CompilerParams(dimension_semantics=("arbitrary",),
                                             vmem_limit_bytes=VMEM_LIMIT),
        name="experts",
    )(src_chunk, block_e, last_blk, xs, w_gate, w_up, w_down)


def _combine_kernel(src_ref, x1_ref, rw_ref, mod_ref, fg_ref, yr_hbm, out_ref, ybuf, sem, *, final_norm):
    i = pl.program_id(0)
    slot = i % 2

    @pl.when(i == 0)
    def _():
        _gather_chunks(src_ref, 0, SORT_CHUNKS, yr_hbm, ybuf.at[0], sem.at[0])

    @pl.when(i + 1 < pl.num_programs(0))
    def _():
        _gather_chunks(src_ref, (i + 1) * SORT_CHUNKS, SORT_CHUNKS, yr_hbm, ybuf.at[1 - slot], sem.at[1 - slot])

    rw = rw_ref[...]
    sp = lax.broadcasted_iota(jnp.int32, (1, SORT_ROWS), 1)
    p1 = rw[:, 2:3].astype(jnp.int32)
    p2 = rw[:, 3:4].astype(jnp.int32)
    wmat = jnp.where(sp == p1, rw[:, 0:1], jnp.where(sp == p2, rw[:, 1:2], 0.0)).astype(jnp.bfloat16)
    _wait_chunks(SORT_CHUNKS, yr_hbm, ybuf.at[slot], sem.at[slot])
    moe = _dot(wmat, _unpack_bf16_pairs(ybuf[slot]))
    x2 = x1_ref[...] + mod_ref[5:6, :] * moe
    if final_norm:
        x2 = x2 * lax.rsqrt(jnp.mean(x2 * x2, axis=-1, keepdims=True) + EPS) * fg_ref[...]
    out_ref[...] = x2


def _combine(src_chunk, x1, rw, mod, final_g, yr, seq, final_norm):
    t, d = x1.shape
    per_seq = seq // TM
    return pl.pallas_call(
        functools.partial(_combine_kernel, final_norm=final_norm),
        grid_spec=pltpu.PrefetchScalarGridSpec(
            num_scalar_prefetch=1,
            grid=(t // TM,),
            in_specs=[pl.BlockSpec((TM, d), lambda i, src: (i, 0)),
                      pl.BlockSpec((TM, LANES), lambda i, src: (i, 0)),
                      pl.BlockSpec((None, 6, d), lambda i, src: (i // per_seq, 0, 0)),
                      pl.BlockSpec((1, d), lambda i, src: (0, 0)),
                      pl.BlockSpec(memory_space=pl.ANY)],
            out_specs=pl.BlockSpec((TM, d), lambda i, src: (i, 0)),
            scratch_shapes=[pltpu.VMEM((2, SORT_ROWS, d // 2), jnp.uint32),
                            pltpu.SemaphoreType.DMA((2,))]),
        out_shape=jax.ShapeDtypeStruct((t, d), jnp.float32),
        compiler_params=pltpu.CompilerParams(dimension_semantics=("arbitrary",),
                                             vmem_limit_bytes=VMEM_LIMIT),
        name="combine",
    )(src_chunk, x1, rw, mod, final_g.reshape(1, d), yr)


def kernel(x, c, positions, rel_bias, w_ada, b_ada, norm1_g, w_in, sinks, gm_ln_g, gm_ln_b, gm_w_s, gm_b_s,
           p_a, p_b, w_o, norm2_g, w_router_g, b_router_g, w_router_e, b_router_e, w_gate, w_up, w_down,
           final_g):
    bsz, seq, d = x.shape
    t = bsz * seq
    bf16 = jnp.bfloat16
    depth = w_ada.shape[0]
    n_tiles = t // TM
    n_chunks = t * TOP_K // CHUNK + n_tiles * N_EXPERTS + N_EXPERTS * BLOCK_CHUNKS
    i32 = jnp.int32
    for l in range(depth):
        mod = _adaln_mod(c, w_ada[l], b_ada[l]).reshape(bsz, 6, d)
        pad = ROUTER_LANES - N_GROUPS - N_EXPERTS
        w_r = jnp.concatenate([w_router_g[l], w_router_e[l], jnp.zeros((d, pad), jnp.float32)], axis=1)
        b_r = jnp.concatenate([b_router_g[l], b_router_e[l], jnp.zeros((pad,), jnp.float32)]).reshape(1, -1)
        bsb = jnp.broadcast_to(gm_b_s[l][:, :, None], (GM_GROUPS, GM_CHUNK, LANES))
        x1, xs, rw, cnt = _mixer(
            x, mod, positions, rel_bias, norm1_g[l].reshape(1, d), w_in[l].astype(bf16), sinks[l],
            gm_ln_g[l].reshape(1, -1), gm_ln_b[l].reshape(1, -1), gm_w_s[l], bsb,
            p_a[l].astype(bf16), p_b[l].astype(bf16), w_o[l].astype(bf16), norm2_g[l].reshape(1, d),
            w_r.astype(bf16), b_r)

        n = cnt[:, 0, EXPERT_LANE0:EXPERT_LANE0 + N_EXPERTS].astype(i32)
        c = (n + CHUNK - 1) // CHUNK
        run0_tile = jnp.cumsum(c, axis=1) - c
        run0_exp = jnp.cumsum(c, axis=0) - c
        tot = jnp.sum(c, axis=0)
        seg = (tot + BLOCK_CHUNKS - 1) // BLOCK_CHUNKS * BLOCK_CHUNKS
        seg_end = jnp.cumsum(seg)
        seg0 = seg_end - seg

        pos_tile = jnp.arange(n_tiles, dtype=i32)[:, None] * SORT_CHUNKS + run0_tile
        pos_exp = seg0[None, :] + run0_exp
        shift = (pos_tile - pos_exp).reshape(-1)
        lo_exp, hi_exp = pos_exp.reshape(-1), (pos_exp + c).reshape(-1)

        j = jnp.arange(n_chunks, dtype=i32)[:, None]
        inside = (j >= lo_exp[None, :]) & (j < hi_exp[None, :])
        src_j = jnp.sum(jnp.where(inside, j + shift[None, :], 0), axis=1).astype(i32)
        blk0 = jnp.arange(n_chunks // BLOCK_CHUNKS, dtype=i32) * BLOCK_CHUNKS
        block_e = jnp.minimum(jnp.sum(blk0[:, None] >= seg_end[None, :], axis=1), N_EXPERTS - 1).astype(i32)
        last_blk = (jnp.maximum(seg_end[-1] // BLOCK_CHUNKS, 1) - 1).astype(i32).reshape(1)

        q = jnp.arange(SORT_CHUNKS, dtype=i32)[None, :, None]
        in_run = (q >= run0_tile[:, None, :]) & (q < (run0_tile + c)[:, None, :])
        dst_q = jnp.sum(jnp.where(in_run, q + (pos_exp - run0_tile)[:, None, :], 0), axis=2).astype(i32).reshape(-1)

        yr = _experts(src_j, block_e, last_blk, xs, w_gate[l], w_up[l], w_down[l])
        x = _combine(dst_q, x1.reshape(t, d), rw.reshape(t, LANES), mod, final_g, yr, seq,
                     final_norm=(l == depth - 1)).reshape(bsz, seq, d)
    return x
```

```python
import functools
import math

import jax
import jax.numpy as jnp
from jax import lax
from jax.experimental import pallas as pl
from jax.experimental.pallas import tpu as pltpu

D_MODEL = 1024
N_HEADS = 8
N_KV_HEADS = 2
HEAD_DIM = 64
BLOCK = 128
ATTN_Q = N_HEADS * HEAD_DIM
ATTN_KV = N_KV_HEADS * HEAD_DIM
N_BUCKETS = 32
MAX_EXACT = N_BUCKETS // 2
MAX_DISTANCE = 128
GM_WIDTH = 512
GM_GROUPS = 4
GM_CHUNK = 128
N_GROUPS = 4
EXPERTS_PER_GROUP = 8
N_EXPERTS = N_GROUPS * EXPERTS_PER_GROUP
TOP_K = 2
D_EXPERT = 512
EPS = 1e-6
NEG = -1e30

LANES = 128
ROUTER_LANES = LANES
EXPERT_LANE0 = N_GROUPS
TM = 512
BM = 512
ROUTER_ROWS = 48
ROUTE_ROWS = 8
CHUNK = 8
SORT_ROWS = TM * TOP_K + N_EXPERTS * CHUNK
SORT_CHUNKS = SORT_ROWS // CHUNK
BLOCK_CHUNKS = BM // CHUNK
GATHER_UNROLL = 8
VMEM_LIMIT = 56 * 1024 * 1024

Q0, K0, V0, GU0, GV0, GA0, GB0, IN_END = (0, 512, 640, 768, 1280, 1792, 2816, 3840)


def _dot(a, b):
    return jnp.dot(a, b, preferred_element_type=jnp.float32)


def _dot_nt(a, b):
    return lax.dot_general(a, b, (((1,), (1,)), ((), ())), preferred_element_type=jnp.float32)


def _gelu_tanh(x):
    return 0.5 * x * (1.0 + jnp.tanh(math.sqrt(2.0 / math.pi) * (x + 0.044715 * (x * x * x))))


def _sigmoid(x):
    return 1.0 / (1.0 + jnp.exp(-x))


def _pack_bf16_pairs(x):
    bits = lax.bitcast_convert_type(x, jnp.uint32)
    half = x.shape[1] // 2
    return (bits[:, 0:half] >> 16) | (bits[:, half:] & jnp.uint32(0xFFFF0000))


def _unpack_bf16_pairs(w):
    lo = lax.bitcast_convert_type(w << 16, jnp.float32)
    hi = lax.bitcast_convert_type(w & jnp.uint32(0xFFFF0000), jnp.float32)
    return jnp.concatenate([lo, hi], axis=1).astype(jnp.bfloat16)


def _adaln_kernel(c_ref, w_ref, b_ref, o_ref):
    c = c_ref[...]
    cs = c * _sigmoid(c)
    o_ref[...] = _dot(cs, w_ref[...]) + b_ref[...]


def _adaln_mod(c, w, b):
    bsz, d = c.shape
    n = w.shape[1]
    tn = 1024
    return pl.pallas_call(
        _adaln_kernel,
        grid=(n // tn,),
        in_specs=[pl.BlockSpec((bsz, d), lambda i: (0, 0)),
                  pl.BlockSpec((d, tn), lambda i: (0, i)),
                  pl.BlockSpec((1, tn), lambda i: (0, i))],
        out_specs=pl.BlockSpec((bsz, tn), lambda i: (0, i)),
        out_shape=jax.ShapeDtypeStruct((bsz, n), jnp.float32),
        compiler_params=pltpu.CompilerParams(dimension_semantics=("arbitrary",)),
        name="adaln_mod",
    )(c, w, b.reshape(1, n))


def _mixer_kernel(relb_ref, sinks_ref,
                  x_ref, mod_ref, pos_ref, n1g_ref, win_ref, lng_ref, lnb_ref, ws_ref, bsb_ref,
                  pa_ref, pb_ref, wo_ref, n2g_ref, wr_ref, br_ref,
                  x1_ref, xs_ref, rw_ref, cnt_ref,
                  kbuf, vbuf, pbuf, biasm, key, ya, yb, strict, sbuf, mbuf, flag):
    b = pl.program_id(0)
    j = pl.program_id(1)
    nblk = TM // BLOCK
    bf16 = jnp.bfloat16

    @pl.when((b == 0) & (j == 0))
    def _():
        key[...] = jnp.zeros_like(key)
        for i in range(nblk):
            flag[i] = 1
        tr = lax.broadcasted_iota(jnp.int32, (TM, TM), 0)
        tc = lax.broadcasted_iota(jnp.int32, (TM, TM), 1)
        strict[...] = jnp.where(tr < tc, 1.0, 0.0).astype(bf16)

    @pl.when(j == 0)
    def _():
        kbuf[0:BLOCK, :] = jnp.zeros((BLOCK, LANES), jnp.float32)
        vbuf[0:BLOCK, :] = jnp.zeros((BLOCK, LANES), jnp.float32)
        pbuf[:, 0:BLOCK] = jnp.zeros((1, BLOCK), jnp.int32)

    pbuf[:, BLOCK:] = pos_ref[...]
    qi = lax.broadcasted_iota(jnp.int32, (BLOCK, BLOCK), 0)
    kc = lax.broadcasted_iota(jnp.int32, (BLOCK, BLOCK), 1)
    from_prev = kc > qi
    rels, changed = [], []
    for i in range(nblk):
        pk = pbuf[:, i * BLOCK:(i + 2) * BLOCK]
        rels.append(pk - pk[:, BLOCK:BLOCK + 1])
        changed.append((rels[i] != key[i]).astype(jnp.int32))
    any_changed = jnp.max(functools.reduce(jnp.maximum, changed))
    any_flag = functools.reduce(jnp.maximum, [flag[i] for i in range(nblk)])
    refresh = (any_changed != 0) | (any_flag != 0) | (j == 0)
    for i in range(nblk):
        @pl.when(refresh)
        def _():
          stale = (jnp.max(changed[i]) != 0) | (flag[i] != 0)
          if i == 0:
              stale = stale | (j == 0)

          @pl.when(stale)
          def _():
            pk = pbuf[:, i * BLOCK:(i + 2) * BLOCK]
            pq = pk[:, BLOCK:]
            pq_col = jnp.transpose(jnp.broadcast_to(pq, (BLOCK, BLOCK)))
            no_prev = (j == 0) if i == 0 else None
            for hd in range(N_HEADS):
                biasm[i * N_HEADS + hd] = jnp.zeros((BLOCK, BLOCK), jnp.float32)
            for side in range(2):
                dist = pq_col - pk[:, side * BLOCK:(side + 1) * BLOCK]
                n = jnp.maximum(dist, 0)
                nf = jnp.maximum(n, 1).astype(jnp.float32)
                large = MAX_EXACT + (jnp.log(nf / MAX_EXACT) / math.log(MAX_DISTANCE / MAX_EXACT)
                                     * (N_BUCKETS - MAX_EXACT)).astype(jnp.int32)
                large = jnp.minimum(large, N_BUCKETS - 1)
                bucket = jnp.where(n < MAX_EXACT, n, large)
                use = from_prev if side == 0 else jnp.logical_not(from_prev)
                for hd in range(N_HEADS):
                    acc = jnp.zeros((BLOCK, BLOCK), jnp.float32)
                    for bk in range(N_BUCKETS):
                        acc = jnp.where(bucket == bk, relb_ref[bk * N_HEADS + hd], acc)
                    if side == 0 and i == 0:
                        acc = jnp.where(no_prev, NEG, acc)
                    slot = i * N_HEADS + hd
                    biasm[slot] = jnp.where(use, acc, biasm[slot])
            key[i] = rels[i]
            flag[i] = no_prev.astype(jnp.int32) if i == 0 else 0

    x = x_ref[...]
    sh1, sc1, g1 = mod_ref[0:1, :], mod_ref[1:2, :], mod_ref[2:3, :]
    sh2, sc2, g2 = mod_ref[3:4, :], mod_ref[4:5, :], mod_ref[5:6, :]

    xn = x * lax.rsqrt(jnp.mean(x * x, axis=-1, keepdims=True) + EPS)
    h = (xn * (n1g_ref[...] * (1.0 + sc1)) + sh1).astype(bf16)

    lane = lax.broadcasted_iota(jnp.int32, (1, LANES), 1)
    lo = lane < HEAD_DIM
    q = _dot(h, win_ref[:, Q0:K0]) * (HEAD_DIM ** -0.5)
    lo4 = jnp.concatenate([lo] * (ATTN_Q // LANES), axis=1)
    q_lo = jnp.where(lo4, q, 0.0).astype(bf16)
    q_hi = jnp.where(lo4, 0.0, q).astype(bf16)
    kv = _dot(h, win_ref[:, K0:GU0])
    kbuf[BLOCK:, :] = kv[:, 0:LANES]
    vbuf[BLOCK:, :] = kv[:, LANES:]
    kf = kbuf[...]
    vf = vbuf[...]
    kr = pltpu.roll(kf, HEAD_DIM, 1)
    vr = pltpu.roll(vf, HEAD_DIM, 1)
    kd = (jnp.where(lo, kf, kr).astype(bf16), jnp.where(lo, kr, kf).astype(bf16))
    v_lo = (jnp.where(lo, vf, 0.0).astype(bf16), jnp.where(lo, vr, 0.0).astype(bf16))
    v_hi = (jnp.where(lo, 0.0, vr).astype(bf16), jnp.where(lo, 0.0, vf).astype(bf16))

    ones_blk = jnp.ones((2 * BLOCK, LANES), bf16)

    def attend(i):
        rows = slice(i * BLOCK, (i + 1) * BLOCK)
        band = slice(i * BLOCK, (i + 2) * BLOCK)
        for hd in range(N_HEADS):
            kvh = hd // (N_HEADS // N_KV_HEADS)
            cols = slice(hd // 2 * LANES, (hd // 2 + 1) * LANES)
            qq = q_lo if hd % 2 == 0 else q_hi
            s2 = _dot_nt(qq[rows, cols], kd[kvh][band])
            s = jnp.where(from_prev, s2[:, 0:BLOCK], s2[:, BLOCK:]) + biasm[i * N_HEADS + hd]
            sbuf[hd] = s
            m = jnp.maximum(jnp.max(s, axis=-1, keepdims=True), sinks_ref[hd])
            mbuf[hd] = jnp.broadcast_to(m, (BLOCK, BLOCK))
        for pr in range(N_HEADS // 2):
            kvh = (2 * pr) // (N_HEADS // N_KV_HEADS)
            o = None
            for half, vv in enumerate((v_lo, v_hi)):
                hd = 2 * pr + half
                m = mbuf[hd]
                p = jnp.exp(sbuf[hd] - m)
                p2 = jnp.concatenate([jnp.where(from_prev, p, 0.0), jnp.where(from_prev, 0.0, p)], axis=1)
                r = _dot(p2.astype(bf16), jnp.concatenate([vv[kvh][band], ones_blk], axis=1))
                den = r[:, LANES:] + jnp.exp(sinks_ref[hd] - m)
                oh = r[:, 0:LANES] * (1.0 / den)
                o = oh if o is None else o + oh
            ya[rows, slice(pr * LANES, (pr + 1) * LANES)] = o.astype(bf16)

    u = _gelu_tanh(_dot(h, win_ref[:, GU0:GV0]))
    attend(0)
    vg = _gelu_tanh(_dot(h, win_ref[:, GV0:GA0]))
    mu = jnp.mean(vg, axis=-1, keepdims=True)
    vc = vg - mu
    var = jnp.mean(vc * vc, axis=-1, keepdims=True)
    vn = (vc * lax.rsqrt(var + EPS) * lng_ref[...] + lnb_ref[...]).astype(bf16)
    attend(1)
    ti = lax.broadcasted_iota(jnp.int32, (GM_CHUNK, GM_CHUNK), 0)
    si = lax.broadcasted_iota(jnp.int32, (GM_CHUNK, GM_CHUNK), 1)
    tril = si <= ti
    for g in range(GM_GROUPS):
        wg = jnp.where(tril, ws_ref[g], 0.0).astype(bf16)
        cols = slice(g * LANES, (g + 1) * LANES)
        for cidx in range(TM // GM_CHUNK):
            rows = slice(cidx * GM_CHUNK, (cidx + 1) * GM_CHUNK)
            sv = _dot(wg, vn[rows, cols]) + bsb_ref[g]
            yb[rows, cols] = (u[rows, cols] * sv).astype(bf16)
    gate_a = _sigmoid(_dot(h, win_ref[:, GA0:GB0]))
    attend(2)
    gate_b = _sigmoid(_dot(h, win_ref[:, GB0:IN_END]))
    attend(3)

    kbuf[0:BLOCK, :] = kv[TM - BLOCK:, 0:LANES]
    vbuf[0:BLOCK, :] = kv[TM - BLOCK:, LANES:]
    pbuf[:, 0:BLOCK] = pos_ref[:, TM - BLOCK:]

    merged = gate_a * _dot(ya[...], pa_ref[...]) + gate_b * _dot(yb[...], pb_ref[...])
    x1 = x + g1 * _dot(merged.astype(bf16), wo_ref[...])
    x1_ref[...] = x1

    xn2 = x1 * lax.rsqrt(jnp.mean(x1 * x1, axis=-1, keepdims=True) + EPS)
    h2 = (xn2 * (n2g_ref[...] * (1.0 + sc2)) + sh2).astype(bf16)
    logits = _dot(h2, wr_ref[...]) + br_ref[...]
    lt = jnp.transpose(logits)[0:ROUTER_ROWS, :]
    row = lax.broadcasted_iota(jnp.int32, (ROUTER_ROWS, TM), 0)
    row_f = row.astype(jnp.float32)
    big = float(2 * LANES)
    is_grp = row < N_GROUPS
    lg = jnp.where(is_grp, lt, NEG)
    lg_max = jnp.max(lg, axis=0, keepdims=True)
    g_idx = jnp.min(jnp.where(lg == lg_max, row_f, big), axis=0, keepdims=True)
    p_g = 1.0 / jnp.sum(jnp.where(is_grp, jnp.exp(lg - lg_max), 0.0), axis=0, keepdims=True)
    row_grp = jnp.floor((row_f - EXPERT_LANE0) * (1.0 / EXPERTS_PER_GROUP))
    in_grp = (row >= EXPERT_LANE0) & (row < EXPERT_LANE0 + N_EXPERTS) & (row_grp == g_idx)
    le = jnp.where(in_grp, lt, NEG)
    m1 = jnp.max(le, axis=0, keepdims=True)
    i1 = jnp.min(jnp.where(le == m1, row_f, big), axis=0, keepdims=True)
    oh1 = row_f == i1
    le2 = jnp.where(oh1, NEG, le)
    m2 = jnp.max(le2, axis=0, keepdims=True)
    i2 = jnp.min(jnp.where(le2 == m2, row_f, big), axis=0, keepdims=True)
    oh2 = row_f == i2
    e2 = jnp.exp(m2 - m1)
    w1 = p_g / (1.0 + e2)
    w2 = p_g * e2 / (1.0 + e2)

    oh = jnp.where(oh1, 1.0, jnp.where(oh2, 1.0, 0.0))
    n_e = jnp.sum(oh, axis=1, keepdims=True)
    cnt_ref[...] = n_e
    padded = jnp.floor((n_e + (CHUNK - 1)) * (1.0 / CHUNK)) * CHUNK
    dst_row = lax.broadcasted_iota(jnp.int32, (ROUTER_ROWS, ROUTER_ROWS), 0)
    src_row = lax.broadcasted_iota(jnp.int32, (ROUTER_ROWS, ROUTER_ROWS), 1)
    lower = jnp.where(src_row < dst_row, 1.0, 0.0).astype(bf16)
    run0 = _dot(lower, jnp.broadcast_to(padded, (ROUTER_ROWS, TM)).astype(bf16))
    slot = _dot(oh.astype(bf16), strict[...]) + run0
    pos1 = jnp.sum(jnp.where(oh1, slot, 0.0), axis=0, keepdims=True)
    pos2 = jnp.sum(jnp.where(oh2, slot, 0.0), axis=0, keepdims=True)
    rrow = lax.broadcasted_iota(jnp.int32, (ROUTE_ROWS, TM), 0)
    rw_ref[...] = jnp.where(rrow == 0, w1, jnp.where(rrow == 1, w2,
                            jnp.where(rrow == 2, pos1, jnp.where(rrow == 3, pos2, 0.0))))

    p1 = pos1.astype(jnp.int32)
    p2 = pos2.astype(jnp.int32)
    sp = lax.broadcasted_iota(jnp.int32, (SORT_ROWS, TM), 0)
    perm = jnp.where(sp == p1, 1.0, jnp.where(sp == p2, 1.0, 0.0)).astype(bf16)
    xs_ref[...] = _pack_bf16_pairs(_dot(perm, h2))


def _mixer(x, mod, positions, rel_bias, n1g, w_in, sinks, lng, lnb, w_s, bsb, p_a, p_b, w_o, n2g, w_r, b_r):
    bsz, seq, d = x.shape
    nj = seq // TM
    const = lambda *shape: pl.BlockSpec(shape, lambda b, j: (0,) * len(shape), pipeline_mode=pl.Buffered(1))
    smem = pl.BlockSpec(memory_space=pltpu.SMEM)
    tile = lambda w: pl.BlockSpec((None, TM, w), lambda b, j: (b, j, 0))
    return pl.pallas_call(
        _mixer_kernel,
        grid=(bsz, nj),
        in_specs=[smem, smem,
                  tile(d),
                  pl.BlockSpec((None, 6, d), lambda b, j: (b, 0, 0)),
                  pl.BlockSpec((None, None, 1, TM), lambda b, j: (b, j, 0, 0)),
                  const(1, d), const(d, IN_END), const(1, GM_WIDTH), const(1, GM_WIDTH),
                  const(GM_GROUPS, GM_CHUNK, GM_CHUNK), const(GM_GROUPS, GM_CHUNK, LANES),
                  const(ATTN_Q, d), const(GM_WIDTH, d), const(d, d), const(1, d),
                  const(d, ROUTER_LANES), const(1, ROUTER_LANES)],
        out_specs=[tile(d),
                   pl.BlockSpec((SORT_ROWS, d // 2), lambda b, j: (b * nj + j, 0)),
                   pl.BlockSpec((ROUTE_ROWS, TM), lambda b, j: (0, b * nj + j)),
                   pl.BlockSpec((None, ROUTER_ROWS, 1), lambda b, j: (b * nj + j, 0, 0))],
        out_shape=[jax.ShapeDtypeStruct((bsz, seq, d), jnp.float32),
                   jax.ShapeDtypeStruct((bsz * nj * SORT_ROWS, d // 2), jnp.uint32),
                   jax.ShapeDtypeStruct((ROUTE_ROWS, bsz * seq), jnp.float32),
                   jax.ShapeDtypeStruct((bsz * nj, ROUTER_ROWS, 1), jnp.float32)],
        scratch_shapes=[pltpu.VMEM((TM + BLOCK, LANES), jnp.float32),
                        pltpu.VMEM((TM + BLOCK, LANES), jnp.float32),
                        pltpu.VMEM((1, TM + BLOCK), jnp.int32),
                        pltpu.VMEM((TM // BLOCK * N_HEADS, BLOCK, BLOCK), jnp.float32),
                        pltpu.VMEM((TM // BLOCK, 1, 2 * BLOCK), jnp.int32),
                        pltpu.VMEM((TM, ATTN_Q), jnp.bfloat16),
                        pltpu.VMEM((TM, GM_WIDTH), jnp.bfloat16),
                        pltpu.VMEM((TM, TM), jnp.bfloat16),
                        pltpu.VMEM((N_HEADS, BLOCK, BLOCK), jnp.float32),
                        pltpu.VMEM((N_HEADS, BLOCK, BLOCK), jnp.float32),
                        pltpu.SMEM((TM // BLOCK,), jnp.int32)],
        compiler_params=pltpu.CompilerParams(dimension_semantics=("arbitrary", "arbitrary"),
                                             vmem_limit_bytes=VMEM_LIMIT),
        name="mixer",
    )(rel_bias.reshape(-1), sinks, x, mod, positions.reshape(bsz, nj, 1, TM), n1g, w_in, lng, lnb,
      w_s, bsb, p_a, p_b, w_o, n2g, w_r, b_r)


def _gather_chunks(idx_ref, idx0, n, src_hbm, dst, sem):
    def issue(c, carry):
        src = pl.multiple_of(idx_ref[idx0 + c] * CHUNK, CHUNK)
        pltpu.make_async_copy(src_hbm.at[pl.ds(src, CHUNK)],
                              dst.at[pl.ds(pl.multiple_of(c * CHUNK, CHUNK), CHUNK)], sem).start()
        return carry

    lax.fori_loop(0, n, issue, 0, unroll=GATHER_UNROLL)


def _wait_chunks(n, src_hbm, dst, sem):
    pltpu.make_async_copy(src_hbm.at[pl.ds(0, n * CHUNK)], dst, sem).wait()


def _expert_kernel(src_ref, be_ref, lb_ref, xs_hbm, wg_ref, wu_ref, wd_ref, yr_ref, xbuf, wgb, wub, wdb, sem):
    i = pl.program_id(0)
    last = lb_ref[0]
    slot = i % 2

    @pl.when(i == 0)
    def _():
        _gather_chunks(src_ref, 0, BLOCK_CHUNKS, xs_hbm, xbuf.at[0], sem.at[0])

    @pl.when(i < last)
    def _():
        _gather_chunks(src_ref, (i + 1) * BLOCK_CHUNKS, BLOCK_CHUNKS, xs_hbm, xbuf.at[1 - slot], sem.at[1 - slot])

    @pl.when(i <= last)
    def _():
        changed = (i == 0) | (be_ref[i] != be_ref[jnp.maximum(i - 1, 0)])

        @pl.when(changed)
        def _():
            wgb[...] = wg_ref[...].astype(jnp.bfloat16)
            wub[...] = wu_ref[...].astype(jnp.bfloat16)
            wdb[...] = wd_ref[...].astype(jnp.bfloat16)

        _wait_chunks(BLOCK_CHUNKS, xs_hbm, xbuf.at[slot], sem.at[slot])
        xb = _unpack_bf16_pairs(xbuf[slot])
        hg = _dot(xb, wgb[...])
        hu = _dot(xb, wub[...])
        hid = (hg * _sigmoid(hg) * hu).astype(jnp.bfloat16)
        yr_ref[...] = _pack_bf16_pairs(_dot(hid, wdb[...]).astype(jnp.bfloat16).astype(jnp.float32))

    @pl.when(i > last)
    def _():
        yr_ref[...] = jnp.zeros_like(yr_ref)


def _experts(src_chunk, block_e, last_blk, xs, w_gate, w_up, w_down):
    d = w_gate.shape[1]
    n_rows = src_chunk.shape[0] * CHUNK

    def row_map(i, rt, be, lb):
        return (i, 0)

    def w_map(i, rt, be, lb):
        return (be[jnp.minimum(i, lb[0])], 0, 0)

    return pl.pallas_call(
        _expert_kernel,
        grid_spec=pltpu.PrefetchScalarGridSpec(
            num_scalar_prefetch=3,
            grid=(n_rows // BM,),
            in_specs=[pl.BlockSpec(memory_space=pl.ANY),
                      pl.BlockSpec((None, d, D_EXPERT), w_map),
                      pl.BlockSpec((None, d, D_EXPERT), w_map),
                      pl.BlockSpec((None, D_EXPERT, d), w_map)],
            out_specs=pl.BlockSpec((BM, d // 2), row_map),
            scratch_shapes=[pltpu.VMEM((2, BM, d // 2), jnp.uint32),
                            pltpu.VMEM((d, D_EXPERT), jnp.bfloat16),
                            pltpu.VMEM((d, D_EXPERT), jnp.bfloat16),
                            pltpu.VMEM((D_EXPERT, d), jnp.bfloat16),
                            pltpu.SemaphoreType.DMA((2,))]),
        out_shape=jax.ShapeDtypeStruct((n_rows, d // 2), jnp.uint32),
        compiler_params=pltpu.CompilerParams(dimension_semantics=("arbitrary",),
                                             vmem_limit_bytes=VMEM_LIMIT),
        name="experts",
    )(src_chunk, block_e, last_blk, xs, w_gate, w_up, w_down)


def _combine_kernel(src_ref, x1_ref, rw_ref, mod_ref, fg_ref, yr_hbm, out_ref, ybuf, sem, *, final_norm):
    i = pl.program_id(0)
    slot = i % 2

    @pl.when(i == 0)
    def _():
        _gather_chunks(src_ref, 0, SORT_CHUNKS, yr_hbm, ybuf.at[0], sem.at[0])

    @pl.when(i + 1 < pl.num_programs(0))
    def _():
        _gather_chunks(src_ref, (i + 1) * SORT_CHUNKS, SORT_CHUNKS, yr_hbm, ybuf.at[1 - slot], sem.at[1 - slot])

    rows = jnp.concatenate([rw_ref[...], jnp.zeros((LANES - ROUTE_ROWS, TM), jnp.float32)], axis=0)
    rw = jnp.transpose(rows)
    sp = lax.broadcasted_iota(jnp.int32, (1, SORT_ROWS), 1)
    p1 = rw[:, 2:3].astype(jnp.int32)
    p2 = rw[:, 3:4].astype(jnp.int32)
    wmat = jnp.where(sp == p1, rw[:, 0:1], jnp.where(sp == p2, rw[:, 1:2], 0.0)).astype(jnp.bfloat16)
    _wait_chunks(SORT_CHUNKS, yr_hbm, ybuf.at[slot], sem.at[slot])
    moe = _dot(wmat, _unpack_bf16_pairs(ybuf[slot]))
    x2 = x1_ref[...] + mod_ref[5:6, :] * moe
    if final_norm:
        x2 = x2 * lax.rsqrt(jnp.mean(x2 * x2, axis=-1, keepdims=True) + EPS) * fg_ref[...]
    out_ref[...] = x2


def _combine(src_chunk, x1, rw, mod, final_g, yr, seq, final_norm):
    t, d = x1.shape
    per_seq = seq // TM
    return pl.pallas_call(
        functools.partial(_combine_kernel, final_norm=final_norm),
        grid_spec=pltpu.PrefetchScalarGridSpec(
            num_scalar_prefetch=1,
            grid=(t // TM,),
            in_specs=[pl.BlockSpec((TM, d), lambda i, src: (i, 0)),
                      pl.BlockSpec((ROUTE_ROWS, TM), lambda i, src: (0, i)),
                      pl.BlockSpec((None, 6, d), lambda i, src: (i // per_seq, 0, 0)),
                      pl.BlockSpec((1, d), lambda i, src: (0, 0)),
                      pl.BlockSpec(memory_space=pl.ANY)],
            out_specs=pl.BlockSpec((TM, d), lambda i, src: (i, 0)),
            scratch_shapes=[pltpu.VMEM((2, SORT_ROWS, d // 2), jnp.uint32),
                            pltpu.SemaphoreType.DMA((2,))]),
        out_shape=jax.ShapeDtypeStruct((t, d), jnp.float32),
        compiler_params=pltpu.CompilerParams(dimension_semantics=("arbitrary",),
                                             vmem_limit_bytes=VMEM_LIMIT),
        name="combine",
    )(src_chunk, x1, rw, mod, final_g.reshape(1, d), yr)


def kernel(x, c, positions, rel_bias, w_ada, b_ada, norm1_g, w_in, sinks, gm_ln_g, gm_ln_b, gm_w_s, gm_b_s,
           p_a, p_b, w_o, norm2_g, w_router_g, b_router_g, w_router_e, b_router_e, w_gate, w_up, w_down,
           final_g):
    bsz, seq, d = x.shape
    t = bsz * seq
    bf16 = jnp.bfloat16
    depth = w_ada.shape[0]
    n_tiles = t // TM
    n_chunks = t * TOP_K // CHUNK + n_tiles * N_EXPERTS + N_EXPERTS * BLOCK_CHUNKS
    i32 = jnp.int32
    for l in range(depth):
        mod = _adaln_mod(c, w_ada[l], b_ada[l]).reshape(bsz, 6, d)
        pad = ROUTER_LANES - N_GROUPS - N_EXPERTS
        w_r = jnp.concatenate([w_router_g[l], w_router_e[l], jnp.zeros((d, pad), jnp.float32)], axis=1)
        b_r = jnp.concatenate([b_router_g[l], b_router_e[l], jnp.zeros((pad,), jnp.float32)]).reshape(1, -1)
        bsb = jnp.broadcast_to(gm_b_s[l][:, :, None], (GM_GROUPS, GM_CHUNK, LANES))
        x1, xs, rw, cnt = _mixer(
            x, mod, positions, rel_bias, norm1_g[l].reshape(1, d), w_in[l].astype(bf16), sinks[l],
            gm_ln_g[l].reshape(1, -1), gm_ln_b[l].reshape(1, -1), gm_w_s[l], bsb,
            p_a[l].astype(bf16), p_b[l].astype(bf16), w_o[l].astype(bf16), norm2_g[l].reshape(1, d),
            w_r.astype(bf16), b_r)

        n = cnt[:, EXPERT_LANE0:EXPERT_LANE0 + N_EXPERTS, 0].astype(i32)
        c = (n + CHUNK - 1) // CHUNK
        run0_tile = jnp.cumsum(c, axis=1) - c
        run0_exp = jnp.cumsum(c, axis=0) - c
        tot = jnp.sum(c, axis=0)
        seg = (tot + BLOCK_CHUNKS - 1) // BLOCK_CHUNKS * BLOCK_CHUNKS
        seg_end = jnp.cumsum(seg)
        seg0 = seg_end - seg

        pos_tile = jnp.arange(n_tiles, dtype=i32)[:, None] * SORT_CHUNKS + run0_tile
        pos_exp = seg0[None, :] + run0_exp
        shift = (pos_tile - pos_exp).reshape(-1)
        lo_exp, hi_exp = pos_exp.reshape(-1), (pos_exp + c).reshape(-1)

        j = jnp.arange(n_chunks, dtype=i32)[:, None]
        inside = (j >= lo_exp[None, :]) & (j < hi_exp[None, :])
        src_j = jnp.sum(jnp.where(inside, j + shift[None, :], 0), axis=1).astype(i32)
        blk0 = jnp.arange(n_chunks // BLOCK_CHUNKS, dtype=i32) * BLOCK_CHUNKS
        block_e = jnp.minimum(jnp.sum(blk0[:, None] >= seg_end[None, :], axis=1), N_EXPERTS - 1).astype(i32)
        last_blk = (jnp.maximum(seg_end[-1] // BLOCK_CHUNKS, 1) - 1).astype(i32).reshape(1)

        q = jnp.arange(SORT_CHUNKS, dtype=i32)[None, :, None]
        in_run = (q >= run0_tile[:, None, :]) & (q < (run0_tile + c)[:, None, :])
        dst_q = jnp.sum(jnp.where(in_run, q + (pos_exp - run0_tile)[:, None, :], 0), axis=2).astype(i32).reshape(-1)

        yr = _experts(src_j, block_e, last_blk, xs, w_gate[l], w_up[l], w_down[l])
        x = _combine(dst_q, x1.reshape(t, d), rw, mod, final_g, yr, seq,
                     final_norm=(l == depth - 1)).reshape(bsz, seq, d)
    return x
```

```python
import functools
import math

import jax
import jax.numpy as jnp
from jax import lax
from jax.experimental import pallas as pl
from jax.experimental.pallas import tpu as pltpu

D_MODEL = 1024
N_HEADS = 8
N_KV_HEADS = 2
HEAD_DIM = 64
BLOCK = 128
ATTN_Q = N_HEADS * HEAD_DIM
ATTN_KV = N_KV_HEADS * HEAD_DIM
N_BUCKETS = 32
MAX_EXACT = N_BUCKETS // 2
MAX_DISTANCE = 128
GM_WIDTH = 512
GM_GROUPS = 4
GM_CHUNK = 128
N_GROUPS = 4
EXPERTS_PER_GROUP = 8
N_EXPERTS = N_GROUPS * EXPERTS_PER_GROUP
TOP_K = 2
D_EXPERT = 512
EPS = 1e-6
NEG = -1e30

LANES = 128
ROUTER_LANES = LANES
EXPERT_LANE0 = N_GROUPS
TM = 512
BM = 512
ROUTER_ROWS = 48
ROUTE_ROWS = 8
CHUNK = 8
SORT_ROWS = TM * TOP_K + N_EXPERTS * CHUNK
SORT_CHUNKS = SORT_ROWS // CHUNK
BLOCK_CHUNKS = BM // CHUNK
GATHER_UNROLL = 8
VMEM_LIMIT = 56 * 1024 * 1024

Q0, K0, V0, GU0, GV0, GA0, GB0, IN_END = (0, 512, 640, 768, 1280, 1792, 2816, 3840)


def _dot(a, b):
    return jnp.dot(a, b, preferred_element_type=jnp.float32)


def _dot_nt(a, b):
    return lax.dot_general(a, b, (((1,), (1,)), ((), ())), preferred_element_type=jnp.float32)


def _gelu_tanh(x):
    return 0.5 * x * (1.0 + jnp.tanh(math.sqrt(2.0 / math.pi) * (x + 0.044715 * (x * x * x))))


def _sigmoid(x):
    return 1.0 / (1.0 + jnp.exp(-x))


def _pack_bf16_pairs(x):
    bits = lax.bitcast_convert_type(x, jnp.uint32)
    half = x.shape[1] // 2
    return (bits[:, 0:half] >> 16) | (bits[:, half:] & jnp.uint32(0xFFFF0000))


def _unpack_bf16_pairs(w):
    lo = lax.bitcast_convert_type(w << 16, jnp.float32)
    hi = lax.bitcast_convert_type(w & jnp.uint32(0xFFFF0000), jnp.float32)
    return jnp.concatenate([lo, hi], axis=1).astype(jnp.bfloat16)


def _adaln_kernel(c_ref, w_ref, b_ref, o_ref):
    c = c_ref[...]
    cs = c * _sigmoid(c)
    o_ref[...] = _dot(cs, w_ref[...]) + b_ref[...]


def _adaln_mod(c, w, b):
    bsz, d = c.shape
    n = w.shape[1]
    tn = 1024
    return pl.pallas_call(
        _adaln_kernel,
        grid=(n // tn,),
        in_specs=[pl.BlockSpec((bsz, d), lambda i: (0, 0)),
                  pl.BlockSpec((d, tn), lambda i: (0, i)),
                  pl.BlockSpec((1, tn), lambda i: (0, i))],
        out_specs=pl.BlockSpec((bsz, tn), lambda i: (0, i)),
        out_shape=jax.ShapeDtypeStruct((bsz, n), jnp.float32),
        compiler_params=pltpu.CompilerParams(dimension_semantics=("arbitrary",)),
        name="adaln_mod",
    )(c, w, b.reshape(1, n))


def _mixer_kernel(relb_ref, sinks_ref, spos_ref,
                  x_ref, mod_ref, pos_ref, n1g_ref, win_ref, lng_ref, lnb_ref, ws_ref, bsb_ref,
                  pa_ref, pb_ref, wo_ref, n2g_ref, wr_ref, br_ref,
                  x1_ref, xs_ref, rw_ref, cnt_ref,
                  kbuf, vbuf, pbuf, biasm, key, ya, yb, strict, sbuf, mbuf, flag):
    b = pl.program_id(0)
    j = pl.program_id(1)
    nblk = TM // BLOCK
    bf16 = jnp.bfloat16

    @pl.when((b == 0) & (j == 0))
    def _():
        key[...] = jnp.zeros_like(key)
        for i in range(nblk):
            flag[i] = 1
        tr = lax.broadcasted_iota(jnp.int32, (TM, TM), 0)
        tc = lax.broadcasted_iota(jnp.int32, (TM, TM), 1)
        strict[...] = jnp.where(tr < tc, 1.0, 0.0).astype(bf16)

    @pl.when(j == 0)
    def _():
        kbuf[0:BLOCK, :] = jnp.zeros((BLOCK, LANES), jnp.float32)
        vbuf[0:BLOCK, :] = jnp.zeros((BLOCK, LANES), jnp.float32)
        pbuf[:, 0:BLOCK] = jnp.zeros((1, BLOCK), jnp.int32)

    pbuf[:, BLOCK:] = pos_ref[...]
    qi = lax.broadcasted_iota(jnp.int32, (BLOCK, BLOCK), 0)
    kc = lax.broadcasted_iota(jnp.int32, (BLOCK, BLOCK), 1)
    from_prev = kc > qi
    rels, changed = [], []
    tile0 = (b * pl.num_programs(1) + j) * TM
    for i in range(nblk):
        pk = pbuf[:, i * BLOCK:(i + 2) * BLOCK]
        rels.append(pk - spos_ref[tile0 + i * BLOCK])
        changed.append(jnp.where(rels[i] != key[i], 1.0, 0.0))
    any_changed = jnp.max(functools.reduce(jnp.maximum, changed))
    any_flag = functools.reduce(jnp.maximum, [flag[i] for i in range(nblk)])
    refresh = (any_changed != 0) | (any_flag != 0) | (j == 0)
    for i in range(nblk):
        @pl.when(refresh)
        def _():
          stale = (jnp.max(changed[i]) != 0) | (flag[i] != 0)
          if i == 0:
              stale = stale | (j == 0)

          @pl.when(stale)
          def _():
            pk = pbuf[:, i * BLOCK:(i + 2) * BLOCK]
            pq = pk[:, BLOCK:]
            pq_col = jnp.transpose(jnp.broadcast_to(pq, (BLOCK, BLOCK)))
            no_prev = (j == 0) if i == 0 else None
            for hd in range(N_HEADS):
                biasm[i * N_HEADS + hd] = jnp.zeros((BLOCK, BLOCK), jnp.float32)
            for side in range(2):
                dist = pq_col - pk[:, side * BLOCK:(side + 1) * BLOCK]
                n = jnp.maximum(dist, 0)
                nf = jnp.maximum(n, 1).astype(jnp.float32)
                large = MAX_EXACT + (jnp.log(nf / MAX_EXACT) / math.log(MAX_DISTANCE / MAX_EXACT)
                                     * (N_BUCKETS - MAX_EXACT)).astype(jnp.int32)
                large = jnp.minimum(large, N_BUCKETS - 1)
                bucket = jnp.where(n < MAX_EXACT, n, large)
                use = from_prev if side == 0 else jnp.logical_not(from_prev)
                for hd in range(N_HEADS):
                    acc = jnp.zeros((BLOCK, BLOCK), jnp.float32)
                    for bk in range(N_BUCKETS):
                        acc = jnp.where(bucket == bk, relb_ref[bk * N_HEADS + hd], acc)
                    if side == 0 and i == 0:
                        acc = jnp.where(no_prev, NEG, acc)
                    slot = i * N_HEADS + hd
                    biasm[slot] = jnp.where(use, acc, biasm[slot])
            key[i] = rels[i]
            flag[i] = no_prev.astype(jnp.int32) if i == 0 else 0

    x = x_ref[...]
    sh1, sc1, g1 = mod_ref[0:1, :], mod_ref[1:2, :], mod_ref[2:3, :]
    sh2, sc2, g2 = mod_ref[3:4, :], mod_ref[4:5, :], mod_ref[5:6, :]

    xn = x * lax.rsqrt(jnp.mean(x * x, axis=-1, keepdims=True) + EPS)
    h = (xn * (n1g_ref[...] * (1.0 + sc1)) + sh1).astype(bf16)

    lane = lax.broadcasted_iota(jnp.int32, (1, LANES), 1)
    lo = lane < HEAD_DIM
    q = _dot(h, win_ref[:, Q0:K0]) * (HEAD_DIM ** -0.5)
    lo4 = jnp.concatenate([lo] * (ATTN_Q // LANES), axis=1)
    q_lo = jnp.where(lo4, q, 0.0).astype(bf16)
    q_hi = jnp.where(lo4, 0.0, q).astype(bf16)
    kv = _dot(h, win_ref[:, K0:GU0])
    kbuf[BLOCK:, :] = kv[:, 0:LANES]
    vbuf[BLOCK:, :] = kv[:, LANES:]
    kf = kbuf[...]
    vf = vbuf[...]
    kr = pltpu.roll(kf, HEAD_DIM, 1)
    vr = pltpu.roll(vf, HEAD_DIM, 1)
    kd = (jnp.where(lo, kf, kr).astype(bf16), jnp.where(lo, kr, kf).astype(bf16))
    v_lo = (jnp.where(lo, vf, 0.0).astype(bf16), jnp.where(lo, vr, 0.0).astype(bf16))
    v_hi = (jnp.where(lo, 0.0, vr).astype(bf16), jnp.where(lo, 0.0, vf).astype(bf16))

    ones_blk = jnp.ones((2 * BLOCK, LANES), bf16)

    def attend(i):
        rows = slice(i * BLOCK, (i + 1) * BLOCK)
        band = slice(i * BLOCK, (i + 2) * BLOCK)
        for hd in range(N_HEADS):
            kvh = hd // (N_HEADS // N_KV_HEADS)
            cols = slice(hd // 2 * LANES, (hd // 2 + 1) * LANES)
            qq = q_lo if hd % 2 == 0 else q_hi
            s2 = _dot_nt(qq[rows, cols], kd[kvh][band])
            s = jnp.where(from_prev, s2[:, 0:BLOCK], s2[:, BLOCK:]) + biasm[i * N_HEADS + hd]
            sbuf[hd] = s
            m = jnp.maximum(jnp.max(s, axis=-1, keepdims=True), sinks_ref[hd])
            mbuf[hd] = jnp.broadcast_to(m, (BLOCK, BLOCK))
        for pr in range(N_HEADS // 2):
            kvh = (2 * pr) // (N_HEADS // N_KV_HEADS)
            o = None
            for half, vv in enumerate((v_lo, v_hi)):
                hd = 2 * pr + half
                m = mbuf[hd]
                p = jnp.exp(sbuf[hd] - m)
                p2 = jnp.concatenate([jnp.where(from_prev, p, 0.0), jnp.where(from_prev, 0.0, p)], axis=1)
                r = _dot(p2.astype(bf16), jnp.concatenate([vv[kvh][band], ones_blk], axis=1))
                den = r[:, LANES:] + jnp.exp(sinks_ref[hd] - m)
                oh = r[:, 0:LANES] * (1.0 / den)
                o = oh if o is None else o + oh
            ya[rows, slice(pr * LANES, (pr + 1) * LANES)] = o.astype(bf16)

    u = _gelu_tanh(_dot(h, win_ref[:, GU0:GV0]))
    attend(0)
    vg = _gelu_tanh(_dot(h, win_ref[:, GV0:GA0]))
    mu = jnp.mean(vg, axis=-1, keepdims=True)
    vc = vg - mu
    var = jnp.mean(vc * vc, axis=-1, keepdims=True)
    vn = (vc * lax.rsqrt(var + EPS) * lng_ref[...] + lnb_ref[...]).astype(bf16)
    attend(1)
    ti = lax.broadcasted_iota(jnp.int32, (GM_CHUNK, GM_CHUNK), 0)
    si = lax.broadcasted_iota(jnp.int32, (GM_CHUNK, GM_CHUNK), 1)
    tril = si <= ti
    for g in range(GM_GROUPS):
        wg = jnp.where(tril, ws_ref[g], 0.0).astype(bf16)
        cols = slice(g * LANES, (g + 1) * LANES)
        for cidx in range(TM // GM_CHUNK):
            rows = slice(cidx * GM_CHUNK, (cidx + 1) * GM_CHUNK)
            sv = _dot(wg, vn[rows, cols]) + bsb_ref[g]
            yb[rows, cols] = (u[rows, cols] * sv).astype(bf16)
    gate_a = _sigmoid(_dot(h, win_ref[:, GA0:GB0]))
    attend(2)
    gate_b = _sigmoid(_dot(h, win_ref[:, GB0:IN_END]))
    attend(3)

    kbuf[0:BLOCK, :] = kv[TM - BLOCK:, 0:LANES]
    vbuf[0:BLOCK, :] = kv[TM - BLOCK:, LANES:]
    pbuf[:, 0:BLOCK] = pos_ref[:, TM - BLOCK:]

    merged = gate_a * _dot(ya[...], pa_ref[...]) + gate_b * _dot(yb[...], pb_ref[...])
    x1 = x + g1 * _dot(merged.astype(bf16), wo_ref[...])
    x1_ref[...] = x1

    xn2 = x1 * lax.rsqrt(jnp.mean(x1 * x1, axis=-1, keepdims=True) + EPS)
    h2 = (xn2 * (n2g_ref[...] * (1.0 + sc2)) + sh2).astype(bf16)
    logits = _dot(h2, wr_ref[...]) + br_ref[...]
    lt = jnp.transpose(logits)[0:ROUTER_ROWS, :]
    row = lax.broadcasted_iota(jnp.int32, (ROUTER_ROWS, TM), 0)
    row_f = row.astype(jnp.float32)
    big = float(2 * LANES)
    is_grp = row < N_GROUPS
    lg = jnp.where(is_grp, lt, NEG)
    lg_max = jnp.max(lg, axis=0, keepdims=True)
    g_idx = jnp.min(jnp.where(lg == lg_max, row_f, big), axis=0, keepdims=True)
    p_g = 1.0 / jnp.sum(jnp.where(is_grp, jnp.exp(lg - lg_max), 0.0), axis=0, keepdims=True)
    row_grp = jnp.floor((row_f - EXPERT_LANE0) * (1.0 / EXPERTS_PER_GROUP))
    in_grp = (row >= EXPERT_LANE0) & (row < EXPERT_LANE0 + N_EXPERTS) & (row_grp == g_idx)
    le = jnp.where(in_grp, lt, NEG)
    m1 = jnp.max(le, axis=0, keepdims=True)
    i1 = jnp.min(jnp.where(le == m1, row_f, big), axis=0, keepdims=True)
    oh1 = row_f == i1
    le2 = jnp.where(oh1, NEG, le)
    m2 = jnp.max(le2, axis=0, keepdims=True)
    i2 = jnp.min(jnp.where(le2 == m2, row_f, big), axis=0, keepdims=True)
    oh2 = row_f == i2
    e2 = jnp.exp(m2 - m1)
    w1 = p_g / (1.0 + e2)
    w2 = p_g * e2 / (1.0 + e2)

    oh = jnp.where(oh1, 1.0, jnp.where(oh2, 1.0, 0.0))
    n_e = jnp.sum(oh, axis=1, keepdims=True)
    cnt_ref[...] = n_e
    padded = jnp.floor((n_e + (CHUNK - 1)) * (1.0 / CHUNK)) * CHUNK
    dst_row = lax.broadcasted_iota(jnp.int32, (ROUTER_ROWS, ROUTER_ROWS), 0)
    src_row = lax.broadcasted_iota(jnp.int32, (ROUTER_ROWS, ROUTER_ROWS), 1)
    lower = jnp.where(src_row < dst_row, 1.0, 0.0).astype(bf16)
    run0 = _dot(lower, jnp.broadcast_to(padded, (ROUTER_ROWS, TM)).astype(bf16))
    slot = _dot(oh.astype(bf16), strict[...]) + run0
    pos1 = jnp.sum(jnp.where(oh1, slot, 0.0), axis=0, keepdims=True)
    pos2 = jnp.sum(jnp.where(oh2, slot, 0.0), axis=0, keepdims=True)
    rrow = lax.broadcasted_iota(jnp.int32, (ROUTE_ROWS, TM), 0)
    rw_ref[...] = jnp.where(rrow == 0, w1, jnp.where(rrow == 1, w2,
                            jnp.where(rrow == 2, pos1, jnp.where(rrow == 3, pos2, 0.0))))

    p1 = pos1.astype(jnp.int32)
    p2 = pos2.astype(jnp.int32)
    sp = lax.broadcasted_iota(jnp.int32, (SORT_ROWS, TM), 0)
    perm = jnp.where(sp == p1, 1.0, jnp.where(sp == p2, 1.0, 0.0)).astype(bf16)
    xs_ref[...] = _pack_bf16_pairs(_dot(perm, h2))


def _mixer(x, mod, positions, rel_bias, n1g, w_in, sinks, lng, lnb, w_s, bsb, p_a, p_b, w_o, n2g, w_r, b_r):
    bsz, seq, d = x.shape
    nj = seq // TM
    const = lambda *shape: pl.BlockSpec(shape, lambda b, j: (0,) * len(shape), pipeline_mode=pl.Buffered(1))
    smem = pl.BlockSpec(memory_space=pltpu.SMEM)
    tile = lambda w: pl.BlockSpec((None, TM, w), lambda b, j: (b, j, 0))
    return pl.pallas_call(
        _mixer_kernel,
        grid=(bsz, nj),
        in_specs=[smem, smem, smem,
                  tile(d),
                  pl.BlockSpec((None, 6, d), lambda b, j: (b, 0, 0)),
                  pl.BlockSpec((None, None, 1, TM), lambda b, j: (b, j, 0, 0)),
                  const(1, d), const(d, IN_END), const(1, GM_WIDTH), const(1, GM_WIDTH),
                  const(GM_GROUPS, GM_CHUNK, GM_CHUNK), const(GM_GROUPS, GM_CHUNK, LANES),
                  const(ATTN_Q, d), const(GM_WIDTH, d), const(d, d), const(1, d),
                  const(d, ROUTER_LANES), const(1, ROUTER_LANES)],
        out_specs=[tile(d),
                   pl.BlockSpec((SORT_ROWS, d // 2), lambda b, j: (b * nj + j, 0)),
                   pl.BlockSpec((ROUTE_ROWS, TM), lambda b, j: (0, b * nj + j)),
                   pl.BlockSpec((None, ROUTER_ROWS, 1), lambda b, j: (b * nj + j, 0, 0))],
        out_shape=[jax.ShapeDtypeStruct((bsz, seq, d), jnp.float32),
                   jax.ShapeDtypeStruct((bsz * nj * SORT_ROWS, d // 2), jnp.uint32),
                   jax.ShapeDtypeStruct((ROUTE_ROWS, bsz * seq), jnp.float32),
                   jax.ShapeDtypeStruct((bsz * nj, ROUTER_ROWS, 1), jnp.float32)],
        scratch_shapes=[pltpu.VMEM((TM + BLOCK, LANES), jnp.float32),
                        pltpu.VMEM((TM + BLOCK, LANES), jnp.float32),
                        pltpu.VMEM((1, TM + BLOCK), jnp.int32),
                        pltpu.VMEM((TM // BLOCK * N_HEADS, BLOCK, BLOCK), jnp.float32),
                        pltpu.VMEM((TM // BLOCK, 1, 2 * BLOCK), jnp.int32),
                        pltpu.VMEM((TM, ATTN_Q), jnp.bfloat16),
                        pltpu.VMEM((TM, GM_WIDTH), jnp.bfloat16),
                        pltpu.VMEM((TM, TM), jnp.bfloat16),
                        pltpu.VMEM((N_HEADS, BLOCK, BLOCK), jnp.float32),
                        pltpu.VMEM((N_HEADS, BLOCK, BLOCK), jnp.float32),
                        pltpu.SMEM((TM // BLOCK,), jnp.int32)],
        compiler_params=pltpu.CompilerParams(dimension_semantics=("arbitrary", "arbitrary"),
                                             vmem_limit_bytes=VMEM_LIMIT),
        name="mixer",
    )(rel_bias.reshape(-1), sinks, positions.reshape(-1), x, mod, positions.reshape(bsz, nj, 1, TM), n1g, w_in, lng, lnb,
      w_s, bsb, p_a, p_b, w_o, n2g, w_r, b_r)


def _gather_chunks(idx_ref, idx0, n, src_hbm, dst, sem, inline=False):
    def issue(c, carry):
        src = pl.multiple_of(idx_ref[idx0 + c] * CHUNK, CHUNK)
        pltpu.make_async_copy(src_hbm.at[pl.ds(src, CHUNK)],
                              dst.at[pl.ds(pl.multiple_of(c * CHUNK, CHUNK), CHUNK)], sem).start()
        return carry

    if inline:
        for c in range(n):
            issue(c, 0)
    else:
        lax.fori_loop(0, n, issue, 0, unroll=GATHER_UNROLL)


def _wait_chunks(n, src_hbm, dst, sem):
    pltpu.make_async_copy(src_hbm.at[pl.ds(0, n * CHUNK)], dst, sem).wait()


def _expert_kernel(src_ref, be_ref, lb_ref, xs_hbm, wg_ref, wu_ref, wd_ref, yr_ref, xbuf, wgb, wub, wdb, sem):
    i = pl.program_id(0)
    last = lb_ref[0]
    slot = i % 2

    @pl.when(i == 0)
    def _():
        _gather_chunks(src_ref, 0, BLOCK_CHUNKS, xs_hbm, xbuf.at[0], sem.at[0])

    @pl.when(i <= last)
    def _():
        changed = (i == 0) | (be_ref[i] != be_ref[jnp.maximum(i - 1, 0)])

        @pl.when(changed)
        def _():
            wgb[...] = wg_ref[...].astype(jnp.bfloat16)
            wub[...] = wu_ref[...].astype(jnp.bfloat16)
            wdb[...] = wd_ref[...].astype(jnp.bfloat16)

        _wait_chunks(BLOCK_CHUNKS, xs_hbm, xbuf.at[slot], sem.at[slot])
        xb = _unpack_bf16_pairs(xbuf[slot])
        nxt = jnp.minimum(i + 1, last)
        _gather_chunks(src_ref, nxt * BLOCK_CHUNKS, BLOCK_CHUNKS, xs_hbm, xbuf.at[1 - slot], sem.at[1 - slot],
                       inline=True)
        hg = _dot(xb, wgb[...])
        hu = _dot(xb, wub[...])
        hid = (hg * _sigmoid(hg) * hu).astype(jnp.bfloat16)
        yr_ref[...] = _pack_bf16_pairs(_dot(hid, wdb[...]).astype(jnp.bfloat16).astype(jnp.float32))

        @pl.when(i == last)
        def _():
            _wait_chunks(BLOCK_CHUNKS, xs_hbm, xbuf.at[1 - slot], sem.at[1 - slot])

    @pl.when(i > last)
    def _():
        yr_ref[...] = jnp.zeros_like(yr_ref)


def _experts(src_chunk, block_e, last_blk, xs, w_gate, w_up, w_down):
    d = w_gate.shape[1]
    n_rows = src_chunk.shape[0] * CHUNK

    def row_map(i, rt, be, lb):
        return (i, 0)

    def w_map(i, rt, be, lb):
        return (be[jnp.minimum(i, lb[0])], 0, 0)

    return pl.pallas_call(
        _expert_kernel,
        grid_spec=pltpu.PrefetchScalarGridSpec(
            num_scalar_prefetch=3,
            grid=(n_rows // BM,),
            in_specs=[pl.BlockSpec(memory_space=pl.ANY),
                      pl.BlockSpec((None, d, D_EXPERT), w_map),
                      pl.BlockSpec((None, d, D_EXPERT), w_map),
                      pl.BlockSpec((None, D_EXPERT, d), w_map)],
            out_specs=pl.BlockSpec((BM, d // 2), row_map),
            scratch_shapes=[pltpu.VMEM((2, BM, d // 2), jnp.uint32),
                            pltpu.VMEM((d, D_EXPERT), jnp.bfloat16),
                            pltpu.VMEM((d, D_EXPERT), jnp.bfloat16),
                            pltpu.VMEM((D_EXPERT, d), jnp.bfloat16),
                            pltpu.SemaphoreType.DMA((2,))]),
        out_shape=jax.ShapeDtypeStruct((n_rows, d // 2), jnp.uint32),
        compiler_params=pltpu.CompilerParams(dimension_semantics=("arbitrary",),
                                             vmem_limit_bytes=VMEM_LIMIT),
        name="experts",
    )(src_chunk, block_e, last_blk, xs, w_gate, w_up, w_down)


def _combine_kernel(src_ref, x1_ref, rw_ref, mod_ref, fg_ref, yr_hbm, out_ref, ybuf, sem, *, final_norm):
    i = pl.program_id(0)
    slot = i % 2

    @pl.when(i == 0)
    def _():
        _gather_chunks(src_ref, 0, SORT_CHUNKS, yr_hbm, ybuf.at[0], sem.at[0])

    rows = jnp.concatenate([rw_ref[...], jnp.zeros((LANES - ROUTE_ROWS, TM), jnp.float32)], axis=0)
    rw = jnp.transpose(rows)
    sp = lax.broadcasted_iota(jnp.int32, (1, SORT_ROWS), 1)
    p1 = rw[:, 2:3].astype(jnp.int32)
    p2 = rw[:, 3:4].astype(jnp.int32)
    wmat = jnp.where(sp == p1, rw[:, 0:1], jnp.where(sp == p2, rw[:, 1:2], 0.0)).astype(jnp.bfloat16)
    _wait_chunks(SORT_CHUNKS, yr_hbm, ybuf.at[slot], sem.at[slot])
    ys = _unpack_bf16_pairs(ybuf[slot])
    nxt = jnp.minimum(i + 1, pl.num_programs(0) - 1)
    _gather_chunks(src_ref, nxt * SORT_CHUNKS, SORT_CHUNKS, yr_hbm, ybuf.at[1 - slot], sem.at[1 - slot],
                   inline=True)
    moe = _dot(wmat, ys)
    x2 = x1_ref[...] + mod_ref[5:6, :] * moe
    if final_norm:
        x2 = x2 * lax.rsqrt(jnp.mean(x2 * x2, axis=-1, keepdims=True) + EPS) * fg_ref[...]
    out_ref[...] = x2

    @pl.when(i == pl.num_programs(0) - 1)
    def _():
        _wait_chunks(SORT_CHUNKS, yr_hbm, ybuf.at[1 - slot], sem.at[1 - slot])


def _combine(src_chunk, x1, rw, mod, final_g, yr, seq, final_norm):
    t, d = x1.shape
    per_seq = seq // TM
    return pl.pallas_call(
        functools.partial(_combine_kernel, final_norm=final_norm),
        grid_spec=pltpu.PrefetchScalarGridSpec(
            num_scalar_prefetch=1,
            grid=(t // TM,),
            in_specs=[pl.BlockSpec((TM, d), lambda i, src: (i, 0)),
                      pl.BlockSpec((ROUTE_ROWS, TM), lambda i, src: (0, i)),
                      pl.BlockSpec((None, 6, d), lambda i, src: (i // per_seq, 0, 0)),
                      pl.BlockSpec((1, d), lambda i, src: (0, 0)),
                      pl.BlockSpec(memory_space=pl.ANY)],
            out_specs=pl.BlockSpec((TM, d), lambda i, src: (i, 0)),
            scratch_shapes=[pltpu.VMEM((2, SORT_ROWS, d // 2), jnp.uint32),
                            pltpu.SemaphoreType.DMA((2,))]),
        out_shape=jax.ShapeDtypeStruct((t, d), jnp.float32),
        compiler_params=pltpu.CompilerParams(dimension_semantics=("arbitrary",),
                                             vmem_limit_bytes=VMEM_LIMIT),
        name="combine",
    )(src_chunk, x1, rw, mod, final_g.reshape(1, d), yr)


def kernel(x, c, positions, rel_bias, w_ada, b_ada, norm1_g, w_in, sinks, gm_ln_g, gm_ln_b, gm_w_s, gm_b_s,
           p_a, p_b, w_o, norm2_g, w_router_g, b_router_g, w_router_e, b_router_e, w_gate, w_up, w_down,
           final_g):
    bsz, seq, d = x.shape
    t = bsz * seq
    bf16 = jnp.bfloat16
    depth = w_ada.shape[0]
    n_tiles = t // TM
    n_chunks = t * TOP_K // CHUNK + n_tiles * N_EXPERTS + N_EXPERTS * BLOCK_CHUNKS
    i32 = jnp.int32
    for l in range(depth):
        mod = _adaln_mod(c, w_ada[l], b_ada[l]).reshape(bsz, 6, d)
        pad = ROUTER_LANES - N_GROUPS - N_EXPERTS
        w_r = jnp.concatenate([w_router_g[l], w_router_e[l], jnp.zeros((d, pad), jnp.float32)], axis=1)
        b_r = jnp.concatenate([b_router_g[l], b_router_e[l], jnp.zeros((pad,), jnp.float32)]).reshape(1, -1)
        bsb = jnp.broadcast_to(gm_b_s[l][:, :, None], (GM_GROUPS, GM_CHUNK, LANES))
        x1, xs, rw, cnt = _mixer(
            x, mod, positions, rel_bias, norm1_g[l].reshape(1, d), w_in[l].astype(bf16), sinks[l],
            gm_ln_g[l].reshape(1, -1), gm_ln_b[l].reshape(1, -1), gm_w_s[l], bsb,
            p_a[l].astype(bf16), p_b[l].astype(bf16), w_o[l].astype(bf16), norm2_g[l].reshape(1, d),
            w_r.astype(bf16), b_r)

        n = cnt[:, EXPERT_LANE0:EXPERT_LANE0 + N_EXPERTS, 0].astype(i32)
        c = (n + CHUNK - 1) // CHUNK
        run0_tile = jnp.cumsum(c, axis=1) - c
        run0_exp = jnp.cumsum(c, axis=0) - c
        tot = jnp.sum(c, axis=0)
        seg = (tot + BLOCK_CHUNKS - 1) // BLOCK_CHUNKS * BLOCK_CHUNKS
        seg_end = jnp.cumsum(seg)
        seg0 = seg_end - seg

        pos_tile = jnp.arange(n_tiles, dtype=i32)[:, None] * SORT_CHUNKS + run0_tile
        pos_exp = seg0[None, :] + run0_exp
        shift = (pos_tile - pos_exp).reshape(-1)
        lo_exp, hi_exp = pos_exp.reshape(-1), (pos_exp + c).reshape(-1)

        j = jnp.arange(n_chunks, dtype=i32)[:, None]
        inside = (j >= lo_exp[None, :]) & (j < hi_exp[None, :])
        src_j = jnp.sum(jnp.where(inside, j + shift[None, :], 0), axis=1).astype(i32)
        blk0 = jnp.arange(n_chunks // BLOCK_CHUNKS, dtype=i32) * BLOCK_CHUNKS
        block_e = jnp.minimum(jnp.sum(blk0[:, None] >= seg_end[None, :], axis=1), N_EXPERTS - 1).astype(i32)
        last_blk = (jnp.maximum(seg_end[-1] // BLOCK_CHUNKS, 1) - 1).astype(i32).reshape(1)

        q = jnp.arange(SORT_CHUNKS, dtype=i32)[None, :, None]
        in_run = (q >= run0_tile[:, None, :]) & (q < (run0_tile + c)[:, None, :])
        dst_q = jnp.sum(jnp.where(in_run, q + (pos_exp - run0_tile)[:, None, :], 0), axis=2).astype(i32).reshape(-1)

        yr = _experts(src_j, block_e, last_blk, xs, w_gate[l], w_up[l], w_down[l])
        x = _combine(dst_q, x1.reshape(t, d), rw, mod, final_g, yr, seq,
                     final_norm=(l == depth - 1)).reshape(bsz, seq, d)
    return x
```

```python
import functools
import math

import jax
import jax.numpy as jnp
from jax import lax
from jax.experimental import pallas as pl
from jax.experimental.pallas import tpu as pltpu

D_MODEL = 1024
N_HEADS = 8
N_KV_HEADS = 2
HEAD_DIM = 64
BLOCK = 128
ATTN_Q = N_HEADS * HEAD_DIM
ATTN_KV = N_KV_HEADS * HEAD_DIM
N_BUCKETS = 32
MAX_EXACT = N_BUCKETS // 2
MAX_DISTANCE = 128
GM_WIDTH = 512
GM_GROUPS = 4
GM_CHUNK = 128
N_GROUPS = 4
EXPERTS_PER_GROUP = 8
N_EXPERTS = N_GROUPS * EXPERTS_PER_GROUP
TOP_K = 2
D_EXPERT = 512
EPS = 1e-6
NEG = -1e30

LANES = 128
ROUTER_LANES = LANES
EXPERT_LANE0 = N_GROUPS
TM = 512
BM = 512
ROUTER_ROWS = 48
ROUTE_ROWS = 8
CHUNK = 8
SORT_ROWS = TM * TOP_K + N_EXPERTS * CHUNK
SORT_CHUNKS = SORT_ROWS // CHUNK
BLOCK_CHUNKS = BM // CHUNK
GATHER_UNROLL = 8
VMEM_LIMIT = 56 * 1024 * 1024

Q0, K0, V0, GU0, GV0, GA0, GB0, IN_END = (0, 512, 640, 768, 1280, 1792, 2816, 3840)


def _dot(a, b):
    return jnp.dot(a, b, preferred_element_type=jnp.float32)


def _dot_nt(a, b):
    return lax.dot_general(a, b, (((1,), (1,)), ((), ())), preferred_element_type=jnp.float32)


def _gelu_tanh(x):
    return 0.5 * x * (1.0 + jnp.tanh(math.sqrt(2.0 / math.pi) * (x + 0.044715 * (x * x * x))))


def _sigmoid(x):
    return 1.0 / (1.0 + jnp.exp(-x))


def _pack_bf16_pairs(x):
    bits = lax.bitcast_convert_type(x, jnp.uint32)
    half = x.shape[1] // 2
    return (bits[:, 0:half] >> 16) | (bits[:, half:] & jnp.uint32(0xFFFF0000))


def _unpack_bf16_pairs(w):
    lo = lax.bitcast_convert_type(w << 16, jnp.float32)
    hi = lax.bitcast_convert_type(w & jnp.uint32(0xFFFF0000), jnp.float32)
    return jnp.concatenate([lo, hi], axis=1).astype(jnp.bfloat16)


def _adaln_kernel(c_ref, w_ref, b_ref, o_ref):
    c = c_ref[...]
    cs = c * _sigmoid(c)
    o_ref[...] = _dot(cs, w_ref[...]) + b_ref[...]


def _adaln_mod(c, w, b):
    bsz, d = c.shape
    n = w.shape[1]
    tn = 1024
    return pl.pallas_call(
        _adaln_kernel,
        grid=(n // tn,),
        in_specs=[pl.BlockSpec((bsz, d), lambda i: (0, 0)),
                  pl.BlockSpec((d, tn), lambda i: (0, i)),
                  pl.BlockSpec((1, tn), lambda i: (0, i))],
        out_specs=pl.BlockSpec((bsz, tn), lambda i: (0, i)),
        out_shape=jax.ShapeDtypeStruct((bsz, n), jnp.float32),
        compiler_params=pltpu.CompilerParams(dimension_semantics=("arbitrary",)),
        name="adaln_mod",
    )(c, w, b.reshape(1, n))


def _mixer_kernel(relb_ref, sinks_ref, spos_ref,
                  x_ref, mod_ref, pos_ref, n1g_ref, win_ref, lng_ref, lnb_ref, ws_ref, bsb_ref,
                  pa_ref, pb_ref, wo_ref, n2g_ref, wr_ref, br_ref,
                  x1_ref, xs_ref, rw_ref, cnt_ref,
                  kbuf, vbuf, pbuf, biasm, key, ya, yb, strict, sbuf, mbuf, flag):
    b = pl.program_id(0)
    j = pl.program_id(1)
    nblk = TM // BLOCK
    bf16 = jnp.bfloat16

    @pl.when((b == 0) & (j == 0))
    def _():
        key[...] = jnp.zeros_like(key)
        for i in range(nblk):
            flag[i] = 1
        tr = lax.broadcasted_iota(jnp.int32, (TM, TM), 0)
        tc = lax.broadcasted_iota(jnp.int32, (TM, TM), 1)
        strict[...] = jnp.where(tr < tc, 1.0, 0.0).astype(bf16)

    @pl.when(j == 0)
    def _():
        kbuf[0:BLOCK, :] = jnp.zeros((BLOCK, LANES), jnp.float32)
        vbuf[0:BLOCK, :] = jnp.zeros((BLOCK, LANES), jnp.float32)
        pbuf[:, 0:BLOCK] = jnp.zeros((1, BLOCK), jnp.int32)

    pbuf[:, BLOCK:] = pos_ref[...]
    qi = lax.broadcasted_iota(jnp.int32, (BLOCK, BLOCK), 0)
    kc = lax.broadcasted_iota(jnp.int32, (BLOCK, BLOCK), 1)
    from_prev = kc > qi
    rels, changed = [], []
    tile0 = (b * pl.num_programs(1) + j) * TM
    for i in range(nblk):
        pk = pbuf[:, i * BLOCK:(i + 2) * BLOCK]
        rels.append(pk - spos_ref[tile0 + i * BLOCK])
        changed.append(jnp.where(rels[i] != key[i], 1.0, 0.0))
    any_changed = jnp.max(functools.reduce(jnp.maximum, changed))
    any_flag = functools.reduce(jnp.maximum, [flag[i] for i in range(nblk)])
    refresh = (any_changed != 0) | (any_flag != 0) | (j == 0)
    for i in range(nblk):
        @pl.when(refresh)
        def _():
          stale = (jnp.max(changed[i]) != 0) | (flag[i] != 0)
          if i == 0:
              stale = stale | (j == 0)

          @pl.when(stale)
          def _():
            pk = pbuf[:, i * BLOCK:(i + 2) * BLOCK]
            pq = pk[:, BLOCK:]
            pq_col = jnp.transpose(jnp.broadcast_to(pq, (BLOCK, BLOCK)))
            no_prev = (j == 0) if i == 0 else None
            for hd in range(N_HEADS):
                biasm[i * N_HEADS + hd] = jnp.zeros((BLOCK, BLOCK), jnp.float32)
            for side in range(2):
                dist = pq_col - pk[:, side * BLOCK:(side + 1) * BLOCK]
                n = jnp.maximum(dist, 0)
                nf = jnp.maximum(n, 1).astype(jnp.float32)
                large = MAX_EXACT + (jnp.log(nf / MAX_EXACT) / math.log(MAX_DISTANCE / MAX_EXACT)
                                     * (N_BUCKETS - MAX_EXACT)).astype(jnp.int32)
                large = jnp.minimum(large, N_BUCKETS - 1)
                bucket = jnp.where(n < MAX_EXACT, n, large)
                use = from_prev if side == 0 else jnp.logical_not(from_prev)
                for hd in range(N_HEADS):
                    acc = jnp.zeros((BLOCK, BLOCK), jnp.float32)
                    for bk in range(N_BUCKETS):
                        acc = jnp.where(bucket == bk, relb_ref[bk * N_HEADS + hd], acc)
                    if side == 0 and i == 0:
                        acc = jnp.where(no_prev, NEG, acc)
                    slot = i * N_HEADS + hd
                    biasm[slot] = jnp.where(use, acc, biasm[slot])
            key[i] = rels[i]
            flag[i] = no_prev.astype(jnp.int32) if i == 0 else 0

    x = x_ref[...]
    sh1, sc1, g1 = mod_ref[0:1, :], mod_ref[1:2, :], mod_ref[2:3, :]
    sh2, sc2, g2 = mod_ref[3:4, :], mod_ref[4:5, :], mod_ref[5:6, :]

    xn = x * lax.rsqrt(jnp.mean(x * x, axis=-1, keepdims=True) + EPS)
    h = (xn * (n1g_ref[...] * (1.0 + sc1)) + sh1).astype(bf16)

    lane = lax.broadcasted_iota(jnp.int32, (1, LANES), 1)
    lo = lane < HEAD_DIM
    q = _dot(h, win_ref[:, Q0:K0]) * (HEAD_DIM ** -0.5)
    lo4 = jnp.concatenate([lo] * (ATTN_Q // LANES), axis=1)
    q_lo = jnp.where(lo4, q, 0.0).astype(bf16)
    q_hi = jnp.where(lo4, 0.0, q).astype(bf16)
    kv = _dot(h, win_ref[:, K0:GU0])
    kbuf[BLOCK:, :] = kv[:, 0:LANES]
    vbuf[BLOCK:, :] = kv[:, LANES:]
    kf = kbuf[...]
    vf = vbuf[...]
    kr = pltpu.roll(kf, HEAD_DIM, 1)
    vr = pltpu.roll(vf, HEAD_DIM, 1)
    kd = (jnp.where(lo, kf, kr).astype(bf16), jnp.where(lo, kr, kf).astype(bf16))
    v_lo = (jnp.where(lo, vf, 0.0).astype(bf16), jnp.where(lo, vr, 0.0).astype(bf16))
    v_hi = (jnp.where(lo, 0.0, vr).astype(bf16), jnp.where(lo, 0.0, vf).astype(bf16))

    ones_blk = jnp.ones((2 * BLOCK, LANES), bf16)

    def attend(i):
        rows = slice(i * BLOCK, (i + 1) * BLOCK)
        band = slice(i * BLOCK, (i + 2) * BLOCK)
        for hd in range(N_HEADS):
            kvh = hd // (N_HEADS // N_KV_HEADS)
            cols = slice(hd // 2 * LANES, (hd // 2 + 1) * LANES)
            qq = q_lo if hd % 2 == 0 else q_hi
            s2 = _dot_nt(qq[rows, cols], kd[kvh][band])
            s = jnp.where(from_prev, s2[:, 0:BLOCK], s2[:, BLOCK:]) + biasm[i * N_HEADS + hd]
            sbuf[hd] = s
            m = jnp.maximum(jnp.max(s, axis=-1, keepdims=True), sinks_ref[hd])
            mbuf[hd] = jnp.broadcast_to(m, (BLOCK, BLOCK))
        for pr in range(N_HEADS // 2):
            kvh = (2 * pr) // (N_HEADS // N_KV_HEADS)
            o = None
            for half, vv in enumerate((v_lo, v_hi)):
                hd = 2 * pr + half
                m = mbuf[hd]
                p = jnp.exp(sbuf[hd] - m)
                p2 = jnp.concatenate([jnp.where(from_prev, p, 0.0), jnp.where(from_prev, 0.0, p)], axis=1)
                r = _dot(p2.astype(bf16), jnp.concatenate([vv[kvh][band], ones_blk], axis=1))
                den = r[:, LANES:] + jnp.exp(sinks_ref[hd] - m)
                oh = r[:, 0:LANES] * (1.0 / den)
                o = oh if o is None else o + oh
            ya[rows, slice(pr * LANES, (pr + 1) * LANES)] = o.astype(bf16)

    u = _gelu_tanh(_dot(h, win_ref[:, GU0:GV0]))
    attend(0)
    vg = _gelu_tanh(_dot(h, win_ref[:, GV0:GA0]))
    mu = jnp.mean(vg, axis=-1, keepdims=True)
    vc = vg - mu
    var = jnp.mean(vc * vc, axis=-1, keepdims=True)
    vn = (vc * lax.rsqrt(var + EPS) * lng_ref[...] + lnb_ref[...]).astype(bf16)
    attend(1)
    ti = lax.broadcasted_iota(jnp.int32, (GM_CHUNK, GM_CHUNK), 0)
    si = lax.broadcasted_iota(jnp.int32, (GM_CHUNK, GM_CHUNK), 1)
    tril = si <= ti
    for g in range(GM_GROUPS):
        wg = jnp.where(tril, ws_ref[g], 0.0).astype(bf16)
        cols = slice(g * LANES, (g + 1) * LANES)
        for cidx in range(TM // GM_CHUNK):
            rows = slice(cidx * GM_CHUNK, (cidx + 1) * GM_CHUNK)
            sv = _dot(wg, vn[rows, cols]) + bsb_ref[g]
            yb[rows, cols] = (u[rows, cols] * sv).astype(bf16)
    gate_a = _sigmoid(_dot(h, win_ref[:, GA0:GB0]))
    attend(2)
    gate_b = _sigmoid(_dot(h, win_ref[:, GB0:IN_END]))
    attend(3)

    kbuf[0:BLOCK, :] = kv[TM - BLOCK:, 0:LANES]
    vbuf[0:BLOCK, :] = kv[TM - BLOCK:, LANES:]
    pbuf[:, 0:BLOCK] = pos_ref[:, TM - BLOCK:]

    merged = gate_a * _dot(ya[...], pa_ref[...]) + gate_b * _dot(yb[...], pb_ref[...])
    x1 = x + g1 * _dot(merged.astype(bf16), wo_ref[...])
    x1_ref[...] = x1

    xn2 = x1 * lax.rsqrt(jnp.mean(x1 * x1, axis=-1, keepdims=True) + EPS)
    h2 = (xn2 * (n2g_ref[...] * (1.0 + sc2)) + sh2).astype(bf16)
    logits = _dot(h2, wr_ref[...]) + br_ref[...]
    lt = jnp.transpose(logits)[0:ROUTER_ROWS, :]
    row = lax.broadcasted_iota(jnp.int32, (ROUTER_ROWS, TM), 0)
    row_f = row.astype(jnp.float32)
    big = float(2 * LANES)
    is_grp = row < N_GROUPS
    lg = jnp.where(is_grp, lt, NEG)
    lg_max = jnp.max(lg, axis=0, keepdims=True)
    g_idx = jnp.min(jnp.where(lg == lg_max, row_f, big), axis=0, keepdims=True)
    p_g = 1.0 / jnp.sum(jnp.where(is_grp, jnp.exp(lg - lg_max), 0.0), axis=0, keepdims=True)
    row_grp = jnp.floor((row_f - EXPERT_LANE0) * (1.0 / EXPERTS_PER_GROUP))
    in_grp = (row >= EXPERT_LANE0) & (row < EXPERT_LANE0 + N_EXPERTS) & (row_grp == g_idx)
    le = jnp.where(in_grp, lt, NEG)
    m1 = jnp.max(le, axis=0, keepdims=True)
    i1 = jnp.min(jnp.where(le == m1, row_f, big), axis=0, keepdims=True)
    oh1 = row_f == i1
    le2 = jnp.where(oh1, NEG, le)
    m2 = jnp.max(le2, axis=0, keepdims=True)
    i2 = jnp.min(jnp.where(le2 == m2, row_f, big), axis=0, keepdims=True)
    oh2 = row_f == i2
    e2 = jnp.exp(m2 - m1)
    w1 = p_g / (1.0 + e2)
    w2 = p_g * e2 / (1.0 + e2)

    oh = jnp.where(oh1, 1.0, jnp.where(oh2, 1.0, 0.0))
    n_e = jnp.sum(oh, axis=1, keepdims=True)
    cnt_ref[...] = n_e
    padded = jnp.floor((n_e + (CHUNK - 1)) * (1.0 / CHUNK)) * CHUNK
    dst_row = lax.broadcasted_iota(jnp.int32, (ROUTER_ROWS, ROUTER_ROWS), 0)
    src_row = lax.broadcasted_iota(jnp.int32, (ROUTER_ROWS, ROUTER_ROWS), 1)
    lower = jnp.where(src_row < dst_row, 1.0, 0.0).astype(bf16)
    run0 = _dot(lower, jnp.broadcast_to(padded, (ROUTER_ROWS, TM)).astype(bf16))
    slot = _dot(oh.astype(bf16), strict[...]) + run0
    pos1 = jnp.sum(jnp.where(oh1, slot, 0.0), axis=0, keepdims=True)
    pos2 = jnp.sum(jnp.where(oh2, slot, 0.0), axis=0, keepdims=True)
    rrow = lax.broadcasted_iota(jnp.int32, (ROUTE_ROWS, TM), 0)
    rw_ref[...] = jnp.where(rrow == 0, w1, jnp.where(rrow == 1, w2,
                            jnp.where(rrow == 2, pos1, jnp.where(rrow == 3, pos2, 0.0))))

    p1 = pos1.astype(jnp.int32)
    p2 = pos2.astype(jnp.int32)
    sp = lax.broadcasted_iota(jnp.int32, (SORT_ROWS, TM), 0)
    perm = jnp.where(sp == p1, 1.0, jnp.where(sp == p2, 1.0, 0.0)).astype(bf16)
    xs_ref[...] = _pack_bf16_pairs(_dot(perm, h2))


def _mixer(x, mod, positions, rel_bias, n1g, w_in, sinks, lng, lnb, w_s, bsb, p_a, p_b, w_o, n2g, w_r, b_r):
    bsz, seq, d = x.shape
    nj = seq // TM
    const = lambda *shape: pl.BlockSpec(shape, lambda b, j: (0,) * len(shape), pipeline_mode=pl.Buffered(1))
    smem = pl.BlockSpec(memory_space=pltpu.SMEM)
    tile = lambda w: pl.BlockSpec((None, TM, w), lambda b, j: (b, j, 0))
    return pl.pallas_call(
        _mixer_kernel,
        grid=(bsz, nj),
        in_specs=[smem, smem, smem,
                  tile(d),
                  pl.BlockSpec((None, 6, d), lambda b, j: (b, 0, 0)),
                  pl.BlockSpec((None, None, 1, TM), lambda b, j: (b, j, 0, 0)),
                  const(1, d), const(d, IN_END), const(1, GM_WIDTH), const(1, GM_WIDTH),
                  const(GM_GROUPS, GM_CHUNK, GM_CHUNK), const(GM_GROUPS, GM_CHUNK, LANES),
                  const(ATTN_Q, d), const(GM_WIDTH, d), const(d, d), const(1, d),
                  const(d, ROUTER_LANES), const(1, ROUTER_LANES)],
        out_specs=[tile(d),
                   pl.BlockSpec((SORT_ROWS, d // 2), lambda b, j: (b * nj + j, 0)),
                   pl.BlockSpec((ROUTE_ROWS, TM), lambda b, j: (0, b * nj + j)),
                   pl.BlockSpec((None, ROUTER_ROWS, 1), lambda b, j: (b * nj + j, 0, 0))],
        out_shape=[jax.ShapeDtypeStruct((bsz, seq, d), jnp.float32),
                   jax.ShapeDtypeStruct((bsz * nj * SORT_ROWS, d // 2), jnp.uint32),
                   jax.ShapeDtypeStruct((ROUTE_ROWS, bsz * seq), jnp.float32),
                   jax.ShapeDtypeStruct((bsz * nj, ROUTER_ROWS, 1), jnp.float32)],
        scratch_shapes=[pltpu.VMEM((TM + BLOCK, LANES), jnp.float32),
                        pltpu.VMEM((TM + BLOCK, LANES), jnp.float32),
                        pltpu.VMEM((1, TM + BLOCK), jnp.int32),
                        pltpu.VMEM((TM // BLOCK * N_HEADS, BLOCK, BLOCK), jnp.float32),
                        pltpu.VMEM((TM // BLOCK, 1, 2 * BLOCK), jnp.int32),
                        pltpu.VMEM((TM, ATTN_Q), jnp.bfloat16),
                        pltpu.VMEM((TM, GM_WIDTH), jnp.bfloat16),
                        pltpu.VMEM((TM, TM), jnp.bfloat16),
                        pltpu.VMEM((N_HEADS, BLOCK, BLOCK), jnp.float32),
                        pltpu.VMEM((N_HEADS, BLOCK, BLOCK), jnp.float32),
                        pltpu.SMEM((TM // BLOCK,), jnp.int32)],
        compiler_params=pltpu.CompilerParams(dimension_semantics=("arbitrary", "arbitrary"),
                                             vmem_limit_bytes=VMEM_LIMIT),
        name="mixer",
    )(rel_bias.reshape(-1), sinks, positions.reshape(-1), x, mod, positions.reshape(bsz, nj, 1, TM), n1g, w_in, lng, lnb,
      w_s, bsb, p_a, p_b, w_o, n2g, w_r, b_r)


def _gather_chunks(idx_ref, idx0, n, src_hbm, dst, sem, inline=False):
    def issue(c, carry):
        src = pl.multiple_of(idx_ref[idx0 + c] * CHUNK, CHUNK)
        pltpu.make_async_copy(src_hbm.at[pl.ds(src, CHUNK)],
                              dst.at[pl.ds(pl.multiple_of(c * CHUNK, CHUNK), CHUNK)], sem).start()
        return carry

    if inline:
        for c in range(n):
            issue(c, 0)
    else:
        lax.fori_loop(0, n, issue, 0, unroll=GATHER_UNROLL)


def _wait_chunks(n, src_hbm, dst, sem):
    pltpu.make_async_copy(src_hbm.at[pl.ds(0, n * CHUNK)], dst, sem).wait()


def _expert_kernel(src_ref, be_ref, lb_ref, xs_hbm, wg_ref, wu_ref, wd_ref, yr_ref, xbuf, wgb, wub, wdb, sem):
    i = pl.program_id(0)
    last = lb_ref[0]
    slot = i % 2

    @pl.when(i == 0)
    def _():
        _gather_chunks(src_ref, 0, BLOCK_CHUNKS, xs_hbm, xbuf.at[0], sem.at[0])

    @pl.when(i <= last)
    def _():
        changed = (i == 0) | (be_ref[i] != be_ref[jnp.maximum(i - 1, 0)])

        @pl.when(changed)
        def _():
            wgb[...] = wg_ref[...].astype(jnp.bfloat16)
            wub[...] = wu_ref[...].astype(jnp.bfloat16)
            wdb[...] = wd_ref[...].astype(jnp.bfloat16)

        nxt = jnp.minimum(i + 1, last)
        _gather_chunks(src_ref, nxt * BLOCK_CHUNKS, BLOCK_CHUNKS, xs_hbm, xbuf.at[1 - slot], sem.at[1 - slot],
                       inline=True)
        _wait_chunks(BLOCK_CHUNKS, xs_hbm, xbuf.at[slot], sem.at[slot])
        xb = _unpack_bf16_pairs(xbuf[slot])
        hg = _dot(xb, wgb[...])
        hu = _dot(xb, wub[...])
        hid = (hg * _sigmoid(hg) * hu).astype(jnp.bfloat16)
        yr_ref[...] = _pack_bf16_pairs(_dot(hid, wdb[...]).astype(jnp.bfloat16).astype(jnp.float32))

        @pl.when(i == last)
        def _():
            _wait_chunks(BLOCK_CHUNKS, xs_hbm, xbuf.at[1 - slot], sem.at[1 - slot])

    @pl.when(i > last)
    def _():
        yr_ref[...] = jnp.zeros_like(yr_ref)


def _experts(src_chunk, block_e, last_blk, xs, w_gate, w_up, w_down):
    d = w_gate.shape[1]
    n_rows = src_chunk.shape[0] * CHUNK

    def row_map(i, rt, be, lb):
        return (i, 0)

    def w_map(i, rt, be, lb):
        return (be[jnp.minimum(i, lb[0])], 0, 0)

    return pl.pallas_call(
        _expert_kernel,
        grid_spec=pltpu.PrefetchScalarGridSpec(
            num_scalar_prefetch=3,
            grid=(n_rows // BM,),
            in_specs=[pl.BlockSpec(memory_space=pl.ANY),
                      pl.BlockSpec((None, d, D_EXPERT), w_map),
                      pl.BlockSpec((None, d, D_EXPERT), w_map),
                      pl.BlockSpec((None, D_EXPERT, d), w_map)],
            out_specs=pl.BlockSpec((BM, d // 2), row_map),
            scratch_shapes=[pltpu.VMEM((2, BM, d // 2), jnp.uint32),
                            pltpu.VMEM((d, D_EXPERT), jnp.bfloat16),
                            pltpu.VMEM((d, D_EXPERT), jnp.bfloat16),
                            pltpu.VMEM((D_EXPERT, d), jnp.bfloat16),
                            pltpu.SemaphoreType.DMA((2,))]),
        out_shape=jax.ShapeDtypeStruct((n_rows, d // 2), jnp.uint32),
        compiler_params=pltpu.CompilerParams(dimension_semantics=("arbitrary",),
                                             vmem_limit_bytes=VMEM_LIMIT),
        name="experts",
    )(src_chunk, block_e, last_blk, xs, w_gate, w_up, w_down)


def _combine_kernel(src_ref, x1_ref, rw_ref, mod_ref, fg_ref, yr_hbm, out_ref, ybuf, sem, *, final_norm):
    i = pl.program_id(0)
    slot = i % 2

    @pl.when(i == 0)
    def _():
        _gather_chunks(src_ref, 0, SORT_CHUNKS, yr_hbm, ybuf.at[0], sem.at[0])

    rows = jnp.concatenate([rw_ref[...], jnp.zeros((LANES - ROUTE_ROWS, TM), jnp.float32)], axis=0)
    rw = jnp.transpose(rows)
    sp = lax.broadcasted_iota(jnp.int32, (1, SORT_ROWS), 1)
    p1 = rw[:, 2:3].astype(jnp.int32)
    p2 = rw[:, 3:4].astype(jnp.int32)
    wmat = jnp.where(sp == p1, rw[:, 0:1], jnp.where(sp == p2, rw[:, 1:2], 0.0)).astype(jnp.bfloat16)
    nxt = jnp.minimum(i + 1, pl.num_programs(0) - 1)
    _gather_chunks(src_ref, nxt * SORT_CHUNKS, SORT_CHUNKS, yr_hbm, ybuf.at[1 - slot], sem.at[1 - slot],
                   inline=True)
    _wait_chunks(SORT_CHUNKS, yr_hbm, ybuf.at[slot], sem.at[slot])
    moe = _dot(wmat, _unpack_bf16_pairs(ybuf[slot]))
    x2 = x1_ref[...] + mod_ref[5:6, :] * moe
    if final_norm:
        x2 = x2 * lax.rsqrt(jnp.mean(x2 * x2, axis=-1, keepdims=True) + EPS) * fg_ref[...]
    out_ref[...] = x2

    @pl.when(i == pl.num_programs(0) - 1)
    def _():
        _wait_chunks(SORT_CHUNKS, yr_hbm, ybuf.at[1 - slot], sem.at[1 - slot])


def _combine(src_chunk, x1, rw, mod, final_g, yr, seq, final_norm):
    t, d = x1.shape
    per_seq = seq // TM
    return pl.pallas_call(
        functools.partial(_combine_kernel, final_norm=final_norm),
        grid_spec=pltpu.PrefetchScalarGridSpec(
            num_scalar_prefetch=1,
            grid=(t // TM,),
            in_specs=[pl.BlockSpec((TM, d), lambda i, src: (i, 0)),
                      pl.BlockSpec((ROUTE_ROWS, TM), lambda i, src: (0, i)),
                      pl.BlockSpec((None, 6, d), lambda i, src: (i // per_seq, 0, 0)),
                      pl.BlockSpec((1, d), lambda i, src: (0, 0)),
                      pl.BlockSpec(memory_space=pl.ANY)],
            out_specs=pl.BlockSpec((TM, d), lambda i, src: (i, 0)),
            scratch_shapes=[pltpu.VMEM((2, SORT_ROWS, d // 2), jnp.uint32),
                            pltpu.SemaphoreType.DMA((2,))]),
        out_shape=jax.ShapeDtypeStruct((t, d), jnp.float32),
        compiler_params=pltpu.CompilerParams(dimension_semantics=("arbitrary",),
                                             vmem_limit_bytes=VMEM_LIMIT),
        name="combine",
    )(src_chunk, x1, rw, mod, final_g.reshape(1, d), yr)


def kernel(x, c, positions, rel_bias, w_ada, b_ada, norm1_g, w_in, sinks, gm_ln_g, gm_ln_b, gm_w_s, gm_b_s,
           p_a, p_b, w_o, norm2_g, w_router_g, b_router_g, w_router_e, b_router_e, w_gate, w_up, w_down,
           final_g):
    bsz, seq, d = x.shape
    t = bsz * seq
    bf16 = jnp.bfloat16
    depth = w_ada.shape[0]
    n_tiles = t // TM
    n_chunks = t * TOP_K // CHUNK + n_tiles * N_EXPERTS + N_EXPERTS * BLOCK_CHUNKS
    i32 = jnp.int32
    for l in range(depth):
        mod = _adaln_mod(c, w_ada[l], b_ada[l]).reshape(bsz, 6, d)
        pad = ROUTER_LANES - N_GROUPS - N_EXPERTS
        w_r = jnp.concatenate([w_router_g[l], w_router_e[l], jnp.zeros((d, pad), jnp.float32)], axis=1)
        b_r = jnp.concatenate([b_router_g[l], b_router_e[l], jnp.zeros((pad,), jnp.float32)]).reshape(1, -1)
        bsb = jnp.broadcast_to(gm_b_s[l][:, :, None], (GM_GROUPS, GM_CHUNK, LANES))
        x1, xs, rw, cnt = _mixer(
            x, mod, positions, rel_bias, norm1_g[l].reshape(1, d), w_in[l].astype(bf16), sinks[l],
            gm_ln_g[l].reshape(1, -1), gm_ln_b[l].reshape(1, -1), gm_w_s[l], bsb,
            p_a[l].astype(bf16), p_b[l].astype(bf16), w_o[l].astype(bf16), norm2_g[l].reshape(1, d),
            w_r.astype(bf16), b_r)

        n = cnt[:, EXPERT_LANE0:EXPERT_LANE0 + N_EXPERTS, 0].astype(i32)
        c = (n + CHUNK - 1) // CHUNK
        run0_tile = jnp.cumsum(c, axis=1) - c
        run0_exp = jnp.cumsum(c, axis=0) - c
        tot = jnp.sum(c, axis=0)
        seg = (tot + BLOCK_CHUNKS - 1) // BLOCK_CHUNKS * BLOCK_CHUNKS
        seg_end = jnp.cumsum(seg)
        seg0 = seg_end - seg

        pos_tile = jnp.arange(n_tiles, dtype=i32)[:, None] * SORT_CHUNKS + run0_tile
        pos_exp = seg0[None, :] + run0_exp
        shift = (pos_tile - pos_exp).reshape(-1)
        lo_exp, hi_exp = pos_exp.reshape(-1), (pos_exp + c).reshape(-1)

        j = jnp.arange(n_chunks, dtype=i32)[:, None]
        inside = (j >= lo_exp[None, :]) & (j < hi_exp[None, :])
        src_j = jnp.sum(jnp.where(inside, j + shift[None, :], 0), axis=1).astype(i32)
        blk0 = jnp.arange(n_chunks // BLOCK_CHUNKS, dtype=i32) * BLOCK_CHUNKS
        block_e = jnp.minimum(jnp.sum(blk0[:, None] >= seg_end[None, :], axis=1), N_EXPERTS - 1).astype(i32)
        last_blk = (jnp.maximum(seg_end[-1] // BLOCK_CHUNKS, 1) - 1).astype(i32).reshape(1)

        q = jnp.arange(SORT_CHUNKS, dtype=i32)[None, :, None]
        in_run = (q >= run0_tile[:, None, :]) & (q < (run0_tile + c)[:, None, :])
        dst_q = jnp.sum(jnp.where(in_run, q + (pos_exp - run0_tile)[:, None, :], 0), axis=2).astype(i32).reshape(-1)

        yr = _experts(src_j, block_e, last_blk, xs, w_gate[l], w_up[l], w_down[l])
        x = _combine(dst_q, x1.reshape(t, d), rw, mod, final_g, yr, seq,
                     final_norm=(l == depth - 1)).reshape(bsz, seq, d)
    return x
```

```python
import functools
import math

import jax
import jax.numpy as jnp
from jax import lax
from jax.experimental import pallas as pl
from jax.experimental.pallas import tpu as pltpu

D_MODEL = 1024
N_HEADS = 8
N_KV_HEADS = 2
HEAD_DIM = 64
BLOCK = 128
ATTN_Q = N_HEADS * HEAD_DIM
ATTN_KV = N_KV_HEADS * HEAD_DIM
N_BUCKETS = 32
MAX_EXACT = N_BUCKETS // 2
MAX_DISTANCE = 128
GM_WIDTH = 512
GM_GROUPS = 4
GM_CHUNK = 128
N_GROUPS = 4
EXPERTS_PER_GROUP = 8
N_EXPERTS = N_GROUPS * EXPERTS_PER_GROUP
TOP_K = 2
D_EXPERT = 512
EPS = 1e-6
NEG = -1e30

LANES = 128
ROUTER_LANES = LANES
EXPERT_LANE0 = N_GROUPS
TM = 512
BM = 512
ROUTER_ROWS = 48
ROUTE_ROWS = 8
CHUNK = 8
SORT_ROWS = TM * TOP_K + N_EXPERTS * CHUNK
SORT_CHUNKS = SORT_ROWS // CHUNK
BLOCK_CHUNKS = BM // CHUNK
GATHER_UNROLL = 8
VMEM_LIMIT = 56 * 1024 * 1024

Q0, K0, V0, GU0, GV0, GA0, GB0, IN_END = (0, 512, 640, 768, 1280, 1792, 2816, 3840)


def _dot(a, b):
    return jnp.dot(a, b, preferred_element_type=jnp.float32)


def _dot_nt(a, b):
    return lax.dot_general(a, b, (((1,), (1,)), ((), ())), preferred_element_type=jnp.float32)


LOG2E = math.log2(math.e)


def _gelu_tanh(x):
    c = math.sqrt(2.0 / math.pi)
    k0, k1 = -2.0 * c * LOG2E, -2.0 * c * 0.044715 * LOG2E
    return x * (1.0 / (1.0 + jnp.exp2(x * (x * x * k1 + k0))))


def _sigmoid(x):
    return 1.0 / (1.0 + jnp.exp2(x * -LOG2E))


def _pack_bf16_pairs(x):
    bits = lax.bitcast_convert_type(x, jnp.uint32)
    half = x.shape[1] // 2
    return (bits[:, 0:half] >> 16) | (bits[:, half:] & jnp.uint32(0xFFFF0000))


def _unpack_bf16_pairs(w):
    lo = lax.bitcast_convert_type(w << 16, jnp.float32)
    hi = lax.bitcast_convert_type(w & jnp.uint32(0xFFFF0000), jnp.float32)
    return jnp.concatenate([lo, hi], axis=1).astype(jnp.bfloat16)


def _adaln_kernel(c_ref, w_ref, b_ref, o_ref):
    c = c_ref[...]
    cs = c * _sigmoid(c)
    o_ref[...] = _dot(cs, w_ref[...]) + b_ref[...]


def _adaln_mod(c, w, b):
    bsz, d = c.shape
    n = w.shape[1]
    tn = 1024
    return pl.pallas_call(
        _adaln_kernel,
        grid=(n // tn,),
        in_specs=[pl.BlockSpec((bsz, d), lambda i: (0, 0)),
                  pl.BlockSpec((d, tn), lambda i: (0, i)),
                  pl.BlockSpec((1, tn), lambda i: (0, i))],
        out_specs=pl.BlockSpec((bsz, tn), lambda i: (0, i)),
        out_shape=jax.ShapeDtypeStruct((bsz, n), jnp.float32),
        compiler_params=pltpu.CompilerParams(dimension_semantics=("arbitrary",)),
        name="adaln_mod",
    )(c, w, b.reshape(1, n))


def _mixer_kernel(relb_ref, sinks_ref, spos_ref,
                  x_ref, mod_ref, pos_ref, n1g_ref, win_ref, lng_ref, lnb_ref, ws_ref, bsb_ref,
                  pa_ref, pb_ref, wo_ref, n2g_ref, wr_ref, br_ref,
                  x1_ref, xs_ref, rw_ref, cnt_ref,
                  kbuf, vbuf, pbuf, biasm, key, ya, yb, strict, sbuf, mbuf, flag):
    b = pl.program_id(0)
    j = pl.program_id(1)
    nblk = TM // BLOCK
    bf16 = jnp.bfloat16

    @pl.when((b == 0) & (j == 0))
    def _():
        key[...] = jnp.zeros_like(key)
        for i in range(nblk):
            flag[i] = 1
        tr = lax.broadcasted_iota(jnp.int32, (TM, TM), 0)
        tc = lax.broadcasted_iota(jnp.int32, (TM, TM), 1)
        strict[...] = jnp.where(tr < tc, 1.0, 0.0).astype(bf16)

    @pl.when(j == 0)
    def _():
        kbuf[0:BLOCK, :] = jnp.zeros((BLOCK, LANES), jnp.float32)
        vbuf[0:BLOCK, :] = jnp.zeros((BLOCK, LANES), jnp.float32)
        pbuf[:, 0:BLOCK] = jnp.zeros((1, BLOCK), jnp.int32)

    pbuf[:, BLOCK:] = pos_ref[...]
    qi = lax.broadcasted_iota(jnp.int32, (BLOCK, BLOCK), 0)
    kc = lax.broadcasted_iota(jnp.int32, (BLOCK, BLOCK), 1)
    from_prev = kc > qi
    rels, changed = [], []
    tile0 = (b * pl.num_programs(1) + j) * TM
    for i in range(nblk):
        pk = pbuf[:, i * BLOCK:(i + 2) * BLOCK]
        rels.append(pk - spos_ref[tile0 + i * BLOCK])
        changed.append(jnp.where(rels[i] != key[i], 1.0, 0.0))
    any_changed = jnp.max(functools.reduce(jnp.maximum, changed))
    any_flag = functools.reduce(jnp.maximum, [flag[i] for i in range(nblk)])
    refresh = (any_changed != 0) | (any_flag != 0) | (j == 0)
    for i in range(nblk):
        @pl.when(refresh)
        def _():
          stale = (jnp.max(changed[i]) != 0) | (flag[i] != 0)
          if i == 0:
              stale = stale | (j == 0)

          @pl.when(stale)
          def _():
            pk = pbuf[:, i * BLOCK:(i + 2) * BLOCK]
            pq = pk[:, BLOCK:]
            pq_col = jnp.transpose(jnp.broadcast_to(pq, (BLOCK, BLOCK)))
            no_prev = (j == 0) if i == 0 else None
            for hd in range(N_HEADS):
                biasm[i * N_HEADS + hd] = jnp.zeros((BLOCK, BLOCK), jnp.float32)
            for side in range(2):
                dist = pq_col - pk[:, side * BLOCK:(side + 1) * BLOCK]
                n = jnp.maximum(dist, 0)
                nf = jnp.maximum(n, 1).astype(jnp.float32)
                large = MAX_EXACT + (jnp.log(nf / MAX_EXACT) / math.log(MAX_DISTANCE / MAX_EXACT)
                                     * (N_BUCKETS - MAX_EXACT)).astype(jnp.int32)
                large = jnp.minimum(large, N_BUCKETS - 1)
                bucket = jnp.where(n < MAX_EXACT, n, large)
                use = from_prev if side == 0 else jnp.logical_not(from_prev)
                for hd in range(N_HEADS):
                    acc = jnp.zeros((BLOCK, BLOCK), jnp.float32)
                    for bk in range(N_BUCKETS):
                        acc = jnp.where(bucket == bk, relb_ref[bk * N_HEADS + hd], acc)
                    if side == 0 and i == 0:
                        acc = jnp.where(no_prev, NEG, acc)
                    slot = i * N_HEADS + hd
                    biasm[slot] = jnp.where(use, acc, biasm[slot])
            key[i] = rels[i]
            flag[i] = no_prev.astype(jnp.int32) if i == 0 else 0

    x = x_ref[...]
    sh1, sc1, g1 = mod_ref[0:1, :], mod_ref[1:2, :], mod_ref[2:3, :]
    sh2, sc2, g2 = mod_ref[3:4, :], mod_ref[4:5, :], mod_ref[5:6, :]

    xn = x * lax.rsqrt(jnp.mean(x * x, axis=-1, keepdims=True) + EPS)
    h = (xn * (n1g_ref[...] * (1.0 + sc1)) + sh1).astype(bf16)

    lane = lax.broadcasted_iota(jnp.int32, (1, LANES), 1)
    lo = lane < HEAD_DIM
    q = _dot(h, win_ref[:, Q0:K0]) * (HEAD_DIM ** -0.5)
    lo4 = jnp.concatenate([lo] * (ATTN_Q // LANES), axis=1)
    q_lo = jnp.where(lo4, q, 0.0).astype(bf16)
    q_hi = jnp.where(lo4, 0.0, q).astype(bf16)
    kv = _dot(h, win_ref[:, K0:GU0])
    kbuf[BLOCK:, :] = kv[:, 0:LANES]
    vbuf[BLOCK:, :] = kv[:, LANES:]
    kf = kbuf[...]
    vf = vbuf[...]
    kr = pltpu.roll(kf, HEAD_DIM, 1)
    vr = pltpu.roll(vf, HEAD_DIM, 1)
    kd = (jnp.where(lo, kf, kr).astype(bf16), jnp.where(lo, kr, kf).astype(bf16))
    v_lo = (jnp.where(lo, vf, 0.0).astype(bf16), jnp.where(lo, vr, 0.0).astype(bf16))
    v_hi = (jnp.where(lo, 0.0, vr).astype(bf16), jnp.where(lo, 0.0, vf).astype(bf16))

    ones_blk = jnp.ones((2 * BLOCK, LANES), bf16)

    def attend(i):
        rows = slice(i * BLOCK, (i + 1) * BLOCK)
        band = slice(i * BLOCK, (i + 2) * BLOCK)
        for hd in range(N_HEADS):
            kvh = hd // (N_HEADS // N_KV_HEADS)
            cols = slice(hd // 2 * LANES, (hd // 2 + 1) * LANES)
            qq = q_lo if hd % 2 == 0 else q_hi
            s2 = _dot_nt(qq[rows, cols], kd[kvh][band])
            s = jnp.where(from_prev, s2[:, 0:BLOCK], s2[:, BLOCK:]) + biasm[i * N_HEADS + hd]
            sbuf[hd] = s
            m = jnp.maximum(jnp.max(s, axis=-1, keepdims=True), sinks_ref[hd])
            mbuf[hd] = jnp.broadcast_to(m, (BLOCK, BLOCK))
        for pr in range(N_HEADS // 2):
            kvh = (2 * pr) // (N_HEADS // N_KV_HEADS)
            o = None
            for half, vv in enumerate((v_lo, v_hi)):
                hd = 2 * pr + half
                m = mbuf[hd]
                p = jnp.exp(sbuf[hd] - m)
                p2 = jnp.concatenate([jnp.where(from_prev, p, 0.0), jnp.where(from_prev, 0.0, p)], axis=1)
                r = _dot(p2.astype(bf16), jnp.concatenate([vv[kvh][band], ones_blk], axis=1))
                den = r[:, LANES:] + jnp.exp(sinks_ref[hd] - m)
                oh = r[:, 0:LANES] * (1.0 / den)
                o = oh if o is None else o + oh
            ya[rows, slice(pr * LANES, (pr + 1) * LANES)] = o.astype(bf16)

    u = _gelu_tanh(_dot(h, win_ref[:, GU0:GV0]))
    attend(0)
    vg = _gelu_tanh(_dot(h, win_ref[:, GV0:GA0]))
    mu = jnp.mean(vg, axis=-1, keepdims=True)
    vc = vg - mu
    var = jnp.mean(vc * vc, axis=-1, keepdims=True)
    vn = (vc * lax.rsqrt(var + EPS) * lng_ref[...] + lnb_ref[...]).astype(bf16)
    attend(1)
    ti = lax.broadcasted_iota(jnp.int32, (GM_CHUNK, GM_CHUNK), 0)
    si = lax.broadcasted_iota(jnp.int32, (GM_CHUNK, GM_CHUNK), 1)
    tril = si <= ti
    for g in range(GM_GROUPS):
        wg = jnp.where(tril, ws_ref[g], 0.0).astype(bf16)
        cols = slice(g * LANES, (g + 1) * LANES)
        for cidx in range(TM // GM_CHUNK):
            rows = slice(cidx * GM_CHUNK, (cidx + 1) * GM_CHUNK)
            sv = _dot(wg, vn[rows, cols]) + bsb_ref[g]
            yb[rows, cols] = (u[rows, cols] * sv).astype(bf16)
    gate_a = _sigmoid(_dot(h, win_ref[:, GA0:GB0]))
    attend(2)
    gate_b = _sigmoid(_dot(h, win_ref[:, GB0:IN_END]))
    attend(3)

    kbuf[0:BLOCK, :] = kv[TM - BLOCK:, 0:LANES]
    vbuf[0:BLOCK, :] = kv[TM - BLOCK:, LANES:]
    pbuf[:, 0:BLOCK] = pos_ref[:, TM - BLOCK:]

    merged = gate_a * _dot(ya[...], pa_ref[...]) + gate_b * _dot(yb[...], pb_ref[...])
    x1 = x + g1 * _dot(merged.astype(bf16), wo_ref[...])
    x1_ref[...] = x1

    xn2 = x1 * lax.rsqrt(jnp.mean(x1 * x1, axis=-1, keepdims=True) + EPS)
    h2 = (xn2 * (n2g_ref[...] * (1.0 + sc2)) + sh2).astype(bf16)
    logits = _dot(h2, wr_ref[...]) + br_ref[...]
    lt = jnp.transpose(logits)[0:ROUTER_ROWS, :]
    row = lax.broadcasted_iota(jnp.int32, (ROUTER_ROWS, TM), 0)
    row_f = row.astype(jnp.float32)
    big = float(2 * LANES)
    is_grp = row < N_GROUPS
    lg = jnp.where(is_grp, lt, NEG)
    lg_max = jnp.max(lg, axis=0, keepdims=True)
    g_idx = jnp.min(jnp.where(lg == lg_max, row_f, big), axis=0, keepdims=True)
    p_g = 1.0 / jnp.sum(jnp.where(is_grp, jnp.exp(lg - lg_max), 0.0), axis=0, keepdims=True)
    row_grp = jnp.floor((row_f - EXPERT_LANE0) * (1.0 / EXPERTS_PER_GROUP))
    in_grp = (row >= EXPERT_LANE0) & (row < EXPERT_LANE0 + N_EXPERTS) & (row_grp == g_idx)
    le = jnp.where(in_grp, lt, NEG)
    m1 = jnp.max(le, axis=0, keepdims=True)
    i1 = jnp.min(jnp.where(le == m1, row_f, big), axis=0, keepdims=True)
    oh1 = row_f == i1
    le2 = jnp.where(oh1, NEG, le)
    m2 = jnp.max(le2, axis=0, keepdims=True)
    i2 = jnp.min(jnp.where(le2 == m2, row_f, big), axis=0, keepdims=True)
    oh2 = row_f == i2
    e2 = jnp.exp(m2 - m1)
    w1 = p_g / (1.0 + e2)
    w2 = p_g * e2 / (1.0 + e2)

    oh = jnp.where(oh1, 1.0, jnp.where(oh2, 1.0, 0.0))
    n_e = jnp.sum(oh, axis=1, keepdims=True)
    cnt_ref[...] = n_e
    padded = jnp.floor((n_e + (CHUNK - 1)) * (1.0 / CHUNK)) * CHUNK
    dst_row = lax.broadcasted_iota(jnp.int32, (ROUTER_ROWS, ROUTER_ROWS), 0)
    src_row = lax.broadcasted_iota(jnp.int32, (ROUTER_ROWS, ROUTER_ROWS), 1)
    lower = jnp.where(src_row < dst_row, 1.0, 0.0).astype(bf16)
    run0 = _dot(lower, jnp.broadcast_to(padded, (ROUTER_ROWS, TM)).astype(bf16))
    slot = _dot(oh.astype(bf16), strict[...]) + run0
    pos1 = jnp.sum(jnp.where(oh1, slot, 0.0), axis=0, keepdims=True)
    pos2 = jnp.sum(jnp.where(oh2, slot, 0.0), axis=0, keepdims=True)
    rrow = lax.broadcasted_iota(jnp.int32, (ROUTE_ROWS, TM), 0)
    rw_ref[...] = jnp.where(rrow == 0, w1, jnp.where(rrow == 1, w2,
                            jnp.where(rrow == 2, pos1, jnp.where(rrow == 3, pos2, 0.0))))

    p1 = pos1.astype(jnp.int32)
    p2 = pos2.astype(jnp.int32)
    sp = lax.broadcasted_iota(jnp.int32, (SORT_ROWS, TM), 0)
    perm = jnp.where(sp == p1, 1.0, jnp.where(sp == p2, 1.0, 0.0)).astype(bf16)
    xs_ref[...] = _pack_bf16_pairs(_dot(perm, h2))


def _mixer(x, mod, positions, rel_bias, n1g, w_in, sinks, lng, lnb, w_s, bsb, p_a, p_b, w_o, n2g, w_r, b_r):
    bsz, seq, d = x.shape
    nj = seq // TM
    const = lambda *shape: pl.BlockSpec(shape, lambda b, j: (0,) * len(shape), pipeline_mode=pl.Buffered(1))
    smem = pl.BlockSpec(memory_space=pltpu.SMEM)
    tile = lambda w: pl.BlockSpec((None, TM, w), lambda b, j: (b, j, 0))
    return pl.pallas_call(
        _mixer_kernel,
        grid=(bsz, nj),
        in_specs=[smem, smem, smem,
                  tile(d),
                  pl.BlockSpec((None, 6, d), lambda b, j: (b, 0, 0)),
                  pl.BlockSpec((None, None, 1, TM), lambda b, j: (b, j, 0, 0)),
                  const(1, d), const(d, IN_END), const(1, GM_WIDTH), const(1, GM_WIDTH),
                  const(GM_GROUPS, GM_CHUNK, GM_CHUNK), const(GM_GROUPS, GM_CHUNK, LANES),
                  const(ATTN_Q, d), const(GM_WIDTH, d), const(d, d), const(1, d),
                  const(d, ROUTER_LANES), const(1, ROUTER_LANES)],
        out_specs=[tile(d),
                   pl.BlockSpec((SORT_ROWS, d // 2), lambda b, j: (b * nj + j, 0)),
                   pl.BlockSpec((ROUTE_ROWS, TM), lambda b, j: (0, b * nj + j)),
                   pl.BlockSpec((None, ROUTER_ROWS, 1), lambda b, j: (b * nj + j, 0, 0))],
        out_shape=[jax.ShapeDtypeStruct((bsz, seq, d), jnp.float32),
                   jax.ShapeDtypeStruct((bsz * nj * SORT_ROWS, d // 2), jnp.uint32),
                   jax.ShapeDtypeStruct((ROUTE_ROWS, bsz * seq), jnp.float32),
                   jax.ShapeDtypeStruct((bsz * nj, ROUTER_ROWS, 1), jnp.float32)],
        scratch_shapes=[pltpu.VMEM((TM + BLOCK, LANES), jnp.float32),
                        pltpu.VMEM((TM + BLOCK, LANES), jnp.float32),
                        pltpu.VMEM((1, TM + BLOCK), jnp.int32),
                        pltpu.VMEM((TM // BLOCK * N_HEADS, BLOCK, BLOCK), jnp.float32),
                        pltpu.VMEM((TM // BLOCK, 1, 2 * BLOCK), jnp.int32),
                        pltpu.VMEM((TM, ATTN_Q), jnp.bfloat16),
                        pltpu.VMEM((TM, GM_WIDTH), jnp.bfloat16),
                        pltpu.VMEM((TM, TM), jnp.bfloat16),
                        pltpu.VMEM((N_HEADS, BLOCK, BLOCK), jnp.float32),
                        pltpu.VMEM((N_HEADS, BLOCK, BLOCK), jnp.float32),
                        pltpu.SMEM((TM // BLOCK,), jnp.int32)],
        compiler_params=pltpu.CompilerParams(dimension_semantics=("arbitrary", "arbitrary"),
                                             vmem_limit_bytes=VMEM_LIMIT),
        name="mixer",
    )(rel_bias.reshape(-1), sinks, positions.reshape(-1), x, mod, positions.reshape(bsz, nj, 1, TM), n1g, w_in, lng, lnb,
      w_s, bsb, p_a, p_b, w_o, n2g, w_r, b_r)


def _gather_chunks(idx_ref, idx0, n, src_hbm, dst, sem, inline=False):
    def issue(c, carry):
        src = pl.multiple_of(idx_ref[idx0 + c] * CHUNK, CHUNK)
        pltpu.make_async_copy(src_hbm.at[pl.ds(src, CHUNK)],
                              dst.at[pl.ds(pl.multiple_of(c * CHUNK, CHUNK), CHUNK)], sem).start()
        return carry

    if inline:
        for c in range(n):
            issue(c, 0)
    else:
        lax.fori_loop(0, n, issue, 0, unroll=GATHER_UNROLL)


def _wait_chunks(n, src_hbm, dst, sem):
    pltpu.make_async_copy(src_hbm.at[pl.ds(0, n * CHUNK)], dst, sem).wait()


def _expert_kernel(src_ref, be_ref, ne_ref, ws_ref, lb_ref, xs_hbm, wg_hbm, wu_hbm, wd_hbm, yr_ref,
                   xbuf, wgf, wuf, wdf, wgb, wub, wdb, sem, wsem):
    i = pl.program_id(0)
    last = lb_ref[0]
    slot = i % 2

    def weight_copies(e, w):
        return [pltpu.make_async_copy(hbm.at[e], buf.at[w], wsem.at[w])
                for hbm, buf in ((wg_hbm, wgf), (wu_hbm, wuf), (wd_hbm, wdf))]

    @pl.when(i == 0)
    def _():
        _gather_chunks(src_ref, 0, BLOCK_CHUNKS, xs_hbm, xbuf.at[0], sem.at[0])
        for cp in weight_copies(be_ref[0], 0):
            cp.start()

    @pl.when(i <= last)
    def _():
        changed = (i == 0) | (be_ref[i] != be_ref[jnp.maximum(i - 1, 0)])

        @pl.when(changed)
        def _():
            w = ws_ref[i]
            for cp in weight_copies(be_ref[i], w):
                cp.wait()
            wgb[...] = wgf[w].astype(jnp.bfloat16)
            wub[...] = wuf[w].astype(jnp.bfloat16)
            wdb[...] = wdf[w].astype(jnp.bfloat16)

            @pl.when(ne_ref[i] >= 0)
            def _():
                for cp in weight_copies(ne_ref[i], 1 - w):
                    cp.start()

        nxt = jnp.minimum(i + 1, last)
        _gather_chunks(src_ref, nxt * BLOCK_CHUNKS, BLOCK_CHUNKS, xs_hbm, xbuf.at[1 - slot], sem.at[1 - slot],
                       inline=True)
        _wait_chunks(BLOCK_CHUNKS, xs_hbm, xbuf.at[slot], sem.at[slot])
        xb = _unpack_bf16_pairs(xbuf[slot])
        hg = _dot(xb, wgb[...])
        hu = _dot(xb, wub[...])
        hid = (hg * _sigmoid(hg) * hu).astype(jnp.bfloat16)
        yr_ref[...] = _pack_bf16_pairs(_dot(hid, wdb[...]).astype(jnp.bfloat16).astype(jnp.float32))

        @pl.when(i == last)
        def _():
            _wait_chunks(BLOCK_CHUNKS, xs_hbm, xbuf.at[1 - slot], sem.at[1 - slot])

    @pl.when(i > last)
    def _():
        yr_ref[...] = jnp.zeros_like(yr_ref)


def _experts(src_chunk, block_e, next_e, w_slot, last_blk, xs, w_gate, w_up, w_down):
    d = w_gate.shape[1]
    n_rows = src_chunk.shape[0] * CHUNK
    hbm = pl.BlockSpec(memory_space=pl.ANY)
    return pl.pallas_call(
        _expert_kernel,
        grid_spec=pltpu.PrefetchScalarGridSpec(
            num_scalar_prefetch=5,
            grid=(n_rows // BM,),
            in_specs=[hbm, hbm, hbm, hbm],
            out_specs=pl.BlockSpec((BM, d // 2), lambda i, *_: (i, 0)),
            scratch_shapes=[pltpu.VMEM((2, BM, d // 2), jnp.uint32),
                            pltpu.VMEM((2, d, D_EXPERT), jnp.float32),
                            pltpu.VMEM((2, d, D_EXPERT), jnp.float32),
                            pltpu.VMEM((2, D_EXPERT, d), jnp.float32),
                            pltpu.VMEM((d, D_EXPERT), jnp.bfloat16),
                            pltpu.VMEM((d, D_EXPERT), jnp.bfloat16),
                            pltpu.VMEM((D_EXPERT, d), jnp.bfloat16),
                            pltpu.SemaphoreType.DMA((2,)),
                            pltpu.SemaphoreType.DMA((2,))]),
        out_shape=jax.ShapeDtypeStruct((n_rows, d // 2), jnp.uint32),
        compiler_params=pltpu.CompilerParams(dimension_semantics=("arbitrary",),
                                             vmem_limit_bytes=VMEM_LIMIT),
        name="experts",
    )(src_chunk, block_e, next_e, w_slot, last_blk, xs, w_gate, w_up, w_down)


def _combine_kernel(src_ref, x1_ref, rw_ref, mod_ref, fg_ref, yr_hbm, out_ref, ybuf, sem, *, final_norm):
    i = pl.program_id(0)
    slot = i % 2

    @pl.when(i == 0)
    def _():
        _gather_chunks(src_ref, 0, SORT_CHUNKS, yr_hbm, ybuf.at[0], sem.at[0])

    rows = jnp.concatenate([rw_ref[...], jnp.zeros((LANES - ROUTE_ROWS, TM), jnp.float32)], axis=0)
    rw = jnp.transpose(rows)
    sp = lax.broadcasted_iota(jnp.int32, (1, SORT_ROWS), 1)
    p1 = rw[:, 2:3].astype(jnp.int32)
    p2 = rw[:, 3:4].astype(jnp.int32)
    wmat = jnp.where(sp == p1, rw[:, 0:1], jnp.where(sp == p2, rw[:, 1:2], 0.0)).astype(jnp.bfloat16)
    nxt = jnp.minimum(i + 1, pl.num_programs(0) - 1)
    _gather_chunks(src_ref, nxt * SORT_CHUNKS, SORT_CHUNKS, yr_hbm, ybuf.at[1 - slot], sem.at[1 - slot],
                   inline=True)
    _wait_chunks(SORT_CHUNKS, yr_hbm, ybuf.at[slot], sem.at[slot])
    moe = _dot(wmat, _unpack_bf16_pairs(ybuf[slot]))
    x2 = x1_ref[...] + mod_ref[5:6, :] * moe
    if final_norm:
        x2 = x2 * lax.rsqrt(jnp.mean(x2 * x2, axis=-1, keepdims=True) + EPS) * fg_ref[...]
    out_ref[...] = x2

    @pl.when(i == pl.num_programs(0) - 1)
    def _():
        _wait_chunks(SORT_CHUNKS, yr_hbm, ybuf.at[1 - slot], sem.at[1 - slot])


def _combine(src_chunk, x1, rw, mod, final_g, yr, seq, final_norm):
    t, d = x1.shape
    per_seq = seq // TM
    return pl.pallas_call(
        functools.partial(_combine_kernel, final_norm=final_norm),
        grid_spec=pltpu.PrefetchScalarGridSpec(
            num_scalar_prefetch=1,
            grid=(t // TM,),
            in_specs=[pl.BlockSpec((TM, d), lambda i, src: (i, 0)),
                      pl.BlockSpec((ROUTE_ROWS, TM), lambda i, src: (0, i)),
                      pl.BlockSpec((None, 6, d), lambda i, src: (i // per_seq, 0, 0)),
                      pl.BlockSpec((1, d), lambda i, src: (0, 0)),
                      pl.BlockSpec(memory_space=pl.ANY)],
            out_specs=pl.BlockSpec((TM, d), lambda i, src: (i, 0)),
            scratch_shapes=[pltpu.VMEM((2, SORT_ROWS, d // 2), jnp.uint32),
                            pltpu.SemaphoreType.DMA((2,))]),
        out_shape=jax.ShapeDtypeStruct((t, d), jnp.float32),
        compiler_params=pltpu.CompilerParams(dimension_semantics=("arbitrary",),
                                             vmem_limit_bytes=VMEM_LIMIT),
        name="combine",
    )(src_chunk, x1, rw, mod, final_g.reshape(1, d), yr)


def kernel(x, c, positions, rel_bias, w_ada, b_ada, norm1_g, w_in, sinks, gm_ln_g, gm_ln_b, gm_w_s, gm_b_s,
           p_a, p_b, w_o, norm2_g, w_router_g, b_router_g, w_router_e, b_router_e, w_gate, w_up, w_down,
           final_g):
    bsz, seq, d = x.shape
    t = bsz * seq
    bf16 = jnp.bfloat16
    depth = w_ada.shape[0]
    n_tiles = t // TM
    n_chunks = t * TOP_K // CHUNK + n_tiles * N_EXPERTS + N_EXPERTS * BLOCK_CHUNKS
    i32 = jnp.int32
    for l in range(depth):
        mod = _adaln_mod(c, w_ada[l], b_ada[l]).reshape(bsz, 6, d)
        pad = ROUTER_LANES - N_GROUPS - N_EXPERTS
        w_r = jnp.concatenate([w_router_g[l], w_router_e[l], jnp.zeros((d, pad), jnp.float32)], axis=1)
        b_r = jnp.concatenate([b_router_g[l], b_router_e[l], jnp.zeros((pad,), jnp.float32)]).reshape(1, -1)
        bsb = jnp.broadcast_to(gm_b_s[l][:, :, None], (GM_GROUPS, GM_CHUNK, LANES))
        x1, xs, rw, cnt = _mixer(
            x, mod, positions, rel_bias, norm1_g[l].reshape(1, d), w_in[l].astype(bf16), sinks[l],
            gm_ln_g[l].reshape(1, -1), gm_ln_b[l].reshape(1, -1), gm_w_s[l], bsb,
            p_a[l].astype(bf16), p_b[l].astype(bf16), w_o[l].astype(bf16), norm2_g[l].reshape(1, d),
            w_r.astype(bf16), b_r)

        n = cnt[:, EXPERT_LANE0:EXPERT_LANE0 + N_EXPERTS, 0].astype(i32)
        c = (n + CHUNK - 1) // CHUNK
        run0_tile = jnp.cumsum(c, axis=1) - c
        run0_exp = jnp.cumsum(c, axis=0) - c
        tot = jnp.sum(c, axis=0)
        seg = (tot + BLOCK_CHUNKS - 1) // BLOCK_CHUNKS * BLOCK_CHUNKS
        seg_end = jnp.cumsum(seg)
        seg0 = seg_end - seg

        pos_tile = jnp.arange(n_tiles, dtype=i32)[:, None] * SORT_CHUNKS + run0_tile
        pos_exp = seg0[None, :] + run0_exp
        shift = (pos_tile - pos_exp).reshape(-1)
        lo_exp, hi_exp = pos_exp.reshape(-1), (pos_exp + c).reshape(-1)

        j = jnp.arange(n_chunks, dtype=i32)[:, None]
        inside = (j >= lo_exp[None, :]) & (j < hi_exp[None, :])
        src_j = jnp.sum(jnp.where(inside, j + shift[None, :], 0), axis=1).astype(i32)
        blk0 = jnp.arange(n_chunks // BLOCK_CHUNKS, dtype=i32) * BLOCK_CHUNKS
        block_e = jnp.minimum(jnp.sum(blk0[:, None] >= seg_end[None, :], axis=1), N_EXPERTS - 1).astype(i32)
        last_blk = (jnp.maximum(seg_end[-1] // BLOCK_CHUNKS, 1) - 1).astype(i32).reshape(1)
        experts = jnp.arange(N_EXPERTS, dtype=i32)
        later = (experts[None, :] > block_e[:, None]) & (seg[None, :] > 0)
        next_e = jnp.min(jnp.where(later, experts[None, :], N_EXPERTS), axis=1)
        next_e = jnp.where(next_e < N_EXPERTS, next_e, -1).astype(i32)
        w_slot = (jnp.sum((experts[None, :] < block_e[:, None]) & (seg[None, :] > 0), axis=1) % 2).astype(i32)

        q = jnp.arange(SORT_CHUNKS, dtype=i32)[None, :, None]
        in_run = (q >= run0_tile[:, None, :]) & (q < (run0_tile + c)[:, None, :])
        dst_q = jnp.sum(jnp.where(in_run, q + (pos_exp - run0_tile)[:, None, :], 0), axis=2).astype(i32).reshape(-1)

        yr = _experts(src_j, block_e, next_e, w_slot, last_blk, xs, w_gate[l], w_up[l], w_down[l])
        x = _combine(dst_q, x1.reshape(t, d), rw, mod, final_g, yr, seq,
                     final_norm=(l == depth - 1)).reshape(bsz, seq, d)
    return x
```

```python
import functools
import math

import jax
import jax.numpy as jnp
from jax import lax
from jax.experimental import pallas as pl
from jax.experimental.pallas import tpu as pltpu

D_MODEL = 1024
N_HEADS = 8
N_KV_HEADS = 2
HEAD_DIM = 64
BLOCK = 128
ATTN_Q = N_HEADS * HEAD_DIM
ATTN_KV = N_KV_HEADS * HEAD_DIM
N_BUCKETS = 32
MAX_EXACT = N_BUCKETS // 2
MAX_DISTANCE = 128
GM_WIDTH = 512
GM_GROUPS = 4
GM_CHUNK = 128
N_GROUPS = 4
EXPERTS_PER_GROUP = 8
N_EXPERTS = N_GROUPS * EXPERTS_PER_GROUP
TOP_K = 2
D_EXPERT = 512
EPS = 1e-6
NEG = -1e30

LANES = 128
ROUTER_LANES = LANES
EXPERT_LANE0 = N_GROUPS
TM = 512
BM = 512
ROUTER_ROWS = 48
ROUTE_ROWS = 8
CHUNK = 8
SORT_ROWS = TM * TOP_K + N_EXPERTS * CHUNK
SORT_CHUNKS = SORT_ROWS // CHUNK
BLOCK_CHUNKS = BM // CHUNK
GATHER_UNROLL = 8
VMEM_LIMIT = 56 * 1024 * 1024

Q0, K0, V0, GU0, GV0, GA0, GB0, IN_END = (0, 512, 640, 768, 1280, 1792, 2816, 3840)


def _dot(a, b):
    return jnp.dot(a, b, preferred_element_type=jnp.float32)


def _dot_nt(a, b):
    return lax.dot_general(a, b, (((1,), (1,)), ((), ())), preferred_element_type=jnp.float32)


LOG2E = math.log2(math.e)


def _gelu_tanh(x):
    c = math.sqrt(2.0 / math.pi)
    k0, k1 = -2.0 * c * LOG2E, -2.0 * c * 0.044715 * LOG2E
    return x * (1.0 / (1.0 + jnp.exp2(x * (x * x * k1 + k0))))


def _sigmoid(x):
    return 1.0 / (1.0 + jnp.exp2(x * -LOG2E))


def _pack_bf16_pairs(x):
    bits = lax.bitcast_convert_type(x, jnp.uint32)
    half = x.shape[1] // 2
    return (bits[:, 0:half] >> 16) | (bits[:, half:] & jnp.uint32(0xFFFF0000))


def _unpack_bf16_pairs(w):
    lo = lax.bitcast_convert_type(w << 16, jnp.float32)
    hi = lax.bitcast_convert_type(w & jnp.uint32(0xFFFF0000), jnp.float32)
    return jnp.concatenate([lo, hi], axis=1).astype(jnp.bfloat16)


def _adaln_kernel(c_ref, w_ref, b_ref, o_ref):
    c = c_ref[...]
    cs = c * _sigmoid(c)
    o_ref[...] = _dot(cs, w_ref[...]) + b_ref[...]


def _adaln_mod(c, w, b):
    bsz, d = c.shape
    n = w.shape[1]
    tn = 1024
    return pl.pallas_call(
        _adaln_kernel,
        grid=(n // tn,),
        in_specs=[pl.BlockSpec((bsz, d), lambda i: (0, 0)),
                  pl.BlockSpec((d, tn), lambda i: (0, i)),
                  pl.BlockSpec((1, tn), lambda i: (0, i))],
        out_specs=pl.BlockSpec((bsz, tn), lambda i: (0, i)),
        out_shape=jax.ShapeDtypeStruct((bsz, n), jnp.float32),
        compiler_params=pltpu.CompilerParams(dimension_semantics=("arbitrary",)),
        name="adaln_mod",
    )(c, w, b.reshape(1, n))


def _mixer_kernel(relb_ref, sinks_ref, spos_ref,
                  x_ref, mod_ref, pos_ref, n1g_ref, win_ref, lng_ref, lnb_ref, ws_ref, bsb_ref,
                  pa_ref, pb_ref, wo_ref, n2g_ref, wr_ref, br_ref,
                  x1_ref, xs_ref, rw_ref, cnt_ref,
                  kbuf, vbuf, pbuf, biasm, key, ya, yb, strict, sbuf, mbuf, flag):
    b = pl.program_id(0)
    j = pl.program_id(1)
    nblk = TM // BLOCK
    bf16 = jnp.bfloat16

    @pl.when((b == 0) & (j == 0))
    def _():
        key[...] = jnp.zeros_like(key)
        for i in range(nblk):
            flag[i] = 1
        tr = lax.broadcasted_iota(jnp.int32, (TM, TM), 0)
        tc = lax.broadcasted_iota(jnp.int32, (TM, TM), 1)
        strict[...] = jnp.where(tr < tc, 1.0, 0.0).astype(bf16)

    @pl.when(j == 0)
    def _():
        kbuf[0:BLOCK, :] = jnp.zeros((BLOCK, LANES), jnp.float32)
        vbuf[0:BLOCK, :] = jnp.zeros((BLOCK, LANES), jnp.float32)
        pbuf[:, 0:BLOCK] = jnp.zeros((1, BLOCK), jnp.int32)

    pbuf[:, BLOCK:] = pos_ref[...]
    qi = lax.broadcasted_iota(jnp.int32, (BLOCK, BLOCK), 0)
    kc = lax.broadcasted_iota(jnp.int32, (BLOCK, BLOCK), 1)
    from_prev = kc > qi
    rels, changed = [], []
    tile0 = (b * pl.num_programs(1) + j) * TM
    for i in range(nblk):
        pk = pbuf[:, i * BLOCK:(i + 2) * BLOCK]
        rels.append(pk - spos_ref[tile0 + i * BLOCK])
        changed.append(jnp.where(rels[i] != key[i], 1.0, 0.0))
    any_changed = jnp.max(functools.reduce(jnp.maximum, changed))
    any_flag = functools.reduce(jnp.maximum, [flag[i] for i in range(nblk)])
    refresh = (any_changed != 0) | (any_flag != 0) | (j == 0)
    for i in range(nblk):
        @pl.when(refresh)
        def _():
          stale = (jnp.max(changed[i]) != 0) | (flag[i] != 0)
          if i == 0:
              stale = stale | (j == 0)

          @pl.when(stale)
          def _():
            pk = pbuf[:, i * BLOCK:(i + 2) * BLOCK]
            pq = pk[:, BLOCK:]
            pq_col = jnp.transpose(jnp.broadcast_to(pq, (BLOCK, BLOCK)))
            no_prev = (j == 0) if i == 0 else None
            for hd in range(N_HEADS):
                biasm[i * N_HEADS + hd] = jnp.zeros((BLOCK, BLOCK), jnp.float32)
            for side in range(2):
                dist = pq_col - pk[:, side * BLOCK:(side + 1) * BLOCK]
                n = jnp.maximum(dist, 0)
                nf = jnp.maximum(n, 1).astype(jnp.float32)
                large = MAX_EXACT + (jnp.log(nf / MAX_EXACT) / math.log(MAX_DISTANCE / MAX_EXACT)
                                     * (N_BUCKETS - MAX_EXACT)).astype(jnp.int32)
                large = jnp.minimum(large, N_BUCKETS - 1)
                bucket = jnp.where(n < MAX_EXACT, n, large)
                use = from_prev if side == 0 else jnp.logical_not(from_prev)
                for hd in range(N_HEADS):
                    acc = jnp.zeros((BLOCK, BLOCK), jnp.float32)
                    for bk in range(N_BUCKETS):
                        acc = jnp.where(bucket == bk, relb_ref[bk * N_HEADS + hd], acc)
                    if side == 0 and i == 0:
                        acc = jnp.where(no_prev, NEG, acc)
                    slot = i * N_HEADS + hd
                    biasm[slot] = jnp.where(use, acc, biasm[slot])
            key[i] = rels[i]
            flag[i] = no_prev.astype(jnp.int32) if i == 0 else 0

    x = x_ref[...]
    sh1, sc1, g1 = mod_ref[0:1, :], mod_ref[1:2, :], mod_ref[2:3, :]
    sh2, sc2, g2 = mod_ref[3:4, :], mod_ref[4:5, :], mod_ref[5:6, :]

    xn = x * lax.rsqrt(jnp.mean(x * x, axis=-1, keepdims=True) + EPS)
    h = (xn * (n1g_ref[...] * (1.0 + sc1)) + sh1).astype(bf16)

    lane = lax.broadcasted_iota(jnp.int32, (1, LANES), 1)
    lo = lane < HEAD_DIM
    q = _dot(h, win_ref[:, Q0:K0]) * (HEAD_DIM ** -0.5)
    lo4 = jnp.concatenate([lo] * (ATTN_Q // LANES), axis=1)
    q_lo = jnp.where(lo4, q, 0.0).astype(bf16)
    q_hi = jnp.where(lo4, 0.0, q).astype(bf16)
    kv = _dot(h, win_ref[:, K0:GU0])
    kbuf[BLOCK:, :] = kv[:, 0:LANES]
    vbuf[BLOCK:, :] = kv[:, LANES:]
    kf = kbuf[...]
    vf = vbuf[...]
    kr = pltpu.roll(kf, HEAD_DIM, 1)
    vr = pltpu.roll(vf, HEAD_DIM, 1)
    kd = (jnp.where(lo, kf, kr).astype(bf16), jnp.where(lo, kr, kf).astype(bf16))
    v_lo = (jnp.where(lo, vf, 0.0).astype(bf16), jnp.where(lo, vr, 0.0).astype(bf16))
    v_hi = (jnp.where(lo, 0.0, vr).astype(bf16), jnp.where(lo, 0.0, vf).astype(bf16))

    ones_blk = jnp.ones((2 * BLOCK, LANES), bf16)

    def attend(i):
        rows = slice(i * BLOCK, (i + 1) * BLOCK)
        band = slice(i * BLOCK, (i + 2) * BLOCK)
        for hd in range(N_HEADS):
            kvh = hd // (N_HEADS // N_KV_HEADS)
            cols = slice(hd // 2 * LANES, (hd // 2 + 1) * LANES)
            qq = q_lo if hd % 2 == 0 else q_hi
            s2 = _dot_nt(qq[rows, cols], kd[kvh][band])
            s = jnp.where(from_prev, s2[:, 0:BLOCK], s2[:, BLOCK:]) + biasm[i * N_HEADS + hd]
            sbuf[hd] = s
            m = jnp.maximum(jnp.max(s, axis=-1, keepdims=True), sinks_ref[hd])
            mbuf[hd] = jnp.broadcast_to(m, (BLOCK, BLOCK))
        for pr in range(N_HEADS // 2):
            kvh = (2 * pr) // (N_HEADS // N_KV_HEADS)
            o = None
            for half, vv in enumerate((v_lo, v_hi)):
                hd = 2 * pr + half
                m = mbuf[hd]
                p = jnp.exp(sbuf[hd] - m)
                p2 = jnp.concatenate([jnp.where(from_prev, p, 0.0), jnp.where(from_prev, 0.0, p)], axis=1)
                r = _dot(p2.astype(bf16), jnp.concatenate([vv[kvh][band], ones_blk], axis=1))
                den = r[:, LANES:] + jnp.exp(sinks_ref[hd] - m)
                oh = r[:, 0:LANES] * (1.0 / den)
                o = oh if o is None else o + oh
            ya[rows, slice(pr * LANES, (pr + 1) * LANES)] = o.astype(bf16)

    u = _gelu_tanh(_dot(h, win_ref[:, GU0:GV0]))
    attend(0)
    vg = _gelu_tanh(_dot(h, win_ref[:, GV0:GA0]))
    mu = jnp.mean(vg, axis=-1, keepdims=True)
    vc = vg - mu
    var = jnp.mean(vc * vc, axis=-1, keepdims=True)
    vn = (vc * lax.rsqrt(var + EPS) * lng_ref[...] + lnb_ref[...]).astype(bf16)
    attend(1)
    ti = lax.broadcasted_iota(jnp.int32, (GM_CHUNK, GM_CHUNK), 0)
    si = lax.broadcasted_iota(jnp.int32, (GM_CHUNK, GM_CHUNK), 1)
    tril = si <= ti
    for g in range(GM_GROUPS):
        wg = jnp.where(tril, ws_ref[g], 0.0).astype(bf16)
        cols = slice(g * LANES, (g + 1) * LANES)
        for cidx in range(TM // GM_CHUNK):
            rows = slice(cidx * GM_CHUNK, (cidx + 1) * GM_CHUNK)
            sv = _dot(wg, vn[rows, cols]) + bsb_ref[g]
            yb[rows, cols] = (u[rows, cols] * sv).astype(bf16)
    gate_a = _sigmoid(_dot(h, win_ref[:, GA0:GB0]))
    attend(2)
    gate_b = _sigmoid(_dot(h, win_ref[:, GB0:IN_END]))
    attend(3)

    kbuf[0:BLOCK, :] = kv[TM - BLOCK:, 0:LANES]
    vbuf[0:BLOCK, :] = kv[TM - BLOCK:, LANES:]
    pbuf[:, 0:BLOCK] = pos_ref[:, TM - BLOCK:]

    merged = gate_a * _dot(ya[...], pa_ref[...]) + gate_b * _dot(yb[...], pb_ref[...])
    x1 = x + g1 * _dot(merged.astype(bf16), wo_ref[...])
    x1_ref[...] = x1

    xn2 = x1 * lax.rsqrt(jnp.mean(x1 * x1, axis=-1, keepdims=True) + EPS)
    h2 = (xn2 * (n2g_ref[...] * (1.0 + sc2)) + sh2).astype(bf16)
    logits = _dot(h2, wr_ref[...]) + br_ref[...]
    lt = jnp.transpose(logits)[0:ROUTER_ROWS, :]
    row = lax.broadcasted_iota(jnp.int32, (ROUTER_ROWS, TM), 0)
    row_f = row.astype(jnp.float32)
    big = float(2 * LANES)
    is_grp = row < N_GROUPS
    lg = jnp.where(is_grp, lt, NEG)
    lg_max = jnp.max(lg, axis=0, keepdims=True)
    g_idx = jnp.min(jnp.where(lg == lg_max, row_f, big), axis=0, keepdims=True)
    p_g = 1.0 / jnp.sum(jnp.where(is_grp, jnp.exp(lg - lg_max), 0.0), axis=0, keepdims=True)
    row_grp = jnp.floor((row_f - EXPERT_LANE0) * (1.0 / EXPERTS_PER_GROUP))
    in_grp = (row >= EXPERT_LANE0) & (row < EXPERT_LANE0 + N_EXPERTS) & (row_grp == g_idx)
    le = jnp.where(in_grp, lt, NEG)
    m1 = jnp.max(le, axis=0, keepdims=True)
    i1 = jnp.min(jnp.where(le == m1, row_f, big), axis=0, keepdims=True)
    oh1 = row_f == i1
    le2 = jnp.where(oh1, NEG, le)
    m2 = jnp.max(le2, axis=0, keepdims=True)
    i2 = jnp.min(jnp.where(le2 == m2, row_f, big), axis=0, keepdims=True)
    oh2 = row_f == i2
    e2 = jnp.exp(m2 - m1)
    w1 = p_g / (1.0 + e2)
    w2 = p_g * e2 / (1.0 + e2)

    oh = jnp.where(oh1, 1.0, jnp.where(oh2, 1.0, 0.0))
    n_e = jnp.sum(oh, axis=1, keepdims=True)
    cnt_ref[...] = n_e
    padded = jnp.floor((n_e + (CHUNK - 1)) * (1.0 / CHUNK)) * CHUNK
    dst_row = lax.broadcasted_iota(jnp.int32, (ROUTER_ROWS, ROUTER_ROWS), 0)
    src_row = lax.broadcasted_iota(jnp.int32, (ROUTER_ROWS, ROUTER_ROWS), 1)
    lower = jnp.where(src_row < dst_row, 1.0, 0.0).astype(bf16)
    run0 = _dot(lower, jnp.broadcast_to(padded, (ROUTER_ROWS, TM)).astype(bf16))
    slot = _dot(oh.astype(bf16), strict[...]) + run0
    pos1 = jnp.sum(jnp.where(oh1, slot, 0.0), axis=0, keepdims=True)
    pos2 = jnp.sum(jnp.where(oh2, slot, 0.0), axis=0, keepdims=True)
    rrow = lax.broadcasted_iota(jnp.int32, (ROUTE_ROWS, TM), 0)
    rw_ref[...] = jnp.where(rrow == 0, w1, jnp.where(rrow == 1, w2,
                            jnp.where(rrow == 2, pos1, jnp.where(rrow == 3, pos2, 0.0))))

    p1 = pos1.astype(jnp.int32)
    p2 = pos2.astype(jnp.int32)
    sp = lax.broadcasted_iota(jnp.int32, (SORT_ROWS, TM), 0)
    perm = jnp.where(sp == p1, 1.0, jnp.where(sp == p2, 1.0, 0.0)).astype(bf16)
    xs_ref[...] = _pack_bf16_pairs(_dot(perm, h2))


def _mixer(x, mod, positions, rel_bias, n1g, w_in, sinks, lng, lnb, w_s, bsb, p_a, p_b, w_o, n2g, w_r, b_r):
    bsz, seq, d = x.shape
    nj = seq // TM
    const = lambda *shape: pl.BlockSpec(shape, lambda b, j: (0,) * len(shape), pipeline_mode=pl.Buffered(1))
    smem = pl.BlockSpec(memory_space=pltpu.SMEM)
    tile = lambda w: pl.BlockSpec((None, TM, w), lambda b, j: (b, j, 0))
    return pl.pallas_call(
        _mixer_kernel,
        grid=(bsz, nj),
        in_specs=[smem, smem, smem,
                  tile(d),
                  pl.BlockSpec((None, 6, d), lambda b, j: (b, 0, 0)),
                  pl.BlockSpec((None, None, 1, TM), lambda b, j: (b, j, 0, 0)),
                  const(1, d), const(d, IN_END), const(1, GM_WIDTH), const(1, GM_WIDTH),
                  const(GM_GROUPS, GM_CHUNK, GM_CHUNK), const(GM_GROUPS, GM_CHUNK, LANES),
                  const(ATTN_Q, d), const(GM_WIDTH, d), const(d, d), const(1, d),
                  const(d, ROUTER_LANES), const(1, ROUTER_LANES)],
        out_specs=[tile(d),
                   pl.BlockSpec((SORT_ROWS, d // 2), lambda b, j: (b * nj + j, 0)),
                   pl.BlockSpec((ROUTE_ROWS, TM), lambda b, j: (0, b * nj + j)),
                   pl.BlockSpec((None, ROUTER_ROWS, 1), lambda b, j: (b * nj + j, 0, 0))],
        out_shape=[jax.ShapeDtypeStruct((bsz, seq, d), jnp.float32),
                   jax.ShapeDtypeStruct((bsz * nj * SORT_ROWS, d // 2), jnp.uint32),
                   jax.ShapeDtypeStruct((ROUTE_ROWS, bsz * seq), jnp.float32),
                   jax.ShapeDtypeStruct((bsz * nj, ROUTER_ROWS, 1), jnp.float32)],
        scratch_shapes=[pltpu.VMEM((TM + BLOCK, LANES), jnp.float32),
                        pltpu.VMEM((TM + BLOCK, LANES), jnp.float32),
                        pltpu.VMEM((1, TM + BLOCK), jnp.int32),
                        pltpu.VMEM((TM // BLOCK * N_HEADS, BLOCK, BLOCK), jnp.float32),
                        pltpu.VMEM((TM // BLOCK, 1, 2 * BLOCK), jnp.int32),
                        pltpu.VMEM((TM, ATTN_Q), jnp.bfloat16),
                        pltpu.VMEM((TM, GM_WIDTH), jnp.bfloat16),
                        pltpu.VMEM((TM, TM), jnp.bfloat16),
                        pltpu.VMEM((N_HEADS, BLOCK, BLOCK), jnp.float32),
                        pltpu.VMEM((N_HEADS, BLOCK, BLOCK), jnp.float32),
                        pltpu.SMEM((TM // BLOCK,), jnp.int32)],
        compiler_params=pltpu.CompilerParams(dimension_semantics=("arbitrary", "arbitrary"),
                                             vmem_limit_bytes=VMEM_LIMIT),
        name="mixer",
    )(rel_bias.reshape(-1), sinks, positions.reshape(-1), x, mod, positions.reshape(bsz, nj, 1, TM), n1g, w_in, lng, lnb,
      w_s, bsb, p_a, p_b, w_o, n2g, w_r, b_r)


def _gather_chunks(idx_ref, idx0, n, src_hbm, dst, sem, inline=False, split_priority=False):
    def issue(c, carry, priority=0):
        src = pl.multiple_of(idx_ref[idx0 + c] * CHUNK, CHUNK)
        pltpu.make_async_copy(src_hbm.at[pl.ds(src, CHUNK)],
                              dst.at[pl.ds(pl.multiple_of(c * CHUNK, CHUNK), CHUNK)], sem).start(priority)
        return carry

    if inline:
        for c in range(n):
            issue(c, 0, c % 2 if split_priority else 0)
    else:
        lax.fori_loop(0, n, issue, 0, unroll=GATHER_UNROLL)


def _wait_chunks(n, src_hbm, dst, sem):
    pltpu.make_async_copy(src_hbm.at[pl.ds(0, n * CHUNK)], dst, sem).wait()


def _expert_kernel(src_ref, be_ref, ne_ref, ws_ref, lb_ref, xs_hbm, wg_hbm, wu_hbm, wd_hbm, yr_ref,
                   xbuf, wgf, wuf, wdf, wgb, wub, wdb, sem, wsem):
    i = pl.program_id(0)
    last = lb_ref[0]
    slot = i % 2

    def weight_copies(e, w):
        return [pltpu.make_async_copy(hbm.at[e], buf.at[w], wsem.at[w])
                for hbm, buf in ((wg_hbm, wgf), (wu_hbm, wuf), (wd_hbm, wdf))]

    @pl.when(i == 0)
    def _():
        _gather_chunks(src_ref, 0, BLOCK_CHUNKS, xs_hbm, xbuf.at[0], sem.at[0])
        for cp in weight_copies(be_ref[0], 0):
            cp.start()

    @pl.when(i <= last)
    def _():
        changed = (i == 0) | (be_ref[i] != be_ref[jnp.maximum(i - 1, 0)])

        @pl.when(changed)
        def _():
            w = ws_ref[i]
            for cp in weight_copies(be_ref[i], w):
                cp.wait()
            wgb[...] = wgf[w].astype(jnp.bfloat16)
            wub[...] = wuf[w].astype(jnp.bfloat16)
            wdb[...] = wdf[w].astype(jnp.bfloat16)

            @pl.when(ne_ref[i] >= 0)
            def _():
                for cp in weight_copies(ne_ref[i], 1 - w):
                    cp.start(priority=1)

        nxt = jnp.minimum(i + 1, last)
        _gather_chunks(src_ref, nxt * BLOCK_CHUNKS, BLOCK_CHUNKS, xs_hbm, xbuf.at[1 - slot], sem.at[1 - slot],
                       inline=True)
        _wait_chunks(BLOCK_CHUNKS, xs_hbm, xbuf.at[slot], sem.at[slot])
        xb = _unpack_bf16_pairs(xbuf[slot])
        hg = _dot(xb, wgb[...])
        hu = _dot(xb, wub[...])
        hid = (hg * _sigmoid(hg) * hu).astype(jnp.bfloat16)
        yr_ref[...] = _pack_bf16_pairs(_dot(hid, wdb[...]).astype(jnp.bfloat16).astype(jnp.float32))

        @pl.when(i == last)
        def _():
            _wait_chunks(BLOCK_CHUNKS, xs_hbm, xbuf.at[1 - slot], sem.at[1 - slot])

    @pl.when(i > last)
    def _():
        yr_ref[...] = jnp.zeros_like(yr_ref)


def _experts(src_chunk, block_e, next_e, w_slot, last_blk, xs, w_gate, w_up, w_down):
    d = w_gate.shape[1]
    n_rows = src_chunk.shape[0] * CHUNK
    hbm = pl.BlockSpec(memory_space=pl.ANY)
    return pl.pallas_call(
        _expert_kernel,
        grid_spec=pltpu.PrefetchScalarGridSpec(
            num_scalar_prefetch=5,
            grid=(n_rows // BM,),
            in_specs=[hbm, hbm, hbm, hbm],
            out_specs=pl.BlockSpec((BM, d // 2), lambda i, *_: (i, 0)),
            scratch_shapes=[pltpu.VMEM((2, BM, d // 2), jnp.uint32),
                            pltpu.VMEM((2, d, D_EXPERT), jnp.float32),
                            pltpu.VMEM((2, d, D_EXPERT), jnp.float32),
                            pltpu.VMEM((2, D_EXPERT, d), jnp.float32),
                            pltpu.VMEM((d, D_EXPERT), jnp.bfloat16),
                            pltpu.VMEM((d, D_EXPERT), jnp.bfloat16),
                            pltpu.VMEM((D_EXPERT, d), jnp.bfloat16),
                            pltpu.SemaphoreType.DMA((2,)),
                            pltpu.SemaphoreType.DMA((2,))]),
        out_shape=jax.ShapeDtypeStruct((n_rows, d // 2), jnp.uint32),
        compiler_params=pltpu.CompilerParams(dimension_semantics=("arbitrary",),
                                             vmem_limit_bytes=VMEM_LIMIT),
        name="experts",
    )(src_chunk, block_e, next_e, w_slot, last_blk, xs, w_gate, w_up, w_down)


def _combine_kernel(src_ref, x1_ref, rw_ref, mod_ref, fg_ref, yr_hbm, out_ref, ybuf, sem, *, final_norm):
    i = pl.program_id(0)
    slot = i % 2

    @pl.when(i == 0)
    def _():
        _gather_chunks(src_ref, 0, SORT_CHUNKS, yr_hbm, ybuf.at[0], sem.at[0])

    rows = jnp.concatenate([rw_ref[...], jnp.zeros((LANES - ROUTE_ROWS, TM), jnp.float32)], axis=0)
    rw = jnp.transpose(rows)
    sp = lax.broadcasted_iota(jnp.int32, (1, SORT_ROWS), 1)
    p1 = rw[:, 2:3].astype(jnp.int32)
    p2 = rw[:, 3:4].astype(jnp.int32)
    wmat = jnp.where(sp == p1, rw[:, 0:1], jnp.where(sp == p2, rw[:, 1:2], 0.0)).astype(jnp.bfloat16)
    nxt = jnp.minimum(i + 1, pl.num_programs(0) - 1)
    _gather_chunks(src_ref, nxt * SORT_CHUNKS, SORT_CHUNKS, yr_hbm, ybuf.at[1 - slot], sem.at[1 - slot],
                   inline=True, split_priority=True)
    _wait_chunks(SORT_CHUNKS, yr_hbm, ybuf.at[slot], sem.at[slot])
    moe = _dot(wmat, _unpack_bf16_pairs(ybuf[slot]))
    x2 = x1_ref[...] + mod_ref[5:6, :] * moe
    if final_norm:
        x2 = x2 * lax.rsqrt(jnp.mean(x2 * x2, axis=-1, keepdims=True) + EPS) * fg_ref[...]
    out_ref[...] = x2

    @pl.when(i == pl.num_programs(0) - 1)
    def _():
        _wait_chunks(SORT_CHUNKS, yr_hbm, ybuf.at[1 - slot], sem.at[1 - slot])


def _combine(src_chunk, x1, rw, mod, final_g, yr, seq, final_norm):
    t, d = x1.shape
    per_seq = seq // TM
    return pl.pallas_call(
        functools.partial(_combine_kernel, final_norm=final_norm),
        grid_spec=pltpu.PrefetchScalarGridSpec(
            num_scalar_prefetch=1,
            grid=(t // TM,),
            in_specs=[pl.BlockSpec((TM, d), lambda i, src: (i, 0)),
                      pl.BlockSpec((ROUTE_ROWS, TM), lambda i, src: (0, i)),
                      pl.BlockSpec((None, 6, d), lambda i, src: (i // per_seq, 0, 0)),
                      pl.BlockSpec((1, d), lambda i, src: (0, 0)),
                      pl.BlockSpec(memory_space=pl.ANY)],
            out_specs=pl.BlockSpec((TM, d), lambda i, src: (i, 0)),
            scratch_shapes=[pltpu.VMEM((2, SORT_ROWS, d // 2), jnp.uint32),
                            pltpu.SemaphoreType.DMA((2,))]),
        out_shape=jax.ShapeDtypeStruct((t, d), jnp.float32),
        compiler_params=pltpu.CompilerParams(dimension_semantics=("arbitrary",),
                                             vmem_limit_bytes=VMEM_LIMIT),
        name="combine",
    )(src_chunk, x1, rw, mod, final_g.reshape(1, d), yr)


def kernel(x, c, positions, rel_bias, w_ada, b_ada, norm1_g, w_in, sinks, gm_ln_g, gm_ln_b, gm_w_s, gm_b_s,
           p_a, p_b, w_o, norm2_g, w_router_g, b_router_g, w_router_e, b_router_e, w_gate, w_up, w_down,
           final_g):
    bsz, seq, d = x.shape
    t = bsz * seq
    bf16 = jnp.bfloat16
    depth = w_ada.shape[0]
    n_tiles = t // TM
    n_chunks = t * TOP_K // CHUNK + n_tiles * N_EXPERTS + N_EXPERTS * BLOCK_CHUNKS
    i32 = jnp.int32
    for l in range(depth):
        mod = _adaln_mod(c, w_ada[l], b_ada[l]).reshape(bsz, 6, d)
        pad = ROUTER_LANES - N_GROUPS - N_EXPERTS
        w_r = jnp.concatenate([w_router_g[l], w_router_e[l], jnp.zeros((d, pad), jnp.float32)], axis=1)
        b_r = jnp.concatenate([b_router_g[l], b_router_e[l], jnp.zeros((pad,), jnp.float32)]).reshape(1, -1)
        bsb = jnp.broadcast_to(gm_b_s[l][:, :, None], (GM_GROUPS, GM_CHUNK, LANES))
        x1, xs, rw, cnt = _mixer(
            x, mod, positions, rel_bias, norm1_g[l].reshape(1, d), w_in[l].astype(bf16), sinks[l],
            gm_ln_g[l].reshape(1, -1), gm_ln_b[l].reshape(1, -1), gm_w_s[l], bsb,
            p_a[l].astype(bf16), p_b[l].astype(bf16), w_o[l].astype(bf16), norm2_g[l].reshape(1, d),
            w_r.astype(bf16), b_r)

        n = cnt[:, EXPERT_LANE0:EXPERT_LANE0 + N_EXPERTS, 0].astype(i32)
        c = (n + CHUNK - 1) // CHUNK
        run0_tile = jnp.cumsum(c, axis=1) - c
        run0_exp = jnp.cumsum(c, axis=0) - c
        tot = jnp.sum(c, axis=0)
        seg = (tot + BLOCK_CHUNKS - 1) // BLOCK_CHUNKS * BLOCK_CHUNKS
        seg_end = jnp.cumsum(seg)
        seg0 = seg_end - seg

        pos_tile = jnp.arange(n_tiles, dtype=i32)[:, None] * SORT_CHUNKS + run0_tile
        pos_exp = seg0[None, :] + run0_exp

        j = jnp.arange(n_chunks, dtype=i32)[:, None]
        e_j = jnp.minimum(jnp.sum(j >= seg_end[None, :], axis=1), N_EXPERTS - 1)
        tables = jnp.concatenate([pos_exp.T, (pos_exp + c).T, (pos_tile - pos_exp).T], axis=1)
        picked = jnp.dot(jax.nn.one_hot(e_j, N_EXPERTS, dtype=jnp.float32), tables.astype(jnp.float32),
                         precision=lax.Precision.HIGHEST).astype(i32)
        lo_j, hi_j, shift_j = picked[:, :n_tiles], picked[:, n_tiles:2 * n_tiles], picked[:, 2 * n_tiles:]
        inside = (j >= lo_j) & (j < hi_j)
        src_j = jnp.sum(jnp.where(inside, j + shift_j, 0), axis=1).astype(i32)
        blk0 = jnp.arange(n_chunks // BLOCK_CHUNKS, dtype=i32) * BLOCK_CHUNKS
        block_e = jnp.minimum(jnp.sum(blk0[:, None] >= seg_end[None, :], axis=1), N_EXPERTS - 1).astype(i32)
        last_blk = (jnp.maximum(seg_end[-1] // BLOCK_CHUNKS, 1) - 1).astype(i32).reshape(1)
        experts = jnp.arange(N_EXPERTS, dtype=i32)
        later = (experts[None, :] > block_e[:, None]) & (seg[None, :] > 0)
        next_e = jnp.min(jnp.where(later, experts[None, :], N_EXPERTS), axis=1)
        next_e = jnp.where(next_e < N_EXPERTS, next_e, -1).astype(i32)
        w_slot = (jnp.sum((experts[None, :] < block_e[:, None]) & (seg[None, :] > 0), axis=1) % 2).astype(i32)

        q = jnp.arange(SORT_CHUNKS, dtype=i32)[None, :, None]
        in_run = (q >= run0_tile[:, None, :]) & (q < (run0_tile + c)[:, None, :])
        dst_q = jnp.sum(jnp.where(in_run, q + (pos_exp - run0_tile)[:, None, :], 0), axis=2).astype(i32).reshape(-1)

        yr = _experts(src_j, block_e, next_e, w_slot, last_blk, xs, w_gate[l], w_up[l], w_down[l])
        x = _combine(dst_q, x1.reshape(t, d), rw, mod, final_g, yr, seq,
                     final_norm=(l == depth - 1)).reshape(bsz, seq, d)
    return x
```

```python
import functools
import math

import jax
import jax.numpy as jnp
from jax import lax
from jax.experimental import pallas as pl
from jax.experimental.pallas import tpu as pltpu

D_MODEL = 1024
N_HEADS = 8
N_KV_HEADS = 2
HEAD_DIM = 64
BLOCK = 128
ATTN_Q = N_HEADS * HEAD_DIM
ATTN_KV = N_KV_HEADS * HEAD_DIM
N_BUCKETS = 32
MAX_EXACT = N_BUCKETS // 2
MAX_DISTANCE = 128
GM_WIDTH = 512
GM_GROUPS = 4
GM_CHUNK = 128
N_GROUPS = 4
EXPERTS_PER_GROUP = 8
N_EXPERTS = N_GROUPS * EXPERTS_PER_GROUP
TOP_K = 2
D_EXPERT = 512
EPS = 1e-6
NEG = -1e30

LANES = 128
ROUTER_LANES = LANES
EXPERT_LANE0 = N_GROUPS
TM = 512
BM = 512
ROUTER_ROWS = 48
ROUTE_ROWS = 8
CHUNK = 8
SORT_ROWS = TM * TOP_K + N_EXPERTS * CHUNK
SORT_CHUNKS = SORT_ROWS // CHUNK
BLOCK_CHUNKS = BM // CHUNK
GATHER_UNROLL = 8
COMBINE_SPLIT = 2
TAIL_SPLIT = 2
VMEM_LIMIT = 56 * 1024 * 1024

Q0, K0, V0, GU0, GV0, GA0, GB0, IN_END = (0, 512, 640, 768, 1280, 1792, 2816, 3840)


def _dot(a, b):
    return jnp.dot(a, b, preferred_element_type=jnp.float32)


def _dot_nt(a, b):
    return lax.dot_general(a, b, (((1,), (1,)), ((), ())), preferred_element_type=jnp.float32)


LOG2E = math.log2(math.e)


def _gelu_tanh(x):
    c = math.sqrt(2.0 / math.pi)
    k0, k1 = -2.0 * c * LOG2E, -2.0 * c * 0.044715 * LOG2E
    return x * (1.0 / (1.0 + jnp.exp2(x * (x * x * k1 + k0))))


def _sigmoid(x):
    return 1.0 / (1.0 + jnp.exp2(x * -LOG2E))


def _pack_bf16_pairs(x):
    bits = lax.bitcast_convert_type(x, jnp.uint32)
    half = x.shape[1] // 2
    return (bits[:, 0:half] >> 16) | (bits[:, half:] & jnp.uint32(0xFFFF0000))


def _unpack_bf16_pairs(w):
    lo = lax.bitcast_convert_type(w << 16, jnp.float32)
    hi = lax.bitcast_convert_type(w & jnp.uint32(0xFFFF0000), jnp.float32)
    return jnp.concatenate([lo, hi], axis=1).astype(jnp.bfloat16)


def _adaln_kernel(c_ref, w_ref, b_ref, o_ref):
    c = c_ref[...]
    cs = c * _sigmoid(c)
    o_ref[...] = _dot(cs, w_ref[...]) + b_ref[...]


def _adaln_mod(c, w, b):
    bsz, d = c.shape
    n = w.shape[1]
    tn = 1024
    return pl.pallas_call(
        _adaln_kernel,
        grid=(n // tn,),
        in_specs=[pl.BlockSpec((bsz, d), lambda i: (0, 0)),
                  pl.BlockSpec((d, tn), lambda i: (0, i)),
                  pl.BlockSpec((1, tn), lambda i: (0, i))],
        out_specs=pl.BlockSpec((bsz, tn), lambda i: (0, i)),
        out_shape=jax.ShapeDtypeStruct((bsz, n), jnp.float32),
        compiler_params=pltpu.CompilerParams(dimension_semantics=("arbitrary",)),
        name="adaln_mod",
    )(c, w, b.reshape(1, n))


def _mixer_kernel(relb_ref, sinks_ref, spos_ref,
                  x_ref, mod_ref, pos_ref, n1g_ref, win_ref, lng_ref, lnb_ref, ws_ref, bsb_ref,
                  pa_ref, pb_ref, wo_ref, n2g_ref, wr_ref, br_ref,
                  x1_ref, xs_ref, rw_ref, cnt_ref,
                  kbuf, vbuf, pbuf, biasm, key, ya, yb, strict, sbuf, mbuf, flag):
    b = pl.program_id(0)
    j = pl.program_id(1)
    nblk = TM // BLOCK
    bf16 = jnp.bfloat16

    @pl.when((b == 0) & (j == 0))
    def _():
        key[...] = jnp.zeros_like(key)
        for i in range(nblk):
            flag[i] = 1
        tr = lax.broadcasted_iota(jnp.int32, (TM, TM), 0)
        tc = lax.broadcasted_iota(jnp.int32, (TM, TM), 1)
        strict[...] = jnp.where(tr < tc, 1.0, 0.0).astype(bf16)

    @pl.when(j == 0)
    def _():
        kbuf[0:BLOCK, :] = jnp.zeros((BLOCK, LANES), jnp.float32)
        vbuf[0:BLOCK, :] = jnp.zeros((BLOCK, LANES), jnp.float32)
        pbuf[:, 0:BLOCK] = jnp.zeros((1, BLOCK), jnp.int32)

    pbuf[:, BLOCK:] = pos_ref[...]
    qi = lax.broadcasted_iota(jnp.int32, (BLOCK, BLOCK), 0)
    kc = lax.broadcasted_iota(jnp.int32, (BLOCK, BLOCK), 1)
    from_prev = kc > qi
    rels, changed = [], []
    tile0 = (b * pl.num_programs(1) + j) * TM
    for i in range(nblk):
        pk = pbuf[:, i * BLOCK:(i + 2) * BLOCK]
        rels.append(pk - spos_ref[tile0 + i * BLOCK])
        changed.append(jnp.where(rels[i] != key[i], 1.0, 0.0))
    any_changed = jnp.max(functools.reduce(jnp.maximum, changed))
    any_flag = functools.reduce(jnp.maximum, [flag[i] for i in range(nblk)])
    refresh = (any_changed != 0) | (any_flag != 0) | (j == 0)

    def refresh_bias(i):
        pk = pbuf[:, i * BLOCK:(i + 2) * BLOCK]
        pq = pk[:, BLOCK:]
        pq_col = jnp.transpose(jnp.broadcast_to(pq, (BLOCK, BLOCK)))
        no_prev = (j == 0) if i == 0 else None
        for hd in range(N_HEADS):
            biasm[i * N_HEADS + hd] = jnp.zeros((BLOCK, BLOCK), jnp.float32)
        for side in range(2):
            dist = pq_col - pk[:, side * BLOCK:(side + 1) * BLOCK]
            n = jnp.maximum(dist, 0)
            nf = jnp.maximum(n, 1).astype(jnp.float32)
            large = MAX_EXACT + (jnp.log(nf / MAX_EXACT) / math.log(MAX_DISTANCE / MAX_EXACT)
                                 * (N_BUCKETS - MAX_EXACT)).astype(jnp.int32)
            large = jnp.minimum(large, N_BUCKETS - 1)
            bucket = jnp.where(n < MAX_EXACT, n, large)
            use = from_prev if side == 0 else jnp.logical_not(from_prev)
            for hd in range(N_HEADS):
                acc = jnp.zeros((BLOCK, BLOCK), jnp.float32)
                for bk in range(N_BUCKETS):
                    acc = jnp.where(bucket == bk, relb_ref[bk * N_HEADS + hd], acc)
                if side == 0 and i == 0:
                    acc = jnp.where(no_prev, NEG, acc)
                slot = i * N_HEADS + hd
                biasm[slot] = jnp.where(use, acc, biasm[slot])
        key[i] = rels[i]
        flag[i] = no_prev.astype(jnp.int32) if i == 0 else 0

    for i in range(nblk):
        @pl.when(refresh)
        def _():
            stale = (jnp.max(changed[i]) != 0) | (flag[i] != 0)
            if i == 0:
                stale = stale | (j == 0)
            pl.when(stale)(functools.partial(refresh_bias, i))

    x = x_ref[...]
    sh1, sc1, g1 = mod_ref[0:1, :], mod_ref[1:2, :], mod_ref[2:3, :]
    sh2, sc2 = mod_ref[3:4, :], mod_ref[4:5, :]

    xn = x * lax.rsqrt(jnp.mean(x * x, axis=-1, keepdims=True) + EPS)
    h = (xn * (n1g_ref[...] * (1.0 + sc1)) + sh1).astype(bf16)

    lane = lax.broadcasted_iota(jnp.int32, (1, LANES), 1)
    lo = lane < HEAD_DIM
    q = _dot(h, win_ref[:, Q0:K0]) * (HEAD_DIM ** -0.5)
    lo4 = jnp.concatenate([lo] * (ATTN_Q // LANES), axis=1)
    q_lo = jnp.where(lo4, q, 0.0).astype(bf16)
    q_hi = jnp.where(lo4, 0.0, q).astype(bf16)
    kv = _dot(h, win_ref[:, K0:GU0])
    kbuf[BLOCK:, :] = kv[:, 0:LANES]
    vbuf[BLOCK:, :] = kv[:, LANES:]
    kf = kbuf[...]
    vf = vbuf[...]
    kr = pltpu.roll(kf, HEAD_DIM, 1)
    vr = pltpu.roll(vf, HEAD_DIM, 1)
    kd = (jnp.where(lo, kf, kr).astype(bf16), jnp.where(lo, kr, kf).astype(bf16))
    v_lo = (jnp.where(lo, vf, 0.0).astype(bf16), jnp.where(lo, vr, 0.0).astype(bf16))
    v_hi = (jnp.where(lo, 0.0, vr).astype(bf16), jnp.where(lo, 0.0, vf).astype(bf16))

    ones_blk = jnp.ones((2 * BLOCK, LANES), bf16)

    def attend(i):
        rows = slice(i * BLOCK, (i + 1) * BLOCK)
        band = slice(i * BLOCK, (i + 2) * BLOCK)
        for hd in range(N_HEADS):
            kvh = hd // (N_HEADS // N_KV_HEADS)
            cols = slice(hd // 2 * LANES, (hd // 2 + 1) * LANES)
            qq = q_lo if hd % 2 == 0 else q_hi
            s2 = _dot_nt(qq[rows, cols], kd[kvh][band])
            s = jnp.where(from_prev, s2[:, 0:BLOCK], s2[:, BLOCK:]) + biasm[i * N_HEADS + hd]
            sbuf[hd] = s
            m = jnp.maximum(jnp.max(s, axis=-1, keepdims=True), sinks_ref[hd])
            mbuf[hd] = jnp.broadcast_to(m, (BLOCK, BLOCK))
        for pr in range(N_HEADS // 2):
            kvh = (2 * pr) // (N_HEADS // N_KV_HEADS)
            o = None
            for half, vv in enumerate((v_lo, v_hi)):
                hd = 2 * pr + half
                m = mbuf[hd]
                p = jnp.exp(sbuf[hd] - m)
                p2 = jnp.concatenate([jnp.where(from_prev, p, 0.0), jnp.where(from_prev, 0.0, p)], axis=1)
                r = _dot(p2.astype(bf16), jnp.concatenate([vv[kvh][band], ones_blk], axis=1))
                den = r[:, LANES:] + jnp.exp(sinks_ref[hd] - m)
                oh = r[:, 0:LANES] * (1.0 / den)
                o = oh if o is None else o + oh
            ya[rows, slice(pr * LANES, (pr + 1) * LANES)] = o.astype(bf16)

    u = _gelu_tanh(_dot(h, win_ref[:, GU0:GV0]))
    attend(0)
    vg = _gelu_tanh(_dot(h, win_ref[:, GV0:GA0]))
    mu = jnp.mean(vg, axis=-1, keepdims=True)
    vc = vg - mu
    var = jnp.mean(vc * vc, axis=-1, keepdims=True)
    vn = (vc * lax.rsqrt(var + EPS) * lng_ref[...] + lnb_ref[...]).astype(bf16)
    attend(1)
    gate_a = _sigmoid(_dot(h, win_ref[:, GA0:GB0]))
    attend(2)
    ti = lax.broadcasted_iota(jnp.int32, (GM_CHUNK, GM_CHUNK), 0)
    si = lax.broadcasted_iota(jnp.int32, (GM_CHUNK, GM_CHUNK), 1)
    tril = si <= ti
    for g in range(GM_GROUPS):
        wg = jnp.where(tril, ws_ref[g], 0.0).astype(bf16)
        cols = slice(g * LANES, (g + 1) * LANES)
        for cidx in range(TM // GM_CHUNK):
            rows = slice(cidx * GM_CHUNK, (cidx + 1) * GM_CHUNK)
            sv = _dot(wg, vn[rows, cols]) + bsb_ref[g]
            yb[rows, cols] = (u[rows, cols] * sv).astype(bf16)
    gate_b = _sigmoid(_dot(h, win_ref[:, GB0:IN_END]))
    attend(3)

    kbuf[0:BLOCK, :] = kv[TM - BLOCK:, 0:LANES]
    vbuf[0:BLOCK, :] = kv[TM - BLOCK:, LANES:]
    pbuf[:, 0:BLOCK] = pos_ref[:, TM - BLOCK:]

    h2_parts, logit_parts = [], []
    for grp in range(TAIL_SPLIT):
        tok = slice(grp * TM // TAIL_SPLIT, (grp + 1) * TM // TAIL_SPLIT)
        merged = gate_a[tok] * _dot(ya[tok, :], pa_ref[...]) + gate_b[tok] * _dot(yb[tok, :], pb_ref[...])
        x1 = x[tok] + g1 * _dot(merged.astype(bf16), wo_ref[...])
        x1_ref[tok, :] = x1

        xn2 = x1 * lax.rsqrt(jnp.mean(x1 * x1, axis=-1, keepdims=True) + EPS)
        h2_parts.append((xn2 * (n2g_ref[...] * (1.0 + sc2)) + sh2).astype(bf16))
        logit_parts.append(_dot(h2_parts[-1], wr_ref[...]) + br_ref[...])
    h2 = jnp.concatenate(h2_parts, axis=0)
    logits = jnp.concatenate(logit_parts, axis=0)
    lt = jnp.transpose(logits)[0:ROUTER_ROWS, :]
    row = lax.broadcasted_iota(jnp.int32, (ROUTER_ROWS, TM), 0)
    row_f = row.astype(jnp.float32)
    big = float(2 * LANES)
    is_grp = row < N_GROUPS
    lg = jnp.where(is_grp, lt, NEG)
    lg_max = jnp.max(lg, axis=0, keepdims=True)
    g_idx = jnp.min(jnp.where(lg == lg_max, row_f, big), axis=0, keepdims=True)
    p_g = 1.0 / jnp.sum(jnp.where(is_grp, jnp.exp(lg - lg_max), 0.0), axis=0, keepdims=True)
    row_grp = jnp.floor((row_f - EXPERT_LANE0) * (1.0 / EXPERTS_PER_GROUP))
    in_grp = (row >= EXPERT_LANE0) & (row < EXPERT_LANE0 + N_EXPERTS) & (row_grp == g_idx)
    le = jnp.where(in_grp, lt, NEG)
    m1 = jnp.max(le, axis=0, keepdims=True)
    i1 = jnp.min(jnp.where(le == m1, row_f, big), axis=0, keepdims=True)
    oh1 = row_f == i1
    le2 = jnp.where(oh1, NEG, le)
    m2 = jnp.max(le2, axis=0, keepdims=True)
    i2 = jnp.min(jnp.where(le2 == m2, row_f, big), axis=0, keepdims=True)
    oh2 = row_f == i2
    e2 = jnp.exp(m2 - m1)
    w1 = p_g / (1.0 + e2)
    w2 = p_g * e2 / (1.0 + e2)

    oh = jnp.where(oh1, 1.0, jnp.where(oh2, 1.0, 0.0))
    n_e = jnp.sum(oh, axis=1, keepdims=True)
    cnt_ref[...] = n_e
    padded = jnp.floor((n_e + (CHUNK - 1)) * (1.0 / CHUNK)) * CHUNK
    dst_row = lax.broadcasted_iota(jnp.int32, (ROUTER_ROWS, ROUTER_ROWS), 0)
    src_row = lax.broadcasted_iota(jnp.int32, (ROUTER_ROWS, ROUTER_ROWS), 1)
    lower = jnp.where(src_row < dst_row, 1.0, 0.0).astype(bf16)
    run0 = _dot(lower, jnp.broadcast_to(padded, (ROUTER_ROWS, TM)).astype(bf16))
    slot = _dot(oh.astype(bf16), strict[...]) + run0
    pos1 = jnp.sum(jnp.where(oh1, slot, 0.0), axis=0, keepdims=True)
    pos2 = jnp.sum(jnp.where(oh2, slot, 0.0), axis=0, keepdims=True)
    rrow = lax.broadcasted_iota(jnp.int32, (ROUTE_ROWS, TM), 0)
    rw_ref[...] = jnp.where(rrow == 0, w1, jnp.where(rrow == 1, w2,
                            jnp.where(rrow == 2, pos1, jnp.where(rrow == 3, pos2, 0.0))))

    p1 = pos1.astype(jnp.int32)
    p2 = pos2.astype(jnp.int32)
    sp = lax.broadcasted_iota(jnp.int32, (SORT_ROWS, TM), 0)
    perm = jnp.where(sp == p1, 1.0, jnp.where(sp == p2, 1.0, 0.0)).astype(bf16)
    xs_ref[...] = _pack_bf16_pairs(_dot(perm, h2))


def _mixer(x, mod, positions, rel_bias, n1g, w_in, sinks, lng, lnb, w_s, bsb, p_a, p_b, w_o, n2g, w_r, b_r):
    bsz, seq, d = x.shape
    nj = seq // TM
    const = lambda *shape: pl.BlockSpec(shape, lambda b, j: (0,) * len(shape), pipeline_mode=pl.Buffered(1))
    smem = pl.BlockSpec(memory_space=pltpu.SMEM)
    tile = lambda w: pl.BlockSpec((None, TM, w), lambda b, j: (b, j, 0))
    return pl.pallas_call(
        _mixer_kernel,
        grid=(bsz, nj),
        in_specs=[smem, smem, smem,
                  tile(d),
                  pl.BlockSpec((None, 6, d), lambda b, j: (b, 0, 0)),
                  pl.BlockSpec((None, None, 1, TM), lambda b, j: (b, j, 0, 0)),
                  const(1, d), const(d, IN_END), const(1, GM_WIDTH), const(1, GM_WIDTH),
                  const(GM_GROUPS, GM_CHUNK, GM_CHUNK), const(GM_GROUPS, GM_CHUNK, LANES),
                  const(ATTN_Q, d), const(GM_WIDTH, d), const(d, d), const(1, d),
                  const(d, ROUTER_LANES), const(1, ROUTER_LANES)],
        out_specs=[tile(d),
                   pl.BlockSpec((SORT_ROWS, d // 2), lambda b, j: (b * nj + j, 0)),
                   pl.BlockSpec((ROUTE_ROWS, TM), lambda b, j: (0, b * nj + j)),
                   pl.BlockSpec((None, ROUTER_ROWS, 1), lambda b, j: (b * nj + j, 0, 0))],
        out_shape=[jax.ShapeDtypeStruct((bsz, seq, d), jnp.float32),
                   jax.ShapeDtypeStruct((bsz * nj * SORT_ROWS, d // 2), jnp.uint32),
                   jax.ShapeDtypeStruct((ROUTE_ROWS, bsz * seq), jnp.float32),
                   jax.ShapeDtypeStruct((bsz * nj, ROUTER_ROWS, 1), jnp.float32)],
        scratch_shapes=[pltpu.VMEM((TM + BLOCK, LANES), jnp.float32),
                        pltpu.VMEM((TM + BLOCK, LANES), jnp.float32),
                        pltpu.VMEM((1, TM + BLOCK), jnp.int32),
                        pltpu.VMEM((TM // BLOCK * N_HEADS, BLOCK, BLOCK), jnp.float32),
                        pltpu.VMEM((TM // BLOCK, 1, 2 * BLOCK), jnp.int32),
                        pltpu.VMEM((TM, ATTN_Q), jnp.bfloat16),
                        pltpu.VMEM((TM, GM_WIDTH), jnp.bfloat16),
                        pltpu.VMEM((TM, TM), jnp.bfloat16),
                        pltpu.VMEM((N_HEADS, BLOCK, BLOCK), jnp.float32),
                        pltpu.VMEM((N_HEADS, BLOCK, BLOCK), jnp.float32),
                        pltpu.SMEM((TM // BLOCK,), jnp.int32)],
        compiler_params=pltpu.CompilerParams(dimension_semantics=("arbitrary", "arbitrary"),
                                             vmem_limit_bytes=VMEM_LIMIT),
        name="mixer",
    )(rel_bias.reshape(-1), sinks, positions.reshape(-1), x, mod, positions.reshape(bsz, nj, 1, TM), n1g, w_in, lng, lnb,
      w_s, bsb, p_a, p_b, w_o, n2g, w_r, b_r)


def _gather_chunks(idx_ref, idx0, n, src_hbm, dst, sem, inline=False, split_priority=False):
    def issue(c, carry, priority=0):
        src = pl.multiple_of(idx_ref[idx0 + c] * CHUNK, CHUNK)
        pltpu.make_async_copy(src_hbm.at[pl.ds(src, CHUNK)],
                              dst.at[pl.ds(pl.multiple_of(c * CHUNK, CHUNK), CHUNK)], sem).start(priority)
        return carry

    if inline:
        for c in range(n):
            issue(c, 0, c % 2 if split_priority else 0)
    else:
        lax.fori_loop(0, n, issue, 0, unroll=GATHER_UNROLL)


def _wait_chunks(n, src_hbm, dst, sem):
    pltpu.make_async_copy(src_hbm.at[pl.ds(0, n * CHUNK)], dst, sem).wait()


def _expert_kernel(src_ref, be_ref, ne_ref, ws_ref, lb_ref, xs_hbm, wg_hbm, wu_hbm, wd_hbm, yr_ref,
                   xbuf, wgf, wuf, wdf, wgb, wub, wdb, sem, wsem):
    i = pl.program_id(0)
    last = lb_ref[0]
    slot = i % 2

    def weight_copies(e, w):
        return [pltpu.make_async_copy(hbm.at[e], buf.at[w], wsem.at[w])
                for hbm, buf in ((wg_hbm, wgf), (wu_hbm, wuf), (wd_hbm, wdf))]

    @pl.when(i == 0)
    def _():
        _gather_chunks(src_ref, 0, BLOCK_CHUNKS, xs_hbm, xbuf.at[0], sem.at[0])
        for cp in weight_copies(be_ref[0], 0):
            cp.start()

    @pl.when(i <= last)
    def _():
        changed = (i == 0) | (be_ref[i] != be_ref[jnp.maximum(i - 1, 0)])

        @pl.when(changed)
        def _():
            w = ws_ref[i]
            for cp in weight_copies(be_ref[i], w):
                cp.wait()
            wgb[...] = wgf[w].astype(jnp.bfloat16)
            wub[...] = wuf[w].astype(jnp.bfloat16)
            wdb[...] = wdf[w].astype(jnp.bfloat16)

            @pl.when(ne_ref[i] >= 0)
            def _():
                for cp in weight_copies(ne_ref[i], 1 - w):
                    cp.start(priority=1)

        nxt = jnp.minimum(i + 1, last)
        _gather_chunks(src_ref, nxt * BLOCK_CHUNKS, BLOCK_CHUNKS, xs_hbm, xbuf.at[1 - slot], sem.at[1 - slot],
                       inline=True)
        _wait_chunks(BLOCK_CHUNKS, xs_hbm, xbuf.at[slot], sem.at[slot])
        xb = _unpack_bf16_pairs(xbuf[slot])
        hg = _dot(xb, wgb[...])
        hu = _dot(xb, wub[...])
        hid = (hg * _sigmoid(hg) * hu).astype(jnp.bfloat16)
        yr_ref[...] = _pack_bf16_pairs(_dot(hid, wdb[...]).astype(jnp.bfloat16).astype(jnp.float32))

        @pl.when(i == last)
        def _():
            _wait_chunks(BLOCK_CHUNKS, xs_hbm, xbuf.at[1 - slot], sem.at[1 - slot])

    @pl.when(i > last)
    def _():
        yr_ref[...] = jnp.zeros_like(yr_ref)


def _experts(src_chunk, block_e, next_e, w_slot, last_blk, xs, w_gate, w_up, w_down):
    d = w_gate.shape[1]
    n_rows = src_chunk.shape[0] * CHUNK
    hbm = pl.BlockSpec(memory_space=pl.ANY)
    return pl.pallas_call(
        _expert_kernel,
        grid_spec=pltpu.PrefetchScalarGridSpec(
            num_scalar_prefetch=5,
            grid=(n_rows // BM,),
            in_specs=[hbm, hbm, hbm, hbm],
            out_specs=pl.BlockSpec((BM, d // 2), lambda i, *_: (i, 0)),
            scratch_shapes=[pltpu.VMEM((2, BM, d // 2), jnp.uint32),
                            pltpu.VMEM((2, d, D_EXPERT), jnp.float32),
                            pltpu.VMEM((2, d, D_EXPERT), jnp.float32),
                            pltpu.VMEM((2, D_EXPERT, d), jnp.float32),
                            pltpu.VMEM((d, D_EXPERT), jnp.bfloat16),
                            pltpu.VMEM((d, D_EXPERT), jnp.bfloat16),
                            pltpu.VMEM((D_EXPERT, d), jnp.bfloat16),
                            pltpu.SemaphoreType.DMA((2,)),
                            pltpu.SemaphoreType.DMA((2,))]),
        out_shape=jax.ShapeDtypeStruct((n_rows, d // 2), jnp.uint32),
        compiler_params=pltpu.CompilerParams(dimension_semantics=("arbitrary",),
                                             vmem_limit_bytes=VMEM_LIMIT),
        name="experts",
    )(src_chunk, block_e, next_e, w_slot, last_blk, xs, w_gate, w_up, w_down)


def _combine_kernel(src_ref, x1_ref, rw_ref, mod_ref, fg_ref, yr_hbm, out_ref, ybuf, sem, *, final_norm):
    i = pl.program_id(0)
    slot = i % 2

    @pl.when(i == 0)
    def _():
        _gather_chunks(src_ref, 0, SORT_CHUNKS, yr_hbm, ybuf.at[0], sem.at[0])

    rows = jnp.concatenate([rw_ref[...], jnp.zeros((LANES - ROUTE_ROWS, TM), jnp.float32)], axis=0)
    rw = jnp.transpose(rows)
    sp = lax.broadcasted_iota(jnp.int32, (1, SORT_ROWS), 1)
    p1 = rw[:, 2:3].astype(jnp.int32)
    p2 = rw[:, 3:4].astype(jnp.int32)
    nxt = jnp.minimum(i + 1, pl.num_programs(0) - 1)
    _gather_chunks(src_ref, nxt * SORT_CHUNKS, SORT_CHUNKS, yr_hbm, ybuf.at[1 - slot], sem.at[1 - slot],
                   inline=True, split_priority=True)
    _wait_chunks(SORT_CHUNKS, yr_hbm, ybuf.at[slot], sem.at[slot])
    ys = _unpack_bf16_pairs(ybuf[slot])
    for half in range(COMBINE_SPLIT):
        tok = slice(half * TM // COMBINE_SPLIT, (half + 1) * TM // COMBINE_SPLIT)
        wmat = jnp.where(sp == p1[tok], rw[tok, 0:1], jnp.where(sp == p2[tok], rw[tok, 1:2], 0.0))
        moe = _dot(wmat.astype(jnp.bfloat16), ys)
        x2 = x1_ref[tok, :] + mod_ref[5:6, :] * moe
        if final_norm:
            x2 = x2 * lax.rsqrt(jnp.mean(x2 * x2, axis=-1, keepdims=True) + EPS) * fg_ref[...]
        out_ref[tok, :] = x2

    @pl.when(i == pl.num_programs(0) - 1)
    def _():
        _wait_chunks(SORT_CHUNKS, yr_hbm, ybuf.at[1 - slot], sem.at[1 - slot])


def _combine(src_chunk, x1, rw, mod, final_g, yr, seq, final_norm):
    t, d = x1.shape
    per_seq = seq // TM
    return pl.pallas_call(
        functools.partial(_combine_kernel, final_norm=final_norm),
        grid_spec=pltpu.PrefetchScalarGridSpec(
            num_scalar_prefetch=1,
            grid=(t // TM,),
            in_specs=[pl.BlockSpec((TM, d), lambda i, src: (i, 0)),
                      pl.BlockSpec((ROUTE_ROWS, TM), lambda i, src: (0, i)),
                      pl.BlockSpec((None, 6, d), lambda i, src: (i // per_seq, 0, 0)),
                      pl.BlockSpec((1, d), lambda i, src: (0, 0)),
                      pl.BlockSpec(memory_space=pl.ANY)],
            out_specs=pl.BlockSpec((TM, d), lambda i, src: (i, 0)),
            scratch_shapes=[pltpu.VMEM((2, SORT_ROWS, d // 2), jnp.uint32),
                            pltpu.SemaphoreType.DMA((2,))]),
        out_shape=jax.ShapeDtypeStruct((t, d), jnp.float32),
        compiler_params=pltpu.CompilerParams(dimension_semantics=("arbitrary",),
                                             vmem_limit_bytes=VMEM_LIMIT),
        name="combine",
    )(src_chunk, x1, rw, mod, final_g.reshape(1, d), yr)


def kernel(x, c, positions, rel_bias, w_ada, b_ada, norm1_g, w_in, sinks, gm_ln_g, gm_ln_b, gm_w_s, gm_b_s,
           p_a, p_b, w_o, norm2_g, w_router_g, b_router_g, w_router_e, b_router_e, w_gate, w_up, w_down,
           final_g):
    bsz, seq, d = x.shape
    t = bsz * seq
    bf16 = jnp.bfloat16
    depth = w_ada.shape[0]
    n_tiles = t // TM
    n_chunks = t * TOP_K // CHUNK + n_tiles * N_EXPERTS + N_EXPERTS * BLOCK_CHUNKS
    i32 = jnp.int32
    for l in range(depth):
        mod = _adaln_mod(c, w_ada[l], b_ada[l]).reshape(bsz, 6, d)
        pad = ROUTER_LANES - N_GROUPS - N_EXPERTS
        w_r = jnp.concatenate([w_router_g[l], w_router_e[l], jnp.zeros((d, pad), jnp.float32)], axis=1)
        b_r = jnp.concatenate([b_router_g[l], b_router_e[l], jnp.zeros((pad,), jnp.float32)]).reshape(1, -1)
        bsb = jnp.broadcast_to(gm_b_s[l][:, :, None], (GM_GROUPS, GM_CHUNK, LANES))
        x1, xs, rw, cnt = _mixer(
            x, mod, positions, rel_bias, norm1_g[l].reshape(1, d), w_in[l].astype(bf16), sinks[l],
            gm_ln_g[l].reshape(1, -1), gm_ln_b[l].reshape(1, -1), gm_w_s[l], bsb,
            p_a[l].astype(bf16), p_b[l].astype(bf16), w_o[l].astype(bf16), norm2_g[l].reshape(1, d),
            w_r.astype(bf16), b_r)

        n = cnt[:, EXPERT_LANE0:EXPERT_LANE0 + N_EXPERTS, 0].astype(i32)
        nch = (n + CHUNK - 1) // CHUNK
        run0_tile = jnp.cumsum(nch, axis=1) - nch
        run0_exp = jnp.cumsum(nch, axis=0) - nch
        tot = jnp.sum(nch, axis=0)
        seg = (tot + BLOCK_CHUNKS - 1) // BLOCK_CHUNKS * BLOCK_CHUNKS
        seg_end = jnp.cumsum(seg)
        seg0 = seg_end - seg

        pos_tile = jnp.arange(n_tiles, dtype=i32)[:, None] * SORT_CHUNKS + run0_tile
        pos_exp = seg0[None, :] + run0_exp

        j = jnp.arange(n_chunks, dtype=i32)[:, None]
        e_j = jnp.minimum(jnp.sum(j >= seg_end[None, :], axis=1), N_EXPERTS - 1)
        tables = jnp.concatenate([pos_exp.T, (pos_exp + nch).T, (pos_tile - pos_exp).T], axis=1)
        picked = jnp.dot(jax.nn.one_hot(e_j, N_EXPERTS, dtype=jnp.float32), tables.astype(jnp.float32),
                         precision=lax.Precision.HIGHEST).astype(i32)
        lo_j, hi_j, shift_j = picked[:, :n_tiles], picked[:, n_tiles:2 * n_tiles], picked[:, 2 * n_tiles:]
        inside = (j >= lo_j) & (j < hi_j)
        src_j = jnp.sum(jnp.where(inside, j + shift_j, 0), axis=1).astype(i32)
        blk0 = jnp.arange(n_chunks // BLOCK_CHUNKS, dtype=i32) * BLOCK_CHUNKS
        block_e = jnp.minimum(jnp.sum(blk0[:, None] >= seg_end[None, :], axis=1), N_EXPERTS - 1).astype(i32)
        last_blk = (jnp.maximum(seg_end[-1] // BLOCK_CHUNKS, 1) - 1).astype(i32).reshape(1)
        experts = jnp.arange(N_EXPERTS, dtype=i32)
        later = (experts[None, :] > block_e[:, None]) & (seg[None, :] > 0)
        next_e = jnp.min(jnp.where(later, experts[None, :], N_EXPERTS), axis=1)
        next_e = jnp.where(next_e < N_EXPERTS, next_e, -1).astype(i32)
        w_slot = (jnp.sum((experts[None, :] < block_e[:, None]) & (seg[None, :] > 0), axis=1) % 2).astype(i32)

        q = jnp.arange(SORT_CHUNKS, dtype=i32)[None, :, None]
        in_run = (q >= run0_tile[:, None, :]) & (q < (run0_tile + nch)[:, None, :])
        dst_q = jnp.sum(jnp.where(in_run, q + (pos_exp - run0_tile)[:, None, :], 0), axis=2).astype(i32).reshape(-1)

        yr = _experts(src_j, block_e, next_e, w_slot, last_blk, xs, w_gate[l], w_up[l], w_down[l])
        x = _combine(dst_q, x1.reshape(t, d), rw, mod, final_g, yr, seq,
                     final_norm=(l == depth - 1)).reshape(bsz, seq, d)
    return x
```

```python
import functools
import math

import jax
import jax.numpy as jnp
from jax import lax
from jax.experimental import pallas as pl
from jax.experimental.pallas import tpu as pltpu

D_MODEL = 1024
N_HEADS = 8
N_KV_HEADS = 2
HEAD_DIM = 64
BLOCK = 128
ATTN_Q = N_HEADS * HEAD_DIM
ATTN_KV = N_KV_HEADS * HEAD_DIM
N_BUCKETS = 32
MAX_EXACT = N_BUCKETS // 2
MAX_DISTANCE = 128
GM_WIDTH = 512
GM_GROUPS = 4
GM_CHUNK = 128
N_GROUPS = 4
EXPERTS_PER_GROUP = 8
N_EXPERTS = N_GROUPS * EXPERTS_PER_GROUP
TOP_K = 2
D_EXPERT = 512
EPS = 1e-6
NEG = -1e30

LANES = 128
ROUTER_LANES = LANES
EXPERT_LANE0 = N_GROUPS
TM = 512
BM = 512
ROUTER_ROWS = 48
ROUTE_ROWS = 8
CHUNK = 8
SORT_ROWS = TM * TOP_K + N_EXPERTS * CHUNK
SORT_CHUNKS = SORT_ROWS // CHUNK
BLOCK_CHUNKS = BM // CHUNK
GATHER_UNROLL = 8
TAIL_SPLIT = 2
VMEM_LIMIT = 56 * 1024 * 1024

Q0, K0, V0, GU0, GV0, GA0, GB0, IN_END = (0, 512, 640, 768, 1280, 1792, 2816, 3840)


def _dot(a, b):
    return jnp.dot(a, b, preferred_element_type=jnp.float32)


def _dot_nt(a, b):
    return lax.dot_general(a, b, (((1,), (1,)), ((), ())), preferred_element_type=jnp.float32)


LOG2E = math.log2(math.e)


def _gelu_tanh(x):
    c = math.sqrt(2.0 / math.pi)
    k0, k1 = -2.0 * c * LOG2E, -2.0 * c * 0.044715 * LOG2E
    return x * (1.0 / (1.0 + jnp.exp2(x * (x * x * k1 + k0))))


def _sigmoid(x):
    return 1.0 / (1.0 + jnp.exp2(x * -LOG2E))


def _pack_bf16_pairs(x):
    bits = lax.bitcast_convert_type(x, jnp.uint32)
    half = x.shape[1] // 2
    return (bits[:, 0:half] >> 16) | (bits[:, half:] & jnp.uint32(0xFFFF0000))


def _unpack_bf16_pairs(w):
    lo = lax.bitcast_convert_type(w << 16, jnp.float32)
    hi = lax.bitcast_convert_type(w & jnp.uint32(0xFFFF0000), jnp.float32)
    return jnp.concatenate([lo, hi], axis=1).astype(jnp.bfloat16)


def _adaln_kernel(c_ref, w_ref, b_ref, o_ref):
    c = c_ref[...]
    cs = c * _sigmoid(c)
    o_ref[...] = _dot(cs, w_ref[...]) + b_ref[...]


def _adaln_mod(c, w, b):
    bsz, d = c.shape
    n = w.shape[1]
    tn = 1024
    return pl.pallas_call(
        _adaln_kernel,
        grid=(n // tn,),
        in_specs=[pl.BlockSpec((bsz, d), lambda i: (0, 0)),
                  pl.BlockSpec((d, tn), lambda i: (0, i)),
                  pl.BlockSpec((1, tn), lambda i: (0, i))],
        out_specs=pl.BlockSpec((bsz, tn), lambda i: (0, i)),
        out_shape=jax.ShapeDtypeStruct((bsz, n), jnp.float32),
        compiler_params=pltpu.CompilerParams(dimension_semantics=("arbitrary",)),
        name="adaln_mod",
    )(c, w, b.reshape(1, n))


def _mixer_kernel(relb_ref, sinks_ref, spos_ref,
                  x_ref, mod_ref, pos_ref, n1g_ref, win_ref, lng_ref, lnb_ref, ws_ref, bsb_ref,
                  pa_ref, pb_ref, wo_ref, n2g_ref, wr_ref, br_ref,
                  x1_ref, xs_ref, rw_ref, cnt_ref,
                  kbuf, vbuf, pbuf, biasm, key, ya, yb, strict, sbuf, mbuf, flag):
    b = pl.program_id(0)
    j = pl.program_id(1)
    nblk = TM // BLOCK
    bf16 = jnp.bfloat16

    @pl.when((b == 0) & (j == 0))
    def _():
        key[...] = jnp.zeros_like(key)
        for i in range(nblk):
            flag[i] = 1
        tr = lax.broadcasted_iota(jnp.int32, (TM, TM), 0)
        tc = lax.broadcasted_iota(jnp.int32, (TM, TM), 1)
        strict[...] = jnp.where(tr < tc, 1.0, 0.0).astype(bf16)

    @pl.when(j == 0)
    def _():
        kbuf[0:BLOCK, :] = jnp.zeros((BLOCK, LANES), jnp.float32)
        vbuf[0:BLOCK, :] = jnp.zeros((BLOCK, LANES), jnp.float32)
        pbuf[:, 0:BLOCK] = jnp.zeros((1, BLOCK), jnp.int32)

    pbuf[:, BLOCK:] = pos_ref[...]
    qi = lax.broadcasted_iota(jnp.int32, (BLOCK, BLOCK), 0)
    kc = lax.broadcasted_iota(jnp.int32, (BLOCK, BLOCK), 1)
    from_prev = kc > qi
    rels, changed = [], []
    tile0 = (b * pl.num_programs(1) + j) * TM
    for i in range(nblk):
        pk = pbuf[:, i * BLOCK:(i + 2) * BLOCK]
        rels.append(pk - spos_ref[tile0 + i * BLOCK])
        changed.append(jnp.where(rels[i] != key[i], 1.0, 0.0))
    any_changed = jnp.max(functools.reduce(jnp.maximum, changed))
    any_flag = functools.reduce(jnp.maximum, [flag[i] for i in range(nblk)])
    refresh = (any_changed != 0) | (any_flag != 0) | (j == 0)

    def refresh_bias(i):
        pk = pbuf[:, i * BLOCK:(i + 2) * BLOCK]
        pq = pk[:, BLOCK:]
        pq_col = jnp.transpose(jnp.broadcast_to(pq, (BLOCK, BLOCK)))
        no_prev = (j == 0) if i == 0 else None
        for hd in range(N_HEADS):
            biasm[i * N_HEADS + hd] = jnp.zeros((BLOCK, BLOCK), jnp.float32)
        for side in range(2):
            dist = pq_col - pk[:, side * BLOCK:(side + 1) * BLOCK]
            n = jnp.maximum(dist, 0)
            nf = jnp.maximum(n, 1).astype(jnp.float32)
            large = MAX_EXACT + (jnp.log(nf / MAX_EXACT) / math.log(MAX_DISTANCE / MAX_EXACT)
                                 * (N_BUCKETS - MAX_EXACT)).astype(jnp.int32)
            large = jnp.minimum(large, N_BUCKETS - 1)
            bucket = jnp.where(n < MAX_EXACT, n, large)
            use = from_prev if side == 0 else jnp.logical_not(from_prev)
            for hd in range(N_HEADS):
                acc = jnp.zeros((BLOCK, BLOCK), jnp.float32)
                for bk in range(N_BUCKETS):
                    acc = jnp.where(bucket == bk, relb_ref[bk * N_HEADS + hd], acc)
                if side == 0 and i == 0:
                    acc = jnp.where(no_prev, NEG, acc)
                slot = i * N_HEADS + hd
                biasm[slot] = jnp.where(use, acc, biasm[slot])
        key[i] = rels[i]
        flag[i] = no_prev.astype(jnp.int32) if i == 0 else 0

    for i in range(nblk):
        @pl.when(refresh)
        def _():
            stale = (jnp.max(changed[i]) != 0) | (flag[i] != 0)
            if i == 0:
                stale = stale | (j == 0)
            pl.when(stale)(functools.partial(refresh_bias, i))

    x = x_ref[...]
    sh1, sc1, g1 = mod_ref[0:1, :], mod_ref[1:2, :], mod_ref[2:3, :]
    sh2, sc2 = mod_ref[3:4, :], mod_ref[4:5, :]

    xn = x * lax.rsqrt(jnp.mean(x * x, axis=-1, keepdims=True) + EPS)
    h = (xn * (n1g_ref[...] * (1.0 + sc1)) + sh1).astype(bf16)

    lane = lax.broadcasted_iota(jnp.int32, (1, LANES), 1)
    lo = lane < HEAD_DIM
    q = _dot(h, win_ref[:, Q0:K0]) * (HEAD_DIM ** -0.5)
    lo4 = jnp.concatenate([lo] * (ATTN_Q // LANES), axis=1)
    q_lo = jnp.where(lo4, q, 0.0).astype(bf16)
    q_hi = jnp.where(lo4, 0.0, q).astype(bf16)
    kv = _dot(h, win_ref[:, K0:GU0])
    kbuf[BLOCK:, :] = kv[:, 0:LANES]
    vbuf[BLOCK:, :] = kv[:, LANES:]
    kf = kbuf[...]
    vf = vbuf[...]
    kr = pltpu.roll(kf, HEAD_DIM, 1)
    vr = pltpu.roll(vf, HEAD_DIM, 1)
    kd = (jnp.where(lo, kf, kr).astype(bf16), jnp.where(lo, kr, kf).astype(bf16))
    v_lo = (jnp.where(lo, vf, 0.0).astype(bf16), jnp.where(lo, vr, 0.0).astype(bf16))
    v_hi = (jnp.where(lo, 0.0, vr).astype(bf16), jnp.where(lo, 0.0, vf).astype(bf16))

    ones_blk = jnp.ones((2 * BLOCK, LANES), bf16)

    def attend(i):
        rows = slice(i * BLOCK, (i + 1) * BLOCK)
        band = slice(i * BLOCK, (i + 2) * BLOCK)
        grp = N_HEADS // N_KV_HEADS
        for kvh in range(N_KV_HEADS):
            heads = range(kvh * grp, (kvh + 1) * grp)
            qs = jnp.concatenate([(q_lo if hd % 2 == 0 else q_hi)[rows, hd // 2 * LANES:(hd // 2 + 1) * LANES]
                                  for hd in heads], axis=0)
            s2 = _dot_nt(qs, kd[kvh][band])
            for g, hd in enumerate(heads):
                sh = s2[g * BLOCK:(g + 1) * BLOCK]
                s = jnp.where(from_prev, sh[:, 0:BLOCK], sh[:, BLOCK:]) + biasm[i * N_HEADS + hd]
                sbuf[hd] = s
                m = jnp.maximum(jnp.max(s, axis=-1, keepdims=True), sinks_ref[hd])
                mbuf[hd] = jnp.broadcast_to(m, (BLOCK, BLOCK))
        for kvh in range(N_KV_HEADS):
            outs = {}
            for half, vv in enumerate((v_lo, v_hi)):
                hds = (kvh * grp + half, kvh * grp + half + 2)
                p2 = []
                for hd in hds:
                    p = jnp.exp(sbuf[hd] - mbuf[hd])
                    p2.append(jnp.concatenate([jnp.where(from_prev, p, 0.0), jnp.where(from_prev, 0.0, p)],
                                              axis=1).astype(bf16))
                r = _dot(jnp.concatenate(p2, axis=0), jnp.concatenate([vv[kvh][band], ones_blk], axis=1))
                for n, hd in enumerate(hds):
                    rh = r[n * BLOCK:(n + 1) * BLOCK]
                    den = rh[:, LANES:] + jnp.exp(sinks_ref[hd] - mbuf[hd])
                    outs[hd] = rh[:, 0:LANES] * (1.0 / den)
            for pr in (kvh * grp // 2, kvh * grp // 2 + 1):
                ya[rows, pr * LANES:(pr + 1) * LANES] = (outs[2 * pr] + outs[2 * pr + 1]).astype(bf16)

    u = _gelu_tanh(_dot(h, win_ref[:, GU0:GV0]))
    attend(0)
    vg = _gelu_tanh(_dot(h, win_ref[:, GV0:GA0]))
    mu = jnp.mean(vg, axis=-1, keepdims=True)
    vc = vg - mu
    var = jnp.mean(vc * vc, axis=-1, keepdims=True)
    vn = (vc * lax.rsqrt(var + EPS) * lng_ref[...] + lnb_ref[...]).astype(bf16)
    attend(1)
    gate_a = _sigmoid(_dot(h, win_ref[:, GA0:GB0]))
    attend(2)
    ti = lax.broadcasted_iota(jnp.int32, (GM_CHUNK, GM_CHUNK), 0)
    si = lax.broadcasted_iota(jnp.int32, (GM_CHUNK, GM_CHUNK), 1)
    tril = si <= ti
    for g in range(GM_GROUPS):
        wg = jnp.where(tril, ws_ref[g], 0.0).astype(bf16)
        cols = slice(g * LANES, (g + 1) * LANES)
        for cidx in range(TM // GM_CHUNK):
            rows = slice(cidx * GM_CHUNK, (cidx + 1) * GM_CHUNK)
            sv = _dot(wg, vn[rows, cols]) + bsb_ref[g]
            yb[rows, cols] = (u[rows, cols] * sv).astype(bf16)
    gate_b = _sigmoid(_dot(h, win_ref[:, GB0:IN_END]))
    attend(3)

    kbuf[0:BLOCK, :] = kv[TM - BLOCK:, 0:LANES]
    vbuf[0:BLOCK, :] = kv[TM - BLOCK:, LANES:]
    pbuf[:, 0:BLOCK] = pos_ref[:, TM - BLOCK:]

    h2_parts, logit_parts = [], []
    for grp in range(TAIL_SPLIT):
        tok = slice(grp * TM // TAIL_SPLIT, (grp + 1) * TM // TAIL_SPLIT)
        merged = gate_a[tok] * _dot(ya[tok, :], pa_ref[...]) + gate_b[tok] * _dot(yb[tok, :], pb_ref[...])
        x1 = x[tok] + g1 * _dot(merged.astype(bf16), wo_ref[...])
        x1_ref[tok, :] = x1

        xn2 = x1 * lax.rsqrt(jnp.mean(x1 * x1, axis=-1, keepdims=True) + EPS)
        h2_parts.append((xn2 * (n2g_ref[...] * (1.0 + sc2)) + sh2).astype(bf16))
        logit_parts.append(_dot(h2_parts[-1], wr_ref[...]) + br_ref[...])
    h2 = jnp.concatenate(h2_parts, axis=0)
    logits = jnp.concatenate(logit_parts, axis=0)
    lt = jnp.transpose(logits)[0:ROUTER_ROWS, :]
    row = lax.broadcasted_iota(jnp.int32, (ROUTER_ROWS, TM), 0)
    row_f = row.astype(jnp.float32)
    big = float(2 * LANES)
    is_grp = row < N_GROUPS
    lg = jnp.where(is_grp, lt, NEG)
    lg_max = jnp.max(lg, axis=0, keepdims=True)
    g_idx = jnp.min(jnp.where(lg == lg_max, row_f, big), axis=0, keepdims=True)
    p_g = 1.0 / jnp.sum(jnp.where(is_grp, jnp.exp(lg - lg_max), 0.0), axis=0, keepdims=True)
    row_grp = jnp.floor((row_f - EXPERT_LANE0) * (1.0 / EXPERTS_PER_GROUP))
    in_grp = (row >= EXPERT_LANE0) & (row < EXPERT_LANE0 + N_EXPERTS) & (row_grp == g_idx)
    le = jnp.where(in_grp, lt, NEG)
    m1 = jnp.max(le, axis=0, keepdims=True)
    i1 = jnp.min(jnp.where(le == m1, row_f, big), axis=0, keepdims=True)
    oh1 = row_f == i1
    le2 = jnp.where(oh1, NEG, le)
    m2 = jnp.max(le2, axis=0, keepdims=True)
    i2 = jnp.min(jnp.where(le2 == m2, row_f, big), axis=0, keepdims=True)
    oh2 = row_f == i2
    e2 = jnp.exp(m2 - m1)
    w1 = p_g / (1.0 + e2)
    w2 = p_g * e2 / (1.0 + e2)

    oh = jnp.where(oh1, 1.0, jnp.where(oh2, 1.0, 0.0))
    n_e = jnp.sum(oh, axis=1, keepdims=True)
    cnt_ref[...] = n_e
    padded = jnp.floor((n_e + (CHUNK - 1)) * (1.0 / CHUNK)) * CHUNK
    dst_row = lax.broadcasted_iota(jnp.int32, (ROUTER_ROWS, ROUTER_ROWS), 0)
    src_row = lax.broadcasted_iota(jnp.int32, (ROUTER_ROWS, ROUTER_ROWS), 1)
    lower = jnp.where(src_row < dst_row, 1.0, 0.0).astype(bf16)
    run0 = _dot(lower, jnp.broadcast_to(padded, (ROUTER_ROWS, TM)).astype(bf16))
    slot = _dot(oh.astype(bf16), strict[...]) + run0
    pos1 = jnp.sum(jnp.where(oh1, slot, 0.0), axis=0, keepdims=True)
    pos2 = jnp.sum(jnp.where(oh2, slot, 0.0), axis=0, keepdims=True)
    rrow = lax.broadcasted_iota(jnp.int32, (ROUTE_ROWS, TM), 0)
    rw_ref[...] = jnp.where(rrow == 0, w1, jnp.where(rrow == 1, w2,
                            jnp.where(rrow == 2, pos1, jnp.where(rrow == 3, pos2, 0.0))))

    p1 = pos1.astype(jnp.int32)
    p2 = pos2.astype(jnp.int32)
    sp = lax.broadcasted_iota(jnp.int32, (SORT_ROWS, TM), 0)
    perm = jnp.where(sp == p1, 1.0, jnp.where(sp == p2, 1.0, 0.0)).astype(bf16)
    xs_ref[...] = _pack_bf16_pairs(_dot(perm, h2))


def _mixer(x, mod, positions, rel_bias, n1g, w_in, sinks, lng, lnb, w_s, bsb, p_a, p_b, w_o, n2g, w_r, b_r):
    bsz, seq, d = x.shape
    nj = seq // TM
    const = lambda *shape: pl.BlockSpec(shape, lambda b, j: (0,) * len(shape), pipeline_mode=pl.Buffered(1))
    smem = pl.BlockSpec(memory_space=pltpu.SMEM)
    tile = lambda w: pl.BlockSpec((None, TM, w), lambda b, j: (b, j, 0))
    return pl.pallas_call(
        _mixer_kernel,
        grid=(bsz, nj),
        in_specs=[smem, smem, smem,
                  tile(d),
                  pl.BlockSpec((None, 6, d), lambda b, j: (b, 0, 0)),
                  pl.BlockSpec((None, None, 1, TM), lambda b, j: (b, j, 0, 0)),
                  const(1, d), const(d, IN_END), const(1, GM_WIDTH), const(1, GM_WIDTH),
                  const(GM_GROUPS, GM_CHUNK, GM_CHUNK), const(GM_GROUPS, GM_CHUNK, LANES),
                  const(ATTN_Q, d), const(GM_WIDTH, d), const(d, d), const(1, d),
                  const(d, ROUTER_LANES), const(1, ROUTER_LANES)],
        out_specs=[tile(d),
                   pl.BlockSpec((SORT_ROWS, d // 2), lambda b, j: (b * nj + j, 0)),
                   pl.BlockSpec((ROUTE_ROWS, TM), lambda b, j: (0, b * nj + j)),
                   pl.BlockSpec((None, ROUTER_ROWS, 1), lambda b, j: (b * nj + j, 0, 0))],
        out_shape=[jax.ShapeDtypeStruct((bsz, seq, d), jnp.float32),
                   jax.ShapeDtypeStruct((bsz * nj * SORT_ROWS, d // 2), jnp.uint32),
                   jax.ShapeDtypeStruct((ROUTE_ROWS, bsz * seq), jnp.float32),
                   jax.ShapeDtypeStruct((bsz * nj, ROUTER_ROWS, 1), jnp.float32)],
        scratch_shapes=[pltpu.VMEM((TM + BLOCK, LANES), jnp.float32),
                        pltpu.VMEM((TM + BLOCK, LANES), jnp.float32),
                        pltpu.VMEM((1, TM + BLOCK), jnp.int32),
                        pltpu.VMEM((TM // BLOCK * N_HEADS, BLOCK, BLOCK), jnp.float32),
                        pltpu.VMEM((TM // BLOCK, 1, 2 * BLOCK), jnp.int32),
                        pltpu.VMEM((TM, ATTN_Q), jnp.bfloat16),
                        pltpu.VMEM((TM, GM_WIDTH), jnp.bfloat16),
                        pltpu.VMEM((TM, TM), jnp.bfloat16),
                        pltpu.VMEM((N_HEADS, BLOCK, BLOCK), jnp.float32),
                        pltpu.VMEM((N_HEADS, BLOCK, BLOCK), jnp.float32),
                        pltpu.SMEM((TM // BLOCK,), jnp.int32)],
        compiler_params=pltpu.CompilerParams(dimension_semantics=("arbitrary", "arbitrary"),
                                             vmem_limit_bytes=VMEM_LIMIT),
        name="mixer",
    )(rel_bias.reshape(-1), sinks, positions.reshape(-1), x, mod, positions.reshape(bsz, nj, 1, TM), n1g, w_in, lng, lnb,
      w_s, bsb, p_a, p_b, w_o, n2g, w_r, b_r)


def _gather_chunks(idx_ref, idx0, n, src_hbm, dst, sem, inline=False, split_priority=False):
    def issue(c, carry, priority=0):
        src = pl.multiple_of(idx_ref[idx0 + c] * CHUNK, CHUNK)
        pltpu.make_async_copy(src_hbm.at[pl.ds(src, CHUNK)],
                              dst.at[pl.ds(pl.multiple_of(c * CHUNK, CHUNK), CHUNK)], sem).start(priority)
        return carry

    if inline:
        for c in range(n):
            issue(c, 0, c % 2 if split_priority else 0)
    else:
        lax.fori_loop(0, n, issue, 0, unroll=GATHER_UNROLL)


def _wait_chunks(n, src_hbm, dst, sem):
    pltpu.make_async_copy(src_hbm.at[pl.ds(0, n * CHUNK)], dst, sem).wait()


def _expert_kernel(src_ref, be_ref, ne_ref, ws_ref, lb_ref, xs_hbm, wg_hbm, wu_hbm, wd_hbm, yr_ref,
                   xbuf, wgf, wuf, wdf, wgb, wub, wdb, sem, wsem):
    i = pl.program_id(0)
    last = lb_ref[0]
    slot = i % 2

    def weight_copies(e, w):
        return [pltpu.make_async_copy(hbm.at[e], buf.at[w], wsem.at[w])
                for hbm, buf in ((wg_hbm, wgf), (wu_hbm, wuf), (wd_hbm, wdf))]

    @pl.when(i == 0)
    def _():
        _gather_chunks(src_ref, 0, BLOCK_CHUNKS, xs_hbm, xbuf.at[0], sem.at[0])
        for cp in weight_copies(be_ref[0], 0):
            cp.start()

    @pl.when(i <= last)
    def _():
        changed = (i == 0) | (be_ref[i] != be_ref[jnp.maximum(i - 1, 0)])

        @pl.when(changed)
        def _():
            w = ws_ref[i]
            for cp in weight_copies(be_ref[i], w):
                cp.wait()
            wgb[...] = wgf[w].astype(jnp.bfloat16)
            wub[...] = wuf[w].astype(jnp.bfloat16)
            wdb[...] = wdf[w].astype(jnp.bfloat16)

            @pl.when(ne_ref[i] >= 0)
            def _():
                for cp in weight_copies(ne_ref[i], 1 - w):
                    cp.start(priority=1)

        nxt = jnp.minimum(i + 1, last)
        _gather_chunks(src_ref, nxt * BLOCK_CHUNKS, BLOCK_CHUNKS, xs_hbm, xbuf.at[1 - slot], sem.at[1 - slot],
                       inline=True)
        _wait_chunks(BLOCK_CHUNKS, xs_hbm, xbuf.at[slot], sem.at[slot])
        xb = _unpack_bf16_pairs(xbuf[slot])
        hg = _dot(xb, wgb[...])
        hu = _dot(xb, wub[...])
        hid = (hg * _sigmoid(hg) * hu).astype(jnp.bfloat16)
        yr_ref[...] = _pack_bf16_pairs(_dot(hid, wdb[...]).astype(jnp.bfloat16).astype(jnp.float32))

        @pl.when(i == last)
        def _():
            _wait_chunks(BLOCK_CHUNKS, xs_hbm, xbuf.at[1 - slot], sem.at[1 - slot])

    @pl.when(i > last)
    def _():
        yr_ref[...] = jnp.zeros_like(yr_ref)


def _experts(src_chunk, block_e, next_e, w_slot, last_blk, xs, w_gate, w_up, w_down):
    d = w_gate.shape[1]
    n_rows = src_chunk.shape[0] * CHUNK
    hbm = pl.BlockSpec(memory_space=pl.ANY)
    return pl.pallas_call(
        _expert_kernel,
        grid_spec=pltpu.PrefetchScalarGridSpec(
            num_scalar_prefetch=5,
            grid=(n_rows // BM,),
            in_specs=[hbm, hbm, hbm, hbm],
            out_specs=pl.BlockSpec((BM, d // 2), lambda i, *_: (i, 0)),
            scratch_shapes=[pltpu.VMEM((2, BM, d // 2), jnp.uint32),
                            pltpu.VMEM((2, d, D_EXPERT), jnp.float32),
                            pltpu.VMEM((2, d, D_EXPERT), jnp.float32),
                            pltpu.VMEM((2, D_EXPERT, d), jnp.float32),
                            pltpu.VMEM((d, D_EXPERT), jnp.bfloat16),
                            pltpu.VMEM((d, D_EXPERT), jnp.bfloat16),
                            pltpu.VMEM((D_EXPERT, d), jnp.bfloat16),
                            pltpu.SemaphoreType.DMA((2,)),
                            pltpu.SemaphoreType.DMA((2,))]),
        out_shape=jax.ShapeDtypeStruct((n_rows, d // 2), jnp.uint32),
        compiler_params=pltpu.CompilerParams(dimension_semantics=("arbitrary",),
                                             vmem_limit_bytes=VMEM_LIMIT),
        name="experts",
    )(src_chunk, block_e, next_e, w_slot, last_blk, xs, w_gate, w_up, w_down)


def _combine_kernel(src_ref, x1_ref, rw_ref, mod_ref, fg_ref, yr_hbm, out_ref, ybuf, sem, *, final_norm):
    i = pl.program_id(0)
    slot = i % 2

    @pl.when(i == 0)
    def _():
        _gather_chunks(src_ref, 0, SORT_CHUNKS, yr_hbm, ybuf.at[0], sem.at[0])

    rows = jnp.concatenate([rw_ref[...], jnp.zeros((LANES - ROUTE_ROWS, TM), jnp.float32)], axis=0)
    rw = jnp.transpose(rows)
    sp = lax.broadcasted_iota(jnp.int32, (1, SORT_ROWS), 1)
    p1 = rw[:, 2:3].astype(jnp.int32)
    p2 = rw[:, 3:4].astype(jnp.int32)
    wmat = jnp.where(sp == p1, rw[:, 0:1], jnp.where(sp == p2, rw[:, 1:2], 0.0)).astype(jnp.bfloat16)
    nxt = jnp.minimum(i + 1, pl.num_programs(0) - 1)
    _gather_chunks(src_ref, nxt * SORT_CHUNKS, SORT_CHUNKS, yr_hbm, ybuf.at[1 - slot], sem.at[1 - slot],
                   inline=True, split_priority=True)
    _wait_chunks(SORT_CHUNKS, yr_hbm, ybuf.at[slot], sem.at[slot])
    moe = _dot(wmat, _unpack_bf16_pairs(ybuf[slot]))
    x2 = x1_ref[...] + mod_ref[5:6, :] * moe
    if final_norm:
        x2 = x2 * lax.rsqrt(jnp.mean(x2 * x2, axis=-1, keepdims=True) + EPS) * fg_ref[...]
    out_ref[...] = x2

    @pl.when(i == pl.num_programs(0) - 1)
    def _():
        _wait_chunks(SORT_CHUNKS, yr_hbm, ybuf.at[1 - slot], sem.at[1 - slot])


def _combine(src_chunk, x1, rw, mod, final_g, yr, seq, final_norm):
    t, d = x1.shape
    per_seq = seq // TM
    return pl.pallas_call(
        functools.partial(_combine_kernel, final_norm=final_norm),
        grid_spec=pltpu.PrefetchScalarGridSpec(
            num_scalar_prefetch=1,
            grid=(t // TM,),
            in_specs=[pl.BlockSpec((TM, d), lambda i, src: (i, 0)),
                      pl.BlockSpec((ROUTE_ROWS, TM), lambda i, src: (0, i)),
                      pl.BlockSpec((None, 6, d), lambda i, src: (i // per_seq, 0, 0)),
                      pl.BlockSpec((1, d), lambda i, src: (0, 0)),
                      pl.BlockSpec(memory_space=pl.ANY)],
            out_specs=pl.BlockSpec((TM, d), lambda i, src: (i, 0)),
            scratch_shapes=[pltpu.VMEM((2, SORT_ROWS, d // 2), jnp.uint32),
                            pltpu.SemaphoreType.DMA((2,))]),
        out_shape=jax.ShapeDtypeStruct((t, d), jnp.float32),
        compiler_params=pltpu.CompilerParams(dimension_semantics=("arbitrary",),
                                             vmem_limit_bytes=VMEM_LIMIT),
        name="combine",
    )(src_chunk, x1, rw, mod, final_g.reshape(1, d), yr)


def kernel(x, c, positions, rel_bias, w_ada, b_ada, norm1_g, w_in, sinks, gm_ln_g, gm_ln_b, gm_w_s, gm_b_s,
           p_a, p_b, w_o, norm2_g, w_router_g, b_router_g, w_router_e, b_router_e, w_gate, w_up, w_down,
           final_g):
    bsz, seq, d = x.shape
    t = bsz * seq
    bf16 = jnp.bfloat16
    depth = w_ada.shape[0]
    n_tiles = t // TM
    n_chunks = t * TOP_K // CHUNK + n_tiles * N_EXPERTS + N_EXPERTS * BLOCK_CHUNKS
    i32 = jnp.int32
    for l in range(depth):
        mod = _adaln_mod(c, w_ada[l], b_ada[l]).reshape(bsz, 6, d)
        pad = ROUTER_LANES - N_GROUPS - N_EXPERTS
        w_r = jnp.concatenate([w_router_g[l], w_router_e[l], jnp.zeros((d, pad), jnp.float32)], axis=1)
        b_r = jnp.concatenate([b_router_g[l], b_router_e[l], jnp.zeros((pad,), jnp.float32)]).reshape(1, -1)
        bsb = jnp.broadcast_to(gm_b_s[l][:, :, None], (GM_GROUPS, GM_CHUNK, LANES))
        x1, xs, rw, cnt = _mixer(
            x, mod, positions, rel_bias, norm1_g[l].reshape(1, d), w_in[l].astype(bf16), sinks[l],
            gm_ln_g[l].reshape(1, -1), gm_ln_b[l].reshape(1, -1), gm_w_s[l], bsb,
            p_a[l].astype(bf16), p_b[l].astype(bf16), w_o[l].astype(bf16), norm2_g[l].reshape(1, d),
            w_r.astype(bf16), b_r)

        n = cnt[:, EXPERT_LANE0:EXPERT_LANE0 + N_EXPERTS, 0].astype(i32)
        nch = (n + CHUNK - 1) // CHUNK
        run0_tile = jnp.cumsum(nch, axis=1) - nch
        run0_exp = jnp.cumsum(nch, axis=0) - nch
        tot = jnp.sum(nch, axis=0)
        seg = (tot + BLOCK_CHUNKS - 1) // BLOCK_CHUNKS * BLOCK_CHUNKS
        seg_end = jnp.cumsum(seg)
        seg0 = seg_end - seg

        pos_tile = jnp.arange(n_tiles, dtype=i32)[:, None] * SORT_CHUNKS + run0_tile
        pos_exp = seg0[None, :] + run0_exp

        j = jnp.arange(n_chunks, dtype=i32)[:, None]
        e_j = jnp.minimum(jnp.sum(j >= seg_end[None, :], axis=1), N_EXPERTS - 1)
        tables = jnp.concatenate([pos_exp.T, (pos_exp + nch).T, (pos_tile - pos_exp).T], axis=1)
        picked = jnp.dot(jax.nn.one_hot(e_j, N_EXPERTS, dtype=jnp.float32), tables.astype(jnp.float32),
                         precision=lax.Precision.HIGHEST).astype(i32)
        lo_j, hi_j, shift_j = picked[:, :n_tiles], picked[:, n_tiles:2 * n_tiles], picked[:, 2 * n_tiles:]
        inside = (j >= lo_j) & (j < hi_j)
        src_j = jnp.sum(jnp.where(inside, j + shift_j, 0), axis=1).astype(i32)
        blk0 = jnp.arange(n_chunks // BLOCK_CHUNKS, dtype=i32) * BLOCK_CHUNKS
        block_e = jnp.minimum(jnp.sum(blk0[:, None] >= seg_end[None, :], axis=1), N_EXPERTS - 1).astype(i32)
        last_blk = (jnp.maximum(seg_end[-1] // BLOCK_CHUNKS, 1) - 1).astype(i32).reshape(1)
        experts = jnp.arange(N_EXPERTS, dtype=i32)
        later = (experts[None, :] > block_e[:, None]) & (seg[None, :] > 0)
        next_e = jnp.min(jnp.where(later, experts[None, :], N_EXPERTS), axis=1)
        next_e = jnp.where(next_e < N_EXPERTS, next_e, -1).astype(i32)
        w_slot = (jnp.sum((experts[None, :] < block_e[:, None]) & (seg[None, :] > 0), axis=1) % 2).astype(i32)

        q = jnp.arange(SORT_CHUNKS, dtype=i32)[None, :, None]
        in_run = (q >= run0_tile[:, None, :]) & (q < (run0_tile + nch)[:, None, :])
        dst_q = jnp.sum(jnp.where(in_run, q + (pos_exp - run0_tile)[:, None, :], 0), axis=2).astype(i32).reshape(-1)

        yr = _experts(src_j, block_e, next_e, w_slot, last_blk, xs, w_gate[l], w_up[l], w_down[l])
        x = _combine(dst_q, x1.reshape(t, d), rw, mod, final_g, yr, seq,
                     final_norm=(l == depth - 1)).reshape(bsz, seq, d)
    return x
```

```python
import functools
import math

import jax
import jax.numpy as jnp
from jax import lax
from jax.experimental import pallas as pl
from jax.experimental.pallas import tpu as pltpu

D_MODEL = 1024
N_HEADS = 8
N_KV_HEADS = 2
HEAD_DIM = 64
BLOCK = 128
ATTN_Q = N_HEADS * HEAD_DIM
ATTN_KV = N_KV_HEADS * HEAD_DIM
N_BUCKETS = 32
MAX_EXACT = N_BUCKETS // 2
MAX_DISTANCE = 128
GM_WIDTH = 512
GM_GROUPS = 4
GM_CHUNK = 128
N_GROUPS = 4
EXPERTS_PER_GROUP = 8
N_EXPERTS = N_GROUPS * EXPERTS_PER_GROUP
TOP_K = 2
D_EXPERT = 512
EPS = 1e-6
NEG = -1e30

LANES = 128
ROUTER_LANES = LANES
EXPERT_LANE0 = N_GROUPS
TM = 512
BM = 512
ROUTER_ROWS = 48
ROUTE_ROWS = 8
CHUNK = 8
SORT_ROWS = TM * TOP_K + N_EXPERTS * CHUNK
SORT_CHUNKS = SORT_ROWS // CHUNK
BLOCK_CHUNKS = BM // CHUNK
GATHER_UNROLL = 8
TAIL_SPLIT = 2
VMEM_LIMIT = 56 * 1024 * 1024

Q0, K0, V0, GU0, GV0, GA0, GB0, IN_END = (0, 512, 640, 768, 1280, 1792, 2816, 3840)


def _dot(a, b):
    return jnp.dot(a, b, preferred_element_type=jnp.float32)


def _dot_nt(a, b):
    return lax.dot_general(a, b, (((1,), (1,)), ((), ())), preferred_element_type=jnp.float32)


LOG2E = math.log2(math.e)


def _gelu_tanh(x):
    c = math.sqrt(2.0 / math.pi)
    k0, k1 = -2.0 * c * LOG2E, -2.0 * c * 0.044715 * LOG2E
    return x * (1.0 / (1.0 + jnp.exp2(x * (x * x * k1 + k0))))


def _sigmoid(x):
    return 1.0 / (1.0 + jnp.exp2(x * -LOG2E))


def _pack_bf16_pairs(x):
    bits = lax.bitcast_convert_type(x, jnp.uint32)
    half = x.shape[1] // 2
    return (bits[:, 0:half] >> 16) | (bits[:, half:] & jnp.uint32(0xFFFF0000))


def _unpack_bf16_pairs(w):
    lo = lax.bitcast_convert_type(w << 16, jnp.float32)
    hi = lax.bitcast_convert_type(w & jnp.uint32(0xFFFF0000), jnp.float32)
    return jnp.concatenate([lo, hi], axis=1).astype(jnp.bfloat16)


def _adaln_kernel(c_ref, w_ref, b_ref, o_ref):
    c = c_ref[...]
    cs = c * _sigmoid(c)
    o_ref[...] = _dot(cs, w_ref[...]) + b_ref[...]


def _adaln_mod(c, w, b):
    bsz, d = c.shape
    n = w.shape[1]
    tn = 1024
    return pl.pallas_call(
        _adaln_kernel,
        grid=(n // tn,),
        in_specs=[pl.BlockSpec((bsz, d), lambda i: (0, 0)),
                  pl.BlockSpec((d, tn), lambda i: (0, i)),
                  pl.BlockSpec((1, tn), lambda i: (0, i))],
        out_specs=pl.BlockSpec((bsz, tn), lambda i: (0, i)),
        out_shape=jax.ShapeDtypeStruct((bsz, n), jnp.float32),
        compiler_params=pltpu.CompilerParams(dimension_semantics=("arbitrary",)),
        name="adaln_mod",
    )(c, w, b.reshape(1, n))


def _mixer_kernel(relb_ref, sinks_ref, spos_ref,
                  x_ref, mod_ref, pos_ref, n1g_ref, win_ref, lng_ref, lnb_ref, ws_ref, bsb_ref,
                  pa_ref, pb_ref, wo_ref, n2g_ref, wr_ref, br_ref,
                  x1_ref, xs_ref, rw_ref, cnt_ref,
                  kbuf, vbuf, pbuf, biasm, key, ya, yb, strict, sbuf, mbuf, flag):
    b = pl.program_id(0)
    j = pl.program_id(1)
    nblk = TM // BLOCK
    bf16 = jnp.bfloat16

    @pl.when((b == 0) & (j == 0))
    def _():
        key[...] = jnp.zeros_like(key)
        for i in range(nblk):
            flag[i] = 1
        tr = lax.broadcasted_iota(jnp.int32, (TM, TM), 0)
        tc = lax.broadcasted_iota(jnp.int32, (TM, TM), 1)
        strict[...] = jnp.where(tr < tc, 1.0, 0.0).astype(bf16)

    @pl.when(j == 0)
    def _():
        kbuf[0:BLOCK, :] = jnp.zeros((BLOCK, LANES), jnp.float32)
        vbuf[0:BLOCK, :] = jnp.zeros((BLOCK, LANES), jnp.float32)
        pbuf[:, 0:BLOCK] = jnp.zeros((1, BLOCK), jnp.int32)

    pbuf[:, BLOCK:] = pos_ref[...]
    qi = lax.broadcasted_iota(jnp.int32, (BLOCK, BLOCK), 0)
    kc = lax.broadcasted_iota(jnp.int32, (BLOCK, BLOCK), 1)
    from_prev = kc > qi
    rels, changed = [], []
    tile0 = (b * pl.num_programs(1) + j) * TM
    for i in range(nblk):
        pk = pbuf[:, i * BLOCK:(i + 2) * BLOCK]
        rels.append(pk - spos_ref[tile0 + i * BLOCK])
        changed.append(jnp.where(rels[i] != key[i], 1.0, 0.0))
    any_changed = jnp.max(functools.reduce(jnp.maximum, changed))
    any_flag = functools.reduce(jnp.maximum, [flag[i] for i in range(nblk)])
    refresh = (any_changed != 0) | (any_flag != 0) | (j == 0)

    def refresh_bias(i):
        pk = pbuf[:, i * BLOCK:(i + 2) * BLOCK]
        pq = pk[:, BLOCK:]
        pq_col = jnp.transpose(jnp.broadcast_to(pq, (BLOCK, BLOCK)))
        no_prev = (j == 0) if i == 0 else None
        for hd in range(N_HEADS):
            biasm[i * N_HEADS + hd] = jnp.zeros((BLOCK, BLOCK), jnp.float32)
        for side in range(2):
            dist = pq_col - pk[:, side * BLOCK:(side + 1) * BLOCK]
            n = jnp.maximum(dist, 0)
            nf = jnp.maximum(n, 1).astype(jnp.float32)
            large = MAX_EXACT + (jnp.log(nf / MAX_EXACT) / math.log(MAX_DISTANCE / MAX_EXACT)
                                 * (N_BUCKETS - MAX_EXACT)).astype(jnp.int32)
            large = jnp.minimum(large, N_BUCKETS - 1)
            bucket = jnp.where(n < MAX_EXACT, n, large)
            use = from_prev if side == 0 else jnp.logical_not(from_prev)
            for hd in range(N_HEADS):
                acc = jnp.zeros((BLOCK, BLOCK), jnp.float32)
                for bk in range(N_BUCKETS):
                    acc = jnp.where(bucket == bk, relb_ref[bk * N_HEADS + hd], acc)
                if side == 0 and i == 0:
                    acc = jnp.where(no_prev, NEG, acc)
                slot = i * N_HEADS + hd
                biasm[slot] = jnp.where(use, acc, biasm[slot])
        key[i] = rels[i]
        flag[i] = no_prev.astype(jnp.int32) if i == 0 else 0

    for i in range(nblk):
        @pl.when(refresh)
        def _():
            stale = (jnp.max(changed[i]) != 0) | (flag[i] != 0)
            if i == 0:
                stale = stale | (j == 0)
            pl.when(stale)(functools.partial(refresh_bias, i))

    x = x_ref[...]
    sh1, sc1, g1 = mod_ref[0:1, :], mod_ref[1:2, :], mod_ref[2:3, :]
    sh2, sc2 = mod_ref[3:4, :], mod_ref[4:5, :]

    xn = x * lax.rsqrt(jnp.mean(x * x, axis=-1, keepdims=True) + EPS)
    h = (xn * (n1g_ref[...] * (1.0 + sc1)) + sh1).astype(bf16)

    lane = lax.broadcasted_iota(jnp.int32, (1, LANES), 1)
    lo = lane < HEAD_DIM
    q = _dot(h, win_ref[:, Q0:K0]) * (HEAD_DIM ** -0.5)
    lo4 = jnp.concatenate([lo] * (ATTN_Q // LANES), axis=1)
    q_lo = jnp.where(lo4, q, 0.0).astype(bf16)
    q_hi = jnp.where(lo4, 0.0, q).astype(bf16)
    kv = _dot(h, win_ref[:, K0:GU0])
    kbuf[BLOCK:, :] = kv[:, 0:LANES]
    vbuf[BLOCK:, :] = kv[:, LANES:]
    kf = kbuf[...]
    vf = vbuf[...]
    kr = pltpu.roll(kf, HEAD_DIM, 1)
    vr = pltpu.roll(vf, HEAD_DIM, 1)
    kd = (jnp.where(lo, kf, kr).astype(bf16), jnp.where(lo, kr, kf).astype(bf16))
    v_lo = (jnp.where(lo, vf, 0.0).astype(bf16), jnp.where(lo, vr, 0.0).astype(bf16))
    v_hi = (jnp.where(lo, 0.0, vr).astype(bf16), jnp.where(lo, 0.0, vf).astype(bf16))

    ones_blk = jnp.ones((2 * BLOCK, LANES), bf16)

    def attend(i):
        rows = slice(i * BLOCK, (i + 1) * BLOCK)
        band = slice(i * BLOCK, (i + 2) * BLOCK)
        grp = N_HEADS // N_KV_HEADS
        for kvh in range(N_KV_HEADS):
            heads = range(kvh * grp, (kvh + 1) * grp)
            qs = jnp.concatenate([(q_lo if hd % 2 == 0 else q_hi)[rows, hd // 2 * LANES:(hd // 2 + 1) * LANES]
                                  for hd in heads], axis=0)
            s2 = _dot_nt(qs, kd[kvh][band])
            for g, hd in enumerate(heads):
                sh = s2[g * BLOCK:(g + 1) * BLOCK]
                s = jnp.where(from_prev, sh[:, 0:BLOCK], sh[:, BLOCK:]) + biasm[i * N_HEADS + hd]
                sbuf[hd] = s
                m = jnp.maximum(jnp.max(s, axis=-1, keepdims=True), sinks_ref[hd])
                mbuf[hd] = jnp.broadcast_to(m, (BLOCK, BLOCK))
        for kvh in range(N_KV_HEADS):
            outs = {}
            for half, vv in enumerate((v_lo, v_hi)):
                hds = (kvh * grp + half, kvh * grp + half + 2)
                p2 = []
                for hd in hds:
                    p = jnp.exp(sbuf[hd] - mbuf[hd])
                    p2.append(jnp.concatenate([jnp.where(from_prev, p, 0.0), jnp.where(from_prev, 0.0, p)],
                                              axis=1).astype(bf16))
                r = _dot(jnp.concatenate(p2, axis=0), jnp.concatenate([vv[kvh][band], ones_blk], axis=1))
                for n, hd in enumerate(hds):
                    rh = r[n * BLOCK:(n + 1) * BLOCK]
                    den = rh[:, LANES:] + jnp.exp(sinks_ref[hd] - mbuf[hd])
                    outs[hd] = rh[:, 0:LANES] * (1.0 / den)
            for pr in (kvh * grp // 2, kvh * grp // 2 + 1):
                ya[rows, pr * LANES:(pr + 1) * LANES] = (outs[2 * pr] + outs[2 * pr + 1]).astype(bf16)

    u = _gelu_tanh(_dot(h, win_ref[:, GU0:GV0]))
    attend(0)
    vg = _gelu_tanh(_dot(h, win_ref[:, GV0:GA0]))
    mu = jnp.mean(vg, axis=-1, keepdims=True)
    vc = vg - mu
    var = jnp.mean(vc * vc, axis=-1, keepdims=True)
    vn = (vc * lax.rsqrt(var + EPS) * lng_ref[...] + lnb_ref[...]).astype(bf16)
    attend(1)
    gate_a = _sigmoid(_dot(h, win_ref[:, GA0:GB0]))
    attend(2)
    ti = lax.broadcasted_iota(jnp.int32, (GM_CHUNK, GM_CHUNK), 0)
    si = lax.broadcasted_iota(jnp.int32, (GM_CHUNK, GM_CHUNK), 1)
    tril = si <= ti
    for g in range(GM_GROUPS):
        wg = jnp.where(tril, ws_ref[g], 0.0).astype(bf16)
        cols = slice(g * LANES, (g + 1) * LANES)
        for cidx in range(TM // GM_CHUNK):
            rows = slice(cidx * GM_CHUNK, (cidx + 1) * GM_CHUNK)
            sv = _dot(wg, vn[rows, cols]) + bsb_ref[g]
            yb[rows, cols] = (u[rows, cols] * sv).astype(bf16)
    gate_b = _sigmoid(_dot(h, win_ref[:, GB0:IN_END]))
    attend(3)

    kbuf[0:BLOCK, :] = kv[TM - BLOCK:, 0:LANES]
    vbuf[0:BLOCK, :] = kv[TM - BLOCK:, LANES:]
    pbuf[:, 0:BLOCK] = pos_ref[:, TM - BLOCK:]

    h2_parts, logit_parts = [], []
    for grp in range(TAIL_SPLIT):
        tok = slice(grp * TM // TAIL_SPLIT, (grp + 1) * TM // TAIL_SPLIT)
        merged = gate_a[tok] * _dot(ya[tok, :], pa_ref[...]) + gate_b[tok] * _dot(yb[tok, :], pb_ref[...])
        x1 = x[tok] + g1 * _dot(merged.astype(bf16), wo_ref[...])
        x1_ref[tok, :] = x1

        xn2 = x1 * lax.rsqrt(jnp.mean(x1 * x1, axis=-1, keepdims=True) + EPS)
        h2_parts.append((xn2 * (n2g_ref[...] * (1.0 + sc2)) + sh2).astype(bf16))
        logit_parts.append(_dot(h2_parts[-1], wr_ref[...]) + br_ref[...])
    h2 = jnp.concatenate(h2_parts, axis=0)
    logits = jnp.concatenate(logit_parts, axis=0)
    lt = jnp.transpose(logits)[0:ROUTER_ROWS, :]
    row = lax.broadcasted_iota(jnp.int32, (ROUTER_ROWS, TM), 0)
    row_f = row.astype(jnp.float32)
    big = float(2 * LANES)
    is_grp = row < N_GROUPS
    lg = jnp.where(is_grp, lt, NEG)
    lg_max = jnp.max(lg, axis=0, keepdims=True)
    g_idx = jnp.min(jnp.where(lg == lg_max, row_f, big), axis=0, keepdims=True)
    p_g = 1.0 / jnp.sum(jnp.where(is_grp, jnp.exp(lg - lg_max), 0.0), axis=0, keepdims=True)
    row_grp = jnp.floor((row_f - EXPERT_LANE0) * (1.0 / EXPERTS_PER_GROUP))
    in_grp = (row >= EXPERT_LANE0) & (row < EXPERT_LANE0 + N_EXPERTS) & (row_grp == g_idx)
    le = jnp.where(in_grp, lt, NEG)
    m1 = jnp.max(le, axis=0, keepdims=True)
    i1 = jnp.min(jnp.where(le == m1, row_f, big), axis=0, keepdims=True)
    oh1 = row_f == i1
    le2 = jnp.where(oh1, NEG, le)
    m2 = jnp.max(le2, axis=0, keepdims=True)
    i2 = jnp.min(jnp.where(le2 == m2, row_f, big), axis=0, keepdims=True)
    oh2 = row_f == i2
    e2 = jnp.exp(m2 - m1)
    w1 = p_g / (1.0 + e2)
    w2 = p_g * e2 / (1.0 + e2)

    oh = jnp.where(oh1, 1.0, jnp.where(oh2, 1.0, 0.0))
    n_e = jnp.sum(oh, axis=1, keepdims=True)
    cnt_ref[...] = n_e
    padded = jnp.floor((n_e + (CHUNK - 1)) * (1.0 / CHUNK)) * CHUNK
    dst_row = lax.broadcasted_iota(jnp.int32, (ROUTER_ROWS, ROUTER_ROWS), 0)
    src_row = lax.broadcasted_iota(jnp.int32, (ROUTER_ROWS, ROUTER_ROWS), 1)
    lower = jnp.where(src_row < dst_row, 1.0, 0.0).astype(bf16)
    run0 = _dot(lower, jnp.broadcast_to(padded, (ROUTER_ROWS, TM)).astype(bf16))
    slot = _dot(oh.astype(bf16), strict[...]) + run0
    pos1 = jnp.sum(jnp.where(oh1, slot, 0.0), axis=0, keepdims=True)
    pos2 = jnp.sum(jnp.where(oh2, slot, 0.0), axis=0, keepdims=True)
    rrow = lax.broadcasted_iota(jnp.int32, (ROUTE_ROWS, TM), 0)
    rw_ref[...] = jnp.where(rrow == 0, w1, jnp.where(rrow == 1, w2,
                            jnp.where(rrow == 2, pos1, jnp.where(rrow == 3, pos2, 0.0))))

    p1 = pos1.astype(jnp.int32)
    p2 = pos2.astype(jnp.int32)
    sp = lax.broadcasted_iota(jnp.int32, (SORT_ROWS, TM), 0)
    perm = jnp.where(sp == p1, 1.0, jnp.where(sp == p2, 1.0, 0.0)).astype(bf16)
    xs_ref[...] = _pack_bf16_pairs(_dot(perm, h2))


def _mixer(x, mod, positions, rel_bias, n1g, w_in, sinks, lng, lnb, w_s, bsb, p_a, p_b, w_o, n2g, w_r, b_r):
    bsz, seq, d = x.shape
    nj = seq // TM
    const = lambda *shape: pl.BlockSpec(shape, lambda b, j: (0,) * len(shape), pipeline_mode=pl.Buffered(1))
    smem = pl.BlockSpec(memory_space=pltpu.SMEM)
    tile = lambda w: pl.BlockSpec((None, TM, w), lambda b, j: (b, j, 0))
    return pl.pallas_call(
        _mixer_kernel,
        grid=(bsz, nj),
        in_specs=[smem, smem, smem,
                  tile(d),
                  pl.BlockSpec((None, 6, d), lambda b, j: (b, 0, 0)),
                  pl.BlockSpec((None, None, 1, TM), lambda b, j: (b, j, 0, 0)),
                  const(1, d), const(d, IN_END), const(1, GM_WIDTH), const(1, GM_WIDTH),
                  const(GM_GROUPS, GM_CHUNK, GM_CHUNK), const(GM_GROUPS, GM_CHUNK, LANES),
                  const(ATTN_Q, d), const(GM_WIDTH, d), const(d, d), const(1, d),
                  const(d, ROUTER_LANES), const(1, ROUTER_LANES)],
        out_specs=[tile(d),
                   pl.BlockSpec((SORT_ROWS, d // 2), lambda b, j: (b * nj + j, 0)),
                   pl.BlockSpec((ROUTE_ROWS, TM), lambda b, j: (0, b * nj + j)),
                   pl.BlockSpec((None, ROUTER_ROWS, 1), lambda b, j: (b * nj + j, 0, 0))],
        out_shape=[jax.ShapeDtypeStruct((bsz, seq, d), jnp.float32),
                   jax.ShapeDtypeStruct((bsz * nj * SORT_ROWS, d // 2), jnp.uint32),
                   jax.ShapeDtypeStruct((ROUTE_ROWS, bsz * seq), jnp.float32),
                   jax.ShapeDtypeStruct((bsz * nj, ROUTER_ROWS, 1), jnp.float32)],
        scratch_shapes=[pltpu.VMEM((TM + BLOCK, LANES), jnp.float32),
                        pltpu.VMEM((TM + BLOCK, LANES), jnp.float32),
                        pltpu.VMEM((1, TM + BLOCK), jnp.int32),
                        pltpu.VMEM((TM // BLOCK * N_HEADS, BLOCK, BLOCK), jnp.float32),
                        pltpu.VMEM((TM // BLOCK, 1, 2 * BLOCK), jnp.int32),
                        pltpu.VMEM((TM, ATTN_Q), jnp.bfloat16),
                        pltpu.VMEM((TM, GM_WIDTH), jnp.bfloat16),
                        pltpu.VMEM((TM, TM), jnp.bfloat16),
                        pltpu.VMEM((N_HEADS, BLOCK, BLOCK), jnp.float32),
                        pltpu.VMEM((N_HEADS, BLOCK, BLOCK), jnp.float32),
                        pltpu.SMEM((TM // BLOCK,), jnp.int32)],
        compiler_params=pltpu.CompilerParams(dimension_semantics=("arbitrary", "arbitrary"),
                                             vmem_limit_bytes=VMEM_LIMIT),
        name="mixer",
    )(rel_bias.reshape(-1), sinks, positions.reshape(-1), x, mod, positions.reshape(bsz, nj, 1, TM), n1g, w_in, lng, lnb,
      w_s, bsb, p_a, p_b, w_o, n2g, w_r, b_r)


def _gather_chunks(idx_ref, idx0, n, src_hbm, dst, sem, inline=False, split_priority=False):
    def issue(c, carry, priority=0):
        src = pl.multiple_of(idx_ref[idx0 + c] * CHUNK, CHUNK)
        pltpu.make_async_copy(src_hbm.at[pl.ds(src, CHUNK)],
                              dst.at[pl.ds(pl.multiple_of(c * CHUNK, CHUNK), CHUNK)], sem).start(priority)
        return carry

    if inline:
        for c in range(n):
            issue(c, 0, c % 2 if split_priority else 0)
    else:
        lax.fori_loop(0, n, issue, 0, unroll=GATHER_UNROLL)


def _wait_chunks(n, src_hbm, dst, sem):
    pltpu.make_async_copy(src_hbm.at[pl.ds(0, n * CHUNK)], dst, sem).wait()


def _expert_kernel(src_ref, be_ref, ne_ref, ws_ref, lb_ref, xs_hbm, wg_hbm, wu_hbm, wd_hbm, yr_ref,
                   xbuf, wgf, wuf, wdf, wgb, wub, wdb, sem, wsem):
    i = pl.program_id(0)
    last = lb_ref[0]
    slot = i % 2

    def weight_copies(e, w):
        return [pltpu.make_async_copy(hbm.at[e], buf.at[w], wsem.at[w])
                for hbm, buf in ((wg_hbm, wgf), (wu_hbm, wuf), (wd_hbm, wdf))]

    @pl.when(i == 0)
    def _():
        _gather_chunks(src_ref, 0, BLOCK_CHUNKS, xs_hbm, xbuf.at[0], sem.at[0])
        for cp in weight_copies(be_ref[0], 0):
            cp.start()

    @pl.when(i <= last)
    def _():
        changed = (i == 0) | (be_ref[i] != be_ref[jnp.maximum(i - 1, 0)])

        @pl.when(changed)
        def _():
            w = ws_ref[i]
            for cp in weight_copies(be_ref[i], w):
                cp.wait()
            wgb[...] = wgf[w].astype(jnp.bfloat16)
            wub[...] = wuf[w].astype(jnp.bfloat16)
            wdb[...] = wdf[w].astype(jnp.bfloat16)

            @pl.when(ne_ref[i] >= 0)
            def _():
                for cp in weight_copies(ne_ref[i], 1 - w):
                    cp.start(priority=1)

        nxt = jnp.minimum(i + 1, last)
        _gather_chunks(src_ref, nxt * BLOCK_CHUNKS, BLOCK_CHUNKS, xs_hbm, xbuf.at[1 - slot], sem.at[1 - slot],
                       inline=True, split_priority=True)
        _wait_chunks(BLOCK_CHUNKS, xs_hbm, xbuf.at[slot], sem.at[slot])
        xb = _unpack_bf16_pairs(xbuf[slot])
        hg = _dot(xb, wgb[...])
        hu = _dot(xb, wub[...])
        hid = (hg * _sigmoid(hg) * hu).astype(jnp.bfloat16)
        yr_ref[...] = _pack_bf16_pairs(_dot(hid, wdb[...]).astype(jnp.bfloat16).astype(jnp.float32))

        @pl.when(i == last)
        def _():
            _wait_chunks(BLOCK_CHUNKS, xs_hbm, xbuf.at[1 - slot], sem.at[1 - slot])

    @pl.when(i > last)
    def _():
        yr_ref[...] = jnp.zeros_like(yr_ref)


def _experts(src_chunk, block_e, next_e, w_slot, last_blk, xs, w_gate, w_up, w_down):
    d = w_gate.shape[1]
    n_rows = src_chunk.shape[0] * CHUNK
    hbm = pl.BlockSpec(memory_space=pl.ANY)
    return pl.pallas_call(
        _expert_kernel,
        grid_spec=pltpu.PrefetchScalarGridSpec(
            num_scalar_prefetch=5,
            grid=(n_rows // BM,),
            in_specs=[hbm, hbm, hbm, hbm],
            out_specs=pl.BlockSpec((BM, d // 2), lambda i, *_: (i, 0)),
            scratch_shapes=[pltpu.VMEM((2, BM, d // 2), jnp.uint32),
                            pltpu.VMEM((2, d, D_EXPERT), jnp.float32),
                            pltpu.VMEM((2, d, D_EXPERT), jnp.float32),
                            pltpu.VMEM((2, D_EXPERT, d), jnp.float32),
                            pltpu.VMEM((d, D_EXPERT), jnp.bfloat16),
                            pltpu.VMEM((d, D_EXPERT), jnp.bfloat16),
                            pltpu.VMEM((D_EXPERT, d), jnp.bfloat16),
                            pltpu.SemaphoreType.DMA((2,)),
                            pltpu.SemaphoreType.DMA((2,))]),
        out_shape=jax.ShapeDtypeStruct((n_rows, d // 2), jnp.uint32),
        compiler_params=pltpu.CompilerParams(dimension_semantics=("arbitrary",),
                                             vmem_limit_bytes=VMEM_LIMIT),
        name="experts",
    )(src_chunk, block_e, next_e, w_slot, last_blk, xs, w_gate, w_up, w_down)


def _combine_kernel(src_ref, x1_ref, rw_ref, mod_ref, fg_ref, yr_hbm, out_ref, ybuf, sem, *, final_norm):
    i = pl.program_id(0)
    slot = i % 2

    @pl.when(i == 0)
    def _():
        _gather_chunks(src_ref, 0, SORT_CHUNKS, yr_hbm, ybuf.at[0], sem.at[0])

    rows = jnp.concatenate([rw_ref[...], jnp.zeros((LANES - ROUTE_ROWS, TM), jnp.float32)], axis=0)
    rw = jnp.transpose(rows)
    sp = lax.broadcasted_iota(jnp.int32, (1, SORT_ROWS), 1)
    p1 = rw[:, 2:3].astype(jnp.int32)
    p2 = rw[:, 3:4].astype(jnp.int32)
    wmat = jnp.where(sp == p1, rw[:, 0:1], jnp.where(sp == p2, rw[:, 1:2], 0.0)).astype(jnp.bfloat16)
    nxt = jnp.minimum(i + 1, pl.num_programs(0) - 1)
    _gather_chunks(src_ref, nxt * SORT_CHUNKS, SORT_CHUNKS, yr_hbm, ybuf.at[1 - slot], sem.at[1 - slot],
                   inline=True, split_priority=True)
    _wait_chunks(SORT_CHUNKS, yr_hbm, ybuf.at[slot], sem.at[slot])
    moe = _dot(wmat, _unpack_bf16_pairs(ybuf[slot]))
    x2 = x1_ref[...] + mod_ref[5:6, :] * moe
    if final_norm:
        x2 = x2 * lax.rsqrt(jnp.mean(x2 * x2, axis=-1, keepdims=True) + EPS) * fg_ref[...]
    out_ref[...] = x2

    @pl.when(i == pl.num_programs(0) - 1)
    def _():
        _wait_chunks(SORT_CHUNKS, yr_hbm, ybuf.at[1 - slot], sem.at[1 - slot])


def _combine(src_chunk, x1, rw, mod, final_g, yr, seq, final_norm):
    t, d = x1.shape
    per_seq = seq // TM
    return pl.pallas_call(
        functools.partial(_combine_kernel, final_norm=final_norm),
        grid_spec=pltpu.PrefetchScalarGridSpec(
            num_scalar_prefetch=1,
            grid=(t // TM,),
            in_specs=[pl.BlockSpec((TM, d), lambda i, src: (i, 0)),
                      pl.BlockSpec((ROUTE_ROWS, TM), lambda i, src: (0, i)),
                      pl.BlockSpec((None, 6, d), lambda i, src: (i // per_seq, 0, 0)),
                      pl.BlockSpec((1, d), lambda i, src: (0, 0)),
                      pl.BlockSpec(memory_space=pl.ANY)],
            out_specs=pl.BlockSpec((TM, d), lambda i, src: (i, 0)),
            scratch_shapes=[pltpu.VMEM((2, SORT_ROWS, d // 2), jnp.uint32),
                            pltpu.SemaphoreType.DMA((2,))]),
        out_shape=jax.ShapeDtypeStruct((t, d), jnp.float32),
        compiler_params=pltpu.CompilerParams(dimension_semantics=("arbitrary",),
                                             vmem_limit_bytes=VMEM_LIMIT),
        name="combine",
    )(src_chunk, x1, rw, mod, final_g.reshape(1, d), yr)


def kernel(x, c, positions, rel_bias, w_ada, b_ada, norm1_g, w_in, sinks, gm_ln_g, gm_ln_b, gm_w_s, gm_b_s,
           p_a, p_b, w_o, norm2_g, w_router_g, b_router_g, w_router_e, b_router_e, w_gate, w_up, w_down,
           final_g):
    bsz, seq, d = x.shape
    t = bsz * seq
    bf16 = jnp.bfloat16
    depth = w_ada.shape[0]
    n_tiles = t // TM
    n_chunks = t * TOP_K // CHUNK + n_tiles * N_EXPERTS + N_EXPERTS * BLOCK_CHUNKS
    i32 = jnp.int32
    for l in range(depth):
        mod = _adaln_mod(c, w_ada[l], b_ada[l]).reshape(bsz, 6, d)
        pad = ROUTER_LANES - N_GROUPS - N_EXPERTS
        w_r = jnp.concatenate([w_router_g[l], w_router_e[l], jnp.zeros((d, pad), jnp.float32)], axis=1)
        b_r = jnp.concatenate([b_router_g[l], b_router_e[l], jnp.zeros((pad,), jnp.float32)]).reshape(1, -1)
        bsb = jnp.broadcast_to(gm_b_s[l][:, :, None], (GM_GROUPS, GM_CHUNK, LANES))
        x1, xs, rw, cnt = _mixer(
            x, mod, positions, rel_bias, norm1_g[l].reshape(1, d), w_in[l].astype(bf16), sinks[l],
            gm_ln_g[l].reshape(1, -1), gm_ln_b[l].reshape(1, -1), gm_w_s[l], bsb,
            p_a[l].astype(bf16), p_b[l].astype(bf16), w_o[l].astype(bf16), norm2_g[l].reshape(1, d),
            w_r.astype(bf16), b_r)

        n = cnt[:, EXPERT_LANE0:EXPERT_LANE0 + N_EXPERTS, 0].astype(i32)
        nch = (n + CHUNK - 1) // CHUNK
        run0_tile = jnp.cumsum(nch, axis=1) - nch
        run0_exp = jnp.cumsum(nch, axis=0) - nch
        tot = jnp.sum(nch, axis=0)
        seg = (tot + BLOCK_CHUNKS - 1) // BLOCK_CHUNKS * BLOCK_CHUNKS
        seg_end = jnp.cumsum(seg)
        seg0 = seg_end - seg

        pos_tile = jnp.arange(n_tiles, dtype=i32)[:, None] * SORT_CHUNKS + run0_tile
        pos_exp = seg0[None, :] + run0_exp

        j = jnp.arange(n_chunks, dtype=i32)[:, None]
        e_j = jnp.minimum(jnp.sum(j >= seg_end[None, :], axis=1), N_EXPERTS - 1)
        tables = jnp.concatenate([pos_exp.T, (pos_exp + nch).T, (pos_tile - pos_exp).T], axis=1)
        picked = jnp.dot(jax.nn.one_hot(e_j, N_EXPERTS, dtype=jnp.float32), tables.astype(jnp.float32),
                         precision=lax.Precision.HIGHEST).astype(i32)
        lo_j, hi_j, shift_j = picked[:, :n_tiles], picked[:, n_tiles:2 * n_tiles], picked[:, 2 * n_tiles:]
        inside = (j >= lo_j) & (j < hi_j)
        src_j = jnp.sum(jnp.where(inside, j + shift_j, 0), axis=1).astype(i32)
        blk0 = jnp.arange(n_chunks // BLOCK_CHUNKS, dtype=i32) * BLOCK_CHUNKS
        block_e = jnp.minimum(jnp.sum(blk0[:, None] >= seg_end[None, :], axis=1), N_EXPERTS - 1).astype(i32)
        last_blk = (jnp.maximum(seg_end[-1] // BLOCK_CHUNKS, 1) - 1).astype(i32).reshape(1)
        experts = jnp.arange(N_EXPERTS, dtype=i32)
        later = (experts[None, :] > block_e[:, None]) & (seg[None, :] > 0)
        next_e = jnp.min(jnp.where(later, experts[None, :], N_EXPERTS), axis=1)
        next_e = jnp.where(next_e < N_EXPERTS, next_e, -1).astype(i32)
        w_slot = (jnp.sum((experts[None, :] < block_e[:, None]) & (seg[None, :] > 0), axis=1) % 2).astype(i32)

        q = jnp.arange(SORT_CHUNKS, dtype=i32)[None, :, None]
        in_run = (q >= run0_tile[:, None, :]) & (q < (run0_tile + nch)[:, None, :])
        dst_q = jnp.sum(jnp.where(in_run, q + (pos_exp - run0_tile)[:, None, :], 0), axis=2).astype(i32).reshape(-1)

        yr = _experts(src_j, block_e, next_e, w_slot, last_blk, xs, w_gate[l], w_up[l], w_down[l])
        x = _combine(dst_q, x1.reshape(t, d), rw, mod, final_g, yr, seq,
                     final_norm=(l == depth - 1)).reshape(bsz, seq, d)
    return x
```

```python
import functools
import math

import jax
import jax.numpy as jnp
from jax import lax
from jax.experimental import pallas as pl
from jax.experimental.pallas import tpu as pltpu

D_MODEL = 1024
N_HEADS = 8
N_KV_HEADS = 2
HEAD_DIM = 64
BLOCK = 128
ATTN_Q = N_HEADS * HEAD_DIM
ATTN_KV = N_KV_HEADS * HEAD_DIM
N_BUCKETS = 32
MAX_EXACT = N_BUCKETS // 2
MAX_DISTANCE = 128
GM_WIDTH = 512
GM_GROUPS = 4
GM_CHUNK = 128
N_GROUPS = 4
EXPERTS_PER_GROUP = 8
N_EXPERTS = N_GROUPS * EXPERTS_PER_GROUP
TOP_K = 2
D_EXPERT = 512
EPS = 1e-6
NEG = -1e30

LANES = 128
ROUTER_LANES = LANES
EXPERT_LANE0 = N_GROUPS
TM = 512
BM = 512
ROUTER_ROWS = 48
ROUTE_ROWS = 8
CHUNK = 8
SORT_ROWS = TM * TOP_K + N_EXPERTS * CHUNK
SORT_CHUNKS = SORT_ROWS // CHUNK
BLOCK_CHUNKS = BM // CHUNK
GATHER_UNROLL = 8
COMBINE_K = 256
EXPERT_K = 256
TAIL_SPLIT = 2
VMEM_LIMIT = 56 * 1024 * 1024

Q0, K0, V0, GU0, GV0, GA0, GB0, IN_END = (0, 512, 640, 768, 1280, 1792, 2816, 3840)


def _dot(a, b):
    return jnp.dot(a, b, preferred_element_type=jnp.float32)


def _dot_nt(a, b):
    return lax.dot_general(a, b, (((1,), (1,)), ((), ())), preferred_element_type=jnp.float32)


LOG2E = math.log2(math.e)


def _gelu_tanh(x):
    c = math.sqrt(2.0 / math.pi)
    k0, k1 = -2.0 * c * LOG2E, -2.0 * c * 0.044715 * LOG2E
    return x * (1.0 / (1.0 + jnp.exp2(x * (x * x * k1 + k0))))


def _sigmoid(x):
    return 1.0 / (1.0 + jnp.exp2(x * -LOG2E))


def _pack_bf16_pairs(x):
    bits = lax.bitcast_convert_type(x, jnp.uint32)
    half = x.shape[1] // 2
    return (bits[:, 0:half] >> 16) | (bits[:, half:] & jnp.uint32(0xFFFF0000))


def _unpack_bf16_pairs(w):
    lo = lax.bitcast_convert_type(w << 16, jnp.float32)
    hi = lax.bitcast_convert_type(w & jnp.uint32(0xFFFF0000), jnp.float32)
    return jnp.concatenate([lo, hi], axis=1).astype(jnp.bfloat16)


def _adaln_kernel(c_ref, w_ref, b_ref, o_ref):
    c = c_ref[...]
    cs = c * _sigmoid(c)
    o_ref[...] = _dot(cs, w_ref[...]) + b_ref[...]


def _adaln_mod(c, w, b):
    bsz, d = c.shape
    n = w.shape[1]
    tn = 1024
    return pl.pallas_call(
        _adaln_kernel,
        grid=(n // tn,),
        in_specs=[pl.BlockSpec((bsz, d), lambda i: (0, 0)),
                  pl.BlockSpec((d, tn), lambda i: (0, i)),
                  pl.BlockSpec((1, tn), lambda i: (0, i))],
        out_specs=pl.BlockSpec((bsz, tn), lambda i: (0, i)),
        out_shape=jax.ShapeDtypeStruct((bsz, n), jnp.float32),
        compiler_params=pltpu.CompilerParams(dimension_semantics=("arbitrary",)),
        name="adaln_mod",
    )(c, w, b.reshape(1, n))


def _mixer_kernel(relb_ref, sinks_ref, spos_ref,
                  x_ref, mod_ref, pos_ref, n1g_ref, win_ref, lng_ref, lnb_ref, ws_ref, bsb_ref,
                  pa_ref, pb_ref, wo_ref, n2g_ref, wr_ref, br_ref,
                  x1_ref, xs_ref, rw_ref, cnt_ref,
                  kbuf, vbuf, pbuf, biasm, key, ya, yb, strict, sbuf, mbuf, flag):
    b = pl.program_id(0)
    j = pl.program_id(1)
    nblk = TM // BLOCK
    bf16 = jnp.bfloat16

    @pl.when((b == 0) & (j == 0))
    def _():
        key[...] = jnp.zeros_like(key)
        for i in range(nblk):
            flag[i] = 1
        tr = lax.broadcasted_iota(jnp.int32, (TM, TM), 0)
        tc = lax.broadcasted_iota(jnp.int32, (TM, TM), 1)
        strict[...] = jnp.where(tr < tc, 1.0, 0.0).astype(bf16)

    @pl.when(j == 0)
    def _():
        kbuf[0:BLOCK, :] = jnp.zeros((BLOCK, LANES), jnp.float32)
        vbuf[0:BLOCK, :] = jnp.zeros((BLOCK, LANES), jnp.float32)
        pbuf[:, 0:BLOCK] = jnp.zeros((1, BLOCK), jnp.int32)

    pbuf[:, BLOCK:] = pos_ref[...]
    qi = lax.broadcasted_iota(jnp.int32, (BLOCK, BLOCK), 0)
    kc = lax.broadcasted_iota(jnp.int32, (BLOCK, BLOCK), 1)
    from_prev = kc > qi
    rels, changed = [], []
    tile0 = (b * pl.num_programs(1) + j) * TM
    for i in range(nblk):
        pk = pbuf[:, i * BLOCK:(i + 2) * BLOCK]
        rels.append(pk - spos_ref[tile0 + i * BLOCK])
        changed.append(jnp.where(rels[i] != key[i], 1.0, 0.0))
    any_changed = jnp.max(functools.reduce(jnp.maximum, changed))
    any_flag = functools.reduce(jnp.maximum, [flag[i] for i in range(nblk)])
    refresh = (any_changed != 0) | (any_flag != 0) | (j == 0)

    def refresh_bias(i):
        pk = pbuf[:, i * BLOCK:(i + 2) * BLOCK]
        pq = pk[:, BLOCK:]
        pq_col = jnp.transpose(jnp.broadcast_to(pq, (BLOCK, BLOCK)))
        no_prev = (j == 0) if i == 0 else None
        for hd in range(N_HEADS):
            biasm[i * N_HEADS + hd] = jnp.zeros((BLOCK, BLOCK), jnp.float32)
        for side in range(2):
            dist = pq_col - pk[:, side * BLOCK:(side + 1) * BLOCK]
            n = jnp.maximum(dist, 0)
            nf = jnp.maximum(n, 1).astype(jnp.float32)
            large = MAX_EXACT + (jnp.log(nf / MAX_EXACT) / math.log(MAX_DISTANCE / MAX_EXACT)
                                 * (N_BUCKETS - MAX_EXACT)).astype(jnp.int32)
            large = jnp.minimum(large, N_BUCKETS - 1)
            bucket = jnp.where(n < MAX_EXACT, n, large)
            use = from_prev if side == 0 else jnp.logical_not(from_prev)
            for hd in range(N_HEADS):
                acc = jnp.zeros((BLOCK, BLOCK), jnp.float32)
                for bk in range(N_BUCKETS):
                    acc = jnp.where(bucket == bk, relb_ref[bk * N_HEADS + hd], acc)
                if side == 0 and i == 0:
                    acc = jnp.where(no_prev, NEG, acc)
                slot = i * N_HEADS + hd
                biasm[slot] = jnp.where(use, acc, biasm[slot])
        key[i] = rels[i]
        flag[i] = no_prev.astype(jnp.int32) if i == 0 else 0

    for i in range(nblk):
        @pl.when(refresh)
        def _():
            stale = (jnp.max(changed[i]) != 0) | (flag[i] != 0)
            if i == 0:
                stale = stale | (j == 0)
            pl.when(stale)(functools.partial(refresh_bias, i))

    x = x_ref[...]
    sh1, sc1, g1 = mod_ref[0:1, :], mod_ref[1:2, :], mod_ref[2:3, :]
    sh2, sc2 = mod_ref[3:4, :], mod_ref[4:5, :]

    xn = x * lax.rsqrt(jnp.mean(x * x, axis=-1, keepdims=True) + EPS)
    h = (xn * (n1g_ref[...] * (1.0 + sc1)) + sh1).astype(bf16)

    lane = lax.broadcasted_iota(jnp.int32, (1, LANES), 1)
    lo = lane < HEAD_DIM
    q = _dot(h, win_ref[:, Q0:K0]) * (HEAD_DIM ** -0.5)
    lo4 = jnp.concatenate([lo] * (ATTN_Q // LANES), axis=1)
    q_lo = jnp.where(lo4, q, 0.0).astype(bf16)
    q_hi = jnp.where(lo4, 0.0, q).astype(bf16)
    kv = _dot(h, win_ref[:, K0:GU0])
    kbuf[BLOCK:, :] = kv[:, 0:LANES]
    vbuf[BLOCK:, :] = kv[:, LANES:]
    kf = kbuf[...]
    vf = vbuf[...]
    kr = pltpu.roll(kf, HEAD_DIM, 1)
    vr = pltpu.roll(vf, HEAD_DIM, 1)
    kd = (jnp.where(lo, kf, kr).astype(bf16), jnp.where(lo, kr, kf).astype(bf16))
    v_lo = (jnp.where(lo, vf, 0.0).astype(bf16), jnp.where(lo, vr, 0.0).astype(bf16))
    v_hi = (jnp.where(lo, 0.0, vr).astype(bf16), jnp.where(lo, 0.0, vf).astype(bf16))

    ones_blk = jnp.ones((2 * BLOCK, LANES), bf16)

    def attend(i):
        rows = slice(i * BLOCK, (i + 1) * BLOCK)
        band = slice(i * BLOCK, (i + 2) * BLOCK)
        grp = N_HEADS // N_KV_HEADS
        for kvh in range(N_KV_HEADS):
            heads = range(kvh * grp, (kvh + 1) * grp)
            qs = jnp.concatenate([(q_lo if hd % 2 == 0 else q_hi)[rows, hd // 2 * LANES:(hd // 2 + 1) * LANES]
                                  for hd in heads], axis=0)
            s2 = _dot_nt(qs, kd[kvh][band])
            for g, hd in enumerate(heads):
                sh = s2[g * BLOCK:(g + 1) * BLOCK]
                s = jnp.where(from_prev, sh[:, 0:BLOCK], sh[:, BLOCK:]) + biasm[i * N_HEADS + hd]
                sbuf[hd] = s
                m = jnp.maximum(jnp.max(s, axis=-1, keepdims=True), sinks_ref[hd])
                mbuf[hd] = jnp.broadcast_to(m, (BLOCK, BLOCK))
        for kvh in range(N_KV_HEADS):
            outs = {}
            for half, vv in enumerate((v_lo, v_hi)):
                hds = (kvh * grp + half, kvh * grp + half + 2)
                p2 = []
                for hd in hds:
                    p = jnp.exp(sbuf[hd] - mbuf[hd])
                    p2.append(jnp.concatenate([jnp.where(from_prev, p, 0.0), jnp.where(from_prev, 0.0, p)],
                                              axis=1).astype(bf16))
                r = _dot(jnp.concatenate(p2, axis=0), jnp.concatenate([vv[kvh][band], ones_blk], axis=1))
                for n, hd in enumerate(hds):
                    rh = r[n * BLOCK:(n + 1) * BLOCK]
                    den = rh[:, LANES:] + jnp.exp(sinks_ref[hd] - mbuf[hd])
                    outs[hd] = rh[:, 0:LANES] * (1.0 / den)
            for pr in (kvh * grp // 2, kvh * grp // 2 + 1):
                ya[rows, pr * LANES:(pr + 1) * LANES] = (outs[2 * pr] + outs[2 * pr + 1]).astype(bf16)

    u = _gelu_tanh(_dot(h, win_ref[:, GU0:GV0]))
    attend(0)
    vg = _gelu_tanh(_dot(h, win_ref[:, GV0:GA0]))
    mu = jnp.mean(vg, axis=-1, keepdims=True)
    vc = vg - mu
    var = jnp.mean(vc * vc, axis=-1, keepdims=True)
    vn = (vc * lax.rsqrt(var + EPS) * lng_ref[...] + lnb_ref[...]).astype(bf16)
    attend(1)
    gate_a = _sigmoid(_dot(h, win_ref[:, GA0:GB0]))
    attend(2)
    ti = lax.broadcasted_iota(jnp.int32, (GM_CHUNK, GM_CHUNK), 0)
    si = lax.broadcasted_iota(jnp.int32, (GM_CHUNK, GM_CHUNK), 1)
    tril = si <= ti
    for g in range(GM_GROUPS):
        wg = jnp.where(tril, ws_ref[g], 0.0).astype(bf16)
        cols = slice(g * LANES, (g + 1) * LANES)
        for cidx in range(TM // GM_CHUNK):
            rows = slice(cidx * GM_CHUNK, (cidx + 1) * GM_CHUNK)
            sv = _dot(wg, vn[rows, cols]) + bsb_ref[g]
            yb[rows, cols] = (u[rows, cols] * sv).astype(bf16)
    gate_b = _sigmoid(_dot(h, win_ref[:, GB0:IN_END]))
    attend(3)

    kbuf[0:BLOCK, :] = kv[TM - BLOCK:, 0:LANES]
    vbuf[0:BLOCK, :] = kv[TM - BLOCK:, LANES:]
    pbuf[:, 0:BLOCK] = pos_ref[:, TM - BLOCK:]

    h2_parts, logit_parts = [], []
    for grp in range(TAIL_SPLIT):
        tok = slice(grp * TM // TAIL_SPLIT, (grp + 1) * TM // TAIL_SPLIT)
        merged = gate_a[tok] * _dot(ya[tok, :], pa_ref[...]) + gate_b[tok] * _dot(yb[tok, :], pb_ref[...])
        x1 = x[tok] + g1 * _dot(merged.astype(bf16), wo_ref[...])
        x1_ref[tok, :] = x1

        xn2 = x1 * lax.rsqrt(jnp.mean(x1 * x1, axis=-1, keepdims=True) + EPS)
        h2_parts.append((xn2 * (n2g_ref[...] * (1.0 + sc2)) + sh2).astype(bf16))
        logit_parts.append(_dot(h2_parts[-1], wr_ref[...]) + br_ref[...])
    h2 = jnp.concatenate(h2_parts, axis=0)
    logits = jnp.concatenate(logit_parts, axis=0)
    lt = jnp.transpose(logits)[0:ROUTER_ROWS, :]
    row = lax.broadcasted_iota(jnp.int32, (ROUTER_ROWS, TM), 0)
    row_f = row.astype(jnp.float32)
    big = float(2 * LANES)
    is_grp = row < N_GROUPS
    lg = jnp.where(is_grp, lt, NEG)
    lg_max = jnp.max(lg, axis=0, keepdims=True)
    g_idx = jnp.min(jnp.where(lg == lg_max, row_f, big), axis=0, keepdims=True)
    p_g = 1.0 / jnp.sum(jnp.where(is_grp, jnp.exp(lg - lg_max), 0.0), axis=0, keepdims=True)
    row_grp = jnp.floor((row_f - EXPERT_LANE0) * (1.0 / EXPERTS_PER_GROUP))
    in_grp = (row >= EXPERT_LANE0) & (row < EXPERT_LANE0 + N_EXPERTS) & (row_grp == g_idx)
    le = jnp.where(in_grp, lt, NEG)
    m1 = jnp.max(le, axis=0, keepdims=True)
    i1 = jnp.min(jnp.where(le == m1, row_f, big), axis=0, keepdims=True)
    oh1 = row_f == i1
    le2 = jnp.where(oh1, NEG, le)
    m2 = jnp.max(le2, axis=0, keepdims=True)
    i2 = jnp.min(jnp.where(le2 == m2, row_f, big), axis=0, keepdims=True)
    oh2 = row_f == i2
    e2 = jnp.exp(m2 - m1)
    w1 = p_g / (1.0 + e2)
    w2 = p_g * e2 / (1.0 + e2)

    oh = jnp.where(oh1, 1.0, jnp.where(oh2, 1.0, 0.0))
    n_e = jnp.sum(oh, axis=1, keepdims=True)
    cnt_ref[...] = n_e
    padded = jnp.floor((n_e + (CHUNK - 1)) * (1.0 / CHUNK)) * CHUNK
    dst_row = lax.broadcasted_iota(jnp.int32, (ROUTER_ROWS, ROUTER_ROWS), 0)
    src_row = lax.broadcasted_iota(jnp.int32, (ROUTER_ROWS, ROUTER_ROWS), 1)
    lower = jnp.where(src_row < dst_row, 1.0, 0.0).astype(bf16)
    run0 = _dot(lower, jnp.broadcast_to(padded, (ROUTER_ROWS, TM)).astype(bf16))
    slot = _dot(oh.astype(bf16), strict[...]) + run0
    pos1 = jnp.sum(jnp.where(oh1, slot, 0.0), axis=0, keepdims=True)
    pos2 = jnp.sum(jnp.where(oh2, slot, 0.0), axis=0, keepdims=True)
    rrow = lax.broadcasted_iota(jnp.int32, (ROUTE_ROWS, TM), 0)
    rw_ref[...] = jnp.where(rrow == 0, w1, jnp.where(rrow == 1, w2,
                            jnp.where(rrow == 2, pos1, jnp.where(rrow == 3, pos2, 0.0))))

    p1 = pos1.astype(jnp.int32)
    p2 = pos2.astype(jnp.int32)
    sp = lax.broadcasted_iota(jnp.int32, (SORT_ROWS, TM), 0)
    perm = jnp.where(sp == p1, 1.0, jnp.where(sp == p2, 1.0, 0.0)).astype(bf16)
    xs_ref[...] = _pack_bf16_pairs(_dot(perm, h2))


def _mixer(x, mod, positions, rel_bias, n1g, w_in, sinks, lng, lnb, w_s, bsb, p_a, p_b, w_o, n2g, w_r, b_r):
    bsz, seq, d = x.shape
    nj = seq // TM
    const = lambda *shape: pl.BlockSpec(shape, lambda b, j: (0,) * len(shape), pipeline_mode=pl.Buffered(1))
    smem = pl.BlockSpec(memory_space=pltpu.SMEM)
    tile = lambda w: pl.BlockSpec((None, TM, w), lambda b, j: (b, j, 0))
    return pl.pallas_call(
        _mixer_kernel,
        grid=(bsz, nj),
        in_specs=[smem, smem, smem,
                  tile(d),
                  pl.BlockSpec((None, 6, d), lambda b, j: (b, 0, 0)),
                  pl.BlockSpec((None, None, 1, TM), lambda b, j: (b, j, 0, 0)),
                  const(1, d), const(d, IN_END), const(1, GM_WIDTH), const(1, GM_WIDTH),
                  const(GM_GROUPS, GM_CHUNK, GM_CHUNK), const(GM_GROUPS, GM_CHUNK, LANES),
                  const(ATTN_Q, d), const(GM_WIDTH, d), const(d, d), const(1, d),
                  const(d, ROUTER_LANES), const(1, ROUTER_LANES)],
        out_specs=[tile(d),
                   pl.BlockSpec((SORT_ROWS, d // 2), lambda b, j: (b * nj + j, 0)),
                   pl.BlockSpec((ROUTE_ROWS, TM), lambda b, j: (0, b * nj + j)),
                   pl.BlockSpec((None, ROUTER_ROWS, 1), lambda b, j: (b * nj + j, 0, 0))],
        out_shape=[jax.ShapeDtypeStruct((bsz, seq, d), jnp.float32),
                   jax.ShapeDtypeStruct((bsz * nj * SORT_ROWS, d // 2), jnp.uint32),
                   jax.ShapeDtypeStruct((ROUTE_ROWS, bsz * seq), jnp.float32),
                   jax.ShapeDtypeStruct((bsz * nj, ROUTER_ROWS, 1), jnp.float32)],
        scratch_shapes=[pltpu.VMEM((TM + BLOCK, LANES), jnp.float32),
                        pltpu.VMEM((TM + BLOCK, LANES), jnp.float32),
                        pltpu.VMEM((1, TM + BLOCK), jnp.int32),
                        pltpu.VMEM((TM // BLOCK * N_HEADS, BLOCK, BLOCK), jnp.float32),
                        pltpu.VMEM((TM // BLOCK, 1, 2 * BLOCK), jnp.int32),
                        pltpu.VMEM((TM, ATTN_Q), jnp.bfloat16),
                        pltpu.VMEM((TM, GM_WIDTH), jnp.bfloat16),
                        pltpu.VMEM((TM, TM), jnp.bfloat16),
                        pltpu.VMEM((N_HEADS, BLOCK, BLOCK), jnp.float32),
                        pltpu.VMEM((N_HEADS, BLOCK, BLOCK), jnp.float32),
                        pltpu.SMEM((TM // BLOCK,), jnp.int32)],
        compiler_params=pltpu.CompilerParams(dimension_semantics=("arbitrary", "arbitrary"),
                                             vmem_limit_bytes=VMEM_LIMIT),
        name="mixer",
    )(rel_bias.reshape(-1), sinks, positions.reshape(-1), x, mod, positions.reshape(bsz, nj, 1, TM), n1g, w_in, lng, lnb,
      w_s, bsb, p_a, p_b, w_o, n2g, w_r, b_r)


def _gather_chunks(idx_ref, idx0, n, src_hbm, dst, sem, inline=False, split_priority=False, first=0):
    def issue(c, carry, priority=0):
        src = pl.multiple_of(idx_ref[idx0 + c] * CHUNK, CHUNK)
        pltpu.make_async_copy(src_hbm.at[pl.ds(src, CHUNK)],
                              dst.at[pl.ds(pl.multiple_of(c * CHUNK, CHUNK), CHUNK)], sem).start(priority)
        return carry

    if inline:
        for c in range(first, first + n):
            issue(c, 0, c % 2 if split_priority else 0)
    else:
        lax.fori_loop(first, first + n, issue, 0, unroll=GATHER_UNROLL)


def _wait_chunks(n, src_hbm, dst, sem):
    pltpu.make_async_copy(src_hbm.at[pl.ds(0, n * CHUNK)], dst, sem).wait()


def _expert_kernel(src_ref, be_ref, ne_ref, ws_ref, lb_ref, xs_hbm, wg_hbm, wu_hbm, wd_hbm, yr_ref,
                   xbuf, wgf, wuf, wdf, wgb, wub, wdb, sem, wsem):
    i = pl.program_id(0)
    last = lb_ref[0]
    slot = i % 2

    def weight_copies(e, w):
        return [pltpu.make_async_copy(hbm.at[e], buf.at[w], wsem.at[w])
                for hbm, buf in ((wg_hbm, wgf), (wu_hbm, wuf), (wd_hbm, wdf))]

    @pl.when(i == 0)
    def _():
        _gather_chunks(src_ref, 0, BLOCK_CHUNKS, xs_hbm, xbuf.at[0], sem.at[0])
        for cp in weight_copies(be_ref[0], 0):
            cp.start()

    @pl.when(i <= last)
    def _():
        changed = (i == 0) | (be_ref[i] != be_ref[jnp.maximum(i - 1, 0)])

        @pl.when(changed)
        def _():
            w = ws_ref[i]
            for cp in weight_copies(be_ref[i], w):
                cp.wait()
            wgb[...] = wgf[w].astype(jnp.bfloat16)
            wub[...] = wuf[w].astype(jnp.bfloat16)
            wdb[...] = wdf[w].astype(jnp.bfloat16)

            @pl.when(ne_ref[i] >= 0)
            def _():
                for cp in weight_copies(ne_ref[i], 1 - w):
                    cp.start(priority=1)

        nxt = jnp.minimum(i + 1, last)
        _wait_chunks(BLOCK_CHUNKS, xs_hbm, xbuf.at[slot], sem.at[slot])
        half = wgb.shape[0] // 2
        n_k = wgb.shape[0] // EXPERT_K
        hg = hu = None
        for k in range(n_k):
            col0 = k * EXPERT_K
            words = xbuf[slot, :, col0 % half:col0 % half + EXPERT_K]
            bits = words << 16 if col0 < half else words & jnp.uint32(0xFFFF0000)
            xk = lax.bitcast_convert_type(bits, jnp.float32).astype(jnp.bfloat16)
            pg = _dot(xk, wgb[col0:col0 + EXPERT_K, :])
            pu = _dot(xk, wub[col0:col0 + EXPERT_K, :])
            hg = pg if hg is None else hg + pg
            hu = pu if hu is None else hu + pu
            _gather_chunks(src_ref, nxt * BLOCK_CHUNKS, BLOCK_CHUNKS // n_k, xs_hbm, xbuf.at[1 - slot],
                           sem.at[1 - slot], inline=True, first=k * (BLOCK_CHUNKS // n_k))
        hid = (hg * _sigmoid(hg) * hu).astype(jnp.bfloat16)
        yr_ref[...] = _pack_bf16_pairs(_dot(hid, wdb[...]).astype(jnp.bfloat16).astype(jnp.float32))

        @pl.when(i == last)
        def _():
            _wait_chunks(BLOCK_CHUNKS, xs_hbm, xbuf.at[1 - slot], sem.at[1 - slot])

    @pl.when(i > last)
    def _():
        yr_ref[...] = jnp.zeros_like(yr_ref)


def _experts(src_chunk, block_e, next_e, w_slot, last_blk, xs, w_gate, w_up, w_down):
    d = w_gate.shape[1]
    n_rows = src_chunk.shape[0] * CHUNK
    hbm = pl.BlockSpec(memory_space=pl.ANY)
    return pl.pallas_call(
        _expert_kernel,
        grid_spec=pltpu.PrefetchScalarGridSpec(
            num_scalar_prefetch=5,
            grid=(n_rows // BM,),
            in_specs=[hbm, hbm, hbm, hbm],
            out_specs=pl.BlockSpec((BM, d // 2), lambda i, *_: (i, 0)),
            scratch_shapes=[pltpu.VMEM((2, BM, d // 2), jnp.uint32),
                            pltpu.VMEM((2, d, D_EXPERT), jnp.float32),
                            pltpu.VMEM((2, d, D_EXPERT), jnp.float32),
                            pltpu.VMEM((2, D_EXPERT, d), jnp.float32),
                            pltpu.VMEM((d, D_EXPERT), jnp.bfloat16),
                            pltpu.VMEM((d, D_EXPERT), jnp.bfloat16),
                            pltpu.VMEM((D_EXPERT, d), jnp.bfloat16),
                            pltpu.SemaphoreType.DMA((2,)),
                            pltpu.SemaphoreType.DMA((2,))]),
        out_shape=jax.ShapeDtypeStruct((n_rows, d // 2), jnp.uint32),
        compiler_params=pltpu.CompilerParams(dimension_semantics=("arbitrary",),
                                             vmem_limit_bytes=VMEM_LIMIT),
        name="experts",
    )(src_chunk, block_e, next_e, w_slot, last_blk, xs, w_gate, w_up, w_down)


def _combine_kernel(src_ref, x1_ref, rw_ref, mod_ref, fg_ref, yr_hbm, out_ref, ybuf, sem, *, final_norm):
    i = pl.program_id(0)
    slot = i % 2

    @pl.when(i == 0)
    def _():
        _gather_chunks(src_ref, 0, SORT_CHUNKS, yr_hbm, ybuf.at[0], sem.at[0])

    rows = jnp.concatenate([rw_ref[...], jnp.zeros((LANES - ROUTE_ROWS, TM), jnp.float32)], axis=0)
    rw = jnp.transpose(rows)
    sp = lax.broadcasted_iota(jnp.int32, (1, SORT_ROWS), 1)
    p1 = rw[:, 2:3].astype(jnp.int32)
    p2 = rw[:, 3:4].astype(jnp.int32)
    wmat = jnp.where(sp == p1, rw[:, 0:1], jnp.where(sp == p2, rw[:, 1:2], 0.0)).astype(jnp.bfloat16)
    nxt = jnp.minimum(i + 1, pl.num_programs(0) - 1)
    _wait_chunks(SORT_CHUNKS, yr_hbm, ybuf.at[slot], sem.at[slot])
    moe = None
    for k in range(SORT_ROWS // COMBINE_K):
        part = _dot(wmat[:, k * COMBINE_K:(k + 1) * COMBINE_K],
                    _unpack_bf16_pairs(ybuf[slot, k * COMBINE_K:(k + 1) * COMBINE_K, :]))
        moe = part if moe is None else moe + part
        _gather_chunks(src_ref, nxt * SORT_CHUNKS, COMBINE_K // CHUNK, yr_hbm, ybuf.at[1 - slot], sem.at[1 - slot],
                       inline=True, split_priority=True, first=k * COMBINE_K // CHUNK)
    x2 = x1_ref[...] + mod_ref[5:6, :] * moe
    if final_norm:
        x2 = x2 * lax.rsqrt(jnp.mean(x2 * x2, axis=-1, keepdims=True) + EPS) * fg_ref[...]
    out_ref[...] = x2

    @pl.when(i == pl.num_programs(0) - 1)
    def _():
        _wait_chunks(SORT_CHUNKS, yr_hbm, ybuf.at[1 - slot], sem.at[1 - slot])


def _combine(src_chunk, x1, rw, mod, final_g, yr, seq, final_norm):
    t, d = x1.shape
    per_seq = seq // TM
    return pl.pallas_call(
        functools.partial(_combine_kernel, final_norm=final_norm),
        grid_spec=pltpu.PrefetchScalarGridSpec(
            num_scalar_prefetch=1,
            grid=(t // TM,),
            in_specs=[pl.BlockSpec((TM, d), lambda i, src: (i, 0)),
                      pl.BlockSpec((ROUTE_ROWS, TM), lambda i, src: (0, i)),
                      pl.BlockSpec((None, 6, d), lambda i, src: (i // per_seq, 0, 0)),
                      pl.BlockSpec((1, d), lambda i, src: (0, 0)),
                      pl.BlockSpec(memory_space=pl.ANY)],
            out_specs=pl.BlockSpec((TM, d), lambda i, src: (i, 0)),
            scratch_shapes=[pltpu.VMEM((2, SORT_ROWS, d // 2), jnp.uint32),
                            pltpu.SemaphoreType.DMA((2,))]),
        out_shape=jax.ShapeDtypeStruct((t, d), jnp.float32),
        compiler_params=pltpu.CompilerParams(dimension_semantics=("arbitrary",),
                                             vmem_limit_bytes=VMEM_LIMIT),
        name="combine",
    )(src_chunk, x1, rw, mod, final_g.reshape(1, d), yr)


def kernel(x, c, positions, rel_bias, w_ada, b_ada, norm1_g, w_in, sinks, gm_ln_g, gm_ln_b, gm_w_s, gm_b_s,
           p_a, p_b, w_o, norm2_g, w_router_g, b_router_g, w_router_e, b_router_e, w_gate, w_up, w_down,
           final_g):
    bsz, seq, d = x.shape
    t = bsz * seq
    bf16 = jnp.bfloat16
    depth = w_ada.shape[0]
    n_tiles = t // TM
    n_chunks = t * TOP_K // CHUNK + n_tiles * N_EXPERTS + N_EXPERTS * BLOCK_CHUNKS
    i32 = jnp.int32
    for l in range(depth):
        mod = _adaln_mod(c, w_ada[l], b_ada[l]).reshape(bsz, 6, d)
        pad = ROUTER_LANES - N_GROUPS - N_EXPERTS
        w_r = jnp.concatenate([w_router_g[l], w_router_e[l], jnp.zeros((d, pad), jnp.float32)], axis=1)
        b_r = jnp.concatenate([b_router_g[l], b_router_e[l], jnp.zeros((pad,), jnp.float32)]).reshape(1, -1)
        bsb = jnp.broadcast_to(gm_b_s[l][:, :, None], (GM_GROUPS, GM_CHUNK, LANES))
        x1, xs, rw, cnt = _mixer(
            x, mod, positions, rel_bias, norm1_g[l].reshape(1, d), w_in[l].astype(bf16), sinks[l],
            gm_ln_g[l].reshape(1, -1), gm_ln_b[l].reshape(1, -1), gm_w_s[l], bsb,
            p_a[l].astype(bf16), p_b[l].astype(bf16), w_o[l].astype(bf16), norm2_g[l].reshape(1, d),
            w_r.astype(bf16), b_r)

        n = cnt[:, EXPERT_LANE0:EXPERT_LANE0 + N_EXPERTS, 0].astype(i32)
        nch = (n + CHUNK - 1) // CHUNK
        run0_tile = jnp.cumsum(nch, axis=1) - nch
        run0_exp = jnp.cumsum(nch, axis=0) - nch
        tot = jnp.sum(nch, axis=0)
        seg = (tot + BLOCK_CHUNKS - 1) // BLOCK_CHUNKS * BLOCK_CHUNKS
        seg_end = jnp.cumsum(seg)
        seg0 = seg_end - seg

        pos_tile = jnp.arange(n_tiles, dtype=i32)[:, None] * SORT_CHUNKS + run0_tile
        pos_exp = seg0[None, :] + run0_exp

        j = jnp.arange(n_chunks, dtype=i32)[:, None]
        e_j = jnp.minimum(jnp.sum(j >= seg_end[None, :], axis=1), N_EXPERTS - 1)
        tables = jnp.concatenate([pos_exp.T, (pos_exp + nch).T, (pos_tile - pos_exp).T], axis=1)
        picked = jnp.dot(jax.nn.one_hot(e_j, N_EXPERTS, dtype=jnp.float32), tables.astype(jnp.float32),
                         precision=lax.Precision.HIGHEST).astype(i32)
        lo_j, hi_j, shift_j = picked[:, :n_tiles], picked[:, n_tiles:2 * n_tiles], picked[:, 2 * n_tiles:]
        inside = (j >= lo_j) & (j < hi_j)
        src_j = jnp.sum(jnp.where(inside, j + shift_j, 0), axis=1).astype(i32)
        blk0 = jnp.arange(n_chunks // BLOCK_CHUNKS, dtype=i32) * BLOCK_CHUNKS
        block_e = jnp.minimum(jnp.sum(blk0[:, None] >= seg_end[None, :], axis=1), N_EXPERTS - 1).astype(i32)
        last_blk = (jnp.maximum(seg_end[-1] // BLOCK_CHUNKS, 1) - 1).astype(i32).reshape(1)
        experts = jnp.arange(N_EXPERTS, dtype=i32)
        later = (experts[None, :] > block_e[:, None]) & (seg[None, :] > 0)
        next_e = jnp.min(jnp.where(later, experts[None, :], N_EXPERTS), axis=1)
        next_e = jnp.where(next_e < N_EXPERTS, next_e, -1).astype(i32)
        w_slot = (jnp.sum((experts[None, :] < block_e[:, None]) & (seg[None, :] > 0), axis=1) % 2).astype(i32)

        q = jnp.arange(SORT_CHUNKS, dtype=i32)[None, :, None]
        in_run = (q >= run0_tile[:, None, :]) & (q < (run0_tile + nch)[:, None, :])
        dst_q = jnp.sum(jnp.where(in_run, q + (pos_exp - run0_tile)[:, None, :], 0), axis=2).astype(i32).reshape(-1)

        yr = _experts(src_j, block_e, next_e, w_slot, last_blk, xs, w_gate[l], w_up[l], w_down[l])
        x = _combine(dst_q, x1.reshape(t, d), rw, mod, final_g, yr, seq,
                     final_norm=(l == depth - 1)).reshape(bsz, seq, d)
    return x
```

```python
import functools
import math

import jax
import jax.numpy as jnp
from jax import lax
from jax.experimental import pallas as pl
from jax.experimental.pallas import tpu as pltpu

D_MODEL = 1024
N_HEADS = 8
N_KV_HEADS = 2
HEAD_DIM = 64
BLOCK = 128
ATTN_Q = N_HEADS * HEAD_DIM
ATTN_KV = N_KV_HEADS * HEAD_DIM
N_BUCKETS = 32
MAX_EXACT = N_BUCKETS // 2
MAX_DISTANCE = 128
GM_WIDTH = 512
GM_GROUPS = 4
GM_CHUNK = 128
N_GROUPS = 4
EXPERTS_PER_GROUP = 8
N_EXPERTS = N_GROUPS * EXPERTS_PER_GROUP
TOP_K = 2
D_EXPERT = 512
EPS = 1e-6
NEG = -1e30

LANES = 128
ROUTER_LANES = LANES
EXPERT_LANE0 = N_GROUPS
TM = 512
BM = 512
ROUTER_ROWS = 48
ROUTE_ROWS = 8
CHUNK = 8
SORT_ROWS = TM * TOP_K + N_EXPERTS * CHUNK
SORT_CHUNKS = SORT_ROWS // CHUNK
BLOCK_CHUNKS = BM // CHUNK
GATHER_UNROLL = 8
TAIL_SPLIT = 2
MIB = 1024 * 1024
MIXER_VMEM = 44 * MIB
EXPERTS_VMEM = 32 * MIB
COMBINE_VMEM = 24 * MIB

ADALN_TN = 1024

Q0 = 0
K0 = Q0 + ATTN_Q
V0 = K0 + ATTN_KV
GU0 = V0 + ATTN_KV
GV0 = GU0 + GM_WIDTH
GA0 = GV0 + GM_WIDTH
GB0 = GA0 + D_MODEL
IN_END = GB0 + D_MODEL


def _dot(a, b):
    return jnp.dot(a, b, preferred_element_type=jnp.float32)


def _dot_nt(a, b):
    return lax.dot_general(a, b, (((1,), (1,)), ((), ())), preferred_element_type=jnp.float32)


LOG2E = math.log2(math.e)


def _gelu_tanh(x):
    c = math.sqrt(2.0 / math.pi)
    k0, k1 = -2.0 * c * LOG2E, -2.0 * c * 0.044715 * LOG2E
    return x * (1.0 / (1.0 + jnp.exp2(x * (x * x * k1 + k0))))


def _sigmoid(x):
    return 1.0 / (1.0 + jnp.exp2(x * -LOG2E))


def _pack_bf16_pairs(x):
    bits = lax.bitcast_convert_type(x, jnp.uint32)
    half = x.shape[1] // 2
    return (bits[:, 0:half] >> 16) | (bits[:, half:] & jnp.uint32(0xFFFF0000))


def _unpack_bf16_pairs(w):
    lo = lax.bitcast_convert_type(w << 16, jnp.float32)
    hi = lax.bitcast_convert_type(w & jnp.uint32(0xFFFF0000), jnp.float32)
    return jnp.concatenate([lo, hi], axis=1).astype(jnp.bfloat16)


def _adaln_kernel(c_ref, w_ref, b_ref, o_ref):
    c = c_ref[...]
    cs = c * _sigmoid(c)
    o_ref[...] = _dot(cs, w_ref[...]) + b_ref[...]


def _adaln_mod(c, w, b):
    bsz, d = c.shape
    n = w.shape[1]
    tn = ADALN_TN
    return pl.pallas_call(
        _adaln_kernel,
        grid=(n // tn,),
        in_specs=[pl.BlockSpec((bsz, d), lambda i: (0, 0)),
                  pl.BlockSpec((d, tn), lambda i: (0, i)),
                  pl.BlockSpec((1, tn), lambda i: (0, i))],
        out_specs=pl.BlockSpec((bsz, tn), lambda i: (0, i)),
        out_shape=jax.ShapeDtypeStruct((bsz, n), jnp.float32),
        compiler_params=pltpu.CompilerParams(dimension_semantics=("arbitrary",)),
        name="adaln_mod",
    )(c, w, b.reshape(1, n))


def _mixer_kernel(relb_ref, sinks_ref, spos_ref,
                  x_ref, mod_ref, pos_ref, n1g_ref, win_ref, lng_ref, lnb_ref, ws_ref, bsb_ref,
                  pa_ref, pb_ref, wo_ref, n2g_ref, wr_ref, br_ref,
                  x1_ref, xs_ref, rw_ref, cnt_ref,
                  kbuf, vbuf, pbuf, biasm, key, ya, yb, strict, sbuf, mbuf, flag):
    b = pl.program_id(0)
    j = pl.program_id(1)
    nblk = TM // BLOCK
    bf16 = jnp.bfloat16

    @pl.when((b == 0) & (j == 0))
    def _():
        key[...] = jnp.zeros_like(key)
        for i in range(nblk):
            flag[i] = 1
        tr = lax.broadcasted_iota(jnp.int32, (TM, TM), 0)
        tc = lax.broadcasted_iota(jnp.int32, (TM, TM), 1)
        strict[...] = jnp.where(tr < tc, 1.0, 0.0).astype(bf16)

    @pl.when(j == 0)
    def _():
        kbuf[0:BLOCK, :] = jnp.zeros((BLOCK, LANES), jnp.float32)
        vbuf[0:BLOCK, :] = jnp.zeros((BLOCK, LANES), jnp.float32)
        pbuf[:, 0:BLOCK] = jnp.zeros((1, BLOCK), jnp.int32)

    pbuf[:, BLOCK:] = pos_ref[...]
    qi = lax.broadcasted_iota(jnp.int32, (BLOCK, BLOCK), 0)
    kc = lax.broadcasted_iota(jnp.int32, (BLOCK, BLOCK), 1)
    from_prev = kc > qi
    rels, changed = [], []
    tile0 = (b * pl.num_programs(1) + j) * TM
    for i in range(nblk):
        pk = pbuf[:, i * BLOCK:(i + 2) * BLOCK]
        rels.append(pk - spos_ref[tile0 + i * BLOCK])
        changed.append(jnp.where(rels[i] != key[i], 1.0, 0.0))
    any_changed = jnp.max(functools.reduce(jnp.maximum, changed))
    any_flag = functools.reduce(jnp.maximum, [flag[i] for i in range(nblk)])
    refresh = (any_changed != 0) | (any_flag != 0) | (j == 0)

    def refresh_bias(i):
        pk = pbuf[:, i * BLOCK:(i + 2) * BLOCK]
        pq = pk[:, BLOCK:]
        pq_col = jnp.transpose(jnp.broadcast_to(pq, (BLOCK, BLOCK)))
        no_prev = (j == 0) if i == 0 else None
        for hd in range(N_HEADS):
            biasm[i * N_HEADS + hd] = jnp.zeros((BLOCK, BLOCK), jnp.float32)
        for side in range(2):
            dist = pq_col - pk[:, side * BLOCK:(side + 1) * BLOCK]
            n = jnp.maximum(dist, 0)
            nf = jnp.maximum(n, 1).astype(jnp.float32)
            large = MAX_EXACT + (jnp.log(nf / MAX_EXACT) / math.log(MAX_DISTANCE / MAX_EXACT)
                                 * (N_BUCKETS - MAX_EXACT)).astype(jnp.int32)
            large = jnp.minimum(large, N_BUCKETS - 1)
            bucket = jnp.where(n < MAX_EXACT, n, large)
            use = from_prev if side == 0 else jnp.logical_not(from_prev)
            for hd in range(N_HEADS):
                acc = jnp.zeros((BLOCK, BLOCK), jnp.float32)
                for bk in range(N_BUCKETS):
                    acc = jnp.where(bucket == bk, relb_ref[bk * N_HEADS + hd], acc)
                if side == 0 and i == 0:
                    acc = jnp.where(no_prev, NEG, acc)
                slot = i * N_HEADS + hd
                biasm[slot] = jnp.where(use, acc, biasm[slot])
        key[i] = rels[i]
        flag[i] = no_prev.astype(jnp.int32) if i == 0 else 0

    for i in range(nblk):
        @pl.when(refresh)
        def _():
            stale = (jnp.max(changed[i]) != 0) | (flag[i] != 0)
            if i == 0:
                stale = stale | (j == 0)
            pl.when(stale)(functools.partial(refresh_bias, i))

    x = x_ref[...]
    sh1, sc1, g1 = mod_ref[0:1, :], mod_ref[1:2, :], mod_ref[2:3, :]
    sh2, sc2 = mod_ref[3:4, :], mod_ref[4:5, :]

    xn = x * lax.rsqrt(jnp.mean(x * x, axis=-1, keepdims=True) + EPS)
    h = (xn * (n1g_ref[...] * (1.0 + sc1)) + sh1).astype(bf16)

    lane = lax.broadcasted_iota(jnp.int32, (1, LANES), 1)
    lo = lane < HEAD_DIM
    q = _dot(h, win_ref[:, Q0:K0]) * (HEAD_DIM ** -0.5)
    lo4 = jnp.concatenate([lo] * (ATTN_Q // LANES), axis=1)
    q_lo = jnp.where(lo4, q, 0.0).astype(bf16)
    q_hi = jnp.where(lo4, 0.0, q).astype(bf16)
    kv = _dot(h, win_ref[:, K0:GU0])
    kbuf[BLOCK:, :] = kv[:, 0:LANES]
    vbuf[BLOCK:, :] = kv[:, LANES:]
    kf = kbuf[...]
    vf = vbuf[...]
    kr = pltpu.roll(kf, HEAD_DIM, 1)
    vr = pltpu.roll(vf, HEAD_DIM, 1)
    kd = (jnp.where(lo, kf, kr).astype(bf16), jnp.where(lo, kr, kf).astype(bf16))
    v_lo = (jnp.where(lo, vf, 0.0).astype(bf16), jnp.where(lo, vr, 0.0).astype(bf16))
    v_hi = (jnp.where(lo, 0.0, vr).astype(bf16), jnp.where(lo, 0.0, vf).astype(bf16))

    ones_blk = jnp.ones((2 * BLOCK, LANES), bf16)

    def attend(i):
        rows = slice(i * BLOCK, (i + 1) * BLOCK)
        band = slice(i * BLOCK, (i + 2) * BLOCK)
        grp = N_HEADS // N_KV_HEADS
        for kvh in range(N_KV_HEADS):
            heads = range(kvh * grp, (kvh + 1) * grp)
            qs = jnp.concatenate([(q_lo if hd % 2 == 0 else q_hi)[rows, hd // 2 * LANES:(hd // 2 + 1) * LANES]
                                  for hd in heads], axis=0)
            s2 = _dot_nt(qs, kd[kvh][band])
            for g, hd in enumerate(heads):
                sh = s2[g * BLOCK:(g + 1) * BLOCK]
                s = jnp.where(from_prev, sh[:, 0:BLOCK], sh[:, BLOCK:]) + biasm[i * N_HEADS + hd]
                sbuf[hd] = s
                m = jnp.maximum(jnp.max(s, axis=-1, keepdims=True), sinks_ref[hd])
                mbuf[hd] = jnp.broadcast_to(m, (BLOCK, BLOCK))
        for kvh in range(N_KV_HEADS):
            outs = {}
            for half, vv in enumerate((v_lo, v_hi)):
                hds = (kvh * grp + half, kvh * grp + half + 2)
                p2 = []
                for hd in hds:
                    p = jnp.exp(sbuf[hd] - mbuf[hd])
                    p2.append(jnp.concatenate([jnp.where(from_prev, p, 0.0), jnp.where(from_prev, 0.0, p)],
                                              axis=1).astype(bf16))
                r = _dot(jnp.concatenate(p2, axis=0), jnp.concatenate([vv[kvh][band], ones_blk], axis=1))
                for n, hd in enumerate(hds):
                    rh = r[n * BLOCK:(n + 1) * BLOCK]
                    den = rh[:, LANES:] + jnp.exp(sinks_ref[hd] - mbuf[hd])
                    outs[hd] = rh[:, 0:LANES] * (1.0 / den)
            for pr in (kvh * grp // 2, kvh * grp // 2 + 1):
                ya[rows, pr * LANES:(pr + 1) * LANES] = (outs[2 * pr] + outs[2 * pr + 1]).astype(bf16)

    u = _gelu_tanh(_dot(h, win_ref[:, GU0:GV0]))
    attend(0)
    vg = _gelu_tanh(_dot(h, win_ref[:, GV0:GA0]))
    mu = jnp.mean(vg, axis=-1, keepdims=True)
    vc = vg - mu
    var = jnp.mean(vc * vc, axis=-1, keepdims=True)
    vn = (vc * lax.rsqrt(var + EPS) * lng_ref[...] + lnb_ref[...]).astype(bf16)
    attend(1)
    gate_a = _sigmoid(_dot(h, win_ref[:, GA0:GB0]))
    attend(2)
    ti = lax.broadcasted_iota(jnp.int32, (GM_CHUNK, GM_CHUNK), 0)
    si = lax.broadcasted_iota(jnp.int32, (GM_CHUNK, GM_CHUNK), 1)
    tril = si <= ti
    for g in range(GM_GROUPS):
        wg = jnp.where(tril, ws_ref[g], 0.0).astype(bf16)
        cols = slice(g * LANES, (g + 1) * LANES)
        for cidx in range(TM // GM_CHUNK):
            rows = slice(cidx * GM_CHUNK, (cidx + 1) * GM_CHUNK)
            sv = _dot(wg, vn[rows, cols]) + bsb_ref[g]
            yb[rows, cols] = (u[rows, cols] * sv).astype(bf16)
    gate_b = _sigmoid(_dot(h, win_ref[:, GB0:IN_END]))
    attend(3)

    kbuf[0:BLOCK, :] = kv[TM - BLOCK:, 0:LANES]
    vbuf[0:BLOCK, :] = kv[TM - BLOCK:, LANES:]
    pbuf[:, 0:BLOCK] = pos_ref[:, TM - BLOCK:]

    h2_parts, logit_parts = [], []
    for grp in range(TAIL_SPLIT):
        tok = slice(grp * TM // TAIL_SPLIT, (grp + 1) * TM // TAIL_SPLIT)
        merged = gate_a[tok] * _dot(ya[tok, :], pa_ref[...]) + gate_b[tok] * _dot(yb[tok, :], pb_ref[...])
        x1 = x[tok] + g1 * _dot(merged.astype(bf16), wo_ref[...])
        x1_ref[tok, :] = x1

        xn2 = x1 * lax.rsqrt(jnp.mean(x1 * x1, axis=-1, keepdims=True) + EPS)
        h2_parts.append((xn2 * (n2g_ref[...] * (1.0 + sc2)) + sh2).astype(bf16))
        logit_parts.append(_dot(h2_parts[-1], wr_ref[...]) + br_ref[...])
    h2 = jnp.concatenate(h2_parts, axis=0)
    logits = jnp.concatenate(logit_parts, axis=0)
    lt = jnp.transpose(logits)[0:ROUTER_ROWS, :]
    row = lax.broadcasted_iota(jnp.int32, (ROUTER_ROWS, TM), 0)
    row_f = row.astype(jnp.float32)
    big = float(ROUTER_ROWS)
    is_grp = row < N_GROUPS
    lg = jnp.where(is_grp, lt, NEG)
    lg_max = jnp.max(lg, axis=0, keepdims=True)
    g_idx = jnp.min(jnp.where(lg == lg_max, row_f, big), axis=0, keepdims=True)
    p_g = 1.0 / jnp.sum(jnp.where(is_grp, jnp.exp(lg - lg_max), 0.0), axis=0, keepdims=True)
    row_grp = jnp.floor((row_f - EXPERT_LANE0) * (1.0 / EXPERTS_PER_GROUP))
    in_grp = (row >= EXPERT_LANE0) & (row < EXPERT_LANE0 + N_EXPERTS) & (row_grp == g_idx)
    le = jnp.where(in_grp, lt, NEG)
    m1 = jnp.max(le, axis=0, keepdims=True)
    i1 = jnp.min(jnp.where(le == m1, row_f, big), axis=0, keepdims=True)
    oh1 = row_f == i1
    le2 = jnp.where(oh1, NEG, le)
    m2 = jnp.max(le2, axis=0, keepdims=True)
    i2 = jnp.min(jnp.where(le2 == m2, row_f, big), axis=0, keepdims=True)
    oh2 = row_f == i2
    e2 = jnp.exp(m2 - m1)
    w1 = p_g / (1.0 + e2)
    w2 = p_g * e2 / (1.0 + e2)

    oh = jnp.where(oh1, 1.0, jnp.where(oh2, 1.0, 0.0))
    n_e = jnp.sum(oh, axis=1, keepdims=True)
    cnt_ref[...] = n_e
    padded = jnp.floor((n_e + (CHUNK - 1)) * (1.0 / CHUNK)) * CHUNK
    dst_row = lax.broadcasted_iota(jnp.int32, (ROUTER_ROWS, ROUTER_ROWS), 0)
    src_row = lax.broadcasted_iota(jnp.int32, (ROUTER_ROWS, ROUTER_ROWS), 1)
    lower = jnp.where(src_row < dst_row, 1.0, 0.0).astype(bf16)
    run0 = _dot(lower, jnp.broadcast_to(padded, (ROUTER_ROWS, TM)).astype(bf16))
    slot = _dot(oh.astype(bf16), strict[...]) + run0
    pos1 = jnp.sum(jnp.where(oh1, slot, 0.0), axis=0, keepdims=True)
    pos2 = jnp.sum(jnp.where(oh2, slot, 0.0), axis=0, keepdims=True)
    rrow = lax.broadcasted_iota(jnp.int32, (ROUTE_ROWS, TM), 0)
    rw_ref[...] = jnp.where(rrow == 0, w1, jnp.where(rrow == 1, w2,
                            jnp.where(rrow == 2, pos1, jnp.where(rrow == 3, pos2, 0.0))))

    p1 = pos1.astype(jnp.int32)
    p2 = pos2.astype(jnp.int32)
    sp = lax.broadcasted_iota(jnp.int32, (SORT_ROWS, TM), 0)
    perm = jnp.where(sp == p1, 1.0, jnp.where(sp == p2, 1.0, 0.0)).astype(bf16)
    xs_ref[...] = _pack_bf16_pairs(_dot(perm, h2))


def _mixer(x, mod, positions, rel_bias, n1g, w_in, sinks, lng, lnb, w_s, bsb, p_a, p_b, w_o, n2g, w_r, b_r):
    bsz, seq, d = x.shape
    nj = seq // TM
    const = lambda *shape: pl.BlockSpec(shape, lambda b, j: (0,) * len(shape), pipeline_mode=pl.Buffered(1))
    smem = pl.BlockSpec(memory_space=pltpu.SMEM)
    tile = lambda w: pl.BlockSpec((None, TM, w), lambda b, j: (b, j, 0))
    return pl.pallas_call(
        _mixer_kernel,
        grid=(bsz, nj),
        in_specs=[smem, smem, smem,
                  tile(d),
                  pl.BlockSpec((None, 6, d), lambda b, j: (b, 0, 0)),
                  pl.BlockSpec((None, None, 1, TM), lambda b, j: (b, j, 0, 0)),
                  const(1, d), const(d, IN_END), const(1, GM_WIDTH), const(1, GM_WIDTH),
                  const(GM_GROUPS, GM_CHUNK, GM_CHUNK), const(GM_GROUPS, GM_CHUNK, LANES),
                  const(ATTN_Q, d), const(GM_WIDTH, d), const(d, d), const(1, d),
                  const(d, ROUTER_LANES), const(1, ROUTER_LANES)],
        out_specs=[tile(d),
                   pl.BlockSpec((SORT_ROWS, d // 2), lambda b, j: (b * nj + j, 0)),
                   pl.BlockSpec((ROUTE_ROWS, TM), lambda b, j: (0, b * nj + j)),
                   pl.BlockSpec((None, ROUTER_ROWS, 1), lambda b, j: (b * nj + j, 0, 0))],
        out_shape=[jax.ShapeDtypeStruct((bsz, seq, d), jnp.float32),
                   jax.ShapeDtypeStruct((bsz * nj * SORT_ROWS, d // 2), jnp.uint32),
                   jax.ShapeDtypeStruct((ROUTE_ROWS, bsz * seq), jnp.float32),
                   jax.ShapeDtypeStruct((bsz * nj, ROUTER_ROWS, 1), jnp.float32)],
        scratch_shapes=[pltpu.VMEM((TM + BLOCK, LANES), jnp.float32),
                        pltpu.VMEM((TM + BLOCK, LANES), jnp.float32),
                        pltpu.VMEM((1, TM + BLOCK), jnp.int32),
                        pltpu.VMEM((TM // BLOCK * N_HEADS, BLOCK, BLOCK), jnp.float32),
                        pltpu.VMEM((TM // BLOCK, 1, 2 * BLOCK), jnp.int32),
                        pltpu.VMEM((TM, ATTN_Q), jnp.bfloat16),
                        pltpu.VMEM((TM, GM_WIDTH), jnp.bfloat16),
                        pltpu.VMEM((TM, TM), jnp.bfloat16),
                        pltpu.VMEM((N_HEADS, BLOCK, BLOCK), jnp.float32),
                        pltpu.VMEM((N_HEADS, BLOCK, BLOCK), jnp.float32),
                        pltpu.SMEM((TM // BLOCK,), jnp.int32)],
        compiler_params=pltpu.CompilerParams(dimension_semantics=("arbitrary", "arbitrary"),
                                             vmem_limit_bytes=MIXER_VMEM),
        name="mixer",
    )(rel_bias.reshape(-1), sinks, positions.reshape(-1), x, mod, positions.reshape(bsz, nj, 1, TM), n1g, w_in, lng, lnb,
      w_s, bsb, p_a, p_b, w_o, n2g, w_r, b_r)


def _gather_chunks(idx_ref, idx0, n, src_hbm, dst, sem, inline=False, split_priority=False):
    def issue(c, carry, priority=0):
        src = pl.multiple_of(idx_ref[idx0 + c] * CHUNK, CHUNK)
        pltpu.make_async_copy(src_hbm.at[pl.ds(src, CHUNK)],
                              dst.at[pl.ds(pl.multiple_of(c * CHUNK, CHUNK), CHUNK)], sem).start(priority)
        return carry

    if inline:
        for c in range(n):
            issue(c, 0, c % 2 if split_priority else 0)
    else:
        lax.fori_loop(0, n, issue, 0, unroll=GATHER_UNROLL)


def _wait_chunks(n, src_hbm, dst, sem):
    pltpu.make_async_copy(src_hbm.at[pl.ds(0, n * CHUNK)], dst, sem).wait()


def _expert_kernel(src_ref, be_ref, ne_ref, ws_ref, lb_ref, xs_hbm, wg_hbm, wu_hbm, wd_hbm, yr_ref,
                   xbuf, wgf, wuf, wdf, wgb, wub, wdb, sem, wsem):
    i = pl.program_id(0)
    last = lb_ref[0]
    slot = i % 2

    def weight_copies(e, w):
        return [pltpu.make_async_copy(hbm.at[e], buf.at[w], wsem.at[w])
                for hbm, buf in ((wg_hbm, wgf), (wu_hbm, wuf), (wd_hbm, wdf))]

    @pl.when(i == 0)
    def _():
        _gather_chunks(src_ref, 0, BLOCK_CHUNKS, xs_hbm, xbuf.at[0], sem.at[0])
        for cp in weight_copies(be_ref[0], 0):
            cp.start()

    @pl.when(i <= last)
    def _():
        changed = (i == 0) | (be_ref[i] != be_ref[jnp.maximum(i - 1, 0)])

        @pl.when(changed)
        def _():
            w = ws_ref[i]
            for cp in weight_copies(be_ref[i], w):
                cp.wait()
            wgb[...] = wgf[w].astype(jnp.bfloat16)
            wub[...] = wuf[w].astype(jnp.bfloat16)
            wdb[...] = wdf[w].astype(jnp.bfloat16)

            @pl.when(ne_ref[i] >= 0)
            def _():
                for cp in weight_copies(ne_ref[i], 1 - w):
                    cp.start(priority=1)

        nxt = jnp.minimum(i + 1, last)
        _gather_chunks(src_ref, nxt * BLOCK_CHUNKS, BLOCK_CHUNKS, xs_hbm, xbuf.at[1 - slot], sem.at[1 - slot],
                       inline=True)
        _wait_chunks(BLOCK_CHUNKS, xs_hbm, xbuf.at[slot], sem.at[slot])
        xb = _unpack_bf16_pairs(xbuf[slot])
        hg = _dot(xb, wgb[...])
        hu = _dot(xb, wub[...])
        hid = (hg * _sigmoid(hg) * hu).astype(jnp.bfloat16)
        yr_ref[...] = _pack_bf16_pairs(_dot(hid, wdb[...]).astype(jnp.bfloat16).astype(jnp.float32))

        @pl.when(i == last)
        def _():
            _wait_chunks(BLOCK_CHUNKS, xs_hbm, xbuf.at[1 - slot], sem.at[1 - slot])

    @pl.when(i > last)
    def _():
        yr_ref[...] = jnp.zeros_like(yr_ref)


def _experts(src_chunk, block_e, next_e, w_slot, last_blk, xs, w_gate, w_up, w_down):
    d = w_gate.shape[1]
    n_rows = src_chunk.shape[0] * CHUNK
    hbm = pl.BlockSpec(memory_space=pl.ANY)
    return pl.pallas_call(
        _expert_kernel,
        grid_spec=pltpu.PrefetchScalarGridSpec(
            num_scalar_prefetch=5,
            grid=(n_rows // BM,),
            in_specs=[hbm, hbm, hbm, hbm],
            out_specs=pl.BlockSpec((BM, d // 2), lambda i, *_: (i, 0)),
            scratch_shapes=[pltpu.VMEM((2, BM, d // 2), jnp.uint32),
                            pltpu.VMEM((2, d, D_EXPERT), jnp.float32),
                            pltpu.VMEM((2, d, D_EXPERT), jnp.float32),
                            pltpu.VMEM((2, D_EXPERT, d), jnp.float32),
                            pltpu.VMEM((d, D_EXPERT), jnp.bfloat16),
                            pltpu.VMEM((d, D_EXPERT), jnp.bfloat16),
                            pltpu.VMEM((D_EXPERT, d), jnp.bfloat16),
                            pltpu.SemaphoreType.DMA((2,)),
                            pltpu.SemaphoreType.DMA((2,))]),
        out_shape=jax.ShapeDtypeStruct((n_rows, d // 2), jnp.uint32),
        compiler_params=pltpu.CompilerParams(dimension_semantics=("arbitrary",),
                                             vmem_limit_bytes=EXPERTS_VMEM),
        name="experts",
    )(src_chunk, block_e, next_e, w_slot, last_blk, xs, w_gate, w_up, w_down)


def _combine_kernel(src_ref, x1_ref, rw_ref, mod_ref, fg_ref, yr_hbm, out_ref, ybuf, sem, *, final_norm):
    i = pl.program_id(0)
    slot = i % 2

    @pl.when(i == 0)
    def _():
        _gather_chunks(src_ref, 0, SORT_CHUNKS, yr_hbm, ybuf.at[0], sem.at[0])

    rows = jnp.concatenate([rw_ref[...], jnp.zeros((LANES - ROUTE_ROWS, TM), jnp.float32)], axis=0)
    rw = jnp.transpose(rows)
    sp = lax.broadcasted_iota(jnp.int32, (1, SORT_ROWS), 1)
    p1 = rw[:, 2:3].astype(jnp.int32)
    p2 = rw[:, 3:4].astype(jnp.int32)
    wmat = jnp.where(sp == p1, rw[:, 0:1], jnp.where(sp == p2, rw[:, 1:2], 0.0)).astype(jnp.bfloat16)
    nxt = jnp.minimum(i + 1, pl.num_programs(0) - 1)
    _gather_chunks(src_ref, nxt * SORT_CHUNKS, SORT_CHUNKS, yr_hbm, ybuf.at[1 - slot], sem.at[1 - slot],
                   inline=True, split_priority=True)
    _wait_chunks(SORT_CHUNKS, yr_hbm, ybuf.at[slot], sem.at[slot])
    moe = _dot(wmat, _unpack_bf16_pairs(ybuf[slot]))
    x2 = x1_ref[...] + mod_ref[5:6, :] * moe
    if final_norm:
        x2 = x2 * lax.rsqrt(jnp.mean(x2 * x2, axis=-1, keepdims=True) + EPS) * fg_ref[...]
    out_ref[...] = x2

    @pl.when(i == pl.num_programs(0) - 1)
    def _():
        _wait_chunks(SORT_CHUNKS, yr_hbm, ybuf.at[1 - slot], sem.at[1 - slot])


def _combine(src_chunk, x1, rw, mod, final_g, yr, seq, final_norm):
    t, d = x1.shape
    per_seq = seq // TM
    return pl.pallas_call(
        functools.partial(_combine_kernel, final_norm=final_norm),
        grid_spec=pltpu.PrefetchScalarGridSpec(
            num_scalar_prefetch=1,
            grid=(t // TM,),
            in_specs=[pl.BlockSpec((TM, d), lambda i, src: (i, 0)),
                      pl.BlockSpec((ROUTE_ROWS, TM), lambda i, src: (0, i)),
                      pl.BlockSpec((None, 6, d), lambda i, src: (i // per_seq, 0, 0)),
                      pl.BlockSpec((1, d), lambda i, src: (0, 0)),
                      pl.BlockSpec(memory_space=pl.ANY)],
            out_specs=pl.BlockSpec((TM, d), lambda i, src: (i, 0)),
            scratch_shapes=[pltpu.VMEM((2, SORT_ROWS, d // 2), jnp.uint32),
                            pltpu.SemaphoreType.DMA((2,))]),
        out_shape=jax.ShapeDtypeStruct((t, d), jnp.float32),
        compiler_params=pltpu.CompilerParams(dimension_semantics=("arbitrary",),
                                             vmem_limit_bytes=COMBINE_VMEM),
        name="combine",
    )(src_chunk, x1, rw, mod, final_g.reshape(1, d), yr)


def kernel(x, c, positions, rel_bias, w_ada, b_ada, norm1_g, w_in, sinks, gm_ln_g, gm_ln_b, gm_w_s, gm_b_s,
           p_a, p_b, w_o, norm2_g, w_router_g, b_router_g, w_router_e, b_router_e, w_gate, w_up, w_down,
           final_g):
    bsz, seq, d = x.shape
    t = bsz * seq
    bf16 = jnp.bfloat16
    depth = w_ada.shape[0]
    n_tiles = t // TM
    n_chunks = t * TOP_K // CHUNK + n_tiles * N_EXPERTS + N_EXPERTS * BLOCK_CHUNKS
    i32 = jnp.int32
    for l in range(depth):
        mod = _adaln_mod(c, w_ada[l], b_ada[l]).reshape(bsz, 6, d)
        pad = ROUTER_LANES - N_GROUPS - N_EXPERTS
        w_r = jnp.concatenate([w_router_g[l], w_router_e[l], jnp.zeros((d, pad), jnp.float32)], axis=1)
        b_r = jnp.concatenate([b_router_g[l], b_router_e[l], jnp.zeros((pad,), jnp.float32)]).reshape(1, -1)
        bsb = jnp.broadcast_to(gm_b_s[l][:, :, None], (GM_GROUPS, GM_CHUNK, LANES))
        x1, xs, rw, cnt = _mixer(
            x, mod, positions, rel_bias, norm1_g[l].reshape(1, d), w_in[l].astype(bf16), sinks[l],
            gm_ln_g[l].reshape(1, -1), gm_ln_b[l].reshape(1, -1), gm_w_s[l], bsb,
            p_a[l].astype(bf16), p_b[l].astype(bf16), w_o[l].astype(bf16), norm2_g[l].reshape(1, d),
            w_r.astype(bf16), b_r)

        n = cnt[:, EXPERT_LANE0:EXPERT_LANE0 + N_EXPERTS, 0].astype(i32)
        nch = (n + CHUNK - 1) // CHUNK
        run0_tile = jnp.cumsum(nch, axis=1) - nch
        run0_exp = jnp.cumsum(nch, axis=0) - nch
        tot = jnp.sum(nch, axis=0)
        seg = (tot + BLOCK_CHUNKS - 1) // BLOCK_CHUNKS * BLOCK_CHUNKS
        seg_end = jnp.cumsum(seg)
        seg0 = seg_end - seg

        pos_tile = jnp.arange(n_tiles, dtype=i32)[:, None] * SORT_CHUNKS + run0_tile
        pos_exp = seg0[None, :] + run0_exp

        j = jnp.arange(n_chunks, dtype=i32)[:, None]
        e_j = jnp.minimum(jnp.sum(j >= seg_end[None, :], axis=1), N_EXPERTS - 1)
        tables = jnp.concatenate([pos_exp.T, (pos_exp + nch).T, (pos_tile - pos_exp).T], axis=1)
        picked = jnp.dot(jax.nn.one_hot(e_j, N_EXPERTS, dtype=jnp.float32), tables.astype(jnp.float32),
                         precision=lax.Precision.HIGHEST).astype(i32)
        lo_j, hi_j, shift_j = picked[:, :n_tiles], picked[:, n_tiles:2 * n_tiles], picked[:, 2 * n_tiles:]
        inside = (j >= lo_j) & (j < hi_j)
        src_j = jnp.sum(jnp.where(inside, j + shift_j, 0), axis=1).astype(i32)
        blk0 = jnp.arange(n_chunks // BLOCK_CHUNKS, dtype=i32) * BLOCK_CHUNKS
        block_e = jnp.minimum(jnp.sum(blk0[:, None] >= seg_end[None, :], axis=1), N_EXPERTS - 1).astype(i32)
        last_blk = (jnp.maximum(seg_end[-1] // BLOCK_CHUNKS, 1) - 1).astype(i32).reshape(1)
        experts = jnp.arange(N_EXPERTS, dtype=i32)
        later = (experts[None, :] > block_e[:, None]) & (seg[None, :] > 0)
        next_e = jnp.min(jnp.where(later, experts[None, :], N_EXPERTS), axis=1)
        next_e = jnp.where(next_e < N_EXPERTS, next_e, -1).astype(i32)
        w_slot = (jnp.sum((experts[None, :] < block_e[:, None]) & (seg[None, :] > 0), axis=1) % 2).astype(i32)

        q = jnp.arange(SORT_CHUNKS, dtype=i32)[None, :, None]
        in_run = (q >= run0_tile[:, None, :]) & (q < (run0_tile + nch)[:, None, :])
        dst_q = jnp.sum(jnp.where(in_run, q + (pos_exp - run0_tile)[:, None, :], 0), axis=2).astype(i32).reshape(-1)

        yr = _experts(src_j, block_e, next_e, w_slot, last_blk, xs, w_gate[l], w_up[l], w_down[l])
        x = _combine(dst_q, x1.reshape(t, d), rw, mod, final_g, yr, seq,
                     final_norm=(l == depth - 1)).reshape(bsz, seq, d)
    return x
```

```python
import functools
import math

import jax
import jax.numpy as jnp
from jax import lax
from jax.experimental import pallas as pl
from jax.experimental.pallas import tpu as pltpu

D_MODEL = 1024
N_HEADS = 8
N_KV_HEADS = 2
HEAD_DIM = 64
BLOCK = 128
ATTN_Q = N_HEADS * HEAD_DIM
ATTN_KV = N_KV_HEADS * HEAD_DIM
N_BUCKETS = 32
MAX_EXACT = N_BUCKETS // 2
MAX_DISTANCE = 128
GM_WIDTH = 512
GM_GROUPS = 4
GM_CHUNK = 128
N_GROUPS = 4
EXPERTS_PER_GROUP = 8
N_EXPERTS = N_GROUPS * EXPERTS_PER_GROUP
TOP_K = 2
D_EXPERT = 512
EPS = 1e-6
NEG = -1e30

LANES = 128
ROUTER_LANES = LANES
EXPERT_LANE0 = N_GROUPS
TM = 512
BM = 512
ROUTER_ROWS = 48
ROUTE_ROWS = 8
CHUNK = 8
SORT_ROWS = TM * TOP_K + N_EXPERTS * CHUNK
SORT_CHUNKS = SORT_ROWS // CHUNK
BLOCK_CHUNKS = BM // CHUNK
BLOCK_QUARTERS = 4
GATHER_UNROLL = 8
TAIL_SPLIT = 2
MIB = 1024 * 1024
MIXER_VMEM = 44 * MIB
EXPERTS_VMEM = 32 * MIB
COMBINE_VMEM = 24 * MIB

ADALN_TN = 1024

Q0 = 0
K0 = Q0 + ATTN_Q
V0 = K0 + ATTN_KV
GU0 = V0 + ATTN_KV
GV0 = GU0 + GM_WIDTH
GA0 = GV0 + GM_WIDTH
GB0 = GA0 + D_MODEL
IN_END = GB0 + D_MODEL


def _dot(a, b):
    return jnp.dot(a, b, preferred_element_type=jnp.float32)


def _dot_nt(a, b):
    return lax.dot_general(a, b, (((1,), (1,)), ((), ())), preferred_element_type=jnp.float32)


LOG2E = math.log2(math.e)


def _gelu_tanh(x):
    c = math.sqrt(2.0 / math.pi)
    k0, k1 = -2.0 * c * LOG2E, -2.0 * c * 0.044715 * LOG2E
    return x * (1.0 / (1.0 + jnp.exp2(x * (x * x * k1 + k0))))


def _sigmoid(x):
    return 1.0 / (1.0 + jnp.exp2(x * -LOG2E))


def _pack_bf16_pairs(x):
    bits = lax.bitcast_convert_type(x, jnp.uint32)
    half = x.shape[1] // 2
    return (bits[:, 0:half] >> 16) | (bits[:, half:] & jnp.uint32(0xFFFF0000))


def _unpack_bf16_pairs(w):
    lo = lax.bitcast_convert_type(w << 16, jnp.float32)
    hi = lax.bitcast_convert_type(w & jnp.uint32(0xFFFF0000), jnp.float32)
    return jnp.concatenate([lo, hi], axis=1).astype(jnp.bfloat16)


def _adaln_kernel(c_ref, w_ref, b_ref, o_ref):
    c = c_ref[...]
    cs = c * _sigmoid(c)
    o_ref[...] = _dot(cs, w_ref[...]) + b_ref[...]


def _adaln_mod(c, w, b):
    bsz, d = c.shape
    n = w.shape[1]
    tn = ADALN_TN
    return pl.pallas_call(
        _adaln_kernel,
        grid=(n // tn,),
        in_specs=[pl.BlockSpec((bsz, d), lambda i: (0, 0)),
                  pl.BlockSpec((d, tn), lambda i: (0, i)),
                  pl.BlockSpec((1, tn), lambda i: (0, i))],
        out_specs=pl.BlockSpec((bsz, tn), lambda i: (0, i)),
        out_shape=jax.ShapeDtypeStruct((bsz, n), jnp.float32),
        compiler_params=pltpu.CompilerParams(dimension_semantics=("arbitrary",)),
        name="adaln_mod",
    )(c, w, b.reshape(1, n))


def _mixer_kernel(relb_ref, sinks_ref, spos_ref,
                  x_ref, mod_ref, pos_ref, n1g_ref, win_ref, lng_ref, lnb_ref, ws_ref, bsb_ref,
                  pa_ref, pb_ref, wo_ref, n2g_ref, wr_ref, br_ref,
                  x1_ref, xs_ref, rw_ref, cnt_ref,
                  kbuf, vbuf, pbuf, biasm, key, ya, yb, strict, sbuf, mbuf, flag):
    b = pl.program_id(0)
    j = pl.program_id(1)
    nblk = TM // BLOCK
    bf16 = jnp.bfloat16

    @pl.when((b == 0) & (j == 0))
    def _():
        key[...] = jnp.zeros_like(key)
        for i in range(nblk):
            flag[i] = 1
        tr = lax.broadcasted_iota(jnp.int32, (TM, TM), 0)
        tc = lax.broadcasted_iota(jnp.int32, (TM, TM), 1)
        strict[...] = jnp.where(tr < tc, 1.0, 0.0).astype(bf16)

    @pl.when(j == 0)
    def _():
        kbuf[0:BLOCK, :] = jnp.zeros((BLOCK, LANES), jnp.float32)
        vbuf[0:BLOCK, :] = jnp.zeros((BLOCK, LANES), jnp.float32)
        pbuf[:, 0:BLOCK] = jnp.zeros((1, BLOCK), jnp.int32)

    pbuf[:, BLOCK:] = pos_ref[...]
    qi = lax.broadcasted_iota(jnp.int32, (BLOCK, BLOCK), 0)
    kc = lax.broadcasted_iota(jnp.int32, (BLOCK, BLOCK), 1)
    from_prev = kc > qi
    rels, changed = [], []
    tile0 = (b * pl.num_programs(1) + j) * TM
    for i in range(nblk):
        pk = pbuf[:, i * BLOCK:(i + 2) * BLOCK]
        rels.append(pk - spos_ref[tile0 + i * BLOCK])
        changed.append(jnp.where(rels[i] != key[i], 1.0, 0.0))
    any_changed = jnp.max(functools.reduce(jnp.maximum, changed))
    any_flag = functools.reduce(jnp.maximum, [flag[i] for i in range(nblk)])
    refresh = (any_changed != 0) | (any_flag != 0) | (j == 0)

    def refresh_bias(i):
        pk = pbuf[:, i * BLOCK:(i + 2) * BLOCK]
        pq = pk[:, BLOCK:]
        pq_col = jnp.transpose(jnp.broadcast_to(pq, (BLOCK, BLOCK)))
        no_prev = (j == 0) if i == 0 else None
        for hd in range(N_HEADS):
            biasm[i * N_HEADS + hd] = jnp.zeros((BLOCK, BLOCK), jnp.float32)
        for side in range(2):
            dist = pq_col - pk[:, side * BLOCK:(side + 1) * BLOCK]
            n = jnp.maximum(dist, 0)
            nf = jnp.maximum(n, 1).astype(jnp.float32)
            large = MAX_EXACT + (jnp.log(nf / MAX_EXACT) / math.log(MAX_DISTANCE / MAX_EXACT)
                                 * (N_BUCKETS - MAX_EXACT)).astype(jnp.int32)
            large = jnp.minimum(large, N_BUCKETS - 1)
            bucket = jnp.where(n < MAX_EXACT, n, large)
            use = from_prev if side == 0 else jnp.logical_not(from_prev)
            for hd in range(N_HEADS):
                acc = jnp.zeros((BLOCK, BLOCK), jnp.float32)
                for bk in range(N_BUCKETS):
                    acc = jnp.where(bucket == bk, relb_ref[bk * N_HEADS + hd], acc)
                if side == 0 and i == 0:
                    acc = jnp.where(no_prev, NEG, acc)
                slot = i * N_HEADS + hd
                biasm[slot] = jnp.where(use, acc, biasm[slot])
        key[i] = rels[i]
        flag[i] = no_prev.astype(jnp.int32) if i == 0 else 0

    for i in range(nblk):
        @pl.when(refresh)
        def _():
            stale = (jnp.max(changed[i]) != 0) | (flag[i] != 0)
            if i == 0:
                stale = stale | (j == 0)
            pl.when(stale)(functools.partial(refresh_bias, i))

    x = x_ref[...]
    sh1, sc1, g1 = mod_ref[0:1, :], mod_ref[1:2, :], mod_ref[2:3, :]
    sh2, sc2 = mod_ref[3:4, :], mod_ref[4:5, :]

    xn = x * lax.rsqrt(jnp.mean(x * x, axis=-1, keepdims=True) + EPS)
    h = (xn * (n1g_ref[...] * (1.0 + sc1)) + sh1).astype(bf16)

    lane = lax.broadcasted_iota(jnp.int32, (1, LANES), 1)
    lo = lane < HEAD_DIM
    q = _dot(h, win_ref[:, Q0:K0]) * (HEAD_DIM ** -0.5)
    lo4 = jnp.concatenate([lo] * (ATTN_Q // LANES), axis=1)
    q_lo = jnp.where(lo4, q, 0.0).astype(bf16)
    q_hi = jnp.where(lo4, 0.0, q).astype(bf16)
    kv = _dot(h, win_ref[:, K0:GU0])
    kbuf[BLOCK:, :] = kv[:, 0:LANES]
    vbuf[BLOCK:, :] = kv[:, LANES:]
    kf = kbuf[...]
    vf = vbuf[...]
    kr = pltpu.roll(kf, HEAD_DIM, 1)
    vr = pltpu.roll(vf, HEAD_DIM, 1)
    kd = (jnp.where(lo, kf, kr).astype(bf16), jnp.where(lo, kr, kf).astype(bf16))
    v_lo = (jnp.where(lo, vf, 0.0).astype(bf16), jnp.where(lo, vr, 0.0).astype(bf16))
    v_hi = (jnp.where(lo, 0.0, vr).astype(bf16), jnp.where(lo, 0.0, vf).astype(bf16))

    ones_blk = jnp.ones((2 * BLOCK, LANES), bf16)

    def attend(i):
        rows = slice(i * BLOCK, (i + 1) * BLOCK)
        band = slice(i * BLOCK, (i + 2) * BLOCK)
        grp = N_HEADS // N_KV_HEADS
        for kvh in range(N_KV_HEADS):
            heads = range(kvh * grp, (kvh + 1) * grp)
            qs = jnp.concatenate([(q_lo if hd % 2 == 0 else q_hi)[rows, hd // 2 * LANES:(hd // 2 + 1) * LANES]
                                  for hd in heads], axis=0)
            s2 = _dot_nt(qs, kd[kvh][band])
            for g, hd in enumerate(heads):
                sh = s2[g * BLOCK:(g + 1) * BLOCK]
                s = jnp.where(from_prev, sh[:, 0:BLOCK], sh[:, BLOCK:]) + biasm[i * N_HEADS + hd]
                sbuf[hd] = s
                m = jnp.maximum(jnp.max(s, axis=-1, keepdims=True), sinks_ref[hd])
                mbuf[hd] = jnp.broadcast_to(m, (BLOCK, BLOCK))
        for kvh in range(N_KV_HEADS):
            outs = {}
            for half, vv in enumerate((v_lo, v_hi)):
                hds = (kvh * grp + half, kvh * grp + half + 2)
                p2 = []
                for hd in hds:
                    p = jnp.exp(sbuf[hd] - mbuf[hd])
                    p2.append(jnp.concatenate([jnp.where(from_prev, p, 0.0), jnp.where(from_prev, 0.0, p)],
                                              axis=1).astype(bf16))
                r = _dot(jnp.concatenate(p2, axis=0), jnp.concatenate([vv[kvh][band], ones_blk], axis=1))
                for n, hd in enumerate(hds):
                    rh = r[n * BLOCK:(n + 1) * BLOCK]
                    den = rh[:, LANES:] + jnp.exp(sinks_ref[hd] - mbuf[hd])
                    outs[hd] = rh[:, 0:LANES] * (1.0 / den)
            for pr in (kvh * grp // 2, kvh * grp // 2 + 1):
                ya[rows, pr * LANES:(pr + 1) * LANES] = (outs[2 * pr] + outs[2 * pr + 1]).astype(bf16)

    u = _gelu_tanh(_dot(h, win_ref[:, GU0:GV0]))
    attend(0)
    vg = _gelu_tanh(_dot(h, win_ref[:, GV0:GA0]))
    mu = jnp.mean(vg, axis=-1, keepdims=True)
    vc = vg - mu
    var = jnp.mean(vc * vc, axis=-1, keepdims=True)
    vn = (vc * lax.rsqrt(var + EPS) * lng_ref[...] + lnb_ref[...]).astype(bf16)
    attend(1)
    gate_a = _sigmoid(_dot(h, win_ref[:, GA0:GB0]))
    attend(2)
    ti = lax.broadcasted_iota(jnp.int32, (GM_CHUNK, GM_CHUNK), 0)
    si = lax.broadcasted_iota(jnp.int32, (GM_CHUNK, GM_CHUNK), 1)
    tril = si <= ti
    for g in range(GM_GROUPS):
        wg = jnp.where(tril, ws_ref[g], 0.0).astype(bf16)
        cols = slice(g * LANES, (g + 1) * LANES)
        for cidx in range(TM // GM_CHUNK):
            rows = slice(cidx * GM_CHUNK, (cidx + 1) * GM_CHUNK)
            sv = _dot(wg, vn[rows, cols]) + bsb_ref[g]
            yb[rows, cols] = (u[rows, cols] * sv).astype(bf16)
    gate_b = _sigmoid(_dot(h, win_ref[:, GB0:IN_END]))
    attend(3)

    kbuf[0:BLOCK, :] = kv[TM - BLOCK:, 0:LANES]
    vbuf[0:BLOCK, :] = kv[TM - BLOCK:, LANES:]
    pbuf[:, 0:BLOCK] = pos_ref[:, TM - BLOCK:]

    h2_parts, logit_parts = [], []
    for grp in range(TAIL_SPLIT):
        tok = slice(grp * TM // TAIL_SPLIT, (grp + 1) * TM // TAIL_SPLIT)
        merged = gate_a[tok] * _dot(ya[tok, :], pa_ref[...]) + gate_b[tok] * _dot(yb[tok, :], pb_ref[...])
        x1 = x[tok] + g1 * _dot(merged.astype(bf16), wo_ref[...])
        x1_ref[tok, :] = x1

        xn2 = x1 * lax.rsqrt(jnp.mean(x1 * x1, axis=-1, keepdims=True) + EPS)
        h2_parts.append((xn2 * (n2g_ref[...] * (1.0 + sc2)) + sh2).astype(bf16))
        logit_parts.append(_dot(h2_parts[-1], wr_ref[...]) + br_ref[...])
    h2 = jnp.concatenate(h2_parts, axis=0)
    logits = jnp.concatenate(logit_parts, axis=0)
    lt = jnp.transpose(logits)[0:ROUTER_ROWS, :]
    row = lax.broadcasted_iota(jnp.int32, (ROUTER_ROWS, TM), 0)
    row_f = row.astype(jnp.float32)
    big = float(ROUTER_ROWS)
    is_grp = row < N_GROUPS
    lg = jnp.where(is_grp, lt, NEG)
    lg_max = jnp.max(lg, axis=0, keepdims=True)
    g_idx = jnp.min(jnp.where(lg == lg_max, row_f, big), axis=0, keepdims=True)
    p_g = 1.0 / jnp.sum(jnp.where(is_grp, jnp.exp(lg - lg_max), 0.0), axis=0, keepdims=True)
    row_grp = jnp.floor((row_f - EXPERT_LANE0) * (1.0 / EXPERTS_PER_GROUP))
    in_grp = (row >= EXPERT_LANE0) & (row < EXPERT_LANE0 + N_EXPERTS) & (row_grp == g_idx)
    le = jnp.where(in_grp, lt, NEG)
    m1 = jnp.max(le, axis=0, keepdims=True)
    i1 = jnp.min(jnp.where(le == m1, row_f, big), axis=0, keepdims=True)
    oh1 = row_f == i1
    le2 = jnp.where(oh1, NEG, le)
    m2 = jnp.max(le2, axis=0, keepdims=True)
    i2 = jnp.min(jnp.where(le2 == m2, row_f, big), axis=0, keepdims=True)
    oh2 = row_f == i2
    e2 = jnp.exp(m2 - m1)
    w1 = p_g / (1.0 + e2)
    w2 = p_g * e2 / (1.0 + e2)

    oh = jnp.where(oh1, 1.0, jnp.where(oh2, 1.0, 0.0))
    n_e = jnp.sum(oh, axis=1, keepdims=True)
    cnt_ref[...] = n_e
    padded = jnp.floor((n_e + (CHUNK - 1)) * (1.0 / CHUNK)) * CHUNK
    dst_row = lax.broadcasted_iota(jnp.int32, (ROUTER_ROWS, ROUTER_ROWS), 0)
    src_row = lax.broadcasted_iota(jnp.int32, (ROUTER_ROWS, ROUTER_ROWS), 1)
    lower = jnp.where(src_row < dst_row, 1.0, 0.0).astype(bf16)
    run0 = _dot(lower, jnp.broadcast_to(padded, (ROUTER_ROWS, TM)).astype(bf16))
    slot = _dot(oh.astype(bf16), strict[...]) + run0
    pos1 = jnp.sum(jnp.where(oh1, slot, 0.0), axis=0, keepdims=True)
    pos2 = jnp.sum(jnp.where(oh2, slot, 0.0), axis=0, keepdims=True)
    rrow = lax.broadcasted_iota(jnp.int32, (ROUTE_ROWS, TM), 0)
    rw_ref[...] = jnp.where(rrow == 0, w1, jnp.where(rrow == 1, w2,
                            jnp.where(rrow == 2, pos1, jnp.where(rrow == 3, pos2, 0.0))))

    p1 = pos1.astype(jnp.int32)
    p2 = pos2.astype(jnp.int32)
    sp = lax.broadcasted_iota(jnp.int32, (SORT_ROWS, TM), 0)
    perm = jnp.where(sp == p1, 1.0, jnp.where(sp == p2, 1.0, 0.0)).astype(bf16)
    xs_ref[...] = _pack_bf16_pairs(_dot(perm, h2))


def _mixer(x, mod, positions, rel_bias, n1g, w_in, sinks, lng, lnb, w_s, bsb, p_a, p_b, w_o, n2g, w_r, b_r):
    bsz, seq, d = x.shape
    nj = seq // TM
    const = lambda *shape: pl.BlockSpec(shape, lambda b, j: (0,) * len(shape), pipeline_mode=pl.Buffered(1))
    smem = pl.BlockSpec(memory_space=pltpu.SMEM)
    tile = lambda w: pl.BlockSpec((None, TM, w), lambda b, j: (b, j, 0))
    return pl.pallas_call(
        _mixer_kernel,
        grid=(bsz, nj),
        in_specs=[smem, smem, smem,
                  tile(d),
                  pl.BlockSpec((None, 6, d), lambda b, j: (b, 0, 0)),
                  pl.BlockSpec((None, None, 1, TM), lambda b, j: (b, j, 0, 0)),
                  const(1, d), const(d, IN_END), const(1, GM_WIDTH), const(1, GM_WIDTH),
                  const(GM_GROUPS, GM_CHUNK, GM_CHUNK), const(GM_GROUPS, GM_CHUNK, LANES),
                  const(ATTN_Q, d), const(GM_WIDTH, d), const(d, d), const(1, d),
                  const(d, ROUTER_LANES), const(1, ROUTER_LANES)],
        out_specs=[tile(d),
                   pl.BlockSpec((SORT_ROWS, d // 2), lambda b, j: (b * nj + j, 0)),
                   pl.BlockSpec((ROUTE_ROWS, TM), lambda b, j: (0, b * nj + j)),
                   pl.BlockSpec((None, ROUTER_ROWS, 1), lambda b, j: (b * nj + j, 0, 0))],
        out_shape=[jax.ShapeDtypeStruct((bsz, seq, d), jnp.float32),
                   jax.ShapeDtypeStruct((bsz * nj * SORT_ROWS, d // 2), jnp.uint32),
                   jax.ShapeDtypeStruct((ROUTE_ROWS, bsz * seq), jnp.float32),
                   jax.ShapeDtypeStruct((bsz * nj, ROUTER_ROWS, 1), jnp.float32)],
        scratch_shapes=[pltpu.VMEM((TM + BLOCK, LANES), jnp.float32),
                        pltpu.VMEM((TM + BLOCK, LANES), jnp.float32),
                        pltpu.VMEM((1, TM + BLOCK), jnp.int32),
                        pltpu.VMEM((TM // BLOCK * N_HEADS, BLOCK, BLOCK), jnp.float32),
                        pltpu.VMEM((TM // BLOCK, 1, 2 * BLOCK), jnp.int32),
                        pltpu.VMEM((TM, ATTN_Q), jnp.bfloat16),
                        pltpu.VMEM((TM, GM_WIDTH), jnp.bfloat16),
                        pltpu.VMEM((TM, TM), jnp.bfloat16),
                        pltpu.VMEM((N_HEADS, BLOCK, BLOCK), jnp.float32),
                        pltpu.VMEM((N_HEADS, BLOCK, BLOCK), jnp.float32),
                        pltpu.SMEM((TM // BLOCK,), jnp.int32)],
        compiler_params=pltpu.CompilerParams(dimension_semantics=("arbitrary", "arbitrary"),
                                             vmem_limit_bytes=MIXER_VMEM),
        name="mixer",
    )(rel_bias.reshape(-1), sinks, positions.reshape(-1), x, mod, positions.reshape(bsz, nj, 1, TM), n1g, w_in, lng, lnb,
      w_s, bsb, p_a, p_b, w_o, n2g, w_r, b_r)


def _gather_chunks(idx_ref, idx0, n, src_hbm, dst, sem, inline=False, split_priority=False):
    def issue(c, carry, priority=0):
        src = pl.multiple_of(idx_ref[idx0 + c] * CHUNK, CHUNK)
        pltpu.make_async_copy(src_hbm.at[pl.ds(src, CHUNK)],
                              dst.at[pl.ds(pl.multiple_of(c * CHUNK, CHUNK), CHUNK)], sem).start(priority)
        return carry

    if inline:
        for c in range(n):
            issue(c, 0, c % 2 if split_priority else 0)
    else:
        lax.fori_loop(0, n, issue, 0, unroll=GATHER_UNROLL)


def _wait_chunks(n, src_hbm, dst, sem):
    pltpu.make_async_copy(src_hbm.at[pl.ds(0, n * CHUNK)], dst, sem).wait()


def _expert_kernel(src_ref, be_ref, ne_ref, ws_ref, nq_ref, lb_ref, xs_hbm, wg_hbm, wu_hbm, wd_hbm, yr_ref,
                   xbuf, wgf, wuf, wdf, wgb, wub, wdb, sem, wsem):
    i = pl.program_id(0)
    last = lb_ref[0]
    slot = i % 2

    def weight_copies(e, w):
        return [pltpu.make_async_copy(hbm.at[e], buf.at[w], wsem.at[w])
                for hbm, buf in ((wg_hbm, wgf), (wu_hbm, wuf), (wd_hbm, wdf))]

    @pl.when(i == 0)
    def _():
        _gather_chunks(src_ref, 0, BLOCK_CHUNKS, xs_hbm, xbuf.at[0], sem.at[0])
        for cp in weight_copies(be_ref[0], 0):
            cp.start()

    @pl.when(i <= last)
    def _():
        changed = (i == 0) | (be_ref[i] != be_ref[jnp.maximum(i - 1, 0)])

        @pl.when(changed)
        def _():
            w = ws_ref[i]
            for cp in weight_copies(be_ref[i], w):
                cp.wait()
            wgb[...] = wgf[w].astype(jnp.bfloat16)
            wub[...] = wuf[w].astype(jnp.bfloat16)
            wdb[...] = wdf[w].astype(jnp.bfloat16)

            @pl.when(ne_ref[i] >= 0)
            def _():
                for cp in weight_copies(ne_ref[i], 1 - w):
                    cp.start(priority=1)

        nxt = jnp.minimum(i + 1, last)
        _gather_chunks(src_ref, nxt * BLOCK_CHUNKS, BLOCK_CHUNKS, xs_hbm, xbuf.at[1 - slot], sem.at[1 - slot],
                       inline=True)
        _wait_chunks(BLOCK_CHUNKS, xs_hbm, xbuf.at[slot], sem.at[slot])

        def compute(rows):
            xb = _unpack_bf16_pairs(xbuf[slot, 0:rows, :])
            hg = _dot(xb, wgb[...])
            hu = _dot(xb, wub[...])
            hid = (hg * _sigmoid(hg) * hu).astype(jnp.bfloat16)
            yr_ref[0:rows, :] = _pack_bf16_pairs(_dot(hid, wdb[...]).astype(jnp.bfloat16).astype(jnp.float32))
            if rows < BM:
                yr_ref[rows:, :] = jnp.zeros((BM - rows, yr_ref.shape[1]), yr_ref.dtype)

        for quarters in range(1, BLOCK_QUARTERS + 1):
            pl.when(nq_ref[i] == quarters)(functools.partial(compute, quarters * BM // BLOCK_QUARTERS))

        @pl.when(i == last)
        def _():
            _wait_chunks(BLOCK_CHUNKS, xs_hbm, xbuf.at[1 - slot], sem.at[1 - slot])

    @pl.when(i > last)
    def _():
        yr_ref[...] = jnp.zeros_like(yr_ref)


def _experts(src_chunk, block_e, next_e, w_slot, quarters, last_blk, xs, w_gate, w_up, w_down):
    d = w_gate.shape[1]
    n_rows = src_chunk.shape[0] * CHUNK
    hbm = pl.BlockSpec(memory_space=pl.ANY)
    return pl.pallas_call(
        _expert_kernel,
        grid_spec=pltpu.PrefetchScalarGridSpec(
            num_scalar_prefetch=6,
            grid=(n_rows // BM,),
            in_specs=[hbm, hbm, hbm, hbm],
            out_specs=pl.BlockSpec((BM, d // 2), lambda i, *_: (i, 0)),
            scratch_shapes=[pltpu.VMEM((2, BM, d // 2), jnp.uint32),
                            pltpu.VMEM((2, d, D_EXPERT), jnp.float32),
                            pltpu.VMEM((2, d, D_EXPERT), jnp.float32),
                            pltpu.VMEM((2, D_EXPERT, d), jnp.float32),
                            pltpu.VMEM((d, D_EXPERT), jnp.bfloat16),
                            pltpu.VMEM((d, D_EXPERT), jnp.bfloat16),
                            pltpu.VMEM((D_EXPERT, d), jnp.bfloat16),
                            pltpu.SemaphoreType.DMA((2,)),
                            pltpu.SemaphoreType.DMA((2,))]),
        out_shape=jax.ShapeDtypeStruct((n_rows, d // 2), jnp.uint32),
        compiler_params=pltpu.CompilerParams(dimension_semantics=("arbitrary",),
                                             vmem_limit_bytes=EXPERTS_VMEM),
        name="experts",
    )(src_chunk, block_e, next_e, w_slot, quarters, last_blk, xs, w_gate, w_up, w_down)


def _combine_kernel(src_ref, x1_ref, rw_ref, mod_ref, fg_ref, yr_hbm, out_ref, ybuf, sem, *, final_norm):
    i = pl.program_id(0)
    slot = i % 2

    @pl.when(i == 0)
    def _():
        _gather_chunks(src_ref, 0, SORT_CHUNKS, yr_hbm, ybuf.at[0], sem.at[0])

    rows = jnp.concatenate([rw_ref[...], jnp.zeros((LANES - ROUTE_ROWS, TM), jnp.float32)], axis=0)
    rw = jnp.transpose(rows)
    sp = lax.broadcasted_iota(jnp.int32, (1, SORT_ROWS), 1)
    p1 = rw[:, 2:3].astype(jnp.int32)
    p2 = rw[:, 3:4].astype(jnp.int32)
    wmat = jnp.where(sp == p1, rw[:, 0:1], jnp.where(sp == p2, rw[:, 1:2], 0.0)).astype(jnp.bfloat16)
    nxt = jnp.minimum(i + 1, pl.num_programs(0) - 1)
    _gather_chunks(src_ref, nxt * SORT_CHUNKS, SORT_CHUNKS, yr_hbm, ybuf.at[1 - slot], sem.at[1 - slot],
                   inline=True, split_priority=True)
    _wait_chunks(SORT_CHUNKS, yr_hbm, ybuf.at[slot], sem.at[slot])
    moe = _dot(wmat, _unpack_bf16_pairs(ybuf[slot]))
    x2 = x1_ref[...] + mod_ref[5:6, :] * moe
    if final_norm:
        x2 = x2 * lax.rsqrt(jnp.mean(x2 * x2, axis=-1, keepdims=True) + EPS) * fg_ref[...]
    out_ref[...] = x2

    @pl.when(i == pl.num_programs(0) - 1)
    def _():
        _wait_chunks(SORT_CHUNKS, yr_hbm, ybuf.at[1 - slot], sem.at[1 - slot])


def _combine(src_chunk, x1, rw, mod, final_g, yr, seq, final_norm):
    t, d = x1.shape
    per_seq = seq // TM
    return pl.pallas_call(
        functools.partial(_combine_kernel, final_norm=final_norm),
        grid_spec=pltpu.PrefetchScalarGridSpec(
            num_scalar_prefetch=1,
            grid=(t // TM,),
            in_specs=[pl.BlockSpec((TM, d), lambda i, src: (i, 0)),
                      pl.BlockSpec((ROUTE_ROWS, TM), lambda i, src: (0, i)),
                      pl.BlockSpec((None, 6, d), lambda i, src: (i // per_seq, 0, 0)),
                      pl.BlockSpec((1, d), lambda i, src: (0, 0)),
                      pl.BlockSpec(memory_space=pl.ANY)],
            out_specs=pl.BlockSpec((TM, d), lambda i, src: (i, 0)),
            scratch_shapes=[pltpu.VMEM((2, SORT_ROWS, d // 2), jnp.uint32),
                            pltpu.SemaphoreType.DMA((2,))]),
        out_shape=jax.ShapeDtypeStruct((t, d), jnp.float32),
        compiler_params=pltpu.CompilerParams(dimension_semantics=("arbitrary",),
                                             vmem_limit_bytes=COMBINE_VMEM),
        name="combine",
    )(src_chunk, x1, rw, mod, final_g.reshape(1, d), yr)


def kernel(x, c, positions, rel_bias, w_ada, b_ada, norm1_g, w_in, sinks, gm_ln_g, gm_ln_b, gm_w_s, gm_b_s,
           p_a, p_b, w_o, norm2_g, w_router_g, b_router_g, w_router_e, b_router_e, w_gate, w_up, w_down,
           final_g):
    bsz, seq, d = x.shape
    t = bsz * seq
    bf16 = jnp.bfloat16
    depth = w_ada.shape[0]
    n_tiles = t // TM
    n_chunks = t * TOP_K // CHUNK + n_tiles * N_EXPERTS + N_EXPERTS * BLOCK_CHUNKS
    i32 = jnp.int32
    for l in range(depth):
        mod = _adaln_mod(c, w_ada[l], b_ada[l]).reshape(bsz, 6, d)
        pad = ROUTER_LANES - N_GROUPS - N_EXPERTS
        w_r = jnp.concatenate([w_router_g[l], w_router_e[l], jnp.zeros((d, pad), jnp.float32)], axis=1)
        b_r = jnp.concatenate([b_router_g[l], b_router_e[l], jnp.zeros((pad,), jnp.float32)]).reshape(1, -1)
        bsb = jnp.broadcast_to(gm_b_s[l][:, :, None], (GM_GROUPS, GM_CHUNK, LANES))
        x1, xs, rw, cnt = _mixer(
            x, mod, positions, rel_bias, norm1_g[l].reshape(1, d), w_in[l].astype(bf16), sinks[l],
            gm_ln_g[l].reshape(1, -1), gm_ln_b[l].reshape(1, -1), gm_w_s[l], bsb,
            p_a[l].astype(bf16), p_b[l].astype(bf16), w_o[l].astype(bf16), norm2_g[l].reshape(1, d),
            w_r.astype(bf16), b_r)

        n = cnt[:, EXPERT_LANE0:EXPERT_LANE0 + N_EXPERTS, 0].astype(i32)
        nch = (n + CHUNK - 1) // CHUNK
        run0_tile = jnp.cumsum(nch, axis=1) - nch
        run0_exp = jnp.cumsum(nch, axis=0) - nch
        tot = jnp.sum(nch, axis=0)
        seg = (tot + BLOCK_CHUNKS - 1) // BLOCK_CHUNKS * BLOCK_CHUNKS
        seg_end = jnp.cumsum(seg)
        seg0 = seg_end - seg

        pos_tile = jnp.arange(n_tiles, dtype=i32)[:, None] * SORT_CHUNKS + run0_tile
        pos_exp = seg0[None, :] + run0_exp

        j = jnp.arange(n_chunks, dtype=i32)[:, None]
        e_j = jnp.minimum(jnp.sum(j >= seg_end[None, :], axis=1), N_EXPERTS - 1)
        tables = jnp.concatenate([pos_exp.T, (pos_exp + nch).T, (pos_tile - pos_exp).T], axis=1)
        picked = jnp.dot(jax.nn.one_hot(e_j, N_EXPERTS, dtype=jnp.float32), tables.astype(jnp.float32),
                         precision=lax.Precision.HIGHEST).astype(i32)
        lo_j, hi_j, shift_j = picked[:, :n_tiles], picked[:, n_tiles:2 * n_tiles], picked[:, 2 * n_tiles:]
        inside = (j >= lo_j) & (j < hi_j)
        src_j = jnp.sum(jnp.where(inside, j + shift_j, 0), axis=1).astype(i32)
        blk0 = jnp.arange(n_chunks // BLOCK_CHUNKS, dtype=i32) * BLOCK_CHUNKS
        block_e = jnp.minimum(jnp.sum(blk0[:, None] >= seg_end[None, :], axis=1), N_EXPERTS - 1).astype(i32)
        last_blk = (jnp.maximum(seg_end[-1] // BLOCK_CHUNKS, 1) - 1).astype(i32).reshape(1)
        experts = jnp.arange(N_EXPERTS, dtype=i32)
        later = (experts[None, :] > block_e[:, None]) & (seg[None, :] > 0)
        next_e = jnp.min(jnp.where(later, experts[None, :], N_EXPERTS), axis=1)
        next_e = jnp.where(next_e < N_EXPERTS, next_e, -1).astype(i32)
        w_slot = (jnp.sum((experts[None, :] < block_e[:, None]) & (seg[None, :] > 0), axis=1) % 2).astype(i32)

        q = jnp.arange(SORT_CHUNKS, dtype=i32)[None, :, None]
        in_run = (q >= run0_tile[:, None, :]) & (q < (run0_tile + nch)[:, None, :])
        dst_q = jnp.sum(jnp.where(in_run, q + (pos_exp - run0_tile)[:, None, :], 0), axis=2).astype(i32).reshape(-1)

        own = block_e[:, None] == experts[None, :]
        filled = jnp.sum(jnp.where(own, (seg0 + tot)[None, :], 0), axis=1) - blk0
        quarter_chunks = BLOCK_CHUNKS // BLOCK_QUARTERS
        quarters = jnp.clip((filled + quarter_chunks - 1) // quarter_chunks, 1, BLOCK_QUARTERS).astype(i32)

        yr = _experts(src_j, block_e, next_e, w_slot, quarters, last_blk, xs, w_gate[l], w_up[l], w_down[l])
        x = _combine(dst_q, x1.reshape(t, d), rw, mod, final_g, yr, seq,
                     final_norm=(l == depth - 1)).reshape(bsz, seq, d)
    return x
```

```python
import functools
import math

import jax
import jax.numpy as jnp
from jax import lax
from jax.experimental import pallas as pl
from jax.experimental.pallas import tpu as pltpu

D_MODEL = 1024
N_HEADS = 8
N_KV_HEADS = 2
HEAD_DIM = 64
BLOCK = 128
ATTN_Q = N_HEADS * HEAD_DIM
ATTN_KV = N_KV_HEADS * HEAD_DIM
N_BUCKETS = 32
MAX_EXACT = N_BUCKETS // 2
MAX_DISTANCE = 128
GM_WIDTH = 512
GM_GROUPS = 4
GM_CHUNK = 128
N_GROUPS = 4
EXPERTS_PER_GROUP = 8
N_EXPERTS = N_GROUPS * EXPERTS_PER_GROUP
TOP_K = 2
D_EXPERT = 512
EPS = 1e-6
NEG = -1e30

LANES = 128
ROUTER_LANES = LANES
EXPERT_LANE0 = N_GROUPS
TM = 512
BM = 1024
ROUTER_ROWS = 48
ROUTE_ROWS = 8
CHUNK = 8
SORT_ROWS = TM * TOP_K + N_EXPERTS * CHUNK
SORT_CHUNKS = SORT_ROWS // CHUNK
BLOCK_CHUNKS = BM // CHUNK
BLOCK_QUARTERS = 8
GATHER_UNROLL = 8
TAIL_SPLIT = 2
MIB = 1024 * 1024
MIXER_VMEM = 44 * MIB
EXPERTS_VMEM = 40 * MIB
COMBINE_VMEM = 24 * MIB

ADALN_TN = 1024

Q0 = 0
K0 = Q0 + ATTN_Q
V0 = K0 + ATTN_KV
GU0 = V0 + ATTN_KV
GV0 = GU0 + GM_WIDTH
GA0 = GV0 + GM_WIDTH
GB0 = GA0 + D_MODEL
IN_END = GB0 + D_MODEL


def _dot(a, b):
    return jnp.dot(a, b, preferred_element_type=jnp.float32)


def _dot_nt(a, b):
    return lax.dot_general(a, b, (((1,), (1,)), ((), ())), preferred_element_type=jnp.float32)


LOG2E = math.log2(math.e)


def _gelu_tanh(x):
    c = math.sqrt(2.0 / math.pi)
    k0, k1 = -2.0 * c * LOG2E, -2.0 * c * 0.044715 * LOG2E
    return x * (1.0 / (1.0 + jnp.exp2(x * (x * x * k1 + k0))))


def _sigmoid(x):
    return 1.0 / (1.0 + jnp.exp2(x * -LOG2E))


def _pack_bf16_pairs(x):
    bits = lax.bitcast_convert_type(x, jnp.uint32)
    half = x.shape[1] // 2
    return (bits[:, 0:half] >> 16) | (bits[:, half:] & jnp.uint32(0xFFFF0000))


def _unpack_bf16_pairs(w):
    lo = lax.bitcast_convert_type(w << 16, jnp.float32)
    hi = lax.bitcast_convert_type(w & jnp.uint32(0xFFFF0000), jnp.float32)
    return jnp.concatenate([lo, hi], axis=1).astype(jnp.bfloat16)


def _adaln_kernel(c_ref, w_ref, b_ref, o_ref):
    c = c_ref[...]
    cs = c * _sigmoid(c)
    o_ref[...] = _dot(cs, w_ref[...]) + b_ref[...]


def _adaln_mod(c, w, b):
    bsz, d = c.shape
    n = w.shape[1]
    tn = ADALN_TN
    return pl.pallas_call(
        _adaln_kernel,
        grid=(n // tn,),
        in_specs=[pl.BlockSpec((bsz, d), lambda i: (0, 0)),
                  pl.BlockSpec((d, tn), lambda i: (0, i)),
                  pl.BlockSpec((1, tn), lambda i: (0, i))],
        out_specs=pl.BlockSpec((bsz, tn), lambda i: (0, i)),
        out_shape=jax.ShapeDtypeStruct((bsz, n), jnp.float32),
        compiler_params=pltpu.CompilerParams(dimension_semantics=("arbitrary",)),
        name="adaln_mod",
    )(c, w, b.reshape(1, n))


def _mixer_kernel(relb_ref, sinks_ref, spos_ref,
                  x_ref, mod_ref, pos_ref, n1g_ref, win_ref, lng_ref, lnb_ref, ws_ref, bsb_ref,
                  pa_ref, pb_ref, wo_ref, n2g_ref, wr_ref, br_ref,
                  x1_ref, xs_ref, rw_ref, cnt_ref,
                  kbuf, vbuf, pbuf, biasm, key, ya, yb, strict, sbuf, mbuf, flag):
    b = pl.program_id(0)
    j = pl.program_id(1)
    nblk = TM // BLOCK
    bf16 = jnp.bfloat16

    @pl.when((b == 0) & (j == 0))
    def _():
        key[...] = jnp.zeros_like(key)
        for i in range(nblk):
            flag[i] = 1
        tr = lax.broadcasted_iota(jnp.int32, (TM, TM), 0)
        tc = lax.broadcasted_iota(jnp.int32, (TM, TM), 1)
        strict[...] = jnp.where(tr < tc, 1.0, 0.0).astype(bf16)

    @pl.when(j == 0)
    def _():
        kbuf[0:BLOCK, :] = jnp.zeros((BLOCK, LANES), jnp.float32)
        vbuf[0:BLOCK, :] = jnp.zeros((BLOCK, LANES), jnp.float32)
        pbuf[:, 0:BLOCK] = jnp.zeros((1, BLOCK), jnp.int32)

    pbuf[:, BLOCK:] = pos_ref[...]
    qi = lax.broadcasted_iota(jnp.int32, (BLOCK, BLOCK), 0)
    kc = lax.broadcasted_iota(jnp.int32, (BLOCK, BLOCK), 1)
    from_prev = kc > qi
    rels, changed = [], []
    tile0 = (b * pl.num_programs(1) + j) * TM
    for i in range(nblk):
        pk = pbuf[:, i * BLOCK:(i + 2) * BLOCK]
        rels.append(pk - spos_ref[tile0 + i * BLOCK])
        changed.append(jnp.where(rels[i] != key[i], 1.0, 0.0))
    any_changed = jnp.max(functools.reduce(jnp.maximum, changed))
    any_flag = functools.reduce(jnp.maximum, [flag[i] for i in range(nblk)])
    refresh = (any_changed != 0) | (any_flag != 0) | (j == 0)

    def refresh_bias(i):
        pk = pbuf[:, i * BLOCK:(i + 2) * BLOCK]
        pq = pk[:, BLOCK:]
        pq_col = jnp.transpose(jnp.broadcast_to(pq, (BLOCK, BLOCK)))
        no_prev = (j == 0) if i == 0 else None
        for hd in range(N_HEADS):
            biasm[i * N_HEADS + hd] = jnp.zeros((BLOCK, BLOCK), jnp.float32)
        for side in range(2):
            dist = pq_col - pk[:, side * BLOCK:(side + 1) * BLOCK]
            n = jnp.maximum(dist, 0)
            nf = jnp.maximum(n, 1).astype(jnp.float32)
            large = MAX_EXACT + (jnp.log(nf / MAX_EXACT) / math.log(MAX_DISTANCE / MAX_EXACT)
                                 * (N_BUCKETS - MAX_EXACT)).astype(jnp.int32)
            large = jnp.minimum(large, N_BUCKETS - 1)
            bucket = jnp.where(n < MAX_EXACT, n, large)
            use = from_prev if side == 0 else jnp.logical_not(from_prev)
            for hd in range(N_HEADS):
                acc = jnp.zeros((BLOCK, BLOCK), jnp.float32)
                for bk in range(N_BUCKETS):
                    acc = jnp.where(bucket == bk, relb_ref[bk * N_HEADS + hd], acc)
                if side == 0 and i == 0:
                    acc = jnp.where(no_prev, NEG, acc)
                slot = i * N_HEADS + hd
                biasm[slot] = jnp.where(use, acc, biasm[slot])
        key[i] = rels[i]
        flag[i] = no_prev.astype(jnp.int32) if i == 0 else 0

    for i in range(nblk):
        @pl.when(refresh)
        def _():
            stale = (jnp.max(changed[i]) != 0) | (flag[i] != 0)
            if i == 0:
                stale = stale | (j == 0)
            pl.when(stale)(functools.partial(refresh_bias, i))

    x = x_ref[...]
    sh1, sc1, g1 = mod_ref[0:1, :], mod_ref[1:2, :], mod_ref[2:3, :]
    sh2, sc2 = mod_ref[3:4, :], mod_ref[4:5, :]

    xn = x * lax.rsqrt(jnp.mean(x * x, axis=-1, keepdims=True) + EPS)
    h = (xn * (n1g_ref[...] * (1.0 + sc1)) + sh1).astype(bf16)

    lane = lax.broadcasted_iota(jnp.int32, (1, LANES), 1)
    lo = lane < HEAD_DIM
    q = _dot(h, win_ref[:, Q0:K0]) * (HEAD_DIM ** -0.5)
    lo4 = jnp.concatenate([lo] * (ATTN_Q // LANES), axis=1)
    q_lo = jnp.where(lo4, q, 0.0).astype(bf16)
    q_hi = jnp.where(lo4, 0.0, q).astype(bf16)
    kv = _dot(h, win_ref[:, K0:GU0])
    kbuf[BLOCK:, :] = kv[:, 0:LANES]
    vbuf[BLOCK:, :] = kv[:, LANES:]
    kf = kbuf[...]
    vf = vbuf[...]
    kr = pltpu.roll(kf, HEAD_DIM, 1)
    vr = pltpu.roll(vf, HEAD_DIM, 1)
    kd = (jnp.where(lo, kf, kr).astype(bf16), jnp.where(lo, kr, kf).astype(bf16))
    v_lo = (jnp.where(lo, vf, 0.0).astype(bf16), jnp.where(lo, vr, 0.0).astype(bf16))
    v_hi = (jnp.where(lo, 0.0, vr).astype(bf16), jnp.where(lo, 0.0, vf).astype(bf16))

    ones_blk = jnp.ones((2 * BLOCK, LANES), bf16)

    def attend(i):
        rows = slice(i * BLOCK, (i + 1) * BLOCK)
        band = slice(i * BLOCK, (i + 2) * BLOCK)
        grp = N_HEADS // N_KV_HEADS
        for kvh in range(N_KV_HEADS):
            heads = range(kvh * grp, (kvh + 1) * grp)
            qs = jnp.concatenate([(q_lo if hd % 2 == 0 else q_hi)[rows, hd // 2 * LANES:(hd // 2 + 1) * LANES]
                                  for hd in heads], axis=0)
            s2 = _dot_nt(qs, kd[kvh][band])
            for g, hd in enumerate(heads):
                sh = s2[g * BLOCK:(g + 1) * BLOCK]
                s = jnp.where(from_prev, sh[:, 0:BLOCK], sh[:, BLOCK:]) + biasm[i * N_HEADS + hd]
                sbuf[hd] = s
                m = jnp.maximum(jnp.max(s, axis=-1, keepdims=True), sinks_ref[hd])
                mbuf[hd] = jnp.broadcast_to(m, (BLOCK, BLOCK))
        for kvh in range(N_KV_HEADS):
            outs = {}
            for half, vv in enumerate((v_lo, v_hi)):
                hds = (kvh * grp + half, kvh * grp + half + 2)
                p2 = []
                for hd in hds:
                    p = jnp.exp(sbuf[hd] - mbuf[hd])
                    p2.append(jnp.concatenate([jnp.where(from_prev, p, 0.0), jnp.where(from_prev, 0.0, p)],
                                              axis=1).astype(bf16))
                r = _dot(jnp.concatenate(p2, axis=0), jnp.concatenate([vv[kvh][band], ones_blk], axis=1))
                for n, hd in enumerate(hds):
                    rh = r[n * BLOCK:(n + 1) * BLOCK]
                    den = rh[:, LANES:] + jnp.exp(sinks_ref[hd] - mbuf[hd])
                    outs[hd] = rh[:, 0:LANES] * (1.0 / den)
            for pr in (kvh * grp // 2, kvh * grp // 2 + 1):
                ya[rows, pr * LANES:(pr + 1) * LANES] = (outs[2 * pr] + outs[2 * pr + 1]).astype(bf16)

    u = _gelu_tanh(_dot(h, win_ref[:, GU0:GV0]))
    attend(0)
    vg = _gelu_tanh(_dot(h, win_ref[:, GV0:GA0]))
    mu = jnp.mean(vg, axis=-1, keepdims=True)
    vc = vg - mu
    var = jnp.mean(vc * vc, axis=-1, keepdims=True)
    vn = (vc * lax.rsqrt(var + EPS) * lng_ref[...] + lnb_ref[...]).astype(bf16)
    attend(1)
    gate_a = _sigmoid(_dot(h, win_ref[:, GA0:GB0]))
    attend(2)
    ti = lax.broadcasted_iota(jnp.int32, (GM_CHUNK, GM_CHUNK), 0)
    si = lax.broadcasted_iota(jnp.int32, (GM_CHUNK, GM_CHUNK), 1)
    tril = si <= ti
    for g in range(GM_GROUPS):
        wg = jnp.where(tril, ws_ref[g], 0.0).astype(bf16)
        cols = slice(g * LANES, (g + 1) * LANES)
        for cidx in range(TM // GM_CHUNK):
            rows = slice(cidx * GM_CHUNK, (cidx + 1) * GM_CHUNK)
            sv = _dot(wg, vn[rows, cols]) + bsb_ref[g]
            yb[rows, cols] = (u[rows, cols] * sv).astype(bf16)
    gate_b = _sigmoid(_dot(h, win_ref[:, GB0:IN_END]))
    attend(3)

    kbuf[0:BLOCK, :] = kv[TM - BLOCK:, 0:LANES]
    vbuf[0:BLOCK, :] = kv[TM - BLOCK:, LANES:]
    pbuf[:, 0:BLOCK] = pos_ref[:, TM - BLOCK:]

    h2_parts, logit_parts = [], []
    for grp in range(TAIL_SPLIT):
        tok = slice(grp * TM // TAIL_SPLIT, (grp + 1) * TM // TAIL_SPLIT)
        merged = gate_a[tok] * _dot(ya[tok, :], pa_ref[...]) + gate_b[tok] * _dot(yb[tok, :], pb_ref[...])
        x1 = x[tok] + g1 * _dot(merged.astype(bf16), wo_ref[...])
        x1_ref[tok, :] = x1

        xn2 = x1 * lax.rsqrt(jnp.mean(x1 * x1, axis=-1, keepdims=True) + EPS)
        h2_parts.append((xn2 * (n2g_ref[...] * (1.0 + sc2)) + sh2).astype(bf16))
        logit_parts.append(_dot(h2_parts[-1], wr_ref[...]) + br_ref[...])
    h2 = jnp.concatenate(h2_parts, axis=0)
    logits = jnp.concatenate(logit_parts, axis=0)
    lt = jnp.transpose(logits)[0:ROUTER_ROWS, :]
    row = lax.broadcasted_iota(jnp.int32, (ROUTER_ROWS, TM), 0)
    row_f = row.astype(jnp.float32)
    big = float(ROUTER_ROWS)
    is_grp = row < N_GROUPS
    lg = jnp.where(is_grp, lt, NEG)
    lg_max = jnp.max(lg, axis=0, keepdims=True)
    g_idx = jnp.min(jnp.where(lg == lg_max, row_f, big), axis=0, keepdims=True)
    p_g = 1.0 / jnp.sum(jnp.where(is_grp, jnp.exp(lg - lg_max), 0.0), axis=0, keepdims=True)
    row_grp = jnp.floor((row_f - EXPERT_LANE0) * (1.0 / EXPERTS_PER_GROUP))
    in_grp = (row >= EXPERT_LANE0) & (row < EXPERT_LANE0 + N_EXPERTS) & (row_grp == g_idx)
    le = jnp.where(in_grp, lt, NEG)
    m1 = jnp.max(le, axis=0, keepdims=True)
    i1 = jnp.min(jnp.where(le == m1, row_f, big), axis=0, keepdims=True)
    oh1 = row_f == i1
    le2 = jnp.where(oh1, NEG, le)
    m2 = jnp.max(le2, axis=0, keepdims=True)
    i2 = jnp.min(jnp.where(le2 == m2, row_f, big), axis=0, keepdims=True)
    oh2 = row_f == i2
    e2 = jnp.exp(m2 - m1)
    w1 = p_g / (1.0 + e2)
    w2 = p_g * e2 / (1.0 + e2)

    oh = jnp.where(oh1, 1.0, jnp.where(oh2, 1.0, 0.0))
    n_e = jnp.sum(oh, axis=1, keepdims=True)
    cnt_ref[...] = n_e
    padded = jnp.floor((n_e + (CHUNK - 1)) * (1.0 / CHUNK)) * CHUNK
    dst_row = lax.broadcasted_iota(jnp.int32, (ROUTER_ROWS, ROUTER_ROWS), 0)
    src_row = lax.broadcasted_iota(jnp.int32, (ROUTER_ROWS, ROUTER_ROWS), 1)
    lower = jnp.where(src_row < dst_row, 1.0, 0.0).astype(bf16)
    run0 = _dot(lower, jnp.broadcast_to(padded, (ROUTER_ROWS, TM)).astype(bf16))
    slot = _dot(oh.astype(bf16), strict[...]) + run0
    pos1 = jnp.sum(jnp.where(oh1, slot, 0.0), axis=0, keepdims=True)
    pos2 = jnp.sum(jnp.where(oh2, slot, 0.0), axis=0, keepdims=True)
    rrow = lax.broadcasted_iota(jnp.int32, (ROUTE_ROWS, TM), 0)
    rw_ref[...] = jnp.where(rrow == 0, w1, jnp.where(rrow == 1, w2,
                            jnp.where(rrow == 2, pos1, jnp.where(rrow == 3, pos2, 0.0))))

    p1 = pos1.astype(jnp.int32)
    p2 = pos2.astype(jnp.int32)
    sp = lax.broadcasted_iota(jnp.int32, (SORT_ROWS, TM), 0)
    perm = jnp.where(sp == p1, 1.0, jnp.where(sp == p2, 1.0, 0.0)).astype(bf16)
    xs_ref[...] = _pack_bf16_pairs(_dot(perm, h2))


def _mixer(x, mod, positions, rel_bias, n1g, w_in, sinks, lng, lnb, w_s, bsb, p_a, p_b, w_o, n2g, w_r, b_r):
    bsz, seq, d = x.shape
    nj = seq // TM
    const = lambda *shape: pl.BlockSpec(shape, lambda b, j: (0,) * len(shape), pipeline_mode=pl.Buffered(1))
    smem = pl.BlockSpec(memory_space=pltpu.SMEM)
    tile = lambda w: pl.BlockSpec((None, TM, w), lambda b, j: (b, j, 0))
    return pl.pallas_call(
        _mixer_kernel,
        grid=(bsz, nj),
        in_specs=[smem, smem, smem,
                  tile(d),
                  pl.BlockSpec((None, 6, d), lambda b, j: (b, 0, 0)),
                  pl.BlockSpec((None, None, 1, TM), lambda b, j: (b, j, 0, 0)),
                  const(1, d), const(d, IN_END), const(1, GM_WIDTH), const(1, GM_WIDTH),
                  const(GM_GROUPS, GM_CHUNK, GM_CHUNK), const(GM_GROUPS, GM_CHUNK, LANES),
                  const(ATTN_Q, d), const(GM_WIDTH, d), const(d, d), const(1, d),
                  const(d, ROUTER_LANES), const(1, ROUTER_LANES)],
        out_specs=[tile(d),
                   pl.BlockSpec((SORT_ROWS, d // 2), lambda b, j: (b * nj + j, 0)),
                   pl.BlockSpec((ROUTE_ROWS, TM), lambda b, j: (0, b * nj + j)),
                   pl.BlockSpec((None, ROUTER_ROWS, 1), lambda b, j: (b * nj + j, 0, 0))],
        out_shape=[jax.ShapeDtypeStruct((bsz, seq, d), jnp.float32),
                   jax.ShapeDtypeStruct((bsz * nj * SORT_ROWS, d // 2), jnp.uint32),
                   jax.ShapeDtypeStruct((ROUTE_ROWS, bsz * seq), jnp.float32),
                   jax.ShapeDtypeStruct((bsz * nj, ROUTER_ROWS, 1), jnp.float32)],
        scratch_shapes=[pltpu.VMEM((TM + BLOCK, LANES), jnp.float32),
                        pltpu.VMEM((TM + BLOCK, LANES), jnp.float32),
                        pltpu.VMEM((1, TM + BLOCK), jnp.int32),
                        pltpu.VMEM((TM // BLOCK * N_HEADS, BLOCK, BLOCK), jnp.float32),
                        pltpu.VMEM((TM // BLOCK, 1, 2 * BLOCK), jnp.int32),
                        pltpu.VMEM((TM, ATTN_Q), jnp.bfloat16),
                        pltpu.VMEM((TM, GM_WIDTH), jnp.bfloat16),
                        pltpu.VMEM((TM, TM), jnp.bfloat16),
                        pltpu.VMEM((N_HEADS, BLOCK, BLOCK), jnp.float32),
                        pltpu.VMEM((N_HEADS, BLOCK, BLOCK), jnp.float32),
                        pltpu.SMEM((TM // BLOCK,), jnp.int32)],
        compiler_params=pltpu.CompilerParams(dimension_semantics=("arbitrary", "arbitrary"),
                                             vmem_limit_bytes=MIXER_VMEM),
        name="mixer",
    )(rel_bias.reshape(-1), sinks, positions.reshape(-1), x, mod, positions.reshape(bsz, nj, 1, TM), n1g, w_in, lng, lnb,
      w_s, bsb, p_a, p_b, w_o, n2g, w_r, b_r)


def _gather_chunks(idx_ref, idx0, n, src_hbm, dst, sem, inline=False, split_priority=False):
    def issue(c, carry, priority=0):
        src = pl.multiple_of(idx_ref[idx0 + c] * CHUNK, CHUNK)
        pltpu.make_async_copy(src_hbm.at[pl.ds(src, CHUNK)],
                              dst.at[pl.ds(pl.multiple_of(c * CHUNK, CHUNK), CHUNK)], sem).start(priority)
        return carry

    if inline:
        for c in range(n):
            issue(c, 0, c % 2 if split_priority else 0)
    else:
        lax.fori_loop(0, n, issue, 0, unroll=GATHER_UNROLL)


def _wait_chunks(n, src_hbm, dst, sem):
    pltpu.make_async_copy(src_hbm.at[pl.ds(0, n * CHUNK)], dst, sem).wait()


def _expert_kernel(src_ref, be_ref, ne_ref, ws_ref, nq_ref, lb_ref, xs_hbm, wg_hbm, wu_hbm, wd_hbm, yr_ref,
                   xbuf, wgf, wuf, wdf, wgb, wub, wdb, sem, wsem):
    i = pl.program_id(0)
    last = lb_ref[0]
    slot = i % 2

    def weight_copies(e, w):
        return [pltpu.make_async_copy(hbm.at[e], buf.at[w], wsem.at[w])
                for hbm, buf in ((wg_hbm, wgf), (wu_hbm, wuf), (wd_hbm, wdf))]

    @pl.when(i == 0)
    def _():
        _gather_chunks(src_ref, 0, BLOCK_CHUNKS, xs_hbm, xbuf.at[0], sem.at[0])
        for cp in weight_copies(be_ref[0], 0):
            cp.start()

    @pl.when(i <= last)
    def _():
        changed = (i == 0) | (be_ref[i] != be_ref[jnp.maximum(i - 1, 0)])

        @pl.when(changed)
        def _():
            w = ws_ref[i]
            for cp in weight_copies(be_ref[i], w):
                cp.wait()
            wgb[...] = wgf[w].astype(jnp.bfloat16)
            wub[...] = wuf[w].astype(jnp.bfloat16)
            wdb[...] = wdf[w].astype(jnp.bfloat16)

            @pl.when(ne_ref[i] >= 0)
            def _():
                for cp in weight_copies(ne_ref[i], 1 - w):
                    cp.start(priority=1)

        nxt = jnp.minimum(i + 1, last)
        _gather_chunks(src_ref, nxt * BLOCK_CHUNKS, BLOCK_CHUNKS, xs_hbm, xbuf.at[1 - slot], sem.at[1 - slot],
                       inline=True)
        _wait_chunks(BLOCK_CHUNKS, xs_hbm, xbuf.at[slot], sem.at[slot])

        def compute(rows):
            xb = _unpack_bf16_pairs(xbuf[slot, 0:rows, :])
            hg = _dot(xb, wgb[...])
            hu = _dot(xb, wub[...])
            hid = (hg * _sigmoid(hg) * hu).astype(jnp.bfloat16)
            yr_ref[0:rows, :] = _pack_bf16_pairs(_dot(hid, wdb[...]).astype(jnp.bfloat16).astype(jnp.float32))
            if rows < BM:
                yr_ref[rows:, :] = jnp.zeros((BM - rows, yr_ref.shape[1]), yr_ref.dtype)

        for quarters in range(1, BLOCK_QUARTERS + 1):
            pl.when(nq_ref[i] == quarters)(functools.partial(compute, quarters * BM // BLOCK_QUARTERS))

        @pl.when(i == last)
        def _():
            _wait_chunks(BLOCK_CHUNKS, xs_hbm, xbuf.at[1 - slot], sem.at[1 - slot])

    @pl.when(i > last)
    def _():
        yr_ref[...] = jnp.zeros_like(yr_ref)


def _experts(src_chunk, block_e, next_e, w_slot, quarters, last_blk, xs, w_gate, w_up, w_down):
    d = w_gate.shape[1]
    n_rows = src_chunk.shape[0] * CHUNK
    hbm = pl.BlockSpec(memory_space=pl.ANY)
    return pl.pallas_call(
        _expert_kernel,
        grid_spec=pltpu.PrefetchScalarGridSpec(
            num_scalar_prefetch=6,
            grid=(n_rows // BM,),
            in_specs=[hbm, hbm, hbm, hbm],
            out_specs=pl.BlockSpec((BM, d // 2), lambda i, *_: (i, 0)),
            scratch_shapes=[pltpu.VMEM((2, BM, d // 2), jnp.uint32),
                            pltpu.VMEM((2, d, D_EXPERT), jnp.float32),
                            pltpu.VMEM((2, d, D_EXPERT), jnp.float32),
                            pltpu.VMEM((2, D_EXPERT, d), jnp.float32),
                            pltpu.VMEM((d, D_EXPERT), jnp.bfloat16),
                            pltpu.VMEM((d, D_EXPERT), jnp.bfloat16),
                            pltpu.VMEM((D_EXPERT, d), jnp.bfloat16),
                            pltpu.SemaphoreType.DMA((2,)),
                            pltpu.SemaphoreType.DMA((2,))]),
        out_shape=jax.ShapeDtypeStruct((n_rows, d // 2), jnp.uint32),
        compiler_params=pltpu.CompilerParams(dimension_semantics=("arbitrary",),
                                             vmem_limit_bytes=EXPERTS_VMEM),
        name="experts",
    )(src_chunk, block_e, next_e, w_slot, quarters, last_blk, xs, w_gate, w_up, w_down)


def _combine_kernel(src_ref, x1_ref, rw_ref, mod_ref, fg_ref, yr_hbm, out_ref, ybuf, sem, *, final_norm):
    i = pl.program_id(0)
    slot = i % 2

    @pl.when(i == 0)
    def _():
        _gather_chunks(src_ref, 0, SORT_CHUNKS, yr_hbm, ybuf.at[0], sem.at[0])

    rows = jnp.concatenate([rw_ref[...], jnp.zeros((LANES - ROUTE_ROWS, TM), jnp.float32)], axis=0)
    rw = jnp.transpose(rows)
    sp = lax.broadcasted_iota(jnp.int32, (1, SORT_ROWS), 1)
    p1 = rw[:, 2:3].astype(jnp.int32)
    p2 = rw[:, 3:4].astype(jnp.int32)
    wmat = jnp.where(sp == p1, rw[:, 0:1], jnp.where(sp == p2, rw[:, 1:2], 0.0)).astype(jnp.bfloat16)
    nxt = jnp.minimum(i + 1, pl.num_programs(0) - 1)
    _gather_chunks(src_ref, nxt * SORT_CHUNKS, SORT_CHUNKS, yr_hbm, ybuf.at[1 - slot], sem.at[1 - slot],
                   inline=True, split_priority=True)
    _wait_chunks(SORT_CHUNKS, yr_hbm, ybuf.at[slot], sem.at[slot])
    moe = _dot(wmat, _unpack_bf16_pairs(ybuf[slot]))
    x2 = x1_ref[...] + mod_ref[5:6, :] * moe
    if final_norm:
        x2 = x2 * lax.rsqrt(jnp.mean(x2 * x2, axis=-1, keepdims=True) + EPS) * fg_ref[...]
    out_ref[...] = x2

    @pl.when(i == pl.num_programs(0) - 1)
    def _():
        _wait_chunks(SORT_CHUNKS, yr_hbm, ybuf.at[1 - slot], sem.at[1 - slot])


def _combine(src_chunk, x1, rw, mod, final_g, yr, seq, final_norm):
    t, d = x1.shape
    per_seq = seq // TM
    return pl.pallas_call(
        functools.partial(_combine_kernel, final_norm=final_norm),
        grid_spec=pltpu.PrefetchScalarGridSpec(
            num_scalar_prefetch=1,
            grid=(t // TM,),
            in_specs=[pl.BlockSpec((TM, d), lambda i, src: (i, 0)),
                      pl.BlockSpec((ROUTE_ROWS, TM), lambda i, src: (0, i)),
                      pl.BlockSpec((None, 6, d), lambda i, src: (i // per_seq, 0, 0)),
                      pl.BlockSpec((1, d), lambda i, src: (0, 0)),
                      pl.BlockSpec(memory_space=pl.ANY)],
            out_specs=pl.BlockSpec((TM, d), lambda i, src: (i, 0)),
            scratch_shapes=[pltpu.VMEM((2, SORT_ROWS, d // 2), jnp.uint32),
                            pltpu.SemaphoreType.DMA((2,))]),
        out_shape=jax.ShapeDtypeStruct((t, d), jnp.float32),
        compiler_params=pltpu.CompilerParams(dimension_semantics=("arbitrary",),
                                             vmem_limit_bytes=COMBINE_VMEM),
        name="combine",
    )(src_chunk, x1, rw, mod, final_g.reshape(1, d), yr)


def kernel(x, c, positions, rel_bias, w_ada, b_ada, norm1_g, w_in, sinks, gm_ln_g, gm_ln_b, gm_w_s, gm_b_s,
           p_a, p_b, w_o, norm2_g, w_router_g, b_router_g, w_router_e, b_router_e, w_gate, w_up, w_down,
           final_g):
    bsz, seq, d = x.shape
    t = bsz * seq
    bf16 = jnp.bfloat16
    depth = w_ada.shape[0]
    n_tiles = t // TM
    n_chunks = t * TOP_K // CHUNK + n_tiles * N_EXPERTS + N_EXPERTS * BLOCK_CHUNKS
    i32 = jnp.int32
    for l in range(depth):
        mod = _adaln_mod(c, w_ada[l], b_ada[l]).reshape(bsz, 6, d)
        pad = ROUTER_LANES - N_GROUPS - N_EXPERTS
        w_r = jnp.concatenate([w_router_g[l], w_router_e[l], jnp.zeros((d, pad), jnp.float32)], axis=1)
        b_r = jnp.concatenate([b_router_g[l], b_router_e[l], jnp.zeros((pad,), jnp.float32)]).reshape(1, -1)
        bsb = jnp.broadcast_to(gm_b_s[l][:, :, None], (GM_GROUPS, GM_CHUNK, LANES))
        x1, xs, rw, cnt = _mixer(
            x, mod, positions, rel_bias, norm1_g[l].reshape(1, d), w_in[l].astype(bf16), sinks[l],
            gm_ln_g[l].reshape(1, -1), gm_ln_b[l].reshape(1, -1), gm_w_s[l], bsb,
            p_a[l].astype(bf16), p_b[l].astype(bf16), w_o[l].astype(bf16), norm2_g[l].reshape(1, d),
            w_r.astype(bf16), b_r)

        n = cnt[:, EXPERT_LANE0:EXPERT_LANE0 + N_EXPERTS, 0].astype(i32)
        nch = (n + CHUNK - 1) // CHUNK
        run0_tile = jnp.cumsum(nch, axis=1) - nch
        run0_exp = jnp.cumsum(nch, axis=0) - nch
        tot = jnp.sum(nch, axis=0)
        seg = (tot + BLOCK_CHUNKS - 1) // BLOCK_CHUNKS * BLOCK_CHUNKS
        seg_end = jnp.cumsum(seg)
        seg0 = seg_end - seg

        pos_tile = jnp.arange(n_tiles, dtype=i32)[:, None] * SORT_CHUNKS + run0_tile
        pos_exp = seg0[None, :] + run0_exp

        j = jnp.arange(n_chunks, dtype=i32)[:, None]
        e_j = jnp.minimum(jnp.sum(j >= seg_end[None, :], axis=1), N_EXPERTS - 1)
        tables = jnp.concatenate([pos_exp.T, (pos_exp + nch).T, (pos_tile - pos_exp).T], axis=1)
        picked = jnp.dot(jax.nn.one_hot(e_j, N_EXPERTS, dtype=jnp.float32), tables.astype(jnp.float32),
                         precision=lax.Precision.HIGHEST).astype(i32)
        lo_j, hi_j, shift_j = picked[:, :n_tiles], picked[:, n_tiles:2 * n_tiles], picked[:, 2 * n_tiles:]
        inside = (j >= lo_j) & (j < hi_j)
        src_j = jnp.sum(jnp.where(inside, j + shift_j, 0), axis=1).astype(i32)
        blk0 = jnp.arange(n_chunks // BLOCK_CHUNKS, dtype=i32) * BLOCK_CHUNKS
        block_e = jnp.minimum(jnp.sum(blk0[:, None] >= seg_end[None, :], axis=1), N_EXPERTS - 1).astype(i32)
        last_blk = (jnp.maximum(seg_end[-1] // BLOCK_CHUNKS, 1) - 1).astype(i32).reshape(1)
        experts = jnp.arange(N_EXPERTS, dtype=i32)
        later = (experts[None, :] > block_e[:, None]) & (seg[None, :] > 0)
        next_e = jnp.min(jnp.where(later, experts[None, :], N_EXPERTS), axis=1)
        next_e = jnp.where(next_e < N_EXPERTS, next_e, -1).astype(i32)
        w_slot = (jnp.sum((experts[None, :] < block_e[:, None]) & (seg[None, :] > 0), axis=1) % 2).astype(i32)

        q = jnp.arange(SORT_CHUNKS, dtype=i32)[None, :, None]
        in_run = (q >= run0_tile[:, None, :]) & (q < (run0_tile + nch)[:, None, :])
        dst_q = jnp.sum(jnp.where(in_run, q + (pos_exp - run0_tile)[:, None, :], 0), axis=2).astype(i32).reshape(-1)

        own = block_e[:, None] == experts[None, :]
        filled = jnp.sum(jnp.where(own, (seg0 + tot)[None, :], 0), axis=1) - blk0
        quarter_chunks = BLOCK_CHUNKS // BLOCK_QUARTERS
        quarters = jnp.clip((filled + quarter_chunks - 1) // quarter_chunks, 1, BLOCK_QUARTERS).astype(i32)

        yr = _experts(src_j, block_e, next_e, w_slot, quarters, last_blk, xs, w_gate[l], w_up[l], w_down[l])
        x = _combine(dst_q, x1.reshape(t, d), rw, mod, final_g, yr, seq,
                     final_norm=(l == depth - 1)).reshape(bsz, seq, d)
    return x
```

```python
import functools
import math

import jax
import jax.numpy as jnp
from jax import lax
from jax.experimental import pallas as pl
from jax.experimental.pallas import tpu as pltpu

D_MODEL = 1024
N_HEADS = 8
N_KV_HEADS = 2
HEAD_DIM = 64
BLOCK = 128
ATTN_Q = N_HEADS * HEAD_DIM
ATTN_KV = N_KV_HEADS * HEAD_DIM
N_BUCKETS = 32
MAX_EXACT = N_BUCKETS // 2
MAX_DISTANCE = 128
GM_WIDTH = 512
GM_GROUPS = 4
GM_CHUNK = 128
N_GROUPS = 4
EXPERTS_PER_GROUP = 8
N_EXPERTS = N_GROUPS * EXPERTS_PER_GROUP
TOP_K = 2
D_EXPERT = 512
EPS = 1e-6
NEG = -1e30

LANES = 128
ROUTER_LANES = LANES
EXPERT_LANE0 = N_GROUPS
TM = 512
BM = 512
ROUTER_ROWS = 48
ROUTE_ROWS = 8
CHUNK = 8
SORT_ROWS = TM * TOP_K + N_EXPERTS * CHUNK
SORT_CHUNKS = SORT_ROWS // CHUNK
BLOCK_CHUNKS = BM // CHUNK
BLOCK_QUARTERS = 4
GATHER_UNROLL = 8
GATHER_SLOTS = 3
TAIL_SPLIT = 2
MIB = 1024 * 1024
MIXER_VMEM = 44 * MIB
EXPERTS_VMEM = 32 * MIB
COMBINE_VMEM = 24 * MIB

ADALN_TN = 1024

Q0 = 0
K0 = Q0 + ATTN_Q
V0 = K0 + ATTN_KV
GU0 = V0 + ATTN_KV
GV0 = GU0 + GM_WIDTH
GA0 = GV0 + GM_WIDTH
GB0 = GA0 + D_MODEL
IN_END = GB0 + D_MODEL


def _dot(a, b):
    return jnp.dot(a, b, preferred_element_type=jnp.float32)


def _dot_nt(a, b):
    return lax.dot_general(a, b, (((1,), (1,)), ((), ())), preferred_element_type=jnp.float32)


LOG2E = math.log2(math.e)


def _gelu_tanh(x):
    c = math.sqrt(2.0 / math.pi)
    k0, k1 = -2.0 * c * LOG2E, -2.0 * c * 0.044715 * LOG2E
    return x * (1.0 / (1.0 + jnp.exp2(x * (x * x * k1 + k0))))


def _sigmoid(x):
    return 1.0 / (1.0 + jnp.exp2(x * -LOG2E))


def _pack_bf16_pairs(x):
    bits = lax.bitcast_convert_type(x, jnp.uint32)
    half = x.shape[1] // 2
    return (bits[:, 0:half] >> 16) | (bits[:, half:] & jnp.uint32(0xFFFF0000))


def _unpack_bf16_pairs(w):
    lo = lax.bitcast_convert_type(w << 16, jnp.float32)
    hi = lax.bitcast_convert_type(w & jnp.uint32(0xFFFF0000), jnp.float32)
    return jnp.concatenate([lo, hi], axis=1).astype(jnp.bfloat16)


def _adaln_kernel(c_ref, w_ref, b_ref, o_ref):
    c = c_ref[...]
    cs = c * _sigmoid(c)
    o_ref[...] = _dot(cs, w_ref[...]) + b_ref[...]


def _adaln_mod(c, w, b):
    bsz, d = c.shape
    n = w.shape[1]
    tn = ADALN_TN
    return pl.pallas_call(
        _adaln_kernel,
        grid=(n // tn,),
        in_specs=[pl.BlockSpec((bsz, d), lambda i: (0, 0)),
                  pl.BlockSpec((d, tn), lambda i: (0, i)),
                  pl.BlockSpec((1, tn), lambda i: (0, i))],
        out_specs=pl.BlockSpec((bsz, tn), lambda i: (0, i)),
        out_shape=jax.ShapeDtypeStruct((bsz, n), jnp.float32),
        compiler_params=pltpu.CompilerParams(dimension_semantics=("arbitrary",)),
        name="adaln_mod",
    )(c, w, b.reshape(1, n))


def _mixer_kernel(relb_ref, sinks_ref, spos_ref,
                  x_ref, mod_ref, pos_ref, n1g_ref, win_ref, lng_ref, lnb_ref, ws_ref, bsb_ref,
                  pa_ref, pb_ref, wo_ref, n2g_ref, wr_ref, br_ref,
                  x1_ref, xs_ref, rw_ref, cnt_ref,
                  kbuf, vbuf, pbuf, biasm, key, ya, yb, strict, sbuf, mbuf, flag):
    b = pl.program_id(0)
    j = pl.program_id(1)
    nblk = TM // BLOCK
    bf16 = jnp.bfloat16

    @pl.when((b == 0) & (j == 0))
    def _():
        key[...] = jnp.zeros_like(key)
        for i in range(nblk):
            flag[i] = 1
        tr = lax.broadcasted_iota(jnp.int32, (TM, TM), 0)
        tc = lax.broadcasted_iota(jnp.int32, (TM, TM), 1)
        strict[...] = jnp.where(tr < tc, 1.0, 0.0).astype(bf16)

    @pl.when(j == 0)
    def _():
        kbuf[0:BLOCK, :] = jnp.zeros((BLOCK, LANES), jnp.float32)
        vbuf[0:BLOCK, :] = jnp.zeros((BLOCK, LANES), jnp.float32)
        pbuf[:, 0:BLOCK] = jnp.zeros((1, BLOCK), jnp.int32)

    pbuf[:, BLOCK:] = pos_ref[...]
    qi = lax.broadcasted_iota(jnp.int32, (BLOCK, BLOCK), 0)
    kc = lax.broadcasted_iota(jnp.int32, (BLOCK, BLOCK), 1)
    from_prev = kc > qi
    rels, changed = [], []
    tile0 = (b * pl.num_programs(1) + j) * TM
    for i in range(nblk):
        pk = pbuf[:, i * BLOCK:(i + 2) * BLOCK]
        rels.append(pk - spos_ref[tile0 + i * BLOCK])
        changed.append(jnp.where(rels[i] != key[i], 1.0, 0.0))
    any_changed = jnp.max(functools.reduce(jnp.maximum, changed))
    any_flag = functools.reduce(jnp.maximum, [flag[i] for i in range(nblk)])
    refresh = (any_changed != 0) | (any_flag != 0) | (j == 0)

    def refresh_bias(i):
        pk = pbuf[:, i * BLOCK:(i + 2) * BLOCK]
        pq = pk[:, BLOCK:]
        pq_col = jnp.transpose(jnp.broadcast_to(pq, (BLOCK, BLOCK)))
        no_prev = (j == 0) if i == 0 else None
        for hd in range(N_HEADS):
            biasm[i * N_HEADS + hd] = jnp.zeros((BLOCK, BLOCK), jnp.float32)
        for side in range(2):
            dist = pq_col - pk[:, side * BLOCK:(side + 1) * BLOCK]
            n = jnp.maximum(dist, 0)
            nf = jnp.maximum(n, 1).astype(jnp.float32)
            large = MAX_EXACT + (jnp.log(nf / MAX_EXACT) / math.log(MAX_DISTANCE / MAX_EXACT)
                                 * (N_BUCKETS - MAX_EXACT)).astype(jnp.int32)
            large = jnp.minimum(large, N_BUCKETS - 1)
            bucket = jnp.where(n < MAX_EXACT, n, large)
            use = from_prev if side == 0 else jnp.logical_not(from_prev)
            for hd in range(N_HEADS):
                acc = jnp.zeros((BLOCK, BLOCK), jnp.float32)
                for bk in range(N_BUCKETS):
                    acc = jnp.where(bucket == bk, relb_ref[bk * N_HEADS + hd], acc)
                if side == 0 and i == 0:
                    acc = jnp.where(no_prev, NEG, acc)
                slot = i * N_HEADS + hd
                biasm[slot] = jnp.where(use, acc, biasm[slot])
        key[i] = rels[i]
        flag[i] = no_prev.astype(jnp.int32) if i == 0 else 0

    for i in range(nblk):
        @pl.when(refresh)
        def _():
            stale = (jnp.max(changed[i]) != 0) | (flag[i] != 0)
            if i == 0:
                stale = stale | (j == 0)
            pl.when(stale)(functools.partial(refresh_bias, i))

    x = x_ref[...]
    sh1, sc1, g1 = mod_ref[0:1, :], mod_ref[1:2, :], mod_ref[2:3, :]
    sh2, sc2 = mod_ref[3:4, :], mod_ref[4:5, :]

    xn = x * lax.rsqrt(jnp.mean(x * x, axis=-1, keepdims=True) + EPS)
    h = (xn * (n1g_ref[...] * (1.0 + sc1)) + sh1).astype(bf16)

    lane = lax.broadcasted_iota(jnp.int32, (1, LANES), 1)
    lo = lane < HEAD_DIM
    q = _dot(h, win_ref[:, Q0:K0]) * (HEAD_DIM ** -0.5)
    lo4 = jnp.concatenate([lo] * (ATTN_Q // LANES), axis=1)
    q_lo = jnp.where(lo4, q, 0.0).astype(bf16)
    q_hi = jnp.where(lo4, 0.0, q).astype(bf16)
    kv = _dot(h, win_ref[:, K0:GU0])
    kbuf[BLOCK:, :] = kv[:, 0:LANES]
    vbuf[BLOCK:, :] = kv[:, LANES:]
    kf = kbuf[...]
    vf = vbuf[...]
    kr = pltpu.roll(kf, HEAD_DIM, 1)
    vr = pltpu.roll(vf, HEAD_DIM, 1)
    kd = (jnp.where(lo, kf, kr).astype(bf16), jnp.where(lo, kr, kf).astype(bf16))
    v_lo = (jnp.where(lo, vf, 0.0).astype(bf16), jnp.where(lo, vr, 0.0).astype(bf16))
    v_hi = (jnp.where(lo, 0.0, vr).astype(bf16), jnp.where(lo, 0.0, vf).astype(bf16))

    ones_blk = jnp.ones((2 * BLOCK, LANES), bf16)

    def attend(i):
        rows = slice(i * BLOCK, (i + 1) * BLOCK)
        band = slice(i * BLOCK, (i + 2) * BLOCK)
        grp = N_HEADS // N_KV_HEADS
        for kvh in range(N_KV_HEADS):
            heads = range(kvh * grp, (kvh + 1) * grp)
            qs = jnp.concatenate([(q_lo if hd % 2 == 0 else q_hi)[rows, hd // 2 * LANES:(hd // 2 + 1) * LANES]
                                  for hd in heads], axis=0)
            s2 = _dot_nt(qs, kd[kvh][band])
            for g, hd in enumerate(heads):
                sh = s2[g * BLOCK:(g + 1) * BLOCK]
                s = jnp.where(from_prev, sh[:, 0:BLOCK], sh[:, BLOCK:]) + biasm[i * N_HEADS + hd]
                sbuf[hd] = s
                m = jnp.maximum(jnp.max(s, axis=-1, keepdims=True), sinks_ref[hd])
                mbuf[hd] = jnp.broadcast_to(m, (BLOCK, BLOCK))
        for kvh in range(N_KV_HEADS):
            outs = {}
            for half, vv in enumerate((v_lo, v_hi)):
                hds = (kvh * grp + half, kvh * grp + half + 2)
                p2 = []
                for hd in hds:
                    p = jnp.exp(sbuf[hd] - mbuf[hd])
                    p2.append(jnp.concatenate([jnp.where(from_prev, p, 0.0), jnp.where(from_prev, 0.0, p)],
                                              axis=1).astype(bf16))
                r = _dot(jnp.concatenate(p2, axis=0), jnp.concatenate([vv[kvh][band], ones_blk], axis=1))
                for n, hd in enumerate(hds):
                    rh = r[n * BLOCK:(n + 1) * BLOCK]
                    den = rh[:, LANES:] + jnp.exp(sinks_ref[hd] - mbuf[hd])
                    outs[hd] = rh[:, 0:LANES] * (1.0 / den)
            for pr in (kvh * grp // 2, kvh * grp // 2 + 1):
                ya[rows, pr * LANES:(pr + 1) * LANES] = (outs[2 * pr] + outs[2 * pr + 1]).astype(bf16)

    u = _gelu_tanh(_dot(h, win_ref[:, GU0:GV0]))
    attend(0)
    vg = _gelu_tanh(_dot(h, win_ref[:, GV0:GA0]))
    mu = jnp.mean(vg, axis=-1, keepdims=True)
    vc = vg - mu
    var = jnp.mean(vc * vc, axis=-1, keepdims=True)
    vn = (vc * lax.rsqrt(var + EPS) * lng_ref[...] + lnb_ref[...]).astype(bf16)
    attend(1)
    gate_a = _sigmoid(_dot(h, win_ref[:, GA0:GB0]))
    attend(2)
    ti = lax.broadcasted_iota(jnp.int32, (GM_CHUNK, GM_CHUNK), 0)
    si = lax.broadcasted_iota(jnp.int32, (GM_CHUNK, GM_CHUNK), 1)
    tril = si <= ti
    for g in range(GM_GROUPS):
        wg = jnp.where(tril, ws_ref[g], 0.0).astype(bf16)
        cols = slice(g * LANES, (g + 1) * LANES)
        for cidx in range(TM // GM_CHUNK):
            rows = slice(cidx * GM_CHUNK, (cidx + 1) * GM_CHUNK)
            sv = _dot(wg, vn[rows, cols]) + bsb_ref[g]
            yb[rows, cols] = (u[rows, cols] * sv).astype(bf16)
    gate_b = _sigmoid(_dot(h, win_ref[:, GB0:IN_END]))
    attend(3)

    kbuf[0:BLOCK, :] = kv[TM - BLOCK:, 0:LANES]
    vbuf[0:BLOCK, :] = kv[TM - BLOCK:, LANES:]
    pbuf[:, 0:BLOCK] = pos_ref[:, TM - BLOCK:]

    h2_parts, logit_parts = [], []
    for grp in range(TAIL_SPLIT):
        tok = slice(grp * TM // TAIL_SPLIT, (grp + 1) * TM // TAIL_SPLIT)
        merged = gate_a[tok] * _dot(ya[tok, :], pa_ref[...]) + gate_b[tok] * _dot(yb[tok, :], pb_ref[...])
        x1 = x[tok] + g1 * _dot(merged.astype(bf16), wo_ref[...])
        x1_ref[tok, :] = x1

        xn2 = x1 * lax.rsqrt(jnp.mean(x1 * x1, axis=-1, keepdims=True) + EPS)
        h2_parts.append((xn2 * (n2g_ref[...] * (1.0 + sc2)) + sh2).astype(bf16))
        logit_parts.append(_dot(h2_parts[-1], wr_ref[...]) + br_ref[...])
    h2 = jnp.concatenate(h2_parts, axis=0)
    logits = jnp.concatenate(logit_parts, axis=0)
    lt = jnp.transpose(logits)[0:ROUTER_ROWS, :]
    row = lax.broadcasted_iota(jnp.int32, (ROUTER_ROWS, TM), 0)
    row_f = row.astype(jnp.float32)
    big = float(ROUTER_ROWS)
    is_grp = row < N_GROUPS
    lg = jnp.where(is_grp, lt, NEG)
    lg_max = jnp.max(lg, axis=0, keepdims=True)
    g_idx = jnp.min(jnp.where(lg == lg_max, row_f, big), axis=0, keepdims=True)
    p_g = 1.0 / jnp.sum(jnp.where(is_grp, jnp.exp(lg - lg_max), 0.0), axis=0, keepdims=True)
    row_grp = jnp.floor((row_f - EXPERT_LANE0) * (1.0 / EXPERTS_PER_GROUP))
    in_grp = (row >= EXPERT_LANE0) & (row < EXPERT_LANE0 + N_EXPERTS) & (row_grp == g_idx)
    le = jnp.where(in_grp, lt, NEG)
    m1 = jnp.max(le, axis=0, keepdims=True)
    i1 = jnp.min(jnp.where(le == m1, row_f, big), axis=0, keepdims=True)
    oh1 = row_f == i1
    le2 = jnp.where(oh1, NEG, le)
    m2 = jnp.max(le2, axis=0, keepdims=True)
    i2 = jnp.min(jnp.where(le2 == m2, row_f, big), axis=0, keepdims=True)
    oh2 = row_f == i2
    e2 = jnp.exp(m2 - m1)
    w1 = p_g / (1.0 + e2)
    w2 = p_g * e2 / (1.0 + e2)

    oh = jnp.where(oh1, 1.0, jnp.where(oh2, 1.0, 0.0))
    n_e = jnp.sum(oh, axis=1, keepdims=True)
    cnt_ref[...] = n_e
    padded = jnp.floor((n_e + (CHUNK - 1)) * (1.0 / CHUNK)) * CHUNK
    dst_row = lax.broadcasted_iota(jnp.int32, (ROUTER_ROWS, ROUTER_ROWS), 0)
    src_row = lax.broadcasted_iota(jnp.int32, (ROUTER_ROWS, ROUTER_ROWS), 1)
    lower = jnp.where(src_row < dst_row, 1.0, 0.0).astype(bf16)
    run0 = _dot(lower, jnp.broadcast_to(padded, (ROUTER_ROWS, TM)).astype(bf16))
    slot = _dot(oh.astype(bf16), strict[...]) + run0
    pos1 = jnp.sum(jnp.where(oh1, slot, 0.0), axis=0, keepdims=True)
    pos2 = jnp.sum(jnp.where(oh2, slot, 0.0), axis=0, keepdims=True)
    rrow = lax.broadcasted_iota(jnp.int32, (ROUTE_ROWS, TM), 0)
    rw_ref[...] = jnp.where(rrow == 0, w1, jnp.where(rrow == 1, w2,
                            jnp.where(rrow == 2, pos1, jnp.where(rrow == 3, pos2, 0.0))))

    p1 = pos1.astype(jnp.int32)
    p2 = pos2.astype(jnp.int32)
    sp = lax.broadcasted_iota(jnp.int32, (SORT_ROWS, TM), 0)
    perm = jnp.where(sp == p1, 1.0, jnp.where(sp == p2, 1.0, 0.0)).astype(bf16)
    xs_ref[...] = _pack_bf16_pairs(_dot(perm, h2))


def _mixer(x, mod, positions, rel_bias, n1g, w_in, sinks, lng, lnb, w_s, bsb, p_a, p_b, w_o, n2g, w_r, b_r):
    bsz, seq, d = x.shape
    nj = seq // TM
    const = lambda *shape: pl.BlockSpec(shape, lambda b, j: (0,) * len(shape), pipeline_mode=pl.Buffered(1))
    smem = pl.BlockSpec(memory_space=pltpu.SMEM)
    tile = lambda w: pl.BlockSpec((None, TM, w), lambda b, j: (b, j, 0))
    return pl.pallas_call(
        _mixer_kernel,
        grid=(bsz, nj),
        in_specs=[smem, smem, smem,
                  tile(d),
                  pl.BlockSpec((None, 6, d), lambda b, j: (b, 0, 0)),
                  pl.BlockSpec((None, None, 1, TM), lambda b, j: (b, j, 0, 0)),
                  const(1, d), const(d, IN_END), const(1, GM_WIDTH), const(1, GM_WIDTH),
                  const(GM_GROUPS, GM_CHUNK, GM_CHUNK), const(GM_GROUPS, GM_CHUNK, LANES),
                  const(ATTN_Q, d), const(GM_WIDTH, d), const(d, d), const(1, d),
                  const(d, ROUTER_LANES), const(1, ROUTER_LANES)],
        out_specs=[tile(d),
                   pl.BlockSpec((SORT_ROWS, d // 2), lambda b, j: (b * nj + j, 0)),
                   pl.BlockSpec((ROUTE_ROWS, TM), lambda b, j: (0, b * nj + j)),
                   pl.BlockSpec((None, ROUTER_ROWS, 1), lambda b, j: (b * nj + j, 0, 0))],
        out_shape=[jax.ShapeDtypeStruct((bsz, seq, d), jnp.float32),
                   jax.ShapeDtypeStruct((bsz * nj * SORT_ROWS, d // 2), jnp.uint32),
                   jax.ShapeDtypeStruct((ROUTE_ROWS, bsz * seq), jnp.float32),
                   jax.ShapeDtypeStruct((bsz * nj, ROUTER_ROWS, 1), jnp.float32)],
        scratch_shapes=[pltpu.VMEM((TM + BLOCK, LANES), jnp.float32),
                        pltpu.VMEM((TM + BLOCK, LANES), jnp.float32),
                        pltpu.VMEM((1, TM + BLOCK), jnp.int32),
                        pltpu.VMEM((TM // BLOCK * N_HEADS, BLOCK, BLOCK), jnp.float32),
                        pltpu.VMEM((TM // BLOCK, 1, 2 * BLOCK), jnp.int32),
                        pltpu.VMEM((TM, ATTN_Q), jnp.bfloat16),
                        pltpu.VMEM((TM, GM_WIDTH), jnp.bfloat16),
                        pltpu.VMEM((TM, TM), jnp.bfloat16),
                        pltpu.VMEM((N_HEADS, BLOCK, BLOCK), jnp.float32),
                        pltpu.VMEM((N_HEADS, BLOCK, BLOCK), jnp.float32),
                        pltpu.SMEM((TM // BLOCK,), jnp.int32)],
        compiler_params=pltpu.CompilerParams(dimension_semantics=("arbitrary", "arbitrary"),
                                             vmem_limit_bytes=MIXER_VMEM),
        name="mixer",
    )(rel_bias.reshape(-1), sinks, positions.reshape(-1), x, mod, positions.reshape(bsz, nj, 1, TM), n1g, w_in, lng, lnb,
      w_s, bsb, p_a, p_b, w_o, n2g, w_r, b_r)


def _gather_chunks(idx_ref, idx0, n, src_hbm, dst, sem, inline=False, split_priority=False):
    def issue(c, carry, priority=0):
        src = pl.multiple_of(idx_ref[idx0 + c] * CHUNK, CHUNK)
        pltpu.make_async_copy(src_hbm.at[pl.ds(src, CHUNK)],
                              dst.at[pl.ds(pl.multiple_of(c * CHUNK, CHUNK), CHUNK)], sem).start(priority)
        return carry

    if inline:
        for c in range(n):
            issue(c, 0, c % 2 if split_priority else 0)
    else:
        lax.fori_loop(0, n, issue, 0, unroll=GATHER_UNROLL)


def _wait_chunks(n, src_hbm, dst, sem):
    pltpu.make_async_copy(src_hbm.at[pl.ds(0, n * CHUNK)], dst, sem).wait()


def _expert_kernel(src_ref, be_ref, ne_ref, ws_ref, nq_ref, lb_ref, xs_hbm, wg_hbm, wu_hbm, wd_hbm, yr_ref,
                   xbuf, wgf, wuf, wdf, wgb, wub, wdb, sem, wsem):
    i = pl.program_id(0)
    last = lb_ref[0]
    slot = i % GATHER_SLOTS
    ahead = jnp.minimum(i + GATHER_SLOTS - 1, last)
    ahead_slot = (i + GATHER_SLOTS - 1) % GATHER_SLOTS

    def weight_copies(e, w):
        return [pltpu.make_async_copy(hbm.at[e], buf.at[w], wsem.at[w])
                for hbm, buf in ((wg_hbm, wgf), (wu_hbm, wuf), (wd_hbm, wdf))]

    @pl.when(i == 0)
    def _():
        for s in range(GATHER_SLOTS - 1):
            _gather_chunks(src_ref, jnp.minimum(s, last) * BLOCK_CHUNKS, BLOCK_CHUNKS, xs_hbm, xbuf.at[s], sem.at[s])
        for cp in weight_copies(be_ref[0], 0):
            cp.start()

    @pl.when(i <= last)
    def _():
        changed = (i == 0) | (be_ref[i] != be_ref[jnp.maximum(i - 1, 0)])

        @pl.when(changed)
        def _():
            w = ws_ref[i]
            for cp in weight_copies(be_ref[i], w):
                cp.wait()
            wgb[...] = wgf[w].astype(jnp.bfloat16)
            wub[...] = wuf[w].astype(jnp.bfloat16)
            wdb[...] = wdf[w].astype(jnp.bfloat16)

            @pl.when(ne_ref[i] >= 0)
            def _():
                for cp in weight_copies(ne_ref[i], 1 - w):
                    cp.start(priority=1)

        _wait_chunks(BLOCK_CHUNKS, xs_hbm, xbuf.at[slot], sem.at[slot])

        def compute(rows):
            xb = _unpack_bf16_pairs(xbuf[slot, 0:rows, :])
            hg = _dot(xb, wgb[...])
            hu = _dot(xb, wub[...])
            hid = (hg * _sigmoid(hg) * hu).astype(jnp.bfloat16)
            yr_ref[0:rows, :] = _pack_bf16_pairs(_dot(hid, wdb[...]).astype(jnp.bfloat16).astype(jnp.float32))
            if rows < BM:
                yr_ref[rows:, :] = jnp.zeros((BM - rows, yr_ref.shape[1]), yr_ref.dtype)
            _gather_chunks(src_ref, ahead * BLOCK_CHUNKS, BLOCK_CHUNKS, xs_hbm, xbuf.at[ahead_slot],
                           sem.at[ahead_slot], inline=True)

        for quarters in range(1, BLOCK_QUARTERS + 1):
            pl.when(nq_ref[i] == quarters)(functools.partial(compute, quarters * BM // BLOCK_QUARTERS))

        @pl.when(i == last)
        def _():
            for s in range(1, GATHER_SLOTS):
                other = (i + s) % GATHER_SLOTS
                _wait_chunks(BLOCK_CHUNKS, xs_hbm, xbuf.at[other], sem.at[other])

    @pl.when(i > last)
    def _():
        yr_ref[...] = jnp.zeros_like(yr_ref)


def _experts(src_chunk, block_e, next_e, w_slot, quarters, last_blk, xs, w_gate, w_up, w_down):
    d = w_gate.shape[1]
    n_rows = src_chunk.shape[0] * CHUNK
    hbm = pl.BlockSpec(memory_space=pl.ANY)
    return pl.pallas_call(
        _expert_kernel,
        grid_spec=pltpu.PrefetchScalarGridSpec(
            num_scalar_prefetch=6,
            grid=(n_rows // BM,),
            in_specs=[hbm, hbm, hbm, hbm],
            out_specs=pl.BlockSpec((BM, d // 2), lambda i, *_: (i, 0)),
            scratch_shapes=[pltpu.VMEM((GATHER_SLOTS, BM, d // 2), jnp.uint32),
                            pltpu.VMEM((2, d, D_EXPERT), jnp.float32),
                            pltpu.VMEM((2, d, D_EXPERT), jnp.float32),
                            pltpu.VMEM((2, D_EXPERT, d), jnp.float32),
                            pltpu.VMEM((d, D_EXPERT), jnp.bfloat16),
                            pltpu.VMEM((d, D_EXPERT), jnp.bfloat16),
                            pltpu.VMEM((D_EXPERT, d), jnp.bfloat16),
                            pltpu.SemaphoreType.DMA((GATHER_SLOTS,)),
                            pltpu.SemaphoreType.DMA((2,))]),
        out_shape=jax.ShapeDtypeStruct((n_rows, d // 2), jnp.uint32),
        compiler_params=pltpu.CompilerParams(dimension_semantics=("arbitrary",),
                                             vmem_limit_bytes=EXPERTS_VMEM),
        name="experts",
    )(src_chunk, block_e, next_e, w_slot, quarters, last_blk, xs, w_gate, w_up, w_down)


def _combine_kernel(src_ref, x1_ref, rw_ref, mod_ref, fg_ref, yr_hbm, out_ref, ybuf, sem, *, final_norm):
    i = pl.program_id(0)
    slot = i % 2

    @pl.when(i == 0)
    def _():
        _gather_chunks(src_ref, 0, SORT_CHUNKS, yr_hbm, ybuf.at[0], sem.at[0])

    rows = jnp.concatenate([rw_ref[...], jnp.zeros((LANES - ROUTE_ROWS, TM), jnp.float32)], axis=0)
    rw = jnp.transpose(rows)
    sp = lax.broadcasted_iota(jnp.int32, (1, SORT_ROWS), 1)
    p1 = rw[:, 2:3].astype(jnp.int32)
    p2 = rw[:, 3:4].astype(jnp.int32)
    wmat = jnp.where(sp == p1, rw[:, 0:1], jnp.where(sp == p2, rw[:, 1:2], 0.0)).astype(jnp.bfloat16)
    nxt = jnp.minimum(i + 1, pl.num_programs(0) - 1)
    _gather_chunks(src_ref, nxt * SORT_CHUNKS, SORT_CHUNKS, yr_hbm, ybuf.at[1 - slot], sem.at[1 - slot],
                   inline=True, split_priority=True)
    _wait_chunks(SORT_CHUNKS, yr_hbm, ybuf.at[slot], sem.at[slot])
    moe = _dot(wmat, _unpack_bf16_pairs(ybuf[slot]))
    x2 = x1_ref[...] + mod_ref[5:6, :] * moe
    if final_norm:
        x2 = x2 * lax.rsqrt(jnp.mean(x2 * x2, axis=-1, keepdims=True) + EPS) * fg_ref[...]
    out_ref[...] = x2

    @pl.when(i == pl.num_programs(0) - 1)
    def _():
        _wait_chunks(SORT_CHUNKS, yr_hbm, ybuf.at[1 - slot], sem.at[1 - slot])


def _combine(src_chunk, x1, rw, mod, final_g, yr, seq, final_norm):
    t, d = x1.shape
    per_seq = seq // TM
    return pl.pallas_call(
        functools.partial(_combine_kernel, final_norm=final_norm),
        grid_spec=pltpu.PrefetchScalarGridSpec(
            num_scalar_prefetch=1,
            grid=(t // TM,),
            in_specs=[pl.BlockSpec((TM, d), lambda i, src: (i, 0)),
                      pl.BlockSpec((ROUTE_ROWS, TM), lambda i, src: (0, i)),
                      pl.BlockSpec((None, 6, d), lambda i, src: (i // per_seq, 0, 0)),
                      pl.BlockSpec((1, d), lambda i, src: (0, 0)),
                      pl.BlockSpec(memory_space=pl.ANY)],
            out_specs=pl.BlockSpec((TM, d), lambda i, src: (i, 0)),
            scratch_shapes=[pltpu.VMEM((2, SORT_ROWS, d // 2), jnp.uint32),
                            pltpu.SemaphoreType.DMA((2,))]),
        out_shape=jax.ShapeDtypeStruct((t, d), jnp.float32),
        compiler_params=pltpu.CompilerParams(dimension_semantics=("arbitrary",),
                                             vmem_limit_bytes=COMBINE_VMEM),
        name="combine",
    )(src_chunk, x1, rw, mod, final_g.reshape(1, d), yr)


def kernel(x, c, positions, rel_bias, w_ada, b_ada, norm1_g, w_in, sinks, gm_ln_g, gm_ln_b, gm_w_s, gm_b_s,
           p_a, p_b, w_o, norm2_g, w_router_g, b_router_g, w_router_e, b_router_e, w_gate, w_up, w_down,
           final_g):
    bsz, seq, d = x.shape
    t = bsz * seq
    bf16 = jnp.bfloat16
    depth = w_ada.shape[0]
    n_tiles = t // TM
    n_chunks = t * TOP_K // CHUNK + n_tiles * N_EXPERTS + N_EXPERTS * BLOCK_CHUNKS
    i32 = jnp.int32
    for l in range(depth):
        mod = _adaln_mod(c, w_ada[l], b_ada[l]).reshape(bsz, 6, d)
        pad = ROUTER_LANES - N_GROUPS - N_EXPERTS
        w_r = jnp.concatenate([w_router_g[l], w_router_e[l], jnp.zeros((d, pad), jnp.float32)], axis=1)
        b_r = jnp.concatenate([b_router_g[l], b_router_e[l], jnp.zeros((pad,), jnp.float32)]).reshape(1, -1)
        bsb = jnp.broadcast_to(gm_b_s[l][:, :, None], (GM_GROUPS, GM_CHUNK, LANES))
        x1, xs, rw, cnt = _mixer(
            x, mod, positions, rel_bias, norm1_g[l].reshape(1, d), w_in[l].astype(bf16), sinks[l],
            gm_ln_g[l].reshape(1, -1), gm_ln_b[l].reshape(1, -1), gm_w_s[l], bsb,
            p_a[l].astype(bf16), p_b[l].astype(bf16), w_o[l].astype(bf16), norm2_g[l].reshape(1, d),
            w_r.astype(bf16), b_r)

        n = cnt[:, EXPERT_LANE0:EXPERT_LANE0 + N_EXPERTS, 0].astype(i32)
        nch = (n + CHUNK - 1) // CHUNK
        run0_tile = jnp.cumsum(nch, axis=1) - nch
        run0_exp = jnp.cumsum(nch, axis=0) - nch
        tot = jnp.sum(nch, axis=0)
        seg = (tot + BLOCK_CHUNKS - 1) // BLOCK_CHUNKS * BLOCK_CHUNKS
        seg_end = jnp.cumsum(seg)
        seg0 = seg_end - seg

        pos_tile = jnp.arange(n_tiles, dtype=i32)[:, None] * SORT_CHUNKS + run0_tile
        pos_exp = seg0[None, :] + run0_exp

        j = jnp.arange(n_chunks, dtype=i32)[:, None]
        e_j = jnp.minimum(jnp.sum(j >= seg_end[None, :], axis=1), N_EXPERTS - 1)
        tables = jnp.concatenate([pos_exp.T, (pos_exp + nch).T, (pos_tile - pos_exp).T], axis=1)
        picked = jnp.dot(jax.nn.one_hot(e_j, N_EXPERTS, dtype=jnp.float32), tables.astype(jnp.float32),
                         precision=lax.Precision.HIGHEST).astype(i32)
        lo_j, hi_j, shift_j = picked[:, :n_tiles], picked[:, n_tiles:2 * n_tiles], picked[:, 2 * n_tiles:]
        inside = (j >= lo_j) & (j < hi_j)
        src_j = jnp.sum(jnp.where(inside, j + shift_j, 0), axis=1).astype(i32)
        blk0 = jnp.arange(n_chunks // BLOCK_CHUNKS, dtype=i32) * BLOCK_CHUNKS
        block_e = jnp.minimum(jnp.sum(blk0[:, None] >= seg_end[None, :], axis=1), N_EXPERTS - 1).astype(i32)
        last_blk = (jnp.maximum(seg_end[-1] // BLOCK_CHUNKS, 1) - 1).astype(i32).reshape(1)
        experts = jnp.arange(N_EXPERTS, dtype=i32)
        later = (experts[None, :] > block_e[:, None]) & (seg[None, :] > 0)
        next_e = jnp.min(jnp.where(later, experts[None, :], N_EXPERTS), axis=1)
        next_e = jnp.where(next_e < N_EXPERTS, next_e, -1).astype(i32)
        w_slot = (jnp.sum((experts[None, :] < block_e[:, None]) & (seg[None, :] > 0), axis=1) % 2).astype(i32)

        q = jnp.arange(SORT_CHUNKS, dtype=i32)[None, :, None]
        in_run = (q >= run0_tile[:, None, :]) & (q < (run0_tile + nch)[:, None, :])
        dst_q = jnp.sum(jnp.where(in_run, q + (pos_exp - run0_tile)[:, None, :], 0), axis=2).astype(i32).reshape(-1)

        own = block_e[:, None] == experts[None, :]
        filled = jnp.sum(jnp.where(own, (seg0 + tot)[None, :], 0), axis=1) - blk0
        quarter_chunks = BLOCK_CHUNKS // BLOCK_QUARTERS
        quarters = jnp.clip((filled + quarter_chunks - 1) // quarter_chunks, 1, BLOCK_QUARTERS).astype(i32)

        yr = _experts(src_j, block_e, next_e, w_slot, quarters, last_blk, xs, w_gate[l], w_up[l], w_down[l])
        x = _combine(dst_q, x1.reshape(t, d), rw, mod, final_g, yr, seq,
                     final_norm=(l == depth - 1)).reshape(bsz, seq, d)
    return x
```

```python
import functools
import math

import jax
import jax.numpy as jnp
from jax import lax
from jax.experimental import pallas as pl
from jax.experimental.pallas import tpu as pltpu

D_MODEL = 1024
N_HEADS = 8
N_KV_HEADS = 2
HEAD_DIM = 64
BLOCK = 128
ATTN_Q = N_HEADS * HEAD_DIM
ATTN_KV = N_KV_HEADS * HEAD_DIM
N_BUCKETS = 32
MAX_EXACT = N_BUCKETS // 2
MAX_DISTANCE = 128
GM_WIDTH = 512
GM_GROUPS = 4
GM_CHUNK = 128
N_GROUPS = 4
EXPERTS_PER_GROUP = 8
N_EXPERTS = N_GROUPS * EXPERTS_PER_GROUP
TOP_K = 2
D_EXPERT = 512
EPS = 1e-6
NEG = -1e30

LANES = 128
ROUTER_LANES = LANES
EXPERT_LANE0 = N_GROUPS
TM = 512
BM = 512
ROUTER_ROWS = 48
ROUTE_ROWS = 8
CHUNK = 8
SORT_ROWS = TM * TOP_K + N_EXPERTS * CHUNK
SORT_CHUNKS = SORT_ROWS // CHUNK
BLOCK_CHUNKS = BM // CHUNK
BLOCK_QUARTERS = 4
GATHER_UNROLL = 8
GATHER_SLOTS = 3
TAIL_SPLIT = 2
MIB = 1024 * 1024
MIXER_VMEM = 44 * MIB
EXPERTS_VMEM = 32 * MIB
COMBINE_VMEM = 24 * MIB

ADALN_TN = 1024

Q0 = 0
K0 = Q0 + ATTN_Q
V0 = K0 + ATTN_KV
GU0 = V0 + ATTN_KV
GV0 = GU0 + GM_WIDTH
GA0 = GV0 + GM_WIDTH
GB0 = GA0 + D_MODEL
IN_END = GB0 + D_MODEL


def _dot(a, b):
    return jnp.dot(a, b, preferred_element_type=jnp.float32)


def _dot_nt(a, b):
    return lax.dot_general(a, b, (((1,), (1,)), ((), ())), preferred_element_type=jnp.float32)


LOG2E = math.log2(math.e)


def _gelu_tanh(x):
    c = math.sqrt(2.0 / math.pi)
    k0, k1 = -2.0 * c * LOG2E, -2.0 * c * 0.044715 * LOG2E
    return x * (1.0 / (1.0 + jnp.exp2(x * (x * x * k1 + k0))))


def _sigmoid(x):
    return 1.0 / (1.0 + jnp.exp2(x * -LOG2E))


def _pack_bf16_pairs(x):
    bits = lax.bitcast_convert_type(x, jnp.uint32)
    half = x.shape[1] // 2
    return (bits[:, 0:half] >> 16) | (bits[:, half:] & jnp.uint32(0xFFFF0000))


def _unpack_bf16_pairs(w):
    lo = lax.bitcast_convert_type(w << 16, jnp.float32)
    hi = lax.bitcast_convert_type(w & jnp.uint32(0xFFFF0000), jnp.float32)
    return jnp.concatenate([lo, hi], axis=1).astype(jnp.bfloat16)


def _adaln_kernel(c_ref, w_ref, b_ref, o_ref):
    c = c_ref[...]
    cs = c * _sigmoid(c)
    o_ref[...] = _dot(cs, w_ref[...]) + b_ref[...]


def _adaln_mod(c, w, b):
    bsz, d = c.shape
    n = w.shape[1]
    tn = ADALN_TN
    return pl.pallas_call(
        _adaln_kernel,
        grid=(n // tn,),
        in_specs=[pl.BlockSpec((bsz, d), lambda i: (0, 0)),
                  pl.BlockSpec((d, tn), lambda i: (0, i)),
                  pl.BlockSpec((1, tn), lambda i: (0, i))],
        out_specs=pl.BlockSpec((bsz, tn), lambda i: (0, i)),
        out_shape=jax.ShapeDtypeStruct((bsz, n), jnp.float32),
        compiler_params=pltpu.CompilerParams(dimension_semantics=("arbitrary",)),
        name="adaln_mod",
    )(c, w, b.reshape(1, n))


def _mixer_kernel(relb_ref, sinks_ref, spos_ref,
                  x_ref, mod_ref, pos_ref, posn_ref, n1g_ref, win_ref, lng_ref, lnb_ref, ws_ref, bsb_ref,
                  pa_ref, pb_ref, wo_ref, n2g_ref, wr_ref, br_ref,
                  x1_ref, xs_ref, rw_ref, cnt_ref,
                  kbuf, vbuf, pbuf, biasm, key, ya, yb, strict, sbuf, mbuf, flag, need):
    b = pl.program_id(0)
    j = pl.program_id(1)
    nblk = TM // BLOCK
    bf16 = jnp.bfloat16

    @pl.when((b == 0) & (j == 0))
    def _():
        key[...] = jnp.zeros_like(key)
        need[0] = 1
        for i in range(nblk):
            flag[i] = 1
        tr = lax.broadcasted_iota(jnp.int32, (TM, TM), 0)
        tc = lax.broadcasted_iota(jnp.int32, (TM, TM), 1)
        strict[...] = jnp.where(tr < tc, 1.0, 0.0).astype(bf16)

    @pl.when(j == 0)
    def _():
        kbuf[0:BLOCK, :] = jnp.zeros((BLOCK, LANES), jnp.float32)
        vbuf[0:BLOCK, :] = jnp.zeros((BLOCK, LANES), jnp.float32)
        pbuf[:, 0:BLOCK] = jnp.zeros((1, BLOCK), jnp.int32)

    pbuf[:, BLOCK:] = pos_ref[...]
    qi = lax.broadcasted_iota(jnp.int32, (BLOCK, BLOCK), 0)
    kc = lax.broadcasted_iota(jnp.int32, (BLOCK, BLOCK), 1)
    from_prev = kc > qi
    n_seq_tiles = pl.num_programs(1)
    tile = b * n_seq_tiles + j

    def rel_positions(pos_row, t, i):
        return pos_row[:, i * BLOCK:(i + 2) * BLOCK] - spos_ref[t * TM + i * BLOCK]

    refresh = need[0] != 0

    def refresh_bias(i):
        pk = pbuf[:, i * BLOCK:(i + 2) * BLOCK]
        pq = pk[:, BLOCK:]
        pq_col = jnp.transpose(jnp.broadcast_to(pq, (BLOCK, BLOCK)))
        no_prev = (j == 0) if i == 0 else None
        for hd in range(N_HEADS):
            biasm[i * N_HEADS + hd] = jnp.zeros((BLOCK, BLOCK), jnp.float32)
        for side in range(2):
            dist = pq_col - pk[:, side * BLOCK:(side + 1) * BLOCK]
            n = jnp.maximum(dist, 0)
            nf = jnp.maximum(n, 1).astype(jnp.float32)
            large = MAX_EXACT + (jnp.log(nf / MAX_EXACT) / math.log(MAX_DISTANCE / MAX_EXACT)
                                 * (N_BUCKETS - MAX_EXACT)).astype(jnp.int32)
            large = jnp.minimum(large, N_BUCKETS - 1)
            bucket = jnp.where(n < MAX_EXACT, n, large)
            use = from_prev if side == 0 else jnp.logical_not(from_prev)
            for hd in range(N_HEADS):
                acc = jnp.zeros((BLOCK, BLOCK), jnp.float32)
                for bk in range(N_BUCKETS):
                    acc = jnp.where(bucket == bk, relb_ref[bk * N_HEADS + hd], acc)
                if side == 0 and i == 0:
                    acc = jnp.where(no_prev, NEG, acc)
                slot = i * N_HEADS + hd
                biasm[slot] = jnp.where(use, acc, biasm[slot])
        key[i] = rel_positions(pbuf[...], tile, i)
        flag[i] = no_prev.astype(jnp.int32) if i == 0 else 0

    for i in range(nblk):
        @pl.when(refresh)
        def _():
            changed = jnp.where(rel_positions(pbuf[...], tile, i) != key[i], 1.0, 0.0)
            stale = (jnp.max(changed) != 0) | (flag[i] != 0)
            if i == 0:
                stale = stale | (j == 0)
            pl.when(stale)(functools.partial(refresh_bias, i))

    nxt_tile = jnp.minimum(tile + 1, pl.num_programs(0) * n_seq_tiles - 1)
    nxt_row = jnp.concatenate([pos_ref[:, TM - BLOCK:], posn_ref[...]], axis=1)
    nxt_changed = [jnp.where(rel_positions(nxt_row, nxt_tile, i) != key[i], 1.0, 0.0) for i in range(nblk)]
    any_changed = jnp.max(functools.reduce(jnp.maximum, nxt_changed))
    any_flag = functools.reduce(jnp.maximum, [flag[i] for i in range(nblk)])
    need[0] = ((any_changed != 0) | (any_flag != 0) | (j + 1 == n_seq_tiles)).astype(jnp.int32)

    x = x_ref[...]
    sh1, sc1, g1 = mod_ref[0:1, :], mod_ref[1:2, :], mod_ref[2:3, :]
    sh2, sc2 = mod_ref[3:4, :], mod_ref[4:5, :]

    xn = x * lax.rsqrt(jnp.mean(x * x, axis=-1, keepdims=True) + EPS)
    h = (xn * (n1g_ref[...] * (1.0 + sc1)) + sh1).astype(bf16)

    lane = lax.broadcasted_iota(jnp.int32, (1, LANES), 1)
    lo = lane < HEAD_DIM
    q = _dot(h, win_ref[:, Q0:K0]) * (HEAD_DIM ** -0.5)
    lo4 = jnp.concatenate([lo] * (ATTN_Q // LANES), axis=1)
    q_lo = jnp.where(lo4, q, 0.0).astype(bf16)
    q_hi = jnp.where(lo4, 0.0, q).astype(bf16)
    kv = _dot(h, win_ref[:, K0:GU0])
    kbuf[BLOCK:, :] = kv[:, 0:LANES]
    vbuf[BLOCK:, :] = kv[:, LANES:]
    kf = kbuf[...]
    vf = vbuf[...]
    kr = pltpu.roll(kf, HEAD_DIM, 1)
    vr = pltpu.roll(vf, HEAD_DIM, 1)
    kd = (jnp.where(lo, kf, kr).astype(bf16), jnp.where(lo, kr, kf).astype(bf16))
    v_lo = (jnp.where(lo, vf, 0.0).astype(bf16), jnp.where(lo, vr, 0.0).astype(bf16))
    v_hi = (jnp.where(lo, 0.0, vr).astype(bf16), jnp.where(lo, 0.0, vf).astype(bf16))

    ones_blk = jnp.ones((2 * BLOCK, LANES), bf16)

    def attend(i):
        rows = slice(i * BLOCK, (i + 1) * BLOCK)
        band = slice(i * BLOCK, (i + 2) * BLOCK)
        grp = N_HEADS // N_KV_HEADS
        for kvh in range(N_KV_HEADS):
            heads = range(kvh * grp, (kvh + 1) * grp)
            qs = jnp.concatenate([(q_lo if hd % 2 == 0 else q_hi)[rows, hd // 2 * LANES:(hd // 2 + 1) * LANES]
                                  for hd in heads], axis=0)
            s2 = _dot_nt(qs, kd[kvh][band])
            for g, hd in enumerate(heads):
                sh = s2[g * BLOCK:(g + 1) * BLOCK]
                s = jnp.where(from_prev, sh[:, 0:BLOCK], sh[:, BLOCK:]) + biasm[i * N_HEADS + hd]
                sbuf[hd] = s
                m = jnp.maximum(jnp.max(s, axis=-1, keepdims=True), sinks_ref[hd])
                mbuf[hd] = jnp.broadcast_to(m, (BLOCK, BLOCK))
        for kvh in range(N_KV_HEADS):
            outs = {}
            for half, vv in enumerate((v_lo, v_hi)):
                hds = (kvh * grp + half, kvh * grp + half + 2)
                p2 = []
                for hd in hds:
                    p = jnp.exp(sbuf[hd] - mbuf[hd])
                    p2.append(jnp.concatenate([jnp.where(from_prev, p, 0.0), jnp.where(from_prev, 0.0, p)],
                                              axis=1).astype(bf16))
                r = _dot(jnp.concatenate(p2, axis=0), jnp.concatenate([vv[kvh][band], ones_blk], axis=1))
                for n, hd in enumerate(hds):
                    rh = r[n * BLOCK:(n + 1) * BLOCK]
                    den = rh[:, LANES:] + jnp.exp(sinks_ref[hd] - mbuf[hd])
                    outs[hd] = rh[:, 0:LANES] * (1.0 / den)
            for pr in (kvh * grp // 2, kvh * grp // 2 + 1):
                ya[rows, pr * LANES:(pr + 1) * LANES] = (outs[2 * pr] + outs[2 * pr + 1]).astype(bf16)

    u = _gelu_tanh(_dot(h, win_ref[:, GU0:GV0]))
    attend(0)
    vg = _gelu_tanh(_dot(h, win_ref[:, GV0:GA0]))
    mu = jnp.mean(vg, axis=-1, keepdims=True)
    vc = vg - mu
    var = jnp.mean(vc * vc, axis=-1, keepdims=True)
    vn = (vc * lax.rsqrt(var + EPS) * lng_ref[...] + lnb_ref[...]).astype(bf16)
    attend(1)
    gate_a = _sigmoid(_dot(h, win_ref[:, GA0:GB0]))
    attend(2)
    ti = lax.broadcasted_iota(jnp.int32, (GM_CHUNK, GM_CHUNK), 0)
    si = lax.broadcasted_iota(jnp.int32, (GM_CHUNK, GM_CHUNK), 1)
    tril = si <= ti
    for g in range(GM_GROUPS):
        wg = jnp.where(tril, ws_ref[g], 0.0).astype(bf16)
        cols = slice(g * LANES, (g + 1) * LANES)
        for cidx in range(TM // GM_CHUNK):
            rows = slice(cidx * GM_CHUNK, (cidx + 1) * GM_CHUNK)
            sv = _dot(wg, vn[rows, cols]) + bsb_ref[g]
            yb[rows, cols] = (u[rows, cols] * sv).astype(bf16)
    gate_b = _sigmoid(_dot(h, win_ref[:, GB0:IN_END]))
    attend(3)

    kbuf[0:BLOCK, :] = kv[TM - BLOCK:, 0:LANES]
    vbuf[0:BLOCK, :] = kv[TM - BLOCK:, LANES:]
    pbuf[:, 0:BLOCK] = pos_ref[:, TM - BLOCK:]

    h2_parts, logit_parts = [], []
    for grp in range(TAIL_SPLIT):
        tok = slice(grp * TM // TAIL_SPLIT, (grp + 1) * TM // TAIL_SPLIT)
        merged = gate_a[tok] * _dot(ya[tok, :], pa_ref[...]) + gate_b[tok] * _dot(yb[tok, :], pb_ref[...])
        x1 = x[tok] + g1 * _dot(merged.astype(bf16), wo_ref[...])
        x1_ref[tok, :] = x1

        xn2 = x1 * lax.rsqrt(jnp.mean(x1 * x1, axis=-1, keepdims=True) + EPS)
        h2_parts.append((xn2 * (n2g_ref[...] * (1.0 + sc2)) + sh2).astype(bf16))
        logit_parts.append(_dot(h2_parts[-1], wr_ref[...]) + br_ref[...])
    h2 = jnp.concatenate(h2_parts, axis=0)
    logits = jnp.concatenate(logit_parts, axis=0)
    lt = jnp.transpose(logits)[0:ROUTER_ROWS, :]
    row = lax.broadcasted_iota(jnp.int32, (ROUTER_ROWS, TM), 0)
    row_f = row.astype(jnp.float32)
    big = float(ROUTER_ROWS)
    is_grp = row < N_GROUPS
    lg = jnp.where(is_grp, lt, NEG)
    lg_max = jnp.max(lg, axis=0, keepdims=True)
    g_idx = jnp.min(jnp.where(lg == lg_max, row_f, big), axis=0, keepdims=True)
    p_g = 1.0 / jnp.sum(jnp.where(is_grp, jnp.exp(lg - lg_max), 0.0), axis=0, keepdims=True)
    row_grp = jnp.floor((row_f - EXPERT_LANE0) * (1.0 / EXPERTS_PER_GROUP))
    in_grp = (row >= EXPERT_LANE0) & (row < EXPERT_LANE0 + N_EXPERTS) & (row_grp == g_idx)
    le = jnp.where(in_grp, lt, NEG)
    m1 = jnp.max(le, axis=0, keepdims=True)
    i1 = jnp.min(jnp.where(le == m1, row_f, big), axis=0, keepdims=True)
    oh1 = row_f == i1
    le2 = jnp.where(oh1, NEG, le)
    m2 = jnp.max(le2, axis=0, keepdims=True)
    i2 = jnp.min(jnp.where(le2 == m2, row_f, big), axis=0, keepdims=True)
    oh2 = row_f == i2
    e2 = jnp.exp(m2 - m1)
    w1 = p_g / (1.0 + e2)
    w2 = p_g * e2 / (1.0 + e2)

    oh = jnp.where(oh1, 1.0, jnp.where(oh2, 1.0, 0.0))
    n_e = jnp.sum(oh, axis=1, keepdims=True)
    cnt_ref[...] = n_e
    padded = jnp.floor((n_e + (CHUNK - 1)) * (1.0 / CHUNK)) * CHUNK
    dst_row = lax.broadcasted_iota(jnp.int32, (ROUTER_ROWS, ROUTER_ROWS), 0)
    src_row = lax.broadcasted_iota(jnp.int32, (ROUTER_ROWS, ROUTER_ROWS), 1)
    lower = jnp.where(src_row < dst_row, 1.0, 0.0).astype(bf16)
    run0 = _dot(lower, jnp.broadcast_to(padded, (ROUTER_ROWS, TM)).astype(bf16))
    slot = _dot(oh.astype(bf16), strict[...]) + run0
    pos1 = jnp.sum(jnp.where(oh1, slot, 0.0), axis=0, keepdims=True)
    pos2 = jnp.sum(jnp.where(oh2, slot, 0.0), axis=0, keepdims=True)
    rrow = lax.broadcasted_iota(jnp.int32, (ROUTE_ROWS, TM), 0)
    rw_ref[...] = jnp.where(rrow == 0, w1, jnp.where(rrow == 1, w2,
                            jnp.where(rrow == 2, pos1, jnp.where(rrow == 3, pos2, 0.0))))

    p1 = pos1.astype(jnp.int32)
    p2 = pos2.astype(jnp.int32)
    sp = lax.broadcasted_iota(jnp.int32, (SORT_ROWS, TM), 0)
    perm = jnp.where(sp == p1, 1.0, jnp.where(sp == p2, 1.0, 0.0)).astype(bf16)
    xs_ref[...] = _pack_bf16_pairs(_dot(perm, h2))


def _mixer(x, mod, positions, rel_bias, n1g, w_in, sinks, lng, lnb, w_s, bsb, p_a, p_b, w_o, n2g, w_r, b_r):
    bsz, seq, d = x.shape
    nj = seq // TM
    const = lambda *shape: pl.BlockSpec(shape, lambda b, j: (0,) * len(shape), pipeline_mode=pl.Buffered(1))
    smem = pl.BlockSpec(memory_space=pltpu.SMEM)
    tile = lambda w: pl.BlockSpec((None, TM, w), lambda b, j: (b, j, 0))
    return pl.pallas_call(
        _mixer_kernel,
        grid=(bsz, nj),
        in_specs=[smem, smem, smem,
                  tile(d),
                  pl.BlockSpec((None, 6, d), lambda b, j: (b, 0, 0)),
                  pl.BlockSpec((None, None, 1, TM), lambda b, j: (b, j, 0, 0)),
                  pl.BlockSpec((None, None, 1, TM),
                               lambda b, j: (jnp.minimum(b * nj + j + 1, bsz * nj - 1) // nj,
                                             jnp.minimum(b * nj + j + 1, bsz * nj - 1) % nj, 0, 0)),
                  const(1, d), const(d, IN_END), const(1, GM_WIDTH), const(1, GM_WIDTH),
                  const(GM_GROUPS, GM_CHUNK, GM_CHUNK), const(GM_GROUPS, GM_CHUNK, LANES),
                  const(ATTN_Q, d), const(GM_WIDTH, d), const(d, d), const(1, d),
                  const(d, ROUTER_LANES), const(1, ROUTER_LANES)],
        out_specs=[tile(d),
                   pl.BlockSpec((SORT_ROWS, d // 2), lambda b, j: (b * nj + j, 0)),
                   pl.BlockSpec((ROUTE_ROWS, TM), lambda b, j: (0, b * nj + j)),
                   pl.BlockSpec((None, ROUTER_ROWS, 1), lambda b, j: (b * nj + j, 0, 0))],
        out_shape=[jax.ShapeDtypeStruct((bsz, seq, d), jnp.float32),
                   jax.ShapeDtypeStruct((bsz * nj * SORT_ROWS, d // 2), jnp.uint32),
                   jax.ShapeDtypeStruct((ROUTE_ROWS, bsz * seq), jnp.float32),
                   jax.ShapeDtypeStruct((bsz * nj, ROUTER_ROWS, 1), jnp.float32)],
        scratch_shapes=[pltpu.VMEM((TM + BLOCK, LANES), jnp.float32),
                        pltpu.VMEM((TM + BLOCK, LANES), jnp.float32),
                        pltpu.VMEM((1, TM + BLOCK), jnp.int32),
                        pltpu.VMEM((TM // BLOCK * N_HEADS, BLOCK, BLOCK), jnp.float32),
                        pltpu.VMEM((TM // BLOCK, 1, 2 * BLOCK), jnp.int32),
                        pltpu.VMEM((TM, ATTN_Q), jnp.bfloat16),
                        pltpu.VMEM((TM, GM_WIDTH), jnp.bfloat16),
                        pltpu.VMEM((TM, TM), jnp.bfloat16),
                        pltpu.VMEM((N_HEADS, BLOCK, BLOCK), jnp.float32),
                        pltpu.VMEM((N_HEADS, BLOCK, BLOCK), jnp.float32),
                        pltpu.SMEM((TM // BLOCK,), jnp.int32),
                        pltpu.SMEM((1,), jnp.int32)],
        compiler_params=pltpu.CompilerParams(dimension_semantics=("arbitrary", "arbitrary"),
                                             vmem_limit_bytes=MIXER_VMEM),
        name="mixer",
    )(rel_bias.reshape(-1), sinks, positions.reshape(-1), x, mod, positions.reshape(bsz, nj, 1, TM),
      positions.reshape(bsz, nj, 1, TM), n1g, w_in, lng, lnb,
      w_s, bsb, p_a, p_b, w_o, n2g, w_r, b_r)


def _gather_chunks(idx_ref, idx0, n, src_hbm, dst, sem, inline=False, split_priority=False):
    def issue(c, carry, priority=0):
        src = pl.multiple_of(idx_ref[idx0 + c] * CHUNK, CHUNK)
        pltpu.make_async_copy(src_hbm.at[pl.ds(src, CHUNK)],
                              dst.at[pl.ds(pl.multiple_of(c * CHUNK, CHUNK), CHUNK)], sem).start(priority)
        return carry

    if inline:
        for c in range(n):
            issue(c, 0, c % 2 if split_priority else 0)
    else:
        lax.fori_loop(0, n, issue, 0, unroll=GATHER_UNROLL)


def _wait_chunks(n, src_hbm, dst, sem):
    pltpu.make_async_copy(src_hbm.at[pl.ds(0, n * CHUNK)], dst, sem).wait()


def _expert_kernel(src_ref, be_ref, ne_ref, ws_ref, nq_ref, lb_ref, xs_hbm, wg_hbm, wu_hbm, wd_hbm, yr_ref,
                   xbuf, wgf, wuf, wdf, wgb, wub, wdb, sem, wsem):
    i = pl.program_id(0)
    last = lb_ref[0]
    slot = i % GATHER_SLOTS
    ahead = jnp.minimum(i + GATHER_SLOTS - 1, last)
    ahead_slot = (i + GATHER_SLOTS - 1) % GATHER_SLOTS

    def weight_copies(e, w):
        return [pltpu.make_async_copy(hbm.at[e], buf.at[w], wsem.at[w])
                for hbm, buf in ((wg_hbm, wgf), (wu_hbm, wuf), (wd_hbm, wdf))]

    @pl.when(i == 0)
    def _():
        for s in range(GATHER_SLOTS - 1):
            _gather_chunks(src_ref, jnp.minimum(s, last) * BLOCK_CHUNKS, BLOCK_CHUNKS, xs_hbm, xbuf.at[s], sem.at[s])
        for cp in weight_copies(be_ref[0], 0):
            cp.start()

    @pl.when(i <= last)
    def _():
        changed = (i == 0) | (be_ref[i] != be_ref[jnp.maximum(i - 1, 0)])

        @pl.when(changed)
        def _():
            w = ws_ref[i]
            for cp in weight_copies(be_ref[i], w):
                cp.wait()
            wgb[...] = wgf[w].astype(jnp.bfloat16)
            wub[...] = wuf[w].astype(jnp.bfloat16)
            wdb[...] = wdf[w].astype(jnp.bfloat16)

            @pl.when(ne_ref[i] >= 0)
            def _():
                for cp in weight_copies(ne_ref[i], 1 - w):
                    cp.start(priority=1)

        _wait_chunks(BLOCK_CHUNKS, xs_hbm, xbuf.at[slot], sem.at[slot])

        def compute(rows):
            xb = _unpack_bf16_pairs(xbuf[slot, 0:rows, :])
            hg = _dot(xb, wgb[...])
            hu = _dot(xb, wub[...])
            hid = (hg * _sigmoid(hg) * hu).astype(jnp.bfloat16)
            yr_ref[0:rows, :] = _pack_bf16_pairs(_dot(hid, wdb[...]).astype(jnp.bfloat16).astype(jnp.float32))
            if rows < BM:
                yr_ref[rows:, :] = jnp.zeros((BM - rows, yr_ref.shape[1]), yr_ref.dtype)
            _gather_chunks(src_ref, ahead * BLOCK_CHUNKS, BLOCK_CHUNKS, xs_hbm, xbuf.at[ahead_slot],
                           sem.at[ahead_slot], inline=True)

        for quarters in range(1, BLOCK_QUARTERS + 1):
            pl.when(nq_ref[i] == quarters)(functools.partial(compute, quarters * BM // BLOCK_QUARTERS))

        @pl.when(i == last)
        def _():
            for s in range(1, GATHER_SLOTS):
                other = (i + s) % GATHER_SLOTS
                _wait_chunks(BLOCK_CHUNKS, xs_hbm, xbuf.at[other], sem.at[other])

    @pl.when(i > last)
    def _():
        yr_ref[...] = jnp.zeros_like(yr_ref)


def _experts(src_chunk, block_e, next_e, w_slot, quarters, last_blk, xs, w_gate, w_up, w_down):
    d = w_gate.shape[1]
    n_rows = src_chunk.shape[0] * CHUNK
    hbm = pl.BlockSpec(memory_space=pl.ANY)
    return pl.pallas_call(
        _expert_kernel,
        grid_spec=pltpu.PrefetchScalarGridSpec(
            num_scalar_prefetch=6,
            grid=(n_rows // BM,),
            in_specs=[hbm, hbm, hbm, hbm],
            out_specs=pl.BlockSpec((BM, d // 2), lambda i, *_: (i, 0)),
            scratch_shapes=[pltpu.VMEM((GATHER_SLOTS, BM, d // 2), jnp.uint32),
                            pltpu.VMEM((2, d, D_EXPERT), jnp.float32),
                            pltpu.VMEM((2, d, D_EXPERT), jnp.float32),
                            pltpu.VMEM((2, D_EXPERT, d), jnp.float32),
                            pltpu.VMEM((d, D_EXPERT), jnp.bfloat16),
                            pltpu.VMEM((d, D_EXPERT), jnp.bfloat16),
                            pltpu.VMEM((D_EXPERT, d), jnp.bfloat16),
                            pltpu.SemaphoreType.DMA((GATHER_SLOTS,)),
                            pltpu.SemaphoreType.DMA((2,))]),
        out_shape=jax.ShapeDtypeStruct((n_rows, d // 2), jnp.uint32),
        compiler_params=pltpu.CompilerParams(dimension_semantics=("arbitrary",),
                                             vmem_limit_bytes=EXPERTS_VMEM),
        name="experts",
    )(src_chunk, block_e, next_e, w_slot, quarters, last_blk, xs, w_gate, w_up, w_down)


def _combine_kernel(src_ref, x1_ref, rw_ref, mod_ref, fg_ref, yr_hbm, out_ref, ybuf, sem, *, final_norm):
    i = pl.program_id(0)
    slot = i % 2

    @pl.when(i == 0)
    def _():
        _gather_chunks(src_ref, 0, SORT_CHUNKS, yr_hbm, ybuf.at[0], sem.at[0])

    rows = jnp.concatenate([rw_ref[...], jnp.zeros((LANES - ROUTE_ROWS, TM), jnp.float32)], axis=0)
    rw = jnp.transpose(rows)
    sp = lax.broadcasted_iota(jnp.int32, (1, SORT_ROWS), 1)
    p1 = rw[:, 2:3].astype(jnp.int32)
    p2 = rw[:, 3:4].astype(jnp.int32)
    wmat = jnp.where(sp == p1, rw[:, 0:1], jnp.where(sp == p2, rw[:, 1:2], 0.0)).astype(jnp.bfloat16)
    nxt = jnp.minimum(i + 1, pl.num_programs(0) - 1)
    _gather_chunks(src_ref, nxt * SORT_CHUNKS, SORT_CHUNKS, yr_hbm, ybuf.at[1 - slot], sem.at[1 - slot],
                   inline=True, split_priority=True)
    _wait_chunks(SORT_CHUNKS, yr_hbm, ybuf.at[slot], sem.at[slot])
    moe = _dot(wmat, _unpack_bf16_pairs(ybuf[slot]))
    x2 = x1_ref[...] + mod_ref[5:6, :] * moe
    if final_norm:
        x2 = x2 * lax.rsqrt(jnp.mean(x2 * x2, axis=-1, keepdims=True) + EPS) * fg_ref[...]
    out_ref[...] = x2

    @pl.when(i == pl.num_programs(0) - 1)
    def _():
        _wait_chunks(SORT_CHUNKS, yr_hbm, ybuf.at[1 - slot], sem.at[1 - slot])


def _combine(src_chunk, x1, rw, mod, final_g, yr, seq, final_norm):
    t, d = x1.shape
    per_seq = seq // TM
    return pl.pallas_call(
        functools.partial(_combine_kernel, final_norm=final_norm),
        grid_spec=pltpu.PrefetchScalarGridSpec(
            num_scalar_prefetch=1,
            grid=(t // TM,),
            in_specs=[pl.BlockSpec((TM, d), lambda i, src: (i, 0)),
                      pl.BlockSpec((ROUTE_ROWS, TM), lambda i, src: (0, i)),
                      pl.BlockSpec((None, 6, d), lambda i, src: (i // per_seq, 0, 0)),
                      pl.BlockSpec((1, d), lambda i, src: (0, 0)),
                      pl.BlockSpec(memory_space=pl.ANY)],
            out_specs=pl.BlockSpec((TM, d), lambda i, src: (i, 0)),
            scratch_shapes=[pltpu.VMEM((2, SORT_ROWS, d // 2), jnp.uint32),
                            pltpu.SemaphoreType.DMA((2,))]),
        out_shape=jax.ShapeDtypeStruct((t, d), jnp.float32),
        compiler_params=pltpu.CompilerParams(dimension_semantics=("arbitrary",),
                                             vmem_limit_bytes=COMBINE_VMEM),
        name="combine",
    )(src_chunk, x1, rw, mod, final_g.reshape(1, d), yr)


def kernel(x, c, positions, rel_bias, w_ada, b_ada, norm1_g, w_in, sinks, gm_ln_g, gm_ln_b, gm_w_s, gm_b_s,
           p_a, p_b, w_o, norm2_g, w_router_g, b_router_g, w_router_e, b_router_e, w_gate, w_up, w_down,
           final_g):
    bsz, seq, d = x.shape
    t = bsz * seq
    bf16 = jnp.bfloat16
    depth = w_ada.shape[0]
    n_tiles = t // TM
    n_chunks = (t * TOP_K + n_tiles * N_EXPERTS * (CHUNK - 1)) // CHUNK + N_EXPERTS * (BLOCK_CHUNKS - 1)
    n_chunks = -(-n_chunks // BLOCK_CHUNKS) * BLOCK_CHUNKS
    i32 = jnp.int32
    for l in range(depth):
        mod = _adaln_mod(c, w_ada[l], b_ada[l]).reshape(bsz, 6, d)
        pad = ROUTER_LANES - N_GROUPS - N_EXPERTS
        w_r = jnp.concatenate([w_router_g[l], w_router_e[l], jnp.zeros((d, pad), jnp.float32)], axis=1)
        b_r = jnp.concatenate([b_router_g[l], b_router_e[l], jnp.zeros((pad,), jnp.float32)]).reshape(1, -1)
        bsb = jnp.broadcast_to(gm_b_s[l][:, :, None], (GM_GROUPS, GM_CHUNK, LANES))
        x1, xs, rw, cnt = _mixer(
            x, mod, positions, rel_bias, norm1_g[l].reshape(1, d), w_in[l].astype(bf16), sinks[l],
            gm_ln_g[l].reshape(1, -1), gm_ln_b[l].reshape(1, -1), gm_w_s[l], bsb,
            p_a[l].astype(bf16), p_b[l].astype(bf16), w_o[l].astype(bf16), norm2_g[l].reshape(1, d),
            w_r.astype(bf16), b_r)

        n = cnt[:, EXPERT_LANE0:EXPERT_LANE0 + N_EXPERTS, 0].astype(i32)
        nch = (n + CHUNK - 1) // CHUNK
        run0_tile = jnp.cumsum(nch, axis=1) - nch
        run0_exp = jnp.cumsum(nch, axis=0) - nch
        tot = jnp.sum(nch, axis=0)
        seg = (tot + BLOCK_CHUNKS - 1) // BLOCK_CHUNKS * BLOCK_CHUNKS
        seg_end = jnp.cumsum(seg)
        seg0 = seg_end - seg

        pos_tile = jnp.arange(n_tiles, dtype=i32)[:, None] * SORT_CHUNKS + run0_tile
        pos_exp = seg0[None, :] + run0_exp

        j = jnp.arange(n_chunks, dtype=i32)[:, None]
        e_j = jnp.minimum(jnp.sum(j >= seg_end[None, :], axis=1), N_EXPERTS - 1)
        tables = jnp.concatenate([pos_exp.T, (pos_exp + nch).T, (pos_tile - pos_exp).T], axis=1)
        picked = jnp.dot(jax.nn.one_hot(e_j, N_EXPERTS, dtype=jnp.float32), tables.astype(jnp.float32),
                         precision=lax.Precision.HIGHEST).astype(i32)
        lo_j, hi_j, shift_j = picked[:, :n_tiles], picked[:, n_tiles:2 * n_tiles], picked[:, 2 * n_tiles:]
        inside = (j >= lo_j) & (j < hi_j)
        src_j = jnp.sum(jnp.where(inside, j + shift_j, 0), axis=1).astype(i32)
        blk0 = jnp.arange(n_chunks // BLOCK_CHUNKS, dtype=i32) * BLOCK_CHUNKS
        block_e = jnp.minimum(jnp.sum(blk0[:, None] >= seg_end[None, :], axis=1), N_EXPERTS - 1).astype(i32)
        last_blk = (jnp.maximum(seg_end[-1] // BLOCK_CHUNKS, 1) - 1).astype(i32).reshape(1)
        experts = jnp.arange(N_EXPERTS, dtype=i32)
        later = (experts[None, :] > block_e[:, None]) & (seg[None, :] > 0)
        next_e = jnp.min(jnp.where(later, experts[None, :], N_EXPERTS), axis=1)
        next_e = jnp.where(next_e < N_EXPERTS, next_e, -1).astype(i32)
        w_slot = (jnp.sum((experts[None, :] < block_e[:, None]) & (seg[None, :] > 0), axis=1) % 2).astype(i32)

        q = jnp.arange(SORT_CHUNKS, dtype=i32)[None, :, None]
        in_run = (q >= run0_tile[:, None, :]) & (q < (run0_tile + nch)[:, None, :])
        dst_q = jnp.sum(jnp.where(in_run, q + (pos_exp - run0_tile)[:, None, :], 0), axis=2).astype(i32).reshape(-1)

        own = block_e[:, None] == experts[None, :]
        filled = jnp.sum(jnp.where(own, (seg0 + tot)[None, :], 0), axis=1) - blk0
        quarter_chunks = BLOCK_CHUNKS // BLOCK_QUARTERS
        quarters = jnp.clip((filled + quarter_chunks - 1) // quarter_chunks, 1, BLOCK_QUARTERS).astype(i32)

        yr = _experts(src_j, block_e, next_e, w_slot, quarters, last_blk, xs, w_gate[l], w_up[l], w_down[l])
        x = _combine(dst_q, x1.reshape(t, d), rw, mod, final_g, yr, seq,
                     final_norm=(l == depth - 1)).reshape(bsz, seq, d)
    return x
```

```python
import functools
import math

import jax
import jax.numpy as jnp
from jax import lax
from jax.experimental import pallas as pl
from jax.experimental.pallas import tpu as pltpu

D_MODEL = 1024
N_HEADS = 8
N_KV_HEADS = 2
HEAD_DIM = 64
BLOCK = 128
ATTN_Q = N_HEADS * HEAD_DIM
ATTN_KV = N_KV_HEADS * HEAD_DIM
N_BUCKETS = 32
MAX_EXACT = N_BUCKETS // 2
MAX_DISTANCE = 128
GM_WIDTH = 512
GM_GROUPS = 4
GM_CHUNK = 128
N_GROUPS = 4
EXPERTS_PER_GROUP = 8
N_EXPERTS = N_GROUPS * EXPERTS_PER_GROUP
TOP_K = 2
D_EXPERT = 512
EPS = 1e-6
NEG = -1e30

LANES = 128
ROUTER_LANES = LANES
EXPERT_LANE0 = N_GROUPS
TM = 512
BM = 512
ROUTER_ROWS = 48
ROUTE_ROWS = 8
CHUNK = 8
SORT_ROWS = TM * TOP_K + N_EXPERTS * CHUNK
SORT_CHUNKS = SORT_ROWS // CHUNK
BLOCK_CHUNKS = BM // CHUNK
BLOCK_QUARTERS = 4
GATHER_UNROLL = 8
GATHER_SLOTS = 3
TAIL_SPLIT = 2
MIB = 1024 * 1024
MIXER_VMEM = 44 * MIB
EXPERTS_VMEM = 32 * MIB
COMBINE_VMEM = 24 * MIB

ADALN_TN = 1024

Q0 = 0
K0 = Q0 + ATTN_Q
V0 = K0 + ATTN_KV
GU0 = V0 + ATTN_KV
GV0 = GU0 + GM_WIDTH
GA0 = GV0 + GM_WIDTH
GB0 = GA0 + D_MODEL
IN_END = GB0 + D_MODEL


def _dot(a, b):
    return jnp.dot(a, b, preferred_element_type=jnp.float32)


def _dot_nt(a, b):
    return lax.dot_general(a, b, (((1,), (1,)), ((), ())), preferred_element_type=jnp.float32)


LOG2E = math.log2(math.e)


def _gelu_tanh(x):
    c = math.sqrt(2.0 / math.pi)
    k0, k1 = -2.0 * c * LOG2E, -2.0 * c * 0.044715 * LOG2E
    return x * (1.0 / (1.0 + jnp.exp2(x * (x * x * k1 + k0))))


def _sigmoid(x):
    return 1.0 / (1.0 + jnp.exp2(x * -LOG2E))


def _pack_bf16_pairs(x):
    bits = lax.bitcast_convert_type(x, jnp.uint32)
    half = x.shape[1] // 2
    return (bits[:, 0:half] >> 16) | (bits[:, half:] & jnp.uint32(0xFFFF0000))


def _unpack_bf16_pairs(w):
    lo = lax.bitcast_convert_type(w << 16, jnp.float32)
    hi = lax.bitcast_convert_type(w & jnp.uint32(0xFFFF0000), jnp.float32)
    return jnp.concatenate([lo, hi], axis=1).astype(jnp.bfloat16)


def _adaln_kernel(c_ref, w_ref, b_ref, o_ref):
    c = c_ref[...]
    cs = c * _sigmoid(c)
    o_ref[...] = _dot(cs, w_ref[...]) + b_ref[...]


def _adaln_mod(c, w, b):
    bsz, d = c.shape
    n = w.shape[1]
    tn = ADALN_TN
    return pl.pallas_call(
        _adaln_kernel,
        grid=(n // tn,),
        in_specs=[pl.BlockSpec((bsz, d), lambda i: (0, 0)),
                  pl.BlockSpec((d, tn), lambda i: (0, i)),
                  pl.BlockSpec((1, tn), lambda i: (0, i))],
        out_specs=pl.BlockSpec((bsz, tn), lambda i: (0, i)),
        out_shape=jax.ShapeDtypeStruct((bsz, n), jnp.float32),
        compiler_params=pltpu.CompilerParams(dimension_semantics=("arbitrary",)),
        name="adaln_mod",
    )(c, w, b.reshape(1, n))


def _mixer_kernel(relb_ref, sinks_ref, spos_ref,
                  x_ref, mod_ref, pos_ref, n1g_ref, win_ref, lng_ref, lnb_ref, ws_ref, bsb_ref,
                  pa_ref, pb_ref, wo_ref, n2g_ref, wr_ref, br_ref,
                  x1_ref, xs_ref, rw_ref, cnt_ref,
                  kbuf, vbuf, pbuf, biasm, key, ya, yb, strict, sbuf, mbuf, flag):
    b = pl.program_id(0)
    j = pl.program_id(1)
    nblk = TM // BLOCK
    bf16 = jnp.bfloat16

    @pl.when((b == 0) & (j == 0))
    def _():
        key[...] = jnp.zeros_like(key)
        for i in range(nblk):
            flag[i] = 1
        tr = lax.broadcasted_iota(jnp.int32, (TM, TM), 0)
        tc = lax.broadcasted_iota(jnp.int32, (TM, TM), 1)
        strict[...] = jnp.where(tr < tc, 1.0, 0.0).astype(bf16)

    @pl.when(j == 0)
    def _():
        kbuf[0:BLOCK, :] = jnp.zeros((BLOCK, LANES), jnp.float32)
        vbuf[0:BLOCK, :] = jnp.zeros((BLOCK, LANES), jnp.float32)
        pbuf[:, 0:BLOCK] = jnp.zeros((1, BLOCK), jnp.int32)

    pbuf[:, BLOCK:] = pos_ref[...]
    qi = lax.broadcasted_iota(jnp.int32, (BLOCK, BLOCK), 0)
    kc = lax.broadcasted_iota(jnp.int32, (BLOCK, BLOCK), 1)
    from_prev = kc > qi
    rels, changed = [], []
    tile0 = (b * pl.num_programs(1) + j) * TM
    for i in range(nblk):
        pk = pbuf[:, i * BLOCK:(i + 2) * BLOCK]
        rels.append(pk - spos_ref[tile0 + i * BLOCK])
        changed.append(jnp.where(rels[i] != key[i], 1.0, 0.0))
    any_changed = jnp.max(functools.reduce(jnp.maximum, changed))
    any_flag = functools.reduce(jnp.maximum, [flag[i] for i in range(nblk)])
    refresh = (any_changed != 0) | (any_flag != 0) | (j == 0)

    def refresh_bias(i):
        pk = pbuf[:, i * BLOCK:(i + 2) * BLOCK]
        pq = pk[:, BLOCK:]
        pq_col = jnp.transpose(jnp.broadcast_to(pq, (BLOCK, BLOCK)))
        no_prev = (j == 0) if i == 0 else None
        for hd in range(N_HEADS):
            biasm[i * N_HEADS + hd] = jnp.zeros((BLOCK, BLOCK), jnp.float32)
        for side in range(2):
            dist = pq_col - pk[:, side * BLOCK:(side + 1) * BLOCK]
            n = jnp.maximum(dist, 0)
            nf = jnp.maximum(n, 1).astype(jnp.float32)
            large = MAX_EXACT + (jnp.log(nf / MAX_EXACT) / math.log(MAX_DISTANCE / MAX_EXACT)
                                 * (N_BUCKETS - MAX_EXACT)).astype(jnp.int32)
            large = jnp.minimum(large, N_BUCKETS - 1)
            bucket = jnp.where(n < MAX_EXACT, n, large)
            use = from_prev if side == 0 else jnp.logical_not(from_prev)
            for hd in range(N_HEADS):
                acc = jnp.zeros((BLOCK, BLOCK), jnp.float32)
                for bk in range(N_BUCKETS):
                    acc = jnp.where(bucket == bk, relb_ref[bk * N_HEADS + hd], acc)
                if side == 0 and i == 0:
                    acc = jnp.where(no_prev, NEG, acc)
                slot = i * N_HEADS + hd
                biasm[slot] = jnp.where(use, acc, biasm[slot])
        key[i] = rels[i]
        flag[i] = no_prev.astype(jnp.int32) if i == 0 else 0

    for i in range(nblk):
        @pl.when(refresh)
        def _():
            stale = (jnp.max(changed[i]) != 0) | (flag[i] != 0)
            if i == 0:
                stale = stale | (j == 0)
            pl.when(stale)(functools.partial(refresh_bias, i))

    x = x_ref[...]
    sh1, sc1, g1 = mod_ref[0:1, :], mod_ref[1:2, :], mod_ref[2:3, :]
    sh2, sc2 = mod_ref[3:4, :], mod_ref[4:5, :]

    xn = x * lax.rsqrt(jnp.mean(x * x, axis=-1, keepdims=True) + EPS)
    h = (xn * (n1g_ref[...] * (1.0 + sc1)) + sh1).astype(bf16)

    lane = lax.broadcasted_iota(jnp.int32, (1, LANES), 1)
    lo = lane < HEAD_DIM
    q = _dot(h, win_ref[:, Q0:K0]) * (HEAD_DIM ** -0.5)
    lo4 = jnp.concatenate([lo] * (ATTN_Q // LANES), axis=1)
    q_lo = jnp.where(lo4, q, 0.0).astype(bf16)
    q_hi = jnp.where(lo4, 0.0, q).astype(bf16)
    kv = _dot(h, win_ref[:, K0:GU0])
    kbuf[BLOCK:, :] = kv[:, 0:LANES]
    vbuf[BLOCK:, :] = kv[:, LANES:]
    kf = kbuf[...]
    vf = vbuf[...]
    kr = pltpu.roll(kf, HEAD_DIM, 1)
    vr = pltpu.roll(vf, HEAD_DIM, 1)
    kd = (jnp.where(lo, kf, kr).astype(bf16), jnp.where(lo, kr, kf).astype(bf16))
    v_lo = (jnp.where(lo, vf, 0.0).astype(bf16), jnp.where(lo, vr, 0.0).astype(bf16))
    v_hi = (jnp.where(lo, 0.0, vr).astype(bf16), jnp.where(lo, 0.0, vf).astype(bf16))

    ones_blk = jnp.ones((2 * BLOCK, LANES), bf16)

    def attend(i):
        rows = slice(i * BLOCK, (i + 1) * BLOCK)
        band = slice(i * BLOCK, (i + 2) * BLOCK)
        grp = N_HEADS // N_KV_HEADS
        for kvh in range(N_KV_HEADS):
            heads = range(kvh * grp, (kvh + 1) * grp)
            qs = jnp.concatenate([(q_lo if hd % 2 == 0 else q_hi)[rows, hd // 2 * LANES:(hd // 2 + 1) * LANES]
                                  for hd in heads], axis=0)
            s2 = _dot_nt(qs, kd[kvh][band])
            for g, hd in enumerate(heads):
                sh = s2[g * BLOCK:(g + 1) * BLOCK]
                s = jnp.where(from_prev, sh[:, 0:BLOCK], sh[:, BLOCK:]) + biasm[i * N_HEADS + hd]
                sbuf[hd] = s
                m = jnp.maximum(jnp.max(s, axis=-1, keepdims=True), sinks_ref[hd])
                mbuf[hd] = jnp.broadcast_to(m, (BLOCK, BLOCK))
        for kvh in range(N_KV_HEADS):
            outs = {}
            for half, vv in enumerate((v_lo, v_hi)):
                hds = (kvh * grp + half, kvh * grp + half + 2)
                p2 = []
                for hd in hds:
                    p = jnp.exp(sbuf[hd] - mbuf[hd])
                    p2.append(jnp.concatenate([jnp.where(from_prev, p, 0.0), jnp.where(from_prev, 0.0, p)],
                                              axis=1).astype(bf16))
                r = _dot(jnp.concatenate(p2, axis=0), jnp.concatenate([vv[kvh][band], ones_blk], axis=1))
                for n, hd in enumerate(hds):
                    rh = r[n * BLOCK:(n + 1) * BLOCK]
                    den = rh[:, LANES:] + jnp.exp(sinks_ref[hd] - mbuf[hd])
                    outs[hd] = rh[:, 0:LANES] * (1.0 / den)
            for pr in (kvh * grp // 2, kvh * grp // 2 + 1):
                ya[rows, pr * LANES:(pr + 1) * LANES] = (outs[2 * pr] + outs[2 * pr + 1]).astype(bf16)

    u = _gelu_tanh(_dot(h, win_ref[:, GU0:GV0]))
    attend(0)
    vg = _gelu_tanh(_dot(h, win_ref[:, GV0:GA0]))
    mu = jnp.mean(vg, axis=-1, keepdims=True)
    vc = vg - mu
    var = jnp.mean(vc * vc, axis=-1, keepdims=True)
    vn = (vc * lax.rsqrt(var + EPS) * lng_ref[...] + lnb_ref[...]).astype(bf16)
    attend(1)
    gate_a = _sigmoid(_dot(h, win_ref[:, GA0:GB0]))
    attend(2)
    ti = lax.broadcasted_iota(jnp.int32, (GM_CHUNK, GM_CHUNK), 0)
    si = lax.broadcasted_iota(jnp.int32, (GM_CHUNK, GM_CHUNK), 1)
    tril = si <= ti
    for g in range(GM_GROUPS):
        wg = jnp.where(tril, ws_ref[g], 0.0).astype(bf16)
        cols = slice(g * LANES, (g + 1) * LANES)
        for cidx in range(TM // GM_CHUNK):
            rows = slice(cidx * GM_CHUNK, (cidx + 1) * GM_CHUNK)
            sv = _dot(wg, vn[rows, cols]) + bsb_ref[g]
            yb[rows, cols] = (u[rows, cols] * sv).astype(bf16)
    gate_b = _sigmoid(_dot(h, win_ref[:, GB0:IN_END]))
    attend(3)

    kbuf[0:BLOCK, :] = kv[TM - BLOCK:, 0:LANES]
    vbuf[0:BLOCK, :] = kv[TM - BLOCK:, LANES:]
    pbuf[:, 0:BLOCK] = pos_ref[:, TM - BLOCK:]

    h2_parts, logit_parts = [], []
    for grp in range(TAIL_SPLIT):
        tok = slice(grp * TM // TAIL_SPLIT, (grp + 1) * TM // TAIL_SPLIT)
        merged = gate_a[tok] * _dot(ya[tok, :], pa_ref[...]) + gate_b[tok] * _dot(yb[tok, :], pb_ref[...])
        x1 = x[tok] + g1 * _dot(merged.astype(bf16), wo_ref[...])
        x1_ref[tok, :] = x1

        xn2 = x1 * lax.rsqrt(jnp.mean(x1 * x1, axis=-1, keepdims=True) + EPS)
        h2_parts.append((xn2 * (n2g_ref[...] * (1.0 + sc2)) + sh2).astype(bf16))
        logit_parts.append(_dot(h2_parts[-1], wr_ref[...]) + br_ref[...])
    h2 = jnp.concatenate(h2_parts, axis=0)
    logits = jnp.concatenate(logit_parts, axis=0)
    lt = jnp.transpose(logits)[0:ROUTER_ROWS, :]
    row = lax.broadcasted_iota(jnp.int32, (ROUTER_ROWS, TM), 0)
    row_f = row.astype(jnp.float32)
    big = float(ROUTER_ROWS)
    is_grp = row < N_GROUPS
    lg = jnp.where(is_grp, lt, NEG)
    lg_max = jnp.max(lg, axis=0, keepdims=True)
    g_idx = jnp.min(jnp.where(lg == lg_max, row_f, big), axis=0, keepdims=True)
    p_g = 1.0 / jnp.sum(jnp.where(is_grp, jnp.exp(lg - lg_max), 0.0), axis=0, keepdims=True)
    row_grp = jnp.floor((row_f - EXPERT_LANE0) * (1.0 / EXPERTS_PER_GROUP))
    in_grp = (row >= EXPERT_LANE0) & (row < EXPERT_LANE0 + N_EXPERTS) & (row_grp == g_idx)
    le = jnp.where(in_grp, lt, NEG)
    m1 = jnp.max(le, axis=0, keepdims=True)
    i1 = jnp.min(jnp.where(le == m1, row_f, big), axis=0, keepdims=True)
    oh1 = row_f == i1
    le2 = jnp.where(oh1, NEG, le)
    m2 = jnp.max(le2, axis=0, keepdims=True)
    i2 = jnp.min(jnp.where(le2 == m2, row_f, big), axis=0, keepdims=True)
    oh2 = row_f == i2
    e2 = jnp.exp(m2 - m1)
    w1 = p_g / (1.0 + e2)
    w2 = p_g * e2 / (1.0 + e2)

    oh = jnp.where(oh1, 1.0, jnp.where(oh2, 1.0, 0.0))
    n_e = jnp.sum(oh, axis=1, keepdims=True)
    cnt_ref[...] = n_e
    padded = jnp.floor((n_e + (CHUNK - 1)) * (1.0 / CHUNK)) * CHUNK
    dst_row = lax.broadcasted_iota(jnp.int32, (ROUTER_ROWS, ROUTER_ROWS), 0)
    src_row = lax.broadcasted_iota(jnp.int32, (ROUTER_ROWS, ROUTER_ROWS), 1)
    lower = jnp.where(src_row < dst_row, 1.0, 0.0).astype(bf16)
    run0 = _dot(lower, jnp.broadcast_to(padded, (ROUTER_ROWS, TM)).astype(bf16))
    slot = _dot(oh.astype(bf16), strict[...]) + run0
    pos1 = jnp.sum(jnp.where(oh1, slot, 0.0), axis=0, keepdims=True)
    pos2 = jnp.sum(jnp.where(oh2, slot, 0.0), axis=0, keepdims=True)
    rrow = lax.broadcasted_iota(jnp.int32, (ROUTE_ROWS, TM), 0)
    rw_ref[...] = jnp.where(rrow == 0, w1, jnp.where(rrow == 1, w2,
                            jnp.where(rrow == 2, pos1, jnp.where(rrow == 3, pos2, 0.0))))

    p1 = pos1.astype(jnp.int32)
    p2 = pos2.astype(jnp.int32)
    sp = lax.broadcasted_iota(jnp.int32, (SORT_ROWS, TM), 0)
    perm = jnp.where(sp == p1, 1.0, jnp.where(sp == p2, 1.0, 0.0)).astype(bf16)
    xs_ref[...] = _pack_bf16_pairs(_dot(perm, h2))


def _mixer(x, mod, positions, rel_bias, n1g, w_in, sinks, lng, lnb, w_s, bsb, p_a, p_b, w_o, n2g, w_r, b_r):
    bsz, seq, d = x.shape
    nj = seq // TM
    const = lambda *shape: pl.BlockSpec(shape, lambda b, j: (0,) * len(shape), pipeline_mode=pl.Buffered(1))
    smem = pl.BlockSpec(memory_space=pltpu.SMEM)
    tile = lambda w: pl.BlockSpec((None, TM, w), lambda b, j: (b, j, 0))
    return pl.pallas_call(
        _mixer_kernel,
        grid=(bsz, nj),
        in_specs=[smem, smem, smem,
                  tile(d),
                  pl.BlockSpec((None, 6, d), lambda b, j: (b, 0, 0)),
                  pl.BlockSpec((None, None, 1, TM), lambda b, j: (b, j, 0, 0)),
                  const(1, d), const(d, IN_END), const(1, GM_WIDTH), const(1, GM_WIDTH),
                  const(GM_GROUPS, GM_CHUNK, GM_CHUNK), const(GM_GROUPS, GM_CHUNK, LANES),
                  const(ATTN_Q, d), const(GM_WIDTH, d), const(d, d), const(1, d),
                  const(d, ROUTER_LANES), const(1, ROUTER_LANES)],
        out_specs=[tile(d),
                   pl.BlockSpec((SORT_ROWS, d // 2), lambda b, j: (b * nj + j, 0)),
                   pl.BlockSpec((ROUTE_ROWS, TM), lambda b, j: (0, b * nj + j)),
                   pl.BlockSpec((None, ROUTER_ROWS, 1), lambda b, j: (b * nj + j, 0, 0))],
        out_shape=[jax.ShapeDtypeStruct((bsz, seq, d), jnp.float32),
                   jax.ShapeDtypeStruct((bsz * nj * SORT_ROWS, d // 2), jnp.uint32),
                   jax.ShapeDtypeStruct((ROUTE_ROWS, bsz * seq), jnp.float32),
                   jax.ShapeDtypeStruct((bsz * nj, ROUTER_ROWS, 1), jnp.float32)],
        scratch_shapes=[pltpu.VMEM((TM + BLOCK, LANES), jnp.float32),
                        pltpu.VMEM((TM + BLOCK, LANES), jnp.float32),
                        pltpu.VMEM((1, TM + BLOCK), jnp.int32),
                        pltpu.VMEM((TM // BLOCK * N_HEADS, BLOCK, BLOCK), jnp.float32),
                        pltpu.VMEM((TM // BLOCK, 1, 2 * BLOCK), jnp.int32),
                        pltpu.VMEM((TM, ATTN_Q), jnp.bfloat16),
                        pltpu.VMEM((TM, GM_WIDTH), jnp.bfloat16),
                        pltpu.VMEM((TM, TM), jnp.bfloat16),
                        pltpu.VMEM((N_HEADS, BLOCK, BLOCK), jnp.float32),
                        pltpu.VMEM((N_HEADS, BLOCK, BLOCK), jnp.float32),
                        pltpu.SMEM((TM // BLOCK,), jnp.int32)],
        compiler_params=pltpu.CompilerParams(dimension_semantics=("arbitrary", "arbitrary"),
                                             vmem_limit_bytes=MIXER_VMEM),
        name="mixer",
    )(rel_bias.reshape(-1), sinks, positions.reshape(-1), x, mod, positions.reshape(bsz, nj, 1, TM), n1g, w_in, lng, lnb,
      w_s, bsb, p_a, p_b, w_o, n2g, w_r, b_r)


def _gather_chunks(idx_ref, idx0, n, src_hbm, dst, sem, inline=False, split_priority=False):
    def issue(c, carry, priority=0):
        src = pl.multiple_of(idx_ref[idx0 + c] * CHUNK, CHUNK)
        pltpu.make_async_copy(src_hbm.at[pl.ds(src, CHUNK)],
                              dst.at[pl.ds(pl.multiple_of(c * CHUNK, CHUNK), CHUNK)], sem).start(priority)
        return carry

    if inline:
        for c in range(n):
            issue(c, 0, c % 2 if split_priority else 0)
    else:
        lax.fori_loop(0, n, issue, 0, unroll=GATHER_UNROLL)


def _wait_chunks(n, src_hbm, dst, sem):
    pltpu.make_async_copy(src_hbm.at[pl.ds(0, n * CHUNK)], dst, sem).wait()


def _expert_kernel(src_ref, be_ref, ne_ref, ws_ref, nq_ref, lb_ref, xs_hbm, wg_hbm, wu_hbm, wd_hbm, yr_ref,
                   xbuf, wgf, wuf, wdf, wgb, wub, wdb, sem, wsem):
    i = pl.program_id(0)
    last = lb_ref[0]
    slot = i % GATHER_SLOTS
    ahead = jnp.minimum(i + GATHER_SLOTS - 1, last)
    ahead_slot = (i + GATHER_SLOTS - 1) % GATHER_SLOTS

    def weight_copies(e, w):
        return [pltpu.make_async_copy(hbm.at[e], buf.at[w], wsem.at[w])
                for hbm, buf in ((wg_hbm, wgf), (wu_hbm, wuf), (wd_hbm, wdf))]

    @pl.when(i == 0)
    def _():
        for s in range(GATHER_SLOTS - 1):
            _gather_chunks(src_ref, jnp.minimum(s, last) * BLOCK_CHUNKS, BLOCK_CHUNKS, xs_hbm, xbuf.at[s], sem.at[s])
        for cp in weight_copies(be_ref[0], 0):
            cp.start()

    @pl.when(i <= last)
    def _():
        changed = (i == 0) | (be_ref[i] != be_ref[jnp.maximum(i - 1, 0)])

        @pl.when(changed)
        def _():
            w = ws_ref[i]
            for cp in weight_copies(be_ref[i], w):
                cp.wait()
            wgb[...] = wgf[w].astype(jnp.bfloat16)
            wub[...] = wuf[w].astype(jnp.bfloat16)
            wdb[...] = wdf[w].astype(jnp.bfloat16)

            @pl.when(ne_ref[i] >= 0)
            def _():
                for cp in weight_copies(ne_ref[i], 1 - w):
                    cp.start(priority=1)

        _wait_chunks(BLOCK_CHUNKS, xs_hbm, xbuf.at[slot], sem.at[slot])

        def compute(rows):
            xb = _unpack_bf16_pairs(xbuf[slot, 0:rows, :])
            hg = _dot(xb, wgb[...])
            hu = _dot(xb, wub[...])
            hid = (hg * _sigmoid(hg) * hu).astype(jnp.bfloat16)
            yr_ref[0:rows, :] = _pack_bf16_pairs(_dot(hid, wdb[...]).astype(jnp.bfloat16).astype(jnp.float32))
            if rows < BM:
                yr_ref[rows:, :] = jnp.zeros((BM - rows, yr_ref.shape[1]), yr_ref.dtype)
            _gather_chunks(src_ref, ahead * BLOCK_CHUNKS, BLOCK_CHUNKS, xs_hbm, xbuf.at[ahead_slot],
                           sem.at[ahead_slot], inline=True)

        for quarters in range(1, BLOCK_QUARTERS + 1):
            pl.when(nq_ref[i] == quarters)(functools.partial(compute, quarters * BM // BLOCK_QUARTERS))

        @pl.when(i == last)
        def _():
            for s in range(1, GATHER_SLOTS):
                other = (i + s) % GATHER_SLOTS
                _wait_chunks(BLOCK_CHUNKS, xs_hbm, xbuf.at[other], sem.at[other])

    @pl.when(i > last)
    def _():
        yr_ref[...] = jnp.zeros_like(yr_ref)


def _experts(src_chunk, block_e, next_e, w_slot, quarters, last_blk, xs, w_gate, w_up, w_down):
    d = w_gate.shape[1]
    n_rows = src_chunk.shape[0] * CHUNK
    hbm = pl.BlockSpec(memory_space=pl.ANY)
    return pl.pallas_call(
        _expert_kernel,
        grid_spec=pltpu.PrefetchScalarGridSpec(
            num_scalar_prefetch=6,
            grid=(n_rows // BM,),
            in_specs=[hbm, hbm, hbm, hbm],
            out_specs=pl.BlockSpec((BM, d // 2), lambda i, *_: (i, 0)),
            scratch_shapes=[pltpu.VMEM((GATHER_SLOTS, BM, d // 2), jnp.uint32),
                            pltpu.VMEM((2, d, D_EXPERT), jnp.float32),
                            pltpu.VMEM((2, d, D_EXPERT), jnp.float32),
                            pltpu.VMEM((2, D_EXPERT, d), jnp.float32),
                            pltpu.VMEM((d, D_EXPERT), jnp.bfloat16),
                            pltpu.VMEM((d, D_EXPERT), jnp.bfloat16),
                            pltpu.VMEM((D_EXPERT, d), jnp.bfloat16),
                            pltpu.SemaphoreType.DMA((GATHER_SLOTS,)),
                            pltpu.SemaphoreType.DMA((2,))]),
        out_shape=jax.ShapeDtypeStruct((n_rows, d // 2), jnp.uint32),
        compiler_params=pltpu.CompilerParams(dimension_semantics=("arbitrary",),
                                             vmem_limit_bytes=EXPERTS_VMEM),
        name="experts",
    )(src_chunk, block_e, next_e, w_slot, quarters, last_blk, xs, w_gate, w_up, w_down)


def _combine_kernel(src_ref, x1_ref, rw_ref, mod_ref, fg_ref, yr_hbm, out_ref, ybuf, sem, *, final_norm):
    i = pl.program_id(0)
    slot = i % 2

    @pl.when(i == 0)
    def _():
        _gather_chunks(src_ref, 0, SORT_CHUNKS, yr_hbm, ybuf.at[0], sem.at[0])

    rows = jnp.concatenate([rw_ref[...], jnp.zeros((LANES - ROUTE_ROWS, TM), jnp.float32)], axis=0)
    rw = jnp.transpose(rows)
    sp = lax.broadcasted_iota(jnp.int16, (1, SORT_ROWS), 1)
    p1 = rw[:, 2:3].astype(jnp.int16)
    p2 = rw[:, 3:4].astype(jnp.int16)
    w1 = rw[:, 0:1].astype(jnp.bfloat16)
    w2 = rw[:, 1:2].astype(jnp.bfloat16)
    wmat = jnp.where(sp == p1, w1, jnp.where(sp == p2, w2, jnp.zeros((), jnp.bfloat16)))
    nxt = jnp.minimum(i + 1, pl.num_programs(0) - 1)
    _gather_chunks(src_ref, nxt * SORT_CHUNKS, SORT_CHUNKS, yr_hbm, ybuf.at[1 - slot], sem.at[1 - slot],
                   inline=True, split_priority=True)
    _wait_chunks(SORT_CHUNKS, yr_hbm, ybuf.at[slot], sem.at[slot])
    moe = _dot(wmat, _unpack_bf16_pairs(ybuf[slot]))
    x2 = x1_ref[...] + mod_ref[5:6, :] * moe
    if final_norm:
        x2 = x2 * lax.rsqrt(jnp.mean(x2 * x2, axis=-1, keepdims=True) + EPS) * fg_ref[...]
    out_ref[...] = x2

    @pl.when(i == pl.num_programs(0) - 1)
    def _():
        _wait_chunks(SORT_CHUNKS, yr_hbm, ybuf.at[1 - slot], sem.at[1 - slot])


def _combine(src_chunk, x1, rw, mod, final_g, yr, seq, final_norm):
    t, d = x1.shape
    per_seq = seq // TM
    return pl.pallas_call(
        functools.partial(_combine_kernel, final_norm=final_norm),
        grid_spec=pltpu.PrefetchScalarGridSpec(
            num_scalar_prefetch=1,
            grid=(t // TM,),
            in_specs=[pl.BlockSpec((TM, d), lambda i, src: (i, 0)),
                      pl.BlockSpec((ROUTE_ROWS, TM), lambda i, src: (0, i)),
                      pl.BlockSpec((None, 6, d), lambda i, src: (i // per_seq, 0, 0)),
                      pl.BlockSpec((1, d), lambda i, src: (0, 0)),
                      pl.BlockSpec(memory_space=pl.ANY)],
            out_specs=pl.BlockSpec((TM, d), lambda i, src: (i, 0)),
            scratch_shapes=[pltpu.VMEM((2, SORT_ROWS, d // 2), jnp.uint32),
                            pltpu.SemaphoreType.DMA((2,))]),
        out_shape=jax.ShapeDtypeStruct((t, d), jnp.float32),
        compiler_params=pltpu.CompilerParams(dimension_semantics=("arbitrary",),
                                             vmem_limit_bytes=COMBINE_VMEM),
        name="combine",
    )(src_chunk, x1, rw, mod, final_g.reshape(1, d), yr)


def kernel(x, c, positions, rel_bias, w_ada, b_ada, norm1_g, w_in, sinks, gm_ln_g, gm_ln_b, gm_w_s, gm_b_s,
           p_a, p_b, w_o, norm2_g, w_router_g, b_router_g, w_router_e, b_router_e, w_gate, w_up, w_down,
           final_g):
    bsz, seq, d = x.shape
    t = bsz * seq
    bf16 = jnp.bfloat16
    depth = w_ada.shape[0]
    n_tiles = t // TM
    n_chunks = t * TOP_K // CHUNK + n_tiles * N_EXPERTS + N_EXPERTS * BLOCK_CHUNKS
    i32 = jnp.int32
    for l in range(depth):
        mod = _adaln_mod(c, w_ada[l], b_ada[l]).reshape(bsz, 6, d)
        pad = ROUTER_LANES - N_GROUPS - N_EXPERTS
        w_r = jnp.concatenate([w_router_g[l], w_router_e[l], jnp.zeros((d, pad), jnp.float32)], axis=1)
        b_r = jnp.concatenate([b_router_g[l], b_router_e[l], jnp.zeros((pad,), jnp.float32)]).reshape(1, -1)
        bsb = jnp.broadcast_to(gm_b_s[l][:, :, None], (GM_GROUPS, GM_CHUNK, LANES))
        x1, xs, rw, cnt = _mixer(
            x, mod, positions, rel_bias, norm1_g[l].reshape(1, d), w_in[l].astype(bf16), sinks[l],
            gm_ln_g[l].reshape(1, -1), gm_ln_b[l].reshape(1, -1), gm_w_s[l], bsb,
            p_a[l].astype(bf16), p_b[l].astype(bf16), w_o[l].astype(bf16), norm2_g[l].reshape(1, d),
            w_r.astype(bf16), b_r)

        n = cnt[:, EXPERT_LANE0:EXPERT_LANE0 + N_EXPERTS, 0].astype(i32)
        nch = (n + CHUNK - 1) // CHUNK
        run0_tile = jnp.cumsum(nch, axis=1) - nch
        run0_exp = jnp.cumsum(nch, axis=0) - nch
        tot = jnp.sum(nch, axis=0)
        seg = (tot + BLOCK_CHUNKS - 1) // BLOCK_CHUNKS * BLOCK_CHUNKS
        seg_end = jnp.cumsum(seg)
        seg0 = seg_end - seg

        pos_tile = jnp.arange(n_tiles, dtype=i32)[:, None] * SORT_CHUNKS + run0_tile
        pos_exp = seg0[None, :] + run0_exp

        j = jnp.arange(n_chunks, dtype=i32)[:, None]
        e_j = jnp.minimum(jnp.sum(j >= seg_end[None, :], axis=1), N_EXPERTS - 1)
        tables = jnp.concatenate([pos_exp.T, (pos_exp + nch).T, (pos_tile - pos_exp).T], axis=1)
        picked = jnp.dot(jax.nn.one_hot(e_j, N_EXPERTS, dtype=jnp.float32), tables.astype(jnp.float32),
                         precision=lax.Precision.HIGHEST).astype(i32)
        lo_j, hi_j, shift_j = picked[:, :n_tiles], picked[:, n_tiles:2 * n_tiles], picked[:, 2 * n_tiles:]
        inside = (j >= lo_j) & (j < hi_j)
        src_j = jnp.sum(jnp.where(inside, j + shift_j, 0), axis=1).astype(i32)
        blk0 = jnp.arange(n_chunks // BLOCK_CHUNKS, dtype=i32) * BLOCK_CHUNKS
        block_e = jnp.minimum(jnp.sum(blk0[:, None] >= seg_end[None, :], axis=1), N_EXPERTS - 1).astype(i32)
        last_blk = (jnp.maximum(seg_end[-1] // BLOCK_CHUNKS, 1) - 1).astype(i32).reshape(1)
        experts = jnp.arange(N_EXPERTS, dtype=i32)
        later = (experts[None, :] > block_e[:, None]) & (seg[None, :] > 0)
        next_e = jnp.min(jnp.where(later, experts[None, :], N_EXPERTS), axis=1)
        next_e = jnp.where(next_e < N_EXPERTS, next_e, -1).astype(i32)
        w_slot = (jnp.sum((experts[None, :] < block_e[:, None]) & (seg[None, :] > 0), axis=1) % 2).astype(i32)

        q = jnp.arange(SORT_CHUNKS, dtype=i32)[None, :, None]
        in_run = (q >= run0_tile[:, None, :]) & (q < (run0_tile + nch)[:, None, :])
        dst_q = jnp.sum(jnp.where(in_run, q + (pos_exp - run0_tile)[:, None, :], 0), axis=2).astype(i32).reshape(-1)

        own = block_e[:, None] == experts[None, :]
        filled = jnp.sum(jnp.where(own, (seg0 + tot)[None, :], 0), axis=1) - blk0
        quarter_chunks = BLOCK_CHUNKS // BLOCK_QUARTERS
        quarters = jnp.clip((filled + quarter_chunks - 1) // quarter_chunks, 1, BLOCK_QUARTERS).astype(i32)

        yr = _experts(src_j, block_e, next_e, w_slot, quarters, last_blk, xs, w_gate[l], w_up[l], w_down[l])
        x = _combine(dst_q, x1.reshape(t, d), rw, mod, final_g, yr, seq,
                     final_norm=(l == depth - 1)).reshape(bsz, seq, d)
    return x
```

```python
import functools
import math

import jax
import jax.numpy as jnp
from jax import lax
from jax.experimental import pallas as pl
from jax.experimental.pallas import tpu as pltpu

D_MODEL = 1024
N_HEADS = 8
N_KV_HEADS = 2
HEAD_DIM = 64
BLOCK = 128
ATTN_Q = N_HEADS * HEAD_DIM
ATTN_KV = N_KV_HEADS * HEAD_DIM
N_BUCKETS = 32
MAX_EXACT = N_BUCKETS // 2
MAX_DISTANCE = 128
GM_WIDTH = 512
GM_GROUPS = 4
GM_CHUNK = 128
N_GROUPS = 4
EXPERTS_PER_GROUP = 8
N_EXPERTS = N_GROUPS * EXPERTS_PER_GROUP
TOP_K = 2
D_EXPERT = 512
EPS = 1e-6
NEG = -1e30

LANES = 128
ROUTER_LANES = LANES
EXPERT_LANE0 = N_GROUPS
TM = 512
BM = 512
ROUTER_ROWS = 48
ROUTE_ROWS = 8
CHUNK = 8
SORT_ROWS = TM * TOP_K + N_EXPERTS * CHUNK
SORT_CHUNKS = SORT_ROWS // CHUNK
BLOCK_CHUNKS = BM // CHUNK
BLOCK_QUARTERS = 4
GATHER_UNROLL = 8
GATHER_SLOTS = 3
MERGE_COLS = 512
TAIL_SPLIT = 2
MIB = 1024 * 1024
MIXER_VMEM = 44 * MIB
EXPERTS_VMEM = 32 * MIB
COMBINE_VMEM = 24 * MIB

ADALN_TN = 1024

Q0 = 0
K0 = Q0 + ATTN_Q
V0 = K0 + ATTN_KV
GU0 = V0 + ATTN_KV
GV0 = GU0 + GM_WIDTH
GA0 = GV0 + GM_WIDTH
GB0 = GA0 + D_MODEL
IN_END = GB0 + D_MODEL


def _dot(a, b):
    return jnp.dot(a, b, preferred_element_type=jnp.float32)


def _dot_nt(a, b):
    return lax.dot_general(a, b, (((1,), (1,)), ((), ())), preferred_element_type=jnp.float32)


LOG2E = math.log2(math.e)


def _gelu_tanh(x):
    c = math.sqrt(2.0 / math.pi)
    k0, k1 = -2.0 * c * LOG2E, -2.0 * c * 0.044715 * LOG2E
    return x * (1.0 / (1.0 + jnp.exp2(x * (x * x * k1 + k0))))


def _sigmoid(x):
    return 1.0 / (1.0 + jnp.exp2(x * -LOG2E))


def _pack_bf16_pairs(x):
    bits = lax.bitcast_convert_type(x, jnp.uint32)
    half = x.shape[1] // 2
    return (bits[:, 0:half] >> 16) | (bits[:, half:] & jnp.uint32(0xFFFF0000))


def _unpack_bf16_pairs(w):
    lo = lax.bitcast_convert_type(w << 16, jnp.float32)
    hi = lax.bitcast_convert_type(w & jnp.uint32(0xFFFF0000), jnp.float32)
    return jnp.concatenate([lo, hi], axis=1).astype(jnp.bfloat16)


def _adaln_kernel(c_ref, w_ref, b_ref, o_ref):
    c = c_ref[...]
    cs = c * _sigmoid(c)
    o_ref[...] = _dot(cs, w_ref[...]) + b_ref[...]


def _adaln_mod(c, w, b):
    bsz, d = c.shape
    n = w.shape[1]
    tn = ADALN_TN
    return pl.pallas_call(
        _adaln_kernel,
        grid=(n // tn,),
        in_specs=[pl.BlockSpec((bsz, d), lambda i: (0, 0)),
                  pl.BlockSpec((d, tn), lambda i: (0, i)),
                  pl.BlockSpec((1, tn), lambda i: (0, i))],
        out_specs=pl.BlockSpec((bsz, tn), lambda i: (0, i)),
        out_shape=jax.ShapeDtypeStruct((bsz, n), jnp.float32),
        compiler_params=pltpu.CompilerParams(dimension_semantics=("arbitrary",)),
        name="adaln_mod",
    )(c, w, b.reshape(1, n))


def _mixer_kernel(relb_ref, sinks_ref, spos_ref,
                  x_ref, mod_ref, pos_ref, n1g_ref, win_ref, lng_ref, lnb_ref, ws_ref, bsb_ref,
                  pa_ref, pb_ref, wo_ref, n2g_ref, wr_ref, br_ref,
                  x1_ref, xs_ref, rw_ref, cnt_ref,
                  kbuf, vbuf, pbuf, biasm, key, ya, yb, strict, sbuf, mbuf, flag):
    b = pl.program_id(0)
    j = pl.program_id(1)
    nblk = TM // BLOCK
    bf16 = jnp.bfloat16

    @pl.when((b == 0) & (j == 0))
    def _():
        key[...] = jnp.zeros_like(key)
        for i in range(nblk):
            flag[i] = 1
        tr = lax.broadcasted_iota(jnp.int32, (TM, TM), 0)
        tc = lax.broadcasted_iota(jnp.int32, (TM, TM), 1)
        strict[...] = jnp.where(tr < tc, 1.0, 0.0).astype(bf16)

    @pl.when(j == 0)
    def _():
        kbuf[0:BLOCK, :] = jnp.zeros((BLOCK, LANES), jnp.float32)
        vbuf[0:BLOCK, :] = jnp.zeros((BLOCK, LANES), jnp.float32)
        pbuf[:, 0:BLOCK] = jnp.zeros((1, BLOCK), jnp.int32)

    pbuf[:, BLOCK:] = pos_ref[...]
    qi = lax.broadcasted_iota(jnp.int32, (BLOCK, BLOCK), 0)
    kc = lax.broadcasted_iota(jnp.int32, (BLOCK, BLOCK), 1)
    from_prev = kc > qi
    rels, changed = [], []
    tile0 = (b * pl.num_programs(1) + j) * TM
    for i in range(nblk):
        pk = pbuf[:, i * BLOCK:(i + 2) * BLOCK]
        rels.append(pk - spos_ref[tile0 + i * BLOCK])
        changed.append(jnp.where(rels[i] != key[i], 1.0, 0.0))
    any_changed = jnp.max(functools.reduce(jnp.maximum, changed))
    any_flag = functools.reduce(jnp.maximum, [flag[i] for i in range(nblk)])
    refresh = (any_changed != 0) | (any_flag != 0) | (j == 0)

    def refresh_bias(i):
        pk = pbuf[:, i * BLOCK:(i + 2) * BLOCK]
        pq = pk[:, BLOCK:]
        pq_col = jnp.transpose(jnp.broadcast_to(pq, (BLOCK, BLOCK)))
        no_prev = (j == 0) if i == 0 else None
        for hd in range(N_HEADS):
            biasm[i * N_HEADS + hd] = jnp.zeros((BLOCK, BLOCK), jnp.float32)
        for side in range(2):
            dist = pq_col - pk[:, side * BLOCK:(side + 1) * BLOCK]
            n = jnp.maximum(dist, 0)
            nf = jnp.maximum(n, 1).astype(jnp.float32)
            large = MAX_EXACT + (jnp.log(nf / MAX_EXACT) / math.log(MAX_DISTANCE / MAX_EXACT)
                                 * (N_BUCKETS - MAX_EXACT)).astype(jnp.int32)
            large = jnp.minimum(large, N_BUCKETS - 1)
            bucket = jnp.where(n < MAX_EXACT, n, large)
            use = from_prev if side == 0 else jnp.logical_not(from_prev)
            for hd in range(N_HEADS):
                acc = jnp.zeros((BLOCK, BLOCK), jnp.float32)
                for bk in range(N_BUCKETS):
                    acc = jnp.where(bucket == bk, relb_ref[bk * N_HEADS + hd], acc)
                if side == 0 and i == 0:
                    acc = jnp.where(no_prev, NEG, acc)
                slot = i * N_HEADS + hd
                biasm[slot] = jnp.where(use, acc, biasm[slot])
        key[i] = rels[i]
        flag[i] = no_prev.astype(jnp.int32) if i == 0 else 0

    for i in range(nblk):
        @pl.when(refresh)
        def _():
            stale = (jnp.max(changed[i]) != 0) | (flag[i] != 0)
            if i == 0:
                stale = stale | (j == 0)
            pl.when(stale)(functools.partial(refresh_bias, i))

    x = x_ref[...]
    sh1, sc1, g1 = mod_ref[0:1, :], mod_ref[1:2, :], mod_ref[2:3, :]
    sh2, sc2 = mod_ref[3:4, :], mod_ref[4:5, :]

    xn = x * lax.rsqrt(jnp.mean(x * x, axis=-1, keepdims=True) + EPS)
    h = (xn * (n1g_ref[...] * (1.0 + sc1)) + sh1).astype(bf16)

    lane = lax.broadcasted_iota(jnp.int32, (1, LANES), 1)
    lo = lane < HEAD_DIM
    q = _dot(h, win_ref[:, Q0:K0]) * (HEAD_DIM ** -0.5)
    lo4 = jnp.concatenate([lo] * (ATTN_Q // LANES), axis=1)
    q_lo = jnp.where(lo4, q, 0.0).astype(bf16)
    q_hi = jnp.where(lo4, 0.0, q).astype(bf16)
    kv = _dot(h, win_ref[:, K0:GU0])
    kbuf[BLOCK:, :] = kv[:, 0:LANES]
    vbuf[BLOCK:, :] = kv[:, LANES:]
    kf = kbuf[...]
    vf = vbuf[...]
    kr = pltpu.roll(kf, HEAD_DIM, 1)
    vr = pltpu.roll(vf, HEAD_DIM, 1)
    kd = (jnp.where(lo, kf, kr).astype(bf16), jnp.where(lo, kr, kf).astype(bf16))
    v_lo = (jnp.where(lo, vf, 0.0).astype(bf16), jnp.where(lo, vr, 0.0).astype(bf16))
    v_hi = (jnp.where(lo, 0.0, vr).astype(bf16), jnp.where(lo, 0.0, vf).astype(bf16))

    ones_blk = jnp.ones((2 * BLOCK, LANES), bf16)

    def attend(i):
        rows = slice(i * BLOCK, (i + 1) * BLOCK)
        band = slice(i * BLOCK, (i + 2) * BLOCK)
        grp = N_HEADS // N_KV_HEADS
        for kvh in range(N_KV_HEADS):
            heads = range(kvh * grp, (kvh + 1) * grp)
            qs = jnp.concatenate([(q_lo if hd % 2 == 0 else q_hi)[rows, hd // 2 * LANES:(hd // 2 + 1) * LANES]
                                  for hd in heads], axis=0)
            s2 = _dot_nt(qs, kd[kvh][band])
            for g, hd in enumerate(heads):
                sh = s2[g * BLOCK:(g + 1) * BLOCK]
                s = jnp.where(from_prev, sh[:, 0:BLOCK], sh[:, BLOCK:]) + biasm[i * N_HEADS + hd]
                sbuf[hd] = s
                m = jnp.maximum(jnp.max(s, axis=-1, keepdims=True), sinks_ref[hd])
                mbuf[hd] = jnp.broadcast_to(m, (BLOCK, BLOCK))
        for kvh in range(N_KV_HEADS):
            outs = {}
            for half, vv in enumerate((v_lo, v_hi)):
                hds = (kvh * grp + half, kvh * grp + half + 2)
                p2 = []
                for hd in hds:
                    p = jnp.exp(sbuf[hd] - mbuf[hd])
                    p2.append(jnp.concatenate([jnp.where(from_prev, p, 0.0), jnp.where(from_prev, 0.0, p)],
                                              axis=1).astype(bf16))
                r = _dot(jnp.concatenate(p2, axis=0), jnp.concatenate([vv[kvh][band], ones_blk], axis=1))
                for n, hd in enumerate(hds):
                    rh = r[n * BLOCK:(n + 1) * BLOCK]
                    den = rh[:, LANES:] + jnp.exp(sinks_ref[hd] - mbuf[hd])
                    outs[hd] = rh[:, 0:LANES] * (1.0 / den)
            for pr in (kvh * grp // 2, kvh * grp // 2 + 1):
                ya[rows, pr * LANES:(pr + 1) * LANES] = (outs[2 * pr] + outs[2 * pr + 1]).astype(bf16)

    u = _gelu_tanh(_dot(h, win_ref[:, GU0:GV0]))
    attend(0)
    vg = _gelu_tanh(_dot(h, win_ref[:, GV0:GA0]))
    mu = jnp.mean(vg, axis=-1, keepdims=True)
    vc = vg - mu
    var = jnp.mean(vc * vc, axis=-1, keepdims=True)
    vn = (vc * lax.rsqrt(var + EPS) * lng_ref[...] + lnb_ref[...]).astype(bf16)
    attend(1)
    gate_a = _sigmoid(_dot(h, win_ref[:, GA0:GB0]))
    attend(2)
    ti = lax.broadcasted_iota(jnp.int32, (GM_CHUNK, GM_CHUNK), 0)
    si = lax.broadcasted_iota(jnp.int32, (GM_CHUNK, GM_CHUNK), 1)
    tril = si <= ti
    for g in range(GM_GROUPS):
        wg = jnp.where(tril, ws_ref[g], 0.0).astype(bf16)
        cols = slice(g * LANES, (g + 1) * LANES)
        for cidx in range(TM // GM_CHUNK):
            rows = slice(cidx * GM_CHUNK, (cidx + 1) * GM_CHUNK)
            sv = _dot(wg, vn[rows, cols]) + bsb_ref[g]
            yb[rows, cols] = (u[rows, cols] * sv).astype(bf16)
    gate_b = _sigmoid(_dot(h, win_ref[:, GB0:IN_END]))
    attend(3)

    kbuf[0:BLOCK, :] = kv[TM - BLOCK:, 0:LANES]
    vbuf[0:BLOCK, :] = kv[TM - BLOCK:, LANES:]
    pbuf[:, 0:BLOCK] = pos_ref[:, TM - BLOCK:]

    h2_parts, logit_parts = [], []
    for grp in range(TAIL_SPLIT):
        tok = slice(grp * TM // TAIL_SPLIT, (grp + 1) * TM // TAIL_SPLIT)
        proj = None
        for c0 in range(0, D_MODEL, MERGE_COLS):
            cs = slice(c0, c0 + MERGE_COLS)
            merged = (gate_a[tok, cs] * _dot(ya[tok, :], pa_ref[:, cs])
                      + gate_b[tok, cs] * _dot(yb[tok, :], pb_ref[:, cs]))
            part = _dot(merged.astype(bf16), wo_ref[cs, :])
            proj = part if proj is None else proj + part
        x1 = x[tok] + g1 * proj
        x1_ref[tok, :] = x1

        xn2 = x1 * lax.rsqrt(jnp.mean(x1 * x1, axis=-1, keepdims=True) + EPS)
        h2_parts.append((xn2 * (n2g_ref[...] * (1.0 + sc2)) + sh2).astype(bf16))
        logit_parts.append(_dot(h2_parts[-1], wr_ref[...]) + br_ref[...])
    h2 = jnp.concatenate(h2_parts, axis=0)
    logits = jnp.concatenate(logit_parts, axis=0)
    lt = jnp.transpose(logits)[0:ROUTER_ROWS, :]
    row = lax.broadcasted_iota(jnp.int32, (ROUTER_ROWS, TM), 0)
    row_f = row.astype(jnp.float32)
    big = float(ROUTER_ROWS)
    is_grp = row < N_GROUPS
    lg = jnp.where(is_grp, lt, NEG)
    lg_max = jnp.max(lg, axis=0, keepdims=True)
    g_idx = jnp.min(jnp.where(lg == lg_max, row_f, big), axis=0, keepdims=True)
    p_g = 1.0 / jnp.sum(jnp.where(is_grp, jnp.exp(lg - lg_max), 0.0), axis=0, keepdims=True)
    row_grp = jnp.floor((row_f - EXPERT_LANE0) * (1.0 / EXPERTS_PER_GROUP))
    in_grp = (row >= EXPERT_LANE0) & (row < EXPERT_LANE0 + N_EXPERTS) & (row_grp == g_idx)
    le = jnp.where(in_grp, lt, NEG)
    m1 = jnp.max(le, axis=0, keepdims=True)
    i1 = jnp.min(jnp.where(le == m1, row_f, big), axis=0, keepdims=True)
    oh1 = row_f == i1
    le2 = jnp.where(oh1, NEG, le)
    m2 = jnp.max(le2, axis=0, keepdims=True)
    i2 = jnp.min(jnp.where(le2 == m2, row_f, big), axis=0, keepdims=True)
    oh2 = row_f == i2
    e2 = jnp.exp(m2 - m1)
    w1 = p_g / (1.0 + e2)
    w2 = p_g * e2 / (1.0 + e2)

    oh = jnp.where(oh1, 1.0, jnp.where(oh2, 1.0, 0.0))
    n_e = jnp.sum(oh, axis=1, keepdims=True)
    cnt_ref[...] = n_e
    padded = jnp.floor((n_e + (CHUNK - 1)) * (1.0 / CHUNK)) * CHUNK
    dst_row = lax.broadcasted_iota(jnp.int32, (ROUTER_ROWS, ROUTER_ROWS), 0)
    src_row = lax.broadcasted_iota(jnp.int32, (ROUTER_ROWS, ROUTER_ROWS), 1)
    lower = jnp.where(src_row < dst_row, 1.0, 0.0).astype(bf16)
    run0 = _dot(lower, jnp.broadcast_to(padded, (ROUTER_ROWS, TM)).astype(bf16))
    slot = _dot(oh.astype(bf16), strict[...]) + run0
    pos1 = jnp.sum(jnp.where(oh1, slot, 0.0), axis=0, keepdims=True)
    pos2 = jnp.sum(jnp.where(oh2, slot, 0.0), axis=0, keepdims=True)
    rrow = lax.broadcasted_iota(jnp.int32, (ROUTE_ROWS, TM), 0)
    rw_ref[...] = jnp.where(rrow == 0, w1, jnp.where(rrow == 1, w2,
                            jnp.where(rrow == 2, pos1, jnp.where(rrow == 3, pos2, 0.0))))

    p1 = pos1.astype(jnp.int32)
    p2 = pos2.astype(jnp.int32)
    sp = lax.broadcasted_iota(jnp.int32, (SORT_ROWS, TM), 0)
    perm = jnp.where(sp == p1, 1.0, jnp.where(sp == p2, 1.0, 0.0)).astype(bf16)
    xs_ref[...] = _pack_bf16_pairs(_dot(perm, h2))


def _mixer(x, mod, positions, rel_bias, n1g, w_in, sinks, lng, lnb, w_s, bsb, p_a, p_b, w_o, n2g, w_r, b_r):
    bsz, seq, d = x.shape
    nj = seq // TM
    const = lambda *shape: pl.BlockSpec(shape, lambda b, j: (0,) * len(shape), pipeline_mode=pl.Buffered(1))
    smem = pl.BlockSpec(memory_space=pltpu.SMEM)
    tile = lambda w: pl.BlockSpec((None, TM, w), lambda b, j: (b, j, 0))
    return pl.pallas_call(
        _mixer_kernel,
        grid=(bsz, nj),
        in_specs=[smem, smem, smem,
                  tile(d),
                  pl.BlockSpec((None, 6, d), lambda b, j: (b, 0, 0)),
                  pl.BlockSpec((None, None, 1, TM), lambda b, j: (b, j, 0, 0)),
                  const(1, d), const(d, IN_END), const(1, GM_WIDTH), const(1, GM_WIDTH),
                  const(GM_GROUPS, GM_CHUNK, GM_CHUNK), const(GM_GROUPS, GM_CHUNK, LANES),
                  const(ATTN_Q, d), const(GM_WIDTH, d), const(d, d), const(1, d),
                  const(d, ROUTER_LANES), const(1, ROUTER_LANES)],
        out_specs=[tile(d),
                   pl.BlockSpec((SORT_ROWS, d // 2), lambda b, j: (b * nj + j, 0)),
                   pl.BlockSpec((ROUTE_ROWS, TM), lambda b, j: (0, b * nj + j)),
                   pl.BlockSpec((None, ROUTER_ROWS, 1), lambda b, j: (b * nj + j, 0, 0))],
        out_shape=[jax.ShapeDtypeStruct((bsz, seq, d), jnp.float32),
                   jax.ShapeDtypeStruct((bsz * nj * SORT_ROWS, d // 2), jnp.uint32),
                   jax.ShapeDtypeStruct((ROUTE_ROWS, bsz * seq), jnp.float32),
                   jax.ShapeDtypeStruct((bsz * nj, ROUTER_ROWS, 1), jnp.float32)],
        scratch_shapes=[pltpu.VMEM((TM + BLOCK, LANES), jnp.float32),
                        pltpu.VMEM((TM + BLOCK, LANES), jnp.float32),
                        pltpu.VMEM((1, TM + BLOCK), jnp.int32),
                        pltpu.VMEM((TM // BLOCK * N_HEADS, BLOCK, BLOCK), jnp.float32),
                        pltpu.VMEM((TM // BLOCK, 1, 2 * BLOCK), jnp.int32),
                        pltpu.VMEM((TM, ATTN_Q), jnp.bfloat16),
                        pltpu.VMEM((TM, GM_WIDTH), jnp.bfloat16),
                        pltpu.VMEM((TM, TM), jnp.bfloat16),
                        pltpu.VMEM((N_HEADS, BLOCK, BLOCK), jnp.float32),
                        pltpu.VMEM((N_HEADS, BLOCK, BLOCK), jnp.float32),
                        pltpu.SMEM((TM // BLOCK,), jnp.int32)],
        compiler_params=pltpu.CompilerParams(dimension_semantics=("arbitrary", "arbitrary"),
                                             vmem_limit_bytes=MIXER_VMEM),
        name="mixer",
    )(rel_bias.reshape(-1), sinks, positions.reshape(-1), x, mod, positions.reshape(bsz, nj, 1, TM), n1g, w_in, lng, lnb,
      w_s, bsb, p_a, p_b, w_o, n2g, w_r, b_r)


def _gather_chunks(idx_ref, idx0, n, src_hbm, dst, sem, inline=False, split_priority=False):
    def issue(c, carry, priority=0):
        src = pl.multiple_of(idx_ref[idx0 + c] * CHUNK, CHUNK)
        pltpu.make_async_copy(src_hbm.at[pl.ds(src, CHUNK)],
                              dst.at[pl.ds(pl.multiple_of(c * CHUNK, CHUNK), CHUNK)], sem).start(priority)
        return carry

    if inline:
        for c in range(n):
            issue(c, 0, c % 2 if split_priority else 0)
    else:
        lax.fori_loop(0, n, issue, 0, unroll=GATHER_UNROLL)


def _wait_chunks(n, src_hbm, dst, sem):
    pltpu.make_async_copy(src_hbm.at[pl.ds(0, n * CHUNK)], dst, sem).wait()


def _expert_kernel(src_ref, be_ref, ne_ref, ws_ref, nq_ref, lb_ref, xs_hbm, wg_hbm, wu_hbm, wd_hbm, yr_ref,
                   xbuf, wgf, wuf, wdf, wgb, wub, wdb, sem, wsem):
    i = pl.program_id(0)
    last = lb_ref[0]
    slot = i % GATHER_SLOTS
    ahead = jnp.minimum(i + GATHER_SLOTS - 1, last)
    ahead_slot = (i + GATHER_SLOTS - 1) % GATHER_SLOTS

    def weight_copies(e, w):
        return [pltpu.make_async_copy(hbm.at[e], buf.at[w], wsem.at[w])
                for hbm, buf in ((wg_hbm, wgf), (wu_hbm, wuf), (wd_hbm, wdf))]

    @pl.when(i == 0)
    def _():
        for s in range(GATHER_SLOTS - 1):
            _gather_chunks(src_ref, jnp.minimum(s, last) * BLOCK_CHUNKS, BLOCK_CHUNKS, xs_hbm, xbuf.at[s], sem.at[s])
        for cp in weight_copies(be_ref[0], 0):
            cp.start()

    @pl.when(i <= last)
    def _():
        changed = (i == 0) | (be_ref[i] != be_ref[jnp.maximum(i - 1, 0)])

        @pl.when(changed)
        def _():
            w = ws_ref[i]
            for cp in weight_copies(be_ref[i], w):
                cp.wait()
            wgb[...] = wgf[w].astype(jnp.bfloat16)
            wub[...] = wuf[w].astype(jnp.bfloat16)
            wdb[...] = wdf[w].astype(jnp.bfloat16)

            @pl.when(ne_ref[i] >= 0)
            def _():
                for cp in weight_copies(ne_ref[i], 1 - w):
                    cp.start(priority=1)

        _wait_chunks(BLOCK_CHUNKS, xs_hbm, xbuf.at[slot], sem.at[slot])

        def compute(rows):
            xb = _unpack_bf16_pairs(xbuf[slot, 0:rows, :])
            hg = _dot(xb, wgb[...])
            hu = _dot(xb, wub[...])
            hid = (hg * _sigmoid(hg) * hu).astype(jnp.bfloat16)
            yr_ref[0:rows, :] = _pack_bf16_pairs(_dot(hid, wdb[...]).astype(jnp.bfloat16).astype(jnp.float32))
            if rows < BM:
                yr_ref[rows:, :] = jnp.zeros((BM - rows, yr_ref.shape[1]), yr_ref.dtype)
            _gather_chunks(src_ref, ahead * BLOCK_CHUNKS, BLOCK_CHUNKS, xs_hbm, xbuf.at[ahead_slot],
                           sem.at[ahead_slot], inline=True)

        for quarters in range(1, BLOCK_QUARTERS + 1):
            pl.when(nq_ref[i] == quarters)(functools.partial(compute, quarters * BM // BLOCK_QUARTERS))

        @pl.when(i == last)
        def _():
            for s in range(1, GATHER_SLOTS):
                other = (i + s) % GATHER_SLOTS
                _wait_chunks(BLOCK_CHUNKS, xs_hbm, xbuf.at[other], sem.at[other])

    @pl.when(i > last)
    def _():
        yr_ref[...] = jnp.zeros_like(yr_ref)


def _experts(src_chunk, block_e, next_e, w_slot, quarters, last_blk, xs, w_gate, w_up, w_down):
    d = w_gate.shape[1]
    n_rows = src_chunk.shape[0] * CHUNK
    hbm = pl.BlockSpec(memory_space=pl.ANY)
    return pl.pallas_call(
        _expert_kernel,
        grid_spec=pltpu.PrefetchScalarGridSpec(
            num_scalar_prefetch=6,
            grid=(n_rows // BM,),
            in_specs=[hbm, hbm, hbm, hbm],
            out_specs=pl.BlockSpec((BM, d // 2), lambda i, *_: (i, 0)),
            scratch_shapes=[pltpu.VMEM((GATHER_SLOTS, BM, d // 2), jnp.uint32),
                            pltpu.VMEM((2, d, D_EXPERT), jnp.float32),
                            pltpu.VMEM((2, d, D_EXPERT), jnp.float32),
                            pltpu.VMEM((2, D_EXPERT, d), jnp.float32),
                            pltpu.VMEM((d, D_EXPERT), jnp.bfloat16),
                            pltpu.VMEM((d, D_EXPERT), jnp.bfloat16),
                            pltpu.VMEM((D_EXPERT, d), jnp.bfloat16),
                            pltpu.SemaphoreType.DMA((GATHER_SLOTS,)),
                            pltpu.SemaphoreType.DMA((2,))]),
        out_shape=jax.ShapeDtypeStruct((n_rows, d // 2), jnp.uint32),
        compiler_params=pltpu.CompilerParams(dimension_semantics=("arbitrary",),
                                             vmem_limit_bytes=EXPERTS_VMEM),
        name="experts",
    )(src_chunk, block_e, next_e, w_slot, quarters, last_blk, xs, w_gate, w_up, w_down)


def _combine_kernel(src_ref, x1_ref, rw_ref, mod_ref, fg_ref, yr_hbm, out_ref, ybuf, sem, *, final_norm):
    i = pl.program_id(0)
    slot = i % 2

    @pl.when(i == 0)
    def _():
        _gather_chunks(src_ref, 0, SORT_CHUNKS, yr_hbm, ybuf.at[0], sem.at[0])

    rows = jnp.concatenate([rw_ref[...], jnp.zeros((LANES - ROUTE_ROWS, TM), jnp.float32)], axis=0)
    rw = jnp.transpose(rows)
    sp = lax.broadcasted_iota(jnp.int32, (1, SORT_ROWS), 1)
    p1 = rw[:, 2:3].astype(jnp.int32)
    p2 = rw[:, 3:4].astype(jnp.int32)
    wmat = jnp.where(sp == p1, rw[:, 0:1], jnp.where(sp == p2, rw[:, 1:2], 0.0)).astype(jnp.bfloat16)
    nxt = jnp.minimum(i + 1, pl.num_programs(0) - 1)
    _gather_chunks(src_ref, nxt * SORT_CHUNKS, SORT_CHUNKS, yr_hbm, ybuf.at[1 - slot], sem.at[1 - slot],
                   inline=True, split_priority=True)
    _wait_chunks(SORT_CHUNKS, yr_hbm, ybuf.at[slot], sem.at[slot])
    moe = _dot(wmat, _unpack_bf16_pairs(ybuf[slot]))
    x2 = x1_ref[...] + mod_ref[5:6, :] * moe
    if final_norm:
        x2 = x2 * lax.rsqrt(jnp.mean(x2 * x2, axis=-1, keepdims=True) + EPS) * fg_ref[...]
    out_ref[...] = x2

    @pl.when(i == pl.num_programs(0) - 1)
    def _():
        _wait_chunks(SORT_CHUNKS, yr_hbm, ybuf.at[1 - slot], sem.at[1 - slot])


def _combine(src_chunk, x1, rw, mod, final_g, yr, seq, final_norm):
    t, d = x1.shape
    per_seq = seq // TM
    return pl.pallas_call(
        functools.partial(_combine_kernel, final_norm=final_norm),
        grid_spec=pltpu.PrefetchScalarGridSpec(
            num_scalar_prefetch=1,
            grid=(t // TM,),
            in_specs=[pl.BlockSpec((TM, d), lambda i, src: (i, 0)),
                      pl.BlockSpec((ROUTE_ROWS, TM), lambda i, src: (0, i)),
                      pl.BlockSpec((None, 6, d), lambda i, src: (i // per_seq, 0, 0)),
                      pl.BlockSpec((1, d), lambda i, src: (0, 0)),
                      pl.BlockSpec(memory_space=pl.ANY)],
            out_specs=pl.BlockSpec((TM, d), lambda i, src: (i, 0)),
            scratch_shapes=[pltpu.VMEM((2, SORT_ROWS, d // 2), jnp.uint32),
                            pltpu.SemaphoreType.DMA((2,))]),
        out_shape=jax.ShapeDtypeStruct((t, d), jnp.float32),
        compiler_params=pltpu.CompilerParams(dimension_semantics=("arbitrary",),
                                             vmem_limit_bytes=COMBINE_VMEM),
        name="combine",
    )(src_chunk, x1, rw, mod, final_g.reshape(1, d), yr)


def kernel(x, c, positions, rel_bias, w_ada, b_ada, norm1_g, w_in, sinks, gm_ln_g, gm_ln_b, gm_w_s, gm_b_s,
           p_a, p_b, w_o, norm2_g, w_router_g, b_router_g, w_router_e, b_router_e, w_gate, w_up, w_down,
           final_g):
    bsz, seq, d = x.shape
    t = bsz * seq
    bf16 = jnp.bfloat16
    depth = w_ada.shape[0]
    n_tiles = t // TM
    n_chunks = t * TOP_K // CHUNK + n_tiles * N_EXPERTS + N_EXPERTS * BLOCK_CHUNKS
    i32 = jnp.int32
    for l in range(depth):
        mod = _adaln_mod(c, w_ada[l], b_ada[l]).reshape(bsz, 6, d)
        pad = ROUTER_LANES - N_GROUPS - N_EXPERTS
        w_r = jnp.concatenate([w_router_g[l], w_router_e[l], jnp.zeros((d, pad), jnp.float32)], axis=1)
        b_r = jnp.concatenate([b_router_g[l], b_router_e[l], jnp.zeros((pad,), jnp.float32)]).reshape(1, -1)
        bsb = jnp.broadcast_to(gm_b_s[l][:, :, None], (GM_GROUPS, GM_CHUNK, LANES))
        x1, xs, rw, cnt = _mixer(
            x, mod, positions, rel_bias, norm1_g[l].reshape(1, d), w_in[l].astype(bf16), sinks[l],
            gm_ln_g[l].reshape(1, -1), gm_ln_b[l].reshape(1, -1), gm_w_s[l], bsb,
            p_a[l].astype(bf16), p_b[l].astype(bf16), w_o[l].astype(bf16), norm2_g[l].reshape(1, d),
            w_r.astype(bf16), b_r)

        n = cnt[:, EXPERT_LANE0:EXPERT_LANE0 + N_EXPERTS, 0].astype(i32)
        nch = (n + CHUNK - 1) // CHUNK
        run0_tile = jnp.cumsum(nch, axis=1) - nch
        run0_exp = jnp.cumsum(nch, axis=0) - nch
        tot = jnp.sum(nch, axis=0)
        seg = (tot + BLOCK_CHUNKS - 1) // BLOCK_CHUNKS * BLOCK_CHUNKS
        seg_end = jnp.cumsum(seg)
        seg0 = seg_end - seg

        pos_tile = jnp.arange(n_tiles, dtype=i32)[:, None] * SORT_CHUNKS + run0_tile
        pos_exp = seg0[None, :] + run0_exp

        j = jnp.arange(n_chunks, dtype=i32)[:, None]
        e_j = jnp.minimum(jnp.sum(j >= seg_end[None, :], axis=1), N_EXPERTS - 1)
        tables = jnp.concatenate([pos_exp.T, (pos_exp + nch).T, (pos_tile - pos_exp).T], axis=1)
        picked = jnp.dot(jax.nn.one_hot(e_j, N_EXPERTS, dtype=jnp.float32), tables.astype(jnp.float32),
                         precision=lax.Precision.HIGHEST).astype(i32)
        lo_j, hi_j, shift_j = picked[:, :n_tiles], picked[:, n_tiles:2 * n_tiles], picked[:, 2 * n_tiles:]
        inside = (j >= lo_j) & (j < hi_j)
        src_j = jnp.sum(jnp.where(inside, j + shift_j, 0), axis=1).astype(i32)
        blk0 = jnp.arange(n_chunks // BLOCK_CHUNKS, dtype=i32) * BLOCK_CHUNKS
        block_e = jnp.minimum(jnp.sum(blk0[:, None] >= seg_end[None, :], axis=1), N_EXPERTS - 1).astype(i32)
        last_blk = (jnp.maximum(seg_end[-1] // BLOCK_CHUNKS, 1) - 1).astype(i32).reshape(1)
        experts = jnp.arange(N_EXPERTS, dtype=i32)
        later = (experts[None, :] > block_e[:, None]) & (seg[None, :] > 0)
        next_e = jnp.min(jnp.where(later, experts[None, :], N_EXPERTS), axis=1)
        next_e = jnp.where(next_e < N_EXPERTS, next_e, -1).astype(i32)
        w_slot = (jnp.sum((experts[None, :] < block_e[:, None]) & (seg[None, :] > 0), axis=1) % 2).astype(i32)

        q = jnp.arange(SORT_CHUNKS, dtype=i32)[None, :, None]
        in_run = (q >= run0_tile[:, None, :]) & (q < (run0_tile + nch)[:, None, :])
        dst_q = jnp.sum(jnp.where(in_run, q + (pos_exp - run0_tile)[:, None, :], 0), axis=2).astype(i32).reshape(-1)

        own = block_e[:, None] == experts[None, :]
        filled = jnp.sum(jnp.where(own, (seg0 + tot)[None, :], 0), axis=1) - blk0
        quarter_chunks = BLOCK_CHUNKS // BLOCK_QUARTERS
        quarters = jnp.clip((filled + quarter_chunks - 1) // quarter_chunks, 1, BLOCK_QUARTERS).astype(i32)

        yr = _experts(src_j, block_e, next_e, w_slot, quarters, last_blk, xs, w_gate[l], w_up[l], w_down[l])
        x = _combine(dst_q, x1.reshape(t, d), rw, mod, final_g, yr, seq,
                     final_norm=(l == depth - 1)).reshape(bsz, seq, d)
    return x
```

```python
import functools
import math

import jax
import jax.numpy as jnp
from jax import lax
from jax.experimental import pallas as pl
from jax.experimental.pallas import tpu as pltpu

D_MODEL = 1024
N_HEADS = 8
N_KV_HEADS = 2
HEAD_DIM = 64
BLOCK = 128
ATTN_Q = N_HEADS * HEAD_DIM
ATTN_KV = N_KV_HEADS * HEAD_DIM
N_BUCKETS = 32
MAX_EXACT = N_BUCKETS // 2
MAX_DISTANCE = 128
GM_WIDTH = 512
GM_GROUPS = 4
GM_CHUNK = 128
N_GROUPS = 4
EXPERTS_PER_GROUP = 8
N_EXPERTS = N_GROUPS * EXPERTS_PER_GROUP
TOP_K = 2
D_EXPERT = 512
EPS = 1e-6
NEG = -1e30

LANES = 128
ROUTER_LANES = LANES
EXPERT_LANE0 = N_GROUPS
TM = 512
BM = 512
ROUTER_ROWS = 48
ROUTE_ROWS = 8
CHUNK = 8
SORT_ROWS = TM * TOP_K + N_EXPERTS * CHUNK
SORT_CHUNKS = SORT_ROWS // CHUNK
BLOCK_CHUNKS = BM // CHUNK
BLOCK_QUARTERS = 4
GATHER_UNROLL = 8
GATHER_SLOTS = 3
HIDDEN_COLS = 256
MERGE_COLS = 512
TAIL_SPLIT = 2
MIB = 1024 * 1024
MIXER_VMEM = 44 * MIB
EXPERTS_VMEM = 32 * MIB
COMBINE_VMEM = 24 * MIB

ADALN_TN = 1024

Q0 = 0
K0 = Q0 + ATTN_Q
V0 = K0 + ATTN_KV
GU0 = V0 + ATTN_KV
GV0 = GU0 + GM_WIDTH
GA0 = GV0 + GM_WIDTH
GB0 = GA0 + D_MODEL
IN_END = GB0 + D_MODEL


def _dot(a, b):
    return jnp.dot(a, b, preferred_element_type=jnp.float32)


def _dot_nt(a, b):
    return lax.dot_general(a, b, (((1,), (1,)), ((), ())), preferred_element_type=jnp.float32)


LOG2E = math.log2(math.e)


def _gelu_tanh(x):
    c = math.sqrt(2.0 / math.pi)
    k0, k1 = -2.0 * c * LOG2E, -2.0 * c * 0.044715 * LOG2E
    return x * (1.0 / (1.0 + jnp.exp2(x * (x * x * k1 + k0))))


def _sigmoid(x):
    return 1.0 / (1.0 + jnp.exp2(x * -LOG2E))


def _pack_bf16_pairs(x):
    bits = lax.bitcast_convert_type(x, jnp.uint32)
    half = x.shape[1] // 2
    return (bits[:, 0:half] >> 16) | (bits[:, half:] & jnp.uint32(0xFFFF0000))


def _unpack_bf16_pairs(w):
    lo = lax.bitcast_convert_type(w << 16, jnp.float32)
    hi = lax.bitcast_convert_type(w & jnp.uint32(0xFFFF0000), jnp.float32)
    return jnp.concatenate([lo, hi], axis=1).astype(jnp.bfloat16)


def _adaln_kernel(c_ref, w_ref, b_ref, o_ref):
    c = c_ref[...]
    cs = c * _sigmoid(c)
    o_ref[...] = _dot(cs, w_ref[...]) + b_ref[...]


def _adaln_mod(c, w, b):
    bsz, d = c.shape
    n = w.shape[1]
    tn = ADALN_TN
    return pl.pallas_call(
        _adaln_kernel,
        grid=(n // tn,),
        in_specs=[pl.BlockSpec((bsz, d), lambda i: (0, 0)),
                  pl.BlockSpec((d, tn), lambda i: (0, i)),
                  pl.BlockSpec((1, tn), lambda i: (0, i))],
        out_specs=pl.BlockSpec((bsz, tn), lambda i: (0, i)),
        out_shape=jax.ShapeDtypeStruct((bsz, n), jnp.float32),
        compiler_params=pltpu.CompilerParams(dimension_semantics=("arbitrary",)),
        name="adaln_mod",
    )(c, w, b.reshape(1, n))


def _mixer_kernel(relb_ref, sinks_ref, spos_ref,
                  x_ref, mod_ref, pos_ref, n1g_ref, win_ref, lng_ref, lnb_ref, ws_ref, bsb_ref,
                  pa_ref, pb_ref, wo_ref, n2g_ref, wr_ref, br_ref,
                  x1_ref, xs_ref, rw_ref, cnt_ref,
                  kbuf, vbuf, pbuf, biasm, key, ya, yb, strict, sbuf, mbuf, flag):
    b = pl.program_id(0)
    j = pl.program_id(1)
    nblk = TM // BLOCK
    bf16 = jnp.bfloat16

    @pl.when((b == 0) & (j == 0))
    def _():
        key[...] = jnp.zeros_like(key)
        for i in range(nblk):
            flag[i] = 1
        tr = lax.broadcasted_iota(jnp.int32, (TM, TM), 0)
        tc = lax.broadcasted_iota(jnp.int32, (TM, TM), 1)
        strict[...] = jnp.where(tr < tc, 1.0, 0.0).astype(bf16)

    @pl.when(j == 0)
    def _():
        kbuf[0:BLOCK, :] = jnp.zeros((BLOCK, LANES), jnp.float32)
        vbuf[0:BLOCK, :] = jnp.zeros((BLOCK, LANES), jnp.float32)
        pbuf[:, 0:BLOCK] = jnp.zeros((1, BLOCK), jnp.int32)

    pbuf[:, BLOCK:] = pos_ref[...]
    qi = lax.broadcasted_iota(jnp.int32, (BLOCK, BLOCK), 0)
    kc = lax.broadcasted_iota(jnp.int32, (BLOCK, BLOCK), 1)
    from_prev = kc > qi
    rels, changed = [], []
    tile0 = (b * pl.num_programs(1) + j) * TM
    for i in range(nblk):
        pk = pbuf[:, i * BLOCK:(i + 2) * BLOCK]
        rels.append(pk - spos_ref[tile0 + i * BLOCK])
        changed.append(jnp.where(rels[i] != key[i], 1.0, 0.0))
    any_changed = jnp.max(functools.reduce(jnp.maximum, changed))
    any_flag = functools.reduce(jnp.maximum, [flag[i] for i in range(nblk)])
    refresh = (any_changed != 0) | (any_flag != 0) | (j == 0)

    def refresh_bias(i):
        pk = pbuf[:, i * BLOCK:(i + 2) * BLOCK]
        pq = pk[:, BLOCK:]
        pq_col = jnp.transpose(jnp.broadcast_to(pq, (BLOCK, BLOCK)))
        no_prev = (j == 0) if i == 0 else None
        for hd in range(N_HEADS):
            biasm[i * N_HEADS + hd] = jnp.zeros((BLOCK, BLOCK), jnp.float32)
        for side in range(2):
            dist = pq_col - pk[:, side * BLOCK:(side + 1) * BLOCK]
            n = jnp.maximum(dist, 0)
            nf = jnp.maximum(n, 1).astype(jnp.float32)
            large = MAX_EXACT + (jnp.log(nf / MAX_EXACT) / math.log(MAX_DISTANCE / MAX_EXACT)
                                 * (N_BUCKETS - MAX_EXACT)).astype(jnp.int32)
            large = jnp.minimum(large, N_BUCKETS - 1)
            bucket = jnp.where(n < MAX_EXACT, n, large)
            use = from_prev if side == 0 else jnp.logical_not(from_prev)
            for hd in range(N_HEADS):
                acc = jnp.zeros((BLOCK, BLOCK), jnp.float32)
                for bk in range(N_BUCKETS):
                    acc = jnp.where(bucket == bk, relb_ref[bk * N_HEADS + hd], acc)
                if side == 0 and i == 0:
                    acc = jnp.where(no_prev, NEG, acc)
                slot = i * N_HEADS + hd
                biasm[slot] = jnp.where(use, acc, biasm[slot])
        key[i] = rels[i]
        flag[i] = no_prev.astype(jnp.int32) if i == 0 else 0

    for i in range(nblk):
        @pl.when(refresh)
        def _():
            stale = (jnp.max(changed[i]) != 0) | (flag[i] != 0)
            if i == 0:
                stale = stale | (j == 0)
            pl.when(stale)(functools.partial(refresh_bias, i))

    x = x_ref[...]
    sh1, sc1, g1 = mod_ref[0:1, :], mod_ref[1:2, :], mod_ref[2:3, :]
    sh2, sc2 = mod_ref[3:4, :], mod_ref[4:5, :]

    xn = x * lax.rsqrt(jnp.mean(x * x, axis=-1, keepdims=True) + EPS)
    h = (xn * (n1g_ref[...] * (1.0 + sc1)) + sh1).astype(bf16)

    lane = lax.broadcasted_iota(jnp.int32, (1, LANES), 1)
    lo = lane < HEAD_DIM
    q = _dot(h, win_ref[:, Q0:K0]) * (HEAD_DIM ** -0.5)
    lo4 = jnp.concatenate([lo] * (ATTN_Q // LANES), axis=1)
    q_lo = jnp.where(lo4, q, 0.0).astype(bf16)
    q_hi = jnp.where(lo4, 0.0, q).astype(bf16)
    kv = _dot(h, win_ref[:, K0:GU0])
    kbuf[BLOCK:, :] = kv[:, 0:LANES]
    vbuf[BLOCK:, :] = kv[:, LANES:]
    kf = kbuf[...]
    vf = vbuf[...]
    kr = pltpu.roll(kf, HEAD_DIM, 1)
    vr = pltpu.roll(vf, HEAD_DIM, 1)
    kd = (jnp.where(lo, kf, kr).astype(bf16), jnp.where(lo, kr, kf).astype(bf16))
    v_lo = (jnp.where(lo, vf, 0.0).astype(bf16), jnp.where(lo, vr, 0.0).astype(bf16))
    v_hi = (jnp.where(lo, 0.0, vr).astype(bf16), jnp.where(lo, 0.0, vf).astype(bf16))

    ones_blk = jnp.ones((2 * BLOCK, LANES), bf16)

    def attend(i):
        rows = slice(i * BLOCK, (i + 1) * BLOCK)
        band = slice(i * BLOCK, (i + 2) * BLOCK)
        grp = N_HEADS // N_KV_HEADS
        for kvh in range(N_KV_HEADS):
            heads = range(kvh * grp, (kvh + 1) * grp)
            qs = jnp.concatenate([(q_lo if hd % 2 == 0 else q_hi)[rows, hd // 2 * LANES:(hd // 2 + 1) * LANES]
                                  for hd in heads], axis=0)
            s2 = _dot_nt(qs, kd[kvh][band])
            for g, hd in enumerate(heads):
                sh = s2[g * BLOCK:(g + 1) * BLOCK]
                s = jnp.where(from_prev, sh[:, 0:BLOCK], sh[:, BLOCK:]) + biasm[i * N_HEADS + hd]
                sbuf[hd] = s
                m = jnp.maximum(jnp.max(s, axis=-1, keepdims=True), sinks_ref[hd])
                mbuf[hd] = jnp.broadcast_to(m, (BLOCK, BLOCK))
        for kvh in range(N_KV_HEADS):
            outs = {}
            for half, vv in enumerate((v_lo, v_hi)):
                hds = (kvh * grp + half, kvh * grp + half + 2)
                p2 = []
                for hd in hds:
                    p = jnp.exp(sbuf[hd] - mbuf[hd])
                    p2.append(jnp.concatenate([jnp.where(from_prev, p, 0.0), jnp.where(from_prev, 0.0, p)],
                                              axis=1).astype(bf16))
                r = _dot(jnp.concatenate(p2, axis=0), jnp.concatenate([vv[kvh][band], ones_blk], axis=1))
                for n, hd in enumerate(hds):
                    rh = r[n * BLOCK:(n + 1) * BLOCK]
                    den = rh[:, LANES:] + jnp.exp(sinks_ref[hd] - mbuf[hd])
                    outs[hd] = rh[:, 0:LANES] * (1.0 / den)
            for pr in (kvh * grp // 2, kvh * grp // 2 + 1):
                ya[rows, pr * LANES:(pr + 1) * LANES] = (outs[2 * pr] + outs[2 * pr + 1]).astype(bf16)

    u = _gelu_tanh(_dot(h, win_ref[:, GU0:GV0]))
    attend(0)
    vg = _gelu_tanh(_dot(h, win_ref[:, GV0:GA0]))
    mu = jnp.mean(vg, axis=-1, keepdims=True)
    vc = vg - mu
    var = jnp.mean(vc * vc, axis=-1, keepdims=True)
    vn = (vc * lax.rsqrt(var + EPS) * lng_ref[...] + lnb_ref[...]).astype(bf16)
    attend(1)
    gate_a = _sigmoid(_dot(h, win_ref[:, GA0:GB0]))
    attend(2)
    ti = lax.broadcasted_iota(jnp.int32, (GM_CHUNK, GM_CHUNK), 0)
    si = lax.broadcasted_iota(jnp.int32, (GM_CHUNK, GM_CHUNK), 1)
    tril = si <= ti
    for g in range(GM_GROUPS):
        wg = jnp.where(tril, ws_ref[g], 0.0).astype(bf16)
        cols = slice(g * LANES, (g + 1) * LANES)
        for cidx in range(TM // GM_CHUNK):
            rows = slice(cidx * GM_CHUNK, (cidx + 1) * GM_CHUNK)
            sv = _dot(wg, vn[rows, cols]) + bsb_ref[g]
            yb[rows, cols] = (u[rows, cols] * sv).astype(bf16)
    gate_b = _sigmoid(_dot(h, win_ref[:, GB0:IN_END]))
    attend(3)

    kbuf[0:BLOCK, :] = kv[TM - BLOCK:, 0:LANES]
    vbuf[0:BLOCK, :] = kv[TM - BLOCK:, LANES:]
    pbuf[:, 0:BLOCK] = pos_ref[:, TM - BLOCK:]

    h2_parts, logit_parts = [], []
    for grp in range(TAIL_SPLIT):
        tok = slice(grp * TM // TAIL_SPLIT, (grp + 1) * TM // TAIL_SPLIT)
        proj = None
        for c0 in range(0, D_MODEL, MERGE_COLS):
            cs = slice(c0, c0 + MERGE_COLS)
            merged = (gate_a[tok, cs] * _dot(ya[tok, :], pa_ref[:, cs])
                      + gate_b[tok, cs] * _dot(yb[tok, :], pb_ref[:, cs]))
            part = _dot(merged.astype(bf16), wo_ref[cs, :])
            proj = part if proj is None else proj + part
        x1 = x[tok] + g1 * proj
        x1_ref[tok, :] = x1

        xn2 = x1 * lax.rsqrt(jnp.mean(x1 * x1, axis=-1, keepdims=True) + EPS)
        h2_parts.append((xn2 * (n2g_ref[...] * (1.0 + sc2)) + sh2).astype(bf16))
        logit_parts.append(_dot(h2_parts[-1], wr_ref[...]) + br_ref[...])
    h2 = jnp.concatenate(h2_parts, axis=0)
    logits = jnp.concatenate(logit_parts, axis=0)
    lt = jnp.transpose(logits)[0:ROUTER_ROWS, :]
    row = lax.broadcasted_iota(jnp.int32, (ROUTER_ROWS, TM), 0)
    row_f = row.astype(jnp.float32)
    big = float(ROUTER_ROWS)
    is_grp = row < N_GROUPS
    lg = jnp.where(is_grp, lt, NEG)
    lg_max = jnp.max(lg, axis=0, keepdims=True)
    g_idx = jnp.min(jnp.where(lg == lg_max, row_f, big), axis=0, keepdims=True)
    p_g = 1.0 / jnp.sum(jnp.where(is_grp, jnp.exp(lg - lg_max), 0.0), axis=0, keepdims=True)
    row_grp = jnp.floor((row_f - EXPERT_LANE0) * (1.0 / EXPERTS_PER_GROUP))
    in_grp = (row >= EXPERT_LANE0) & (row < EXPERT_LANE0 + N_EXPERTS) & (row_grp == g_idx)
    le = jnp.where(in_grp, lt, NEG)
    m1 = jnp.max(le, axis=0, keepdims=True)
    i1 = jnp.min(jnp.where(le == m1, row_f, big), axis=0, keepdims=True)
    oh1 = row_f == i1
    le2 = jnp.where(oh1, NEG, le)
    m2 = jnp.max(le2, axis=0, keepdims=True)
    i2 = jnp.min(jnp.where(le2 == m2, row_f, big), axis=0, keepdims=True)
    oh2 = row_f == i2
    e2 = jnp.exp(m2 - m1)
    w1 = p_g / (1.0 + e2)
    w2 = p_g * e2 / (1.0 + e2)

    oh = jnp.where(oh1, 1.0, jnp.where(oh2, 1.0, 0.0))
    n_e = jnp.sum(oh, axis=1, keepdims=True)
    cnt_ref[...] = n_e
    padded = jnp.floor((n_e + (CHUNK - 1)) * (1.0 / CHUNK)) * CHUNK
    dst_row = lax.broadcasted_iota(jnp.int32, (ROUTER_ROWS, ROUTER_ROWS), 0)
    src_row = lax.broadcasted_iota(jnp.int32, (ROUTER_ROWS, ROUTER_ROWS), 1)
    lower = jnp.where(src_row < dst_row, 1.0, 0.0).astype(bf16)
    run0 = _dot(lower, jnp.broadcast_to(padded, (ROUTER_ROWS, TM)).astype(bf16))
    slot = _dot(oh.astype(bf16), strict[...]) + run0
    pos1 = jnp.sum(jnp.where(oh1, slot, 0.0), axis=0, keepdims=True)
    pos2 = jnp.sum(jnp.where(oh2, slot, 0.0), axis=0, keepdims=True)
    rrow = lax.broadcasted_iota(jnp.int32, (ROUTE_ROWS, TM), 0)
    rw_ref[...] = jnp.where(rrow == 0, w1, jnp.where(rrow == 1, w2,
                            jnp.where(rrow == 2, pos1, jnp.where(rrow == 3, pos2, 0.0))))

    p1 = pos1.astype(jnp.int32)
    p2 = pos2.astype(jnp.int32)
    sp = lax.broadcasted_iota(jnp.int32, (SORT_ROWS, TM), 0)
    perm = jnp.where(sp == p1, 1.0, jnp.where(sp == p2, 1.0, 0.0)).astype(bf16)
    xs_ref[...] = _pack_bf16_pairs(_dot(perm, h2))


def _mixer(x, mod, positions, rel_bias, n1g, w_in, sinks, lng, lnb, w_s, bsb, p_a, p_b, w_o, n2g, w_r, b_r):
    bsz, seq, d = x.shape
    nj = seq // TM
    const = lambda *shape: pl.BlockSpec(shape, lambda b, j: (0,) * len(shape), pipeline_mode=pl.Buffered(1))
    smem = pl.BlockSpec(memory_space=pltpu.SMEM)
    tile = lambda w: pl.BlockSpec((None, TM, w), lambda b, j: (b, j, 0))
    return pl.pallas_call(
        _mixer_kernel,
        grid=(bsz, nj),
        in_specs=[smem, smem, smem,
                  tile(d),
                  pl.BlockSpec((None, 6, d), lambda b, j: (b, 0, 0)),
                  pl.BlockSpec((None, None, 1, TM), lambda b, j: (b, j, 0, 0)),
                  const(1, d), const(d, IN_END), const(1, GM_WIDTH), const(1, GM_WIDTH),
                  const(GM_GROUPS, GM_CHUNK, GM_CHUNK), const(GM_GROUPS, GM_CHUNK, LANES),
                  const(ATTN_Q, d), const(GM_WIDTH, d), const(d, d), const(1, d),
                  const(d, ROUTER_LANES), const(1, ROUTER_LANES)],
        out_specs=[tile(d),
                   pl.BlockSpec((SORT_ROWS, d // 2), lambda b, j: (b * nj + j, 0)),
                   pl.BlockSpec((ROUTE_ROWS, TM), lambda b, j: (0, b * nj + j)),
                   pl.BlockSpec((None, ROUTER_ROWS, 1), lambda b, j: (b * nj + j, 0, 0))],
        out_shape=[jax.ShapeDtypeStruct((bsz, seq, d), jnp.float32),
                   jax.ShapeDtypeStruct((bsz * nj * SORT_ROWS, d // 2), jnp.uint32),
                   jax.ShapeDtypeStruct((ROUTE_ROWS, bsz * seq), jnp.float32),
                   jax.ShapeDtypeStruct((bsz * nj, ROUTER_ROWS, 1), jnp.float32)],
        scratch_shapes=[pltpu.VMEM((TM + BLOCK, LANES), jnp.float32),
                        pltpu.VMEM((TM + BLOCK, LANES), jnp.float32),
                        pltpu.VMEM((1, TM + BLOCK), jnp.int32),
                        pltpu.VMEM((TM // BLOCK * N_HEADS, BLOCK, BLOCK), jnp.float32),
                        pltpu.VMEM((TM // BLOCK, 1, 2 * BLOCK), jnp.int32),
                        pltpu.VMEM((TM, ATTN_Q), jnp.bfloat16),
                        pltpu.VMEM((TM, GM_WIDTH), jnp.bfloat16),
                        pltpu.VMEM((TM, TM), jnp.bfloat16),
                        pltpu.VMEM((N_HEADS, BLOCK, BLOCK), jnp.float32),
                        pltpu.VMEM((N_HEADS, BLOCK, BLOCK), jnp.float32),
                        pltpu.SMEM((TM // BLOCK,), jnp.int32)],
        compiler_params=pltpu.CompilerParams(dimension_semantics=("arbitrary", "arbitrary"),
                                             vmem_limit_bytes=MIXER_VMEM),
        name="mixer",
    )(rel_bias.reshape(-1), sinks, positions.reshape(-1), x, mod, positions.reshape(bsz, nj, 1, TM), n1g, w_in, lng, lnb,
      w_s, bsb, p_a, p_b, w_o, n2g, w_r, b_r)


def _gather_chunks(idx_ref, idx0, n, src_hbm, dst, sem, inline=False, split_priority=False):
    def issue(c, carry, priority=0):
        src = pl.multiple_of(idx_ref[idx0 + c] * CHUNK, CHUNK)
        pltpu.make_async_copy(src_hbm.at[pl.ds(src, CHUNK)],
                              dst.at[pl.ds(pl.multiple_of(c * CHUNK, CHUNK), CHUNK)], sem).start(priority)
        return carry

    if inline:
        for c in range(n):
            issue(c, 0, c % 2 if split_priority else 0)
    else:
        lax.fori_loop(0, n, issue, 0, unroll=GATHER_UNROLL)


def _wait_chunks(n, src_hbm, dst, sem):
    pltpu.make_async_copy(src_hbm.at[pl.ds(0, n * CHUNK)], dst, sem).wait()


def _expert_kernel(src_ref, be_ref, ne_ref, ws_ref, nq_ref, lb_ref, xs_hbm, wg_hbm, wu_hbm, wd_hbm, yr_ref,
                   xbuf, wgf, wuf, wdf, wgb, wub, wdb, sem, wsem):
    i = pl.program_id(0)
    last = lb_ref[0]
    slot = i % GATHER_SLOTS
    ahead = jnp.minimum(i + GATHER_SLOTS - 1, last)
    ahead_slot = (i + GATHER_SLOTS - 1) % GATHER_SLOTS

    def weight_copies(e, w):
        return [pltpu.make_async_copy(hbm.at[e], buf.at[w], wsem.at[w])
                for hbm, buf in ((wg_hbm, wgf), (wu_hbm, wuf), (wd_hbm, wdf))]

    @pl.when(i == 0)
    def _():
        for s in range(GATHER_SLOTS - 1):
            _gather_chunks(src_ref, jnp.minimum(s, last) * BLOCK_CHUNKS, BLOCK_CHUNKS, xs_hbm, xbuf.at[s], sem.at[s])
        for cp in weight_copies(be_ref[0], 0):
            cp.start()

    @pl.when(i <= last)
    def _():
        changed = (i == 0) | (be_ref[i] != be_ref[jnp.maximum(i - 1, 0)])

        @pl.when(changed)
        def _():
            w = ws_ref[i]
            for cp in weight_copies(be_ref[i], w):
                cp.wait()
            wgb[...] = wgf[w].astype(jnp.bfloat16)
            wub[...] = wuf[w].astype(jnp.bfloat16)
            wdb[...] = wdf[w].astype(jnp.bfloat16)

            @pl.when(ne_ref[i] >= 0)
            def _():
                for cp in weight_copies(ne_ref[i], 1 - w):
                    cp.start(priority=1)

        _wait_chunks(BLOCK_CHUNKS, xs_hbm, xbuf.at[slot], sem.at[slot])

        def compute(rows):
            xb = _unpack_bf16_pairs(xbuf[slot, 0:rows, :])
            y = None
            for c0 in range(0, D_EXPERT, HIDDEN_COLS):
                cs = slice(c0, c0 + HIDDEN_COLS)
                hg = _dot(xb, wgb[:, cs])
                hu = _dot(xb, wub[:, cs])
                hid = (hg * _sigmoid(hg) * hu).astype(jnp.bfloat16)
                part = _dot(hid, wdb[cs, :])
                y = part if y is None else y + part
            yr_ref[0:rows, :] = _pack_bf16_pairs(y.astype(jnp.bfloat16).astype(jnp.float32))
            if rows < BM:
                yr_ref[rows:, :] = jnp.zeros((BM - rows, yr_ref.shape[1]), yr_ref.dtype)
            _gather_chunks(src_ref, ahead * BLOCK_CHUNKS, BLOCK_CHUNKS, xs_hbm, xbuf.at[ahead_slot],
                           sem.at[ahead_slot], inline=True)

        for quarters in range(1, BLOCK_QUARTERS + 1):
            pl.when(nq_ref[i] == quarters)(functools.partial(compute, quarters * BM // BLOCK_QUARTERS))

        @pl.when(i == last)
        def _():
            for s in range(1, GATHER_SLOTS):
                other = (i + s) % GATHER_SLOTS
                _wait_chunks(BLOCK_CHUNKS, xs_hbm, xbuf.at[other], sem.at[other])

    @pl.when(i > last)
    def _():
        yr_ref[...] = jnp.zeros_like(yr_ref)


def _experts(src_chunk, block_e, next_e, w_slot, quarters, last_blk, xs, w_gate, w_up, w_down):
    d = w_gate.shape[1]
    n_rows = src_chunk.shape[0] * CHUNK
    hbm = pl.BlockSpec(memory_space=pl.ANY)
    return pl.pallas_call(
        _expert_kernel,
        grid_spec=pltpu.PrefetchScalarGridSpec(
            num_scalar_prefetch=6,
            grid=(n_rows // BM,),
            in_specs=[hbm, hbm, hbm, hbm],
            out_specs=pl.BlockSpec((BM, d // 2), lambda i, *_: (i, 0)),
            scratch_shapes=[pltpu.VMEM((GATHER_SLOTS, BM, d // 2), jnp.uint32),
                            pltpu.VMEM((2, d, D_EXPERT), jnp.float32),
                            pltpu.VMEM((2, d, D_EXPERT), jnp.float32),
                            pltpu.VMEM((2, D_EXPERT, d), jnp.float32),
                            pltpu.VMEM((d, D_EXPERT), jnp.bfloat16),
                            pltpu.VMEM((d, D_EXPERT), jnp.bfloat16),
                            pltpu.VMEM((D_EXPERT, d), jnp.bfloat16),
                            pltpu.SemaphoreType.DMA((GATHER_SLOTS,)),
                            pltpu.SemaphoreType.DMA((2,))]),
        out_shape=jax.ShapeDtypeStruct((n_rows, d // 2), jnp.uint32),
        compiler_params=pltpu.CompilerParams(dimension_semantics=("arbitrary",),
                                             vmem_limit_bytes=EXPERTS_VMEM),
        name="experts",
    )(src_chunk, block_e, next_e, w_slot, quarters, last_blk, xs, w_gate, w_up, w_down)


def _combine_kernel(src_ref, x1_ref, rw_ref, mod_ref, fg_ref, yr_hbm, out_ref, ybuf, sem, *, final_norm):
    i = pl.program_id(0)
    slot = i % 2

    @pl.when(i == 0)
    def _():
        _gather_chunks(src_ref, 0, SORT_CHUNKS, yr_hbm, ybuf.at[0], sem.at[0])

    rows = jnp.concatenate([rw_ref[...], jnp.zeros((LANES - ROUTE_ROWS, TM), jnp.float32)], axis=0)
    rw = jnp.transpose(rows)
    sp = lax.broadcasted_iota(jnp.int32, (1, SORT_ROWS), 1)
    p1 = rw[:, 2:3].astype(jnp.int32)
    p2 = rw[:, 3:4].astype(jnp.int32)
    wmat = jnp.where(sp == p1, rw[:, 0:1], jnp.where(sp == p2, rw[:, 1:2], 0.0)).astype(jnp.bfloat16)
    nxt = jnp.minimum(i + 1, pl.num_programs(0) - 1)
    _gather_chunks(src_ref, nxt * SORT_CHUNKS, SORT_CHUNKS, yr_hbm, ybuf.at[1 - slot], sem.at[1 - slot],
                   inline=True, split_priority=True)
    _wait_chunks(SORT_CHUNKS, yr_hbm, ybuf.at[slot], sem.at[slot])
    moe = _dot(wmat, _unpack_bf16_pairs(ybuf[slot]))
    x2 = x1_ref[...] + mod_ref[5:6, :] * moe
    if final_norm:
        x2 = x2 * lax.rsqrt(jnp.mean(x2 * x2, axis=-1, keepdims=True) + EPS) * fg_ref[...]
    out_ref[...] = x2

    @pl.when(i == pl.num_programs(0) - 1)
    def _():
        _wait_chunks(SORT_CHUNKS, yr_hbm, ybuf.at[1 - slot], sem.at[1 - slot])


def _combine(src_chunk, x1, rw, mod, final_g, yr, seq, final_norm):
    t, d = x1.shape
    per_seq = seq // TM
    return pl.pallas_call(
        functools.partial(_combine_kernel, final_norm=final_norm),
        grid_spec=pltpu.PrefetchScalarGridSpec(
            num_scalar_prefetch=1,
            grid=(t // TM,),
            in_specs=[pl.BlockSpec((TM, d), lambda i, src: (i, 0)),
                      pl.BlockSpec((ROUTE_ROWS, TM), lambda i, src: (0, i)),
                      pl.BlockSpec((None, 6, d), lambda i, src: (i // per_seq, 0, 0)),
                      pl.BlockSpec((1, d), lambda i, src: (0, 0)),
                      pl.BlockSpec(memory_space=pl.ANY)],
            out_specs=pl.BlockSpec((TM, d), lambda i, src: (i, 0)),
            scratch_shapes=[pltpu.VMEM((2, SORT_ROWS, d // 2), jnp.uint32),
                            pltpu.SemaphoreType.DMA((2,))]),
        out_shape=jax.ShapeDtypeStruct((t, d), jnp.float32),
        compiler_params=pltpu.CompilerParams(dimension_semantics=("arbitrary",),
                                             vmem_limit_bytes=COMBINE_VMEM),
        name="combine",
    )(src_chunk, x1, rw, mod, final_g.reshape(1, d), yr)


def kernel(x, c, positions, rel_bias, w_ada, b_ada, norm1_g, w_in, sinks, gm_ln_g, gm_ln_b, gm_w_s, gm_b_s,
           p_a, p_b, w_o, norm2_g, w_router_g, b_router_g, w_router_e, b_router_e, w_gate, w_up, w_down,
           final_g):
    bsz, seq, d = x.shape
    t = bsz * seq
    bf16 = jnp.bfloat16
    depth = w_ada.shape[0]
    n_tiles = t // TM
    n_chunks = t * TOP_K // CHUNK + n_tiles * N_EXPERTS + N_EXPERTS * BLOCK_CHUNKS
    i32 = jnp.int32
    for l in range(depth):
        mod = _adaln_mod(c, w_ada[l], b_ada[l]).reshape(bsz, 6, d)
        pad = ROUTER_LANES - N_GROUPS - N_EXPERTS
        w_r = jnp.concatenate([w_router_g[l], w_router_e[l], jnp.zeros((d, pad), jnp.float32)], axis=1)
        b_r = jnp.concatenate([b_router_g[l], b_router_e[l], jnp.zeros((pad,), jnp.float32)]).reshape(1, -1)
        bsb = jnp.broadcast_to(gm_b_s[l][:, :, None], (GM_GROUPS, GM_CHUNK, LANES))
        x1, xs, rw, cnt = _mixer(
            x, mod, positions, rel_bias, norm1_g[l].reshape(1, d), w_in[l].astype(bf16), sinks[l],
            gm_ln_g[l].reshape(1, -1), gm_ln_b[l].reshape(1, -1), gm_w_s[l], bsb,
            p_a[l].astype(bf16), p_b[l].astype(bf16), w_o[l].astype(bf16), norm2_g[l].reshape(1, d),
            w_r.astype(bf16), b_r)

        n = cnt[:, EXPERT_LANE0:EXPERT_LANE0 + N_EXPERTS, 0].astype(i32)
        nch = (n + CHUNK - 1) // CHUNK
        run0_tile = jnp.cumsum(nch, axis=1) - nch
        run0_exp = jnp.cumsum(nch, axis=0) - nch
        tot = jnp.sum(nch, axis=0)
        seg = (tot + BLOCK_CHUNKS - 1) // BLOCK_CHUNKS * BLOCK_CHUNKS
        seg_end = jnp.cumsum(seg)
        seg0 = seg_end - seg

        pos_tile = jnp.arange(n_tiles, dtype=i32)[:, None] * SORT_CHUNKS + run0_tile
        pos_exp = seg0[None, :] + run0_exp

        j = jnp.arange(n_chunks, dtype=i32)[:, None]
        e_j = jnp.minimum(jnp.sum(j >= seg_end[None, :], axis=1), N_EXPERTS - 1)
        tables = jnp.concatenate([pos_exp.T, (pos_exp + nch).T, (pos_tile - pos_exp).T], axis=1)
        picked = jnp.dot(jax.nn.one_hot(e_j, N_EXPERTS, dtype=jnp.float32), tables.astype(jnp.float32),
                         precision=lax.Precision.HIGHEST).astype(i32)
        lo_j, hi_j, shift_j = picked[:, :n_tiles], picked[:, n_tiles:2 * n_tiles], picked[:, 2 * n_tiles:]
        inside = (j >= lo_j) & (j < hi_j)
        src_j = jnp.sum(jnp.where(inside, j + shift_j, 0), axis=1).astype(i32)
        blk0 = jnp.arange(n_chunks // BLOCK_CHUNKS, dtype=i32) * BLOCK_CHUNKS
        block_e = jnp.minimum(jnp.sum(blk0[:, None] >= seg_end[None, :], axis=1), N_EXPERTS - 1).astype(i32)
        last_blk = (jnp.maximum(seg_end[-1] // BLOCK_CHUNKS, 1) - 1).astype(i32).reshape(1)
        experts = jnp.arange(N_EXPERTS, dtype=i32)
        later = (experts[None, :] > block_e[:, None]) & (seg[None, :] > 0)
        next_e = jnp.min(jnp.where(later, experts[None, :], N_EXPERTS), axis=1)
        next_e = jnp.where(next_e < N_EXPERTS, next_e, -1).astype(i32)
        w_slot = (jnp.sum((experts[None, :] < block_e[:, None]) & (seg[None, :] > 0), axis=1) % 2).astype(i32)

        q = jnp.arange(SORT_CHUNKS, dtype=i32)[None, :, None]
        in_run = (q >= run0_tile[:, None, :]) & (q < (run0_tile + nch)[:, None, :])
        dst_q = jnp.sum(jnp.where(in_run, q + (pos_exp - run0_tile)[:, None, :], 0), axis=2).astype(i32).reshape(-1)

        own = block_e[:, None] == experts[None, :]
        filled = jnp.sum(jnp.where(own, (seg0 + tot)[None, :], 0), axis=1) - blk0
        quarter_chunks = BLOCK_CHUNKS // BLOCK_QUARTERS
        quarters = jnp.clip((filled + quarter_chunks - 1) // quarter_chunks, 1, BLOCK_QUARTERS).astype(i32)

        yr = _experts(src_j, block_e, next_e, w_slot, quarters, last_blk, xs, w_gate[l], w_up[l], w_down[l])
        x = _combine(dst_q, x1.reshape(t, d), rw, mod, final_g, yr, seq,
                     final_norm=(l == depth - 1)).reshape(bsz, seq, d)
    return x
```
